```python
import math
import jax
import jax.numpy as jnp
from jax import lax
import numpy as np

D_MODEL = 1024
BATCH = 1
SEQ = 16384
DEPTH = 2

MEM_LEN = 256
LN_EPS = 1e-5
DN_ALPHA = (2 * DEPTH) ** 0.25
DN_BETA = (8 * DEPTH) ** -0.25

GLA_HEADS = 4
GLA_DK = D_MODEL // 16
GLA_DV = D_MODEL // 8
GLA_QK = GLA_HEADS * GLA_DK
GLA_V = GLA_HEADS * GLA_DV
GLA_LR = 16
GLA_TAU = 16.0
GLA_CHUNK = 64

RWKV_HD = 64
RWKV_HEADS = D_MODEL // 2 // RWKV_HD
RWKV_D = RWKV_HEADS * RWKV_HD
RWKV_W_LR = 64
RWKV_A_LR = 64
RWKV_G_LR = 128
RWKV_COLS = 3 * RWKV_D + RWKV_W_LR + RWKV_A_LR + RWKV_G_LR
RWKV_DECAY_SCALE = math.exp(-0.5)
RWKV_GN_EPS = 64e-5

EVEN_WIDTHS = (GLA_QK, GLA_QK, GLA_V, GLA_V, GLA_LR, RWKV_COLS)
EVEN_IN = sum(EVEN_WIDTHS)
EVEN_OUT = GLA_V + RWKV_D

SSD_HD = 64
SSD_HEADS = D_MODEL // SSD_HD
SSD_INNER = SSD_HEADS * SSD_HD
SSD_GROUPS = 2
SSD_STATE = 128
SSD_BC = SSD_GROUPS * SSD_STATE
SSD_CONV = 4
SSD_CONV_CH = SSD_INNER + 2 * SSD_BC
SSD_CHUNK = 128

MOBA_HD = 64
MOBA_HEADS = 8
MOBA_D = MOBA_HEADS * MOBA_HD
MOBA_BLOCK = 256
MOBA_TOPK = 3
MOBA_QBLOCK = 128

ODD_WIDTHS = (SSD_INNER, SSD_CONV_CH, SSD_HEADS, MOBA_D, MOBA_D, MOBA_D)
ODD_IN = sum(ODD_WIDTHS)
ODD_OUT = SSD_INNER + MOBA_D

XATTN_HEADS = 4
XATTN_HD = D_MODEL // XATTN_HEADS

D_FF = 2816
N_EXPERTS = 8
TOP_K = 2
EXPERT_FF = 2816
MOE_BLOCK = 128

kernel_name = 'hybrid_gla_rwkv7_ssd_moba_deepnorm'


def _offsets(widths):
    return [int(v) for v in np.cumsum(widths)[:-1]]


def layer_norm(x, g, b):
    xf = x.astype(jnp.float32)
    mu = jnp.mean(xf, -1, keepdims=True)
    var = jnp.mean(jnp.square(xf - mu), -1, keepdims=True)
    return ((xf - mu) * lax.rsqrt(var + LN_EPS) * g + b).astype(x.dtype)


def rms_norm(x, g, eps):
    xf = x.astype(jnp.float32)
    return xf * lax.rsqrt(jnp.mean(jnp.square(xf), -1, keepdims=True) + eps) * g


def post_norm(x, f, g, b):
    return layer_norm(DN_ALPHA * x + f, g, b)


def token_shift(p):
    return jnp.pad(p[:, :-1], ((0, 0), (1, 0), (0, 0)))


def alibi_slopes(n):
    return jnp.exp2(-8.0 * jnp.arange(1, n + 1, dtype=jnp.float32) / n)


def causal_depthwise_conv(x, w, b):
    K, C = w.shape
    y = lax.conv_general_dilated(x, w[:, None, :].astype(x.dtype), window_strides=(1,),
                                 padding=[(K - 1, 0)], dimension_numbers=('NWC', 'WIO', 'NWC'),
                                 feature_group_count=C)
    return y + b


def segsum_exp(a):
    L = a.shape[-1]
    cs = jnp.cumsum(a, -1)
    mask = jnp.tril(jnp.ones((L, L), bool))
    return jnp.exp(jnp.where(mask, cs[..., :, None] - cs[..., None, :], -jnp.inf))


def gla_mixer(q, k, v, g, a_lr, wa2, ba, norm_g):
    Bsz, S, _ = q.shape
    H, dk, dv, C = GLA_HEADS, GLA_DK, GLA_DV, GLA_CHUNK
    nc = S // C
    log_a = jax.nn.log_sigmoid((a_lr @ wa2 + ba).astype(jnp.float32)) / GLA_TAU

    def chunks(t, d):
        return t.reshape(Bsz, nc, C, H, d).transpose(0, 3, 1, 2, 4).astype(jnp.float32)

    qc = chunks(q, dk) * dk ** -0.5
    kc = chunks(k, dk)
    vc = chunks(v, dv)
    b = jnp.cumsum(chunks(log_a, dk), axis=3)
    b_last = b[:, :, :, -1:, :]
    q_dec = qc * jnp.exp(b)
    k_dec = kc * jnp.exp(-b)
    causal = jnp.tril(jnp.ones((C, C), bool))
    attn = jnp.where(causal, jnp.einsum('bhnid,bhnjd->bhnij', q_dec, k_dec), 0.0)
    o_intra = jnp.einsum('bhnij,bhnjv->bhniv', attn, vc)
    kv_chunk = jnp.einsum('bhncd,bhncv->bhndv', kc * jnp.exp(b_last - b), vc)
    decay_chunk = jnp.exp(b_last[:, :, :, 0, :])

    def step(state, inp):
        dec, kv = inp
        return dec[..., None] * state + kv, state

    init = jnp.zeros((Bsz, H, dk, dv), jnp.float32)
    _, s_prev = lax.scan(step, init, (jnp.moveaxis(decay_chunk, 2, 0), jnp.moveaxis(kv_chunk, 2, 0)))
    s_prev = jnp.moveaxis(s_prev, 0, 2)
    o = o_intra + jnp.einsum('bhncd,bhndv->bhncv', q_dec, s_prev)
    o = o.transpose(0, 2, 3, 1, 4).reshape(Bsz, S, H, dv)
    o = rms_norm(o, norm_g.reshape(H, dv), 1e-5)
    o = o * jax.nn.silu(g.astype(jnp.float32).reshape(Bsz, S, H, dv))
    return o.reshape(Bsz, S, H * dv)


def rwkv7_mixer(p, mu, w0, w2, a0, a2, g2, k_k, k_a, r_k, ln_w, ln_b):
    Bsz, S, _ = p.shape
    H, N, D = RWKV_HEADS, RWKV_HD, RWKV_D
    p = p.astype(jnp.float32)
    p = p + mu * (token_shift(p) - p)
    r, k, v, xw, xa, xg = jnp.split(p, _offsets((D, D, D, RWKV_W_LR, RWKV_A_LR, RWKV_G_LR)), axis=-1)
    log_w = -RWKV_DECAY_SCALE * jax.nn.sigmoid(w0 + jnp.tanh(xw) @ w2)
    a = jax.nn.sigmoid(a0 + xa @ a2)
    g = jax.nn.sigmoid(xg) @ g2

    def heads(t):
        return t.reshape(Bsz, S, H, N)

    kk = heads(k * k_k)
    kk = kk * lax.rsqrt(jnp.maximum(jnp.sum(kk * kk, -1, keepdims=True), 1e-24))
    k = k * (1.0 + (a - 1.0) * k_a)
    r_h, w_h, k_h, v_h, a_h = heads(r), jnp.exp(heads(log_w)), heads(k), heads(v), heads(a)

    def step(state, inp):
        r_t, w_t, k_t, v_t, kk_t, a_t = inp
        sk = jnp.einsum('bhvk,bhk->bhv', state, kk_t)
        state = (state * w_t[:, :, None, :]
                 - sk[..., None] * (kk_t * a_t)[:, :, None, :]
                 + v_t[..., None] * k_t[:, :, None, :])
        return state, jnp.einsum('bhvk,bhk->bhv', state, r_t)

    xs = tuple(jnp.moveaxis(t, 1, 0) for t in (r_h, w_h, k_h, v_h, kk, a_h))
    _, o = lax.scan(step, jnp.zeros((Bsz, H, N, N), jnp.float32), xs)
    o = jnp.moveaxis(o, 0, 1)
    mean = jnp.mean(o, -1, keepdims=True)
    var = jnp.mean(jnp.square(o - mean), -1, keepdims=True)
    o = (o - mean) * lax.rsqrt(var + RWKV_GN_EPS) * ln_w.reshape(H, N) + ln_b.reshape(H, N)
    bonus = jnp.sum(r_h * k_h * r_k.reshape(H, N), -1, keepdims=True) * v_h
    return (o + bonus).reshape(Bsz, S, D) * g


def ssd_scan(x, dt, A, Bm, Cm):
    Bsz, S, H, P = x.shape
    G, N = Bm.shape[2], Bm.shape[3]
    HG = H // G
    L = SSD_CHUNK
    nc = S // L
    xd = (x * dt[..., None]).reshape(Bsz, nc, L, G, HG, P)
    a = (dt * A).reshape(Bsz, nc, L, G, HG).transpose(0, 3, 4, 1, 2)
    Bc = Bm.reshape(Bsz, nc, L, G, N)
    Cc = Cm.reshape(Bsz, nc, L, G, N)
    a_cs = jnp.cumsum(a, -1)
    cb = jnp.einsum('bclgn,bcsgn->bgcls', Cc, Bc)
    y_diag = jnp.einsum('bgcls,bghcls,bcsghp->bclghp', cb, segsum_exp(a), xd)
    decay_states = jnp.exp(a_cs[..., -1:] - a_cs)
    states = jnp.einsum('bcsgn,bghcs,bcsghp->bcghpn', Bc, decay_states, xd)
    chunk_decay = jnp.exp(a_cs[..., -1])

    def step(h, inp):
        dec, st = inp
        return dec[..., None, None] * h + st, h

    init = jnp.zeros((Bsz, G, HG, P, N), jnp.float32)
    _, prev = lax.scan(step, init, (jnp.moveaxis(chunk_decay, 3, 0), jnp.moveaxis(states, 1, 0)))
    prev = jnp.moveaxis(prev, 0, 1)
    y_off = jnp.einsum('bclgn,bcghpn,bghcl->bclghp', Cc, prev, jnp.exp(a_cs))
    return (y_diag + y_off).reshape(Bsz, S, H, P)


def ssd_mixer(z, xbc, dt_raw, conv_w, conv_b, dt_bias, a_log, d_skip, norm_g):
    Bsz, S, _ = z.shape
    xbc = jax.nn.silu(causal_depthwise_conv(xbc, conv_w, conv_b)).astype(jnp.float32)
    xs, Bm, Cm = jnp.split(xbc, [SSD_INNER, SSD_INNER + SSD_BC], axis=-1)
    x_h = xs.reshape(Bsz, S, SSD_HEADS, SSD_HD)
    dt = jax.nn.softplus(dt_raw.astype(jnp.float32) + dt_bias.astype(jnp.float32))
    A = -jnp.exp(a_log.astype(jnp.float32))
    y = ssd_scan(x_h, dt, A, Bm.reshape(Bsz, S, SSD_GROUPS, SSD_STATE), Cm.reshape(Bsz, S, SSD_GROUPS, SSD_STATE))
    y = y + d_skip.astype(jnp.float32)[:, None] * x_h
    y = y.reshape(Bsz, S, SSD_INNER) * jax.nn.silu(z.astype(jnp.float32))
    y = rms_norm(y.reshape(Bsz, S, SSD_GROUPS, SSD_INNER // SSD_GROUPS),
                 norm_g.reshape(SSD_GROUPS, SSD_INNER // SSD_GROUPS), 1e-5)
    return y.reshape(Bsz, S, SSD_INNER)


def moba_attention(q, k, v):
    Bsz, S, _ = q.shape
    H, dh, BS, QB, K = MOBA_HEADS, MOBA_HD, MOBA_BLOCK, MOBA_QBLOCK, MOBA_TOPK
    nb = max(-(-S // BS), K + 1)
    pad = nb * BS - S

    def heads(t):
        return t.reshape(Bsz, S, H, dh).transpose(0, 2, 1, 3).astype(jnp.float32)

    qh = heads(q) * dh ** -0.5
    kb = jnp.pad(heads(k), ((0, 0), (0, 0), (0, pad), (0, 0))).reshape(Bsz, H, nb, BS, dh)
    vb = jnp.pad(heads(v), ((0, 0), (0, 0), (0, pad), (0, 0))).reshape(Bsz, H, nb, BS, dh)
    kmean = jnp.mean(kb, axis=3)
    slopes = alibi_slopes(H)
    gather = jax.vmap(jax.vmap(lambda blocks, idx: blocks[idx]))

    def query_block(qi):
        q_blk = lax.dynamic_slice_in_dim(qh, qi * QB, QB, axis=2)
        own = (qi * QB) // BS
        t_pos = qi * QB + jnp.arange(QB)
        gate = jnp.einsum('bhqd,bhnd->bhqn', q_blk, kmean)
        gate = jnp.where(jnp.arange(nb) < own, gate, -jnp.inf)
        _, sel = lax.top_k(gate, K)
        valid = jnp.arange(K) < own
        k_sel = gather(kb, sel)
        v_sel = gather(vb, sel)
        s_pos = sel[..., None] * BS + jnp.arange(BS)
        dist_sel = (t_pos[:, None, None] - s_pos).astype(jnp.float32)
        s_sel = jnp.einsum('bhqd,bhqjkd->bhqjk', q_blk, k_sel) - slopes[:, None, None, None] * dist_sel
        s_sel = jnp.where(valid[:, None], s_sel, -jnp.inf)
        k_own = lax.dynamic_index_in_dim(kb, own, axis=2, keepdims=False)
        v_own = lax.dynamic_index_in_dim(vb, own, axis=2, keepdims=False)
        dist_own = t_pos[:, None] - (own * BS + jnp.arange(BS))[None, :]
        s_own = jnp.einsum('bhqd,bhkd->bhqk', q_blk, k_own) - slopes[:, None, None] * dist_own.astype(jnp.float32)
        s_own = jnp.where(dist_own >= 0, s_own, -jnp.inf)
        logits = jnp.concatenate([s_own, s_sel.reshape(Bsz, H, QB, K * BS)], axis=-1)
        prob = jax.nn.softmax(logits, axis=-1)
        p_own = prob[..., :BS]
        p_sel = prob[..., BS:].reshape(Bsz, H, QB, K, BS)
        return (jnp.einsum('bhqk,bhkd->bhqd', p_own, v_own)
                + jnp.einsum('bhqjk,bhqjkd->bhqd', p_sel, v_sel))

    out = lax.map(query_block, jnp.arange(S // QB))
    return out.transpose(1, 0, 3, 2, 4).reshape(Bsz, S, H * dh)


def gla_rwkv_mixer(h, w_in, gla_wa2, gla_ba, gla_norm, rwkv_mu, rwkv_w0, rwkv_w2, rwkv_a0, rwkv_a2,
                   rwkv_g2, rwkv_kk, rwkv_ka, rwkv_rk, rwkv_lnw, rwkv_lnb, w_out):
    p = h @ w_in
    q, k, v, g, a_lr, p_rwkv = jnp.split(p, _offsets(EVEN_WIDTHS), axis=-1)
    o_gla = gla_mixer(q, k, v, g, a_lr, gla_wa2, gla_ba, gla_norm)
    o_rwkv = rwkv7_mixer(p_rwkv, rwkv_mu, rwkv_w0, rwkv_w2, rwkv_a0, rwkv_a2, rwkv_g2,
                         rwkv_kk, rwkv_ka, rwkv_rk, rwkv_lnw, rwkv_lnb)
    return jnp.concatenate([o_gla, o_rwkv], axis=-1).astype(h.dtype) @ w_out


def ssd_moba_mixer(h, w_in, conv_w, conv_b, dt_bias, a_log, d_skip, ssd_norm, w_out):
    p = h @ w_in
    z, xbc, dt_raw, mq, mk, mv = jnp.split(p, _offsets(ODD_WIDTHS), axis=-1)
    o_ssd = ssd_mixer(z, xbc, dt_raw, conv_w, conv_b, dt_bias, a_log, d_skip, ssd_norm)
    o_moba = moba_attention(mq, mk, mv)
    return jnp.concatenate([o_ssd, o_moba], axis=-1).astype(h.dtype) @ w_out


def memory_cross_attention(x, mem, wq, wk, wv, wo):
    Bsz, S, _ = x.shape
    M = mem.shape[1]
    q = (x @ wq).reshape(Bsz, S, XATTN_HEADS, XATTN_HD)
    k = (mem @ wk).reshape(Bsz, M, XATTN_HEADS, XATTN_HD)
    v = (mem @ wv).reshape(Bsz, M, XATTN_HEADS, XATTN_HD)
    s = jnp.einsum('bshd,bmhd->bhsm', q, k).astype(jnp.float32) * XATTN_HD ** -0.5
    prob = jax.nn.softmax(s, axis=-1).astype(v.dtype)
    o = jnp.einsum('bhsm,bmhd->bshd', prob, v).reshape(Bsz, S, D_MODEL)
    return o @ wo


def swiglu(h, wg, wu, wd):
    return (jax.nn.silu(h @ wg) * (h @ wu)) @ wd


def moe_swiglu(x, w_router, b_router, wg, wu, wd):
    Bsz, S, D = x.shape
    T = Bsz * S
    xt = x.reshape(T, D)
    logits = (xt @ w_router).astype(jnp.float32) + b_router.astype(jnp.float32)
    top_logit, top_e = lax.top_k(logits, TOP_K)
    gates = jax.nn.softmax(top_logit, axis=-1)
    flat_e = top_e.reshape(-1)
    flat_tok = jnp.repeat(jnp.arange(T, dtype=jnp.int32), TOP_K)
    flat_gate = gates.reshape(-1)
    order = jnp.argsort(flat_e)
    se = flat_e[order]
    counts = jnp.bincount(flat_e, length=N_EXPERTS)
    padded = (counts + MOE_BLOCK - 1) // MOE_BLOCK * MOE_BLOCK
    start = jnp.cumsum(counts) - counts
    pend = jnp.cumsum(padded)
    pstart = pend - padded
    dest = pstart[se] + jnp.arange(T * TOP_K) - start[se]
    n_rows = T * TOP_K + N_EXPERTS * MOE_BLOCK
    n_blocks = n_rows // MOE_BLOCK
    row_tok = jnp.full((n_rows,), T, jnp.int32).at[dest].set(flat_tok[order])
    row_gate = jnp.zeros((n_rows,), jnp.float32).at[dest].set(flat_gate[order])
    block_e = jnp.minimum(jnp.searchsorted(pend, jnp.arange(n_blocks) * MOE_BLOCK, side='right'), N_EXPERTS - 1)
    x_rows = jnp.take(xt, row_tok, axis=0, mode='fill', fill_value=0).reshape(n_blocks, MOE_BLOCK, D)

    def expert_block(args):
        xb, e = args
        return (jax.nn.silu(xb @ wg[e]) * (xb @ wu[e])) @ wd[e]

    y_rows = lax.map(expert_block, (x_rows, block_e)).reshape(n_rows, D)
    y = jnp.zeros((T, D), jnp.float32).at[row_tok].add(y_rows.astype(jnp.float32) * row_gate[:, None], mode='drop')
    return y.reshape(Bsz, S, D).astype(x.dtype)


def setup_inputs(seed: int = 0) -> dict:
    key = jax.random.key(seed)
    keys = iter(jax.random.split(key, 96))

    def normal(shape, scale):
        return jax.random.normal(next(keys), shape, jnp.float32) * scale

    def gain(n):
        return 1.0 + normal((n,), 0.02)

    d = D_MODEL
    inp = {}
    inp['x'] = normal((BATCH, SEQ, d), 1.0)
    inp['mem'] = normal((BATCH, MEM_LEN, d), 1.0)
    inp['l0_w_in'] = normal((d, EVEN_IN), d ** -0.5)
    inp['l0_gla_wa2'] = normal((GLA_LR, GLA_QK), GLA_LR ** -0.5)
    inp['l0_gla_ba'] = normal((GLA_QK,), 0.1)
    inp['l0_gla_norm'] = gain(GLA_V)
    inp['l0_rwkv_mu'] = jax.random.uniform(next(keys), (RWKV_COLS,), jnp.float32)
    inp['l0_rwkv_w0'] = normal((RWKV_D,), 0.5)
    inp['l0_rwkv_w2'] = normal((RWKV_W_LR, RWKV_D), 0.5 * RWKV_W_LR ** -0.5)
    inp['l0_rwkv_a0'] = normal((RWKV_D,), 0.5)
    inp['l0_rwkv_a2'] = normal((RWKV_A_LR, RWKV_D), 0.5 * RWKV_A_LR ** -0.5)
    inp['l0_rwkv_g2'] = normal((RWKV_G_LR, RWKV_D), RWKV_G_LR ** -0.5)
    inp['l0_rwkv_kk'] = 0.85 + normal((RWKV_D,), 0.02)
    inp['l0_rwkv_ka'] = 1.0 + normal((RWKV_D,), 0.02)
    inp['l0_rwkv_rk'] = normal((RWKV_D,), 0.1)
    inp['l0_rwkv_lnw'] = gain(RWKV_D)
    inp['l0_rwkv_lnb'] = normal((RWKV_D,), 0.02)
    inp['l0_w_out'] = normal((EVEN_OUT, d), EVEN_OUT ** -0.5 * DN_BETA)
    inp['l0_ln1_g'] = gain(d)
    inp['l0_ln1_b'] = normal((d,), 0.02)
    inp['l0_xq'] = normal((d, d), d ** -0.5)
    inp['l0_xk'] = normal((d, d), d ** -0.5)
    inp['l0_xv'] = normal((d, d), d ** -0.5)
    inp['l0_xo'] = normal((d, d), d ** -0.5 * DN_BETA)
    inp['l0_ln2_g'] = gain(d)
    inp['l0_ln2_b'] = normal((d,), 0.02)
    inp['l0_ffn_wg'] = normal((d, D_FF), d ** -0.5)
    inp['l0_ffn_wu'] = normal((d, D_FF), d ** -0.5)
    inp['l0_ffn_wd'] = normal((D_FF, d), D_FF ** -0.5 * DN_BETA)
    inp['l0_ln3_g'] = gain(d)
    inp['l0_ln3_b'] = normal((d,), 0.02)
    inp['l1_w_in'] = normal((d, ODD_IN), d ** -0.5)
    inp['l1_conv_w'] = normal((SSD_CONV, SSD_CONV_CH), SSD_CONV ** -0.5)
    inp['l1_conv_b'] = normal((SSD_CONV_CH,), 0.02)
    dt0 = jnp.exp(jax.random.uniform(next(keys), (SSD_HEADS,), jnp.float32,
                                     minval=math.log(1e-3), maxval=math.log(1e-1)))
    inp['l1_dt_bias'] = dt0 + jnp.log(-jnp.expm1(-dt0))
    inp['l1_a_log'] = jnp.log(jax.random.uniform(next(keys), (SSD_HEADS,), jnp.float32, minval=1.0, maxval=16.0))
    inp['l1_d_skip'] = 1.0 + normal((SSD_HEADS,), 0.1)
    inp['l1_ssd_norm'] = gain(SSD_INNER)
    inp['l1_w_out'] = normal((ODD_OUT, d), ODD_OUT ** -0.5 * DN_BETA)
    inp['l1_ln1_g'] = gain(d)
    inp['l1_ln1_b'] = normal((d,), 0.02)
    inp['l1_xq'] = normal((d, d), d ** -0.5)
    inp['l1_xk'] = normal((d, d), d ** -0.5)
    inp['l1_xv'] = normal((d, d), d ** -0.5)
    inp['l1_xo'] = normal((d, d), d ** -0.5 * DN_BETA)
    inp['l1_ln2_g'] = gain(d)
    inp['l1_ln2_b'] = normal((d,), 0.02)
    inp['l1_router'] = normal((d, N_EXPERTS), d ** -0.5)
    inp['l1_router_b'] = normal((N_EXPERTS,), 0.01)
    inp['l1_exp_wg'] = normal((N_EXPERTS, d, EXPERT_FF), d ** -0.5)
    inp['l1_exp_wu'] = normal((N_EXPERTS, d, EXPERT_FF), d ** -0.5)
    inp['l1_exp_wd'] = normal((N_EXPERTS, EXPERT_FF, d), EXPERT_FF ** -0.5 * DN_BETA)
    inp['l1_ln3_g'] = gain(d)
    inp['l1_ln3_b'] = normal((d,), 0.02)
    return inp


def reference(x, mem,
              l0_w_in, l0_gla_wa2, l0_gla_ba, l0_gla_norm,
              l0_rwkv_mu, l0_rwkv_w0, l0_rwkv_w2, l0_rwkv_a0, l0_rwkv_a2, l0_rwkv_g2,
              l0_rwkv_kk, l0_rwkv_ka, l0_rwkv_rk, l0_rwkv_lnw, l0_rwkv_lnb,
              l0_w_out, l0_ln1_g, l0_ln1_b,
              l0_xq, l0_xk, l0_xv, l0_xo, l0_ln2_g, l0_ln2_b,
              l0_ffn_wg, l0_ffn_wu, l0_ffn_wd, l0_ln3_g, l0_ln3_b,
              l1_w_in, l1_conv_w, l1_conv_b, l1_dt_bias, l1_a_log, l1_d_skip, l1_ssd_norm,
              l1_w_out, l1_ln1_g, l1_ln1_b,
              l1_xq, l1_xk, l1_xv, l1_xo, l1_ln2_g, l1_ln2_b,
              l1_router, l1_router_b, l1_exp_wg, l1_exp_wu, l1_exp_wd, l1_ln3_g, l1_ln3_b):
    token_mixers = (
        lambda h: gla_rwkv_mixer(h, l0_w_in, l0_gla_wa2, l0_gla_ba, l0_gla_norm, l0_rwkv_mu, l0_rwkv_w0,
                                 l0_rwkv_w2, l0_rwkv_a0, l0_rwkv_a2, l0_rwkv_g2, l0_rwkv_kk, l0_rwkv_ka,
                                 l0_rwkv_rk, l0_rwkv_lnw, l0_rwkv_lnb, l0_w_out),
        lambda h: ssd_moba_mixer(h, l1_w_in, l1_conv_w, l1_conv_b, l1_dt_bias, l1_a_log, l1_d_skip,
                                 l1_ssd_norm, l1_w_out),
    )
    cross = ((l0_xq, l0_xk, l0_xv, l0_xo), (l1_xq, l1_xk, l1_xv, l1_xo))
    channel_mixers = (
        lambda h: swiglu(h, l0_ffn_wg, l0_ffn_wu, l0_ffn_wd),
        lambda h: moe_swiglu(h, l1_router, l1_router_b, l1_exp_wg, l1_exp_wu, l1_exp_wd),
    )
    norms = (((l0_ln1_g, l0_ln1_b), (l0_ln2_g, l0_ln2_b), (l0_ln3_g, l0_ln3_b)),
             ((l1_ln1_g, l1_ln1_b), (l1_ln2_g, l1_ln2_b), (l1_ln3_g, l1_ln3_b)))
    for layer in range(DEPTH):
        x = post_norm(x, token_mixers[layer](x), *norms[layer][0])
        x = post_norm(x, memory_cross_attention(x, mem, *cross[layer]), *norms[layer][1])
        x = post_norm(x, channel_mixers[layer](x), *norms[layer][2])
    return x
```

```python
import functools
import math

import jax
import jax.numpy as jnp
from jax import lax
from jax.experimental import pallas as pl
from jax.experimental.pallas import tpu as pltpu

BF16 = jnp.bfloat16
F32 = jnp.float32

D_MODEL = 1024
LN_EPS = 1e-5
DEPTH = 2
DN_ALPHA = (2 * DEPTH) ** 0.25

GLA_HEADS, GLA_DK, GLA_DV, GLA_CHUNK = 4, 64, 128, 64
GLA_QK, GLA_V, GLA_LR, GLA_TAU = 256, 512, 16, 16.0
GLA_COLS = 2 * GLA_QK + 2 * GLA_V + 128

RWKV_HEADS, RWKV_HD, RWKV_D, RWKV_CHUNK = 8, 64, 512, 64
RWKV_COLS = 1792
RWKV_DECAY_SCALE = math.exp(-0.5)
RWKV_GN_EPS = 64e-5

VMEM_LIMIT = 56 * 1024 * 1024


def _cparams(sem):
    return pltpu.CompilerParams(dimension_semantics=sem, vmem_limit_bytes=VMEM_LIMIT)


def _dot(a, b):
    return jnp.dot(a.astype(BF16), b.astype(BF16), preferred_element_type=F32)


def _dot_nt(a, b):
    return lax.dot_general(a.astype(BF16), b.astype(BF16), (((1,), (1,)), ((), ())), preferred_element_type=F32)


def _dot_tn(a, b):
    return lax.dot_general(a.astype(BF16), b.astype(BF16), (((0,), (0,)), ((), ())), preferred_element_type=F32)


def _split3(x):
    hi = x.astype(BF16)
    r1 = x - hi.astype(F32)
    mid = r1.astype(BF16)
    lo = (r1 - mid.astype(F32)).astype(BF16)
    return hi, mid, lo


def _dot_exact_lhs(m, x):
    mb = m.astype(BF16)
    hi, mid, lo = _split3(x)
    return (jnp.dot(mb, hi, preferred_element_type=F32) + jnp.dot(mb, mid, preferred_element_type=F32)
            + jnp.dot(mb, lo, preferred_element_type=F32))


def _dot_exact_rhs(x, m):
    mb = m.astype(BF16)
    hi, mid, lo = _split3(x)
    return (jnp.dot(hi, mb, preferred_element_type=F32) + jnp.dot(mid, mb, preferred_element_type=F32)
            + jnp.dot(lo, mb, preferred_element_type=F32))


def _sigmoid(x):
    return 1.0 / (1.0 + jnp.exp(-x))


def _silu(x):
    return x * _sigmoid(x)


def _iota2(shape, axis):
    return lax.broadcasted_iota(jnp.int32, shape, axis)


def _chunk_tril(n, chunk):
    r = _iota2((n, n), 0)
    c = _iota2((n, n), 1)
    return jnp.where((c <= r) & ((r // chunk) == (c // chunk)), 1.0, 0.0)


def _head_block(n, width, value):
    r = _iota2((n, n), 0)
    c = _iota2((n, n), 1)
    return jnp.where((r // width) == (c // width), value, 0.0)


def _mm_kernel(x_ref, w_ref, o_ref):
    o_ref[...] = _dot(x_ref[...], w_ref[...]).astype(o_ref.dtype)


def _matmul(x, w, out_dtype, tm, tn):
    S, K = x.shape
    N = w.shape[1]
    return pl.pallas_call(
        _mm_kernel,
        grid=(S // tm, N // tn),
        in_specs=[pl.BlockSpec((tm, K), lambda i, j: (i, 0)),
                  pl.BlockSpec((K, tn), lambda i, j: (0, j))],
        out_specs=pl.BlockSpec((tm, tn), lambda i, j: (i, j)),
        out_shape=jax.ShapeDtypeStruct((S, N), out_dtype),
        compiler_params=_cparams(("parallel", "arbitrary")),
        name="matmul",
    )(x, w)


def _mm_t_kernel(wt_ref, x_ref, o_ref):
    o_ref[...] = _dot_nt(wt_ref[...], x_ref[...]).astype(o_ref.dtype)


def _matmul_t(wt, x, out_dtype, tm):
    S, K = x.shape
    N = wt.shape[0]
    return pl.pallas_call(
        _mm_t_kernel,
        grid=(S // tm,),
        in_specs=[pl.BlockSpec((N, K), lambda i: (0, 0)),
                  pl.BlockSpec((tm, K), lambda i: (i, 0))],
        out_specs=pl.BlockSpec((N, tm), lambda i: (0, i)),
        out_shape=jax.ShapeDtypeStruct((N, S), out_dtype),
        compiler_params=_cparams(("parallel",)),
        name="matmul_t",
    )(wt, x)


def _layer_norm_rows(y, g, b):
    mu = jnp.mean(y, -1, keepdims=True)
    d = y - mu
    var = jnp.mean(d * d, -1, keepdims=True)
    return d * lax.rsqrt(var + LN_EPS) * g + b


def _mm_ln_kernel(transposed, *refs):
    n_in = len(transposed)
    a_refs = refs[:n_in]
    w_refs = refs[n_in:2 * n_in]
    x_ref, g_ref, b_ref, o_ref = refs[2 * n_in:]
    acc = DN_ALPHA * x_ref[...]
    for a_ref, w_ref, tr in zip(a_refs, w_refs, transposed):
        acc = acc + (_dot_tn if tr else _dot)(a_ref[...], w_ref[...])
    o_ref[...] = _layer_norm_rows(acc, g_ref[...], b_ref[...])


def _matmul_ln(a_list, w_list, x, g, b, tm, transposed=None):
    S, D = x.shape
    transposed = tuple(transposed or (False,) * len(a_list))
    in_specs = ([pl.BlockSpec((a.shape[0], tm), lambda i: (0, i)) if tr else
                 pl.BlockSpec((tm, a.shape[1]), lambda i: (i, 0)) for a, tr in zip(a_list, transposed)]
                + [pl.BlockSpec(w.shape, lambda i: (0, 0)) for w in w_list]
                + [pl.BlockSpec((tm, D), lambda i: (i, 0)),
                   pl.BlockSpec((1, D), lambda i: (0, 0)),
                   pl.BlockSpec((1, D), lambda i: (0, 0))])
    return pl.pallas_call(
        functools.partial(_mm_ln_kernel, transposed),
        grid=(S // tm,),
        in_specs=in_specs,
        out_specs=pl.BlockSpec((tm, D), lambda i: (i, 0)),
        out_shape=jax.ShapeDtypeStruct((S, D), F32),
        compiler_params=_cparams(("parallel",)),
        name="matmul_ln",
    )(*a_list, *w_list, x, g.reshape(1, D), b.reshape(1, D))


def _gla_kernel(p_ref, wa2_ref, ba_ref, ng_ref, o_ref, st_ref, o_scr):
    C, H, dk, dv = GLA_CHUNK, GLA_HEADS, GLA_DK, GLA_DV
    tb = p_ref.shape[0]

    @pl.when(pl.program_id(0) == 0)
    def _():
        st_ref[...] = jnp.zeros_like(st_ref)

    z = _dot(p_ref[:, 2 * GLA_QK + 2 * GLA_V:], wa2_ref[...]) + ba_ref[...]
    log_a = -(jnp.maximum(-z, 0.0) + jnp.log(1.0 + jnp.exp(-jnp.abs(z)))) / GLA_TAU
    b = _dot_exact_lhs(_chunk_tril(tb, C), log_a)
    causal = _iota2((C, C), 1) <= _iota2((C, C), 0)

    for c in range(tb // C):
        rows = slice(c * C, (c + 1) * C)
        b_c = b[rows]
        b_last = b_c[C - 1:C]
        q_dec = p_ref[rows, 0:GLA_QK] * (dk ** -0.5) * jnp.exp(b_c)
        k_c = p_ref[rows, GLA_QK:2 * GLA_QK]
        k_dec = k_c * jnp.exp(-b_c)
        k_end = k_c * jnp.exp(b_last - b_c)
        decay = jnp.exp(b_last)
        for h in range(H):
            ks = slice(h * dk, (h + 1) * dk)
            vs = slice(h * dv, (h + 1) * dv)
            v_h = p_ref[rows, 2 * GLA_QK + h * dv:2 * GLA_QK + (h + 1) * dv]
            q_h = q_dec[:, ks]
            attn = jnp.where(causal, _dot_nt(q_h, k_dec[:, ks]), 0.0)
            st_h = st_ref[:, ks]
            o_scr[rows, vs] = _dot(attn, v_h) + _dot_nt(q_h, st_h)
            st_ref[:, ks] = st_h * decay[:, ks] + _dot_tn(v_h, k_end[:, ks])

    for h in range(H):
        vs = slice(h * dv, (h + 1) * dv)
        o_h = o_scr[:, vs]
        g_h = p_ref[:, 2 * GLA_QK + GLA_V + h * dv:2 * GLA_QK + GLA_V + (h + 1) * dv]
        o_h = o_h * lax.rsqrt(jnp.mean(o_h * o_h, -1, keepdims=True) + 1e-5) * ng_ref[:, vs]
        o_ref[:, vs] = (o_h * _silu(g_h)).astype(o_ref.dtype)


def _gla_call(p_gla, wa2p, ba, norm_g, tb=256):
    S = p_gla.shape[0]
    return pl.pallas_call(
        _gla_kernel,
        grid=(S // tb,),
        in_specs=[pl.BlockSpec((tb, GLA_COLS), lambda i: (i, 0)),
                  pl.BlockSpec((128, GLA_QK), lambda i: (0, 0)),
                  pl.BlockSpec((1, GLA_QK), lambda i: (0, 0)),
                  pl.BlockSpec((1, GLA_V), lambda i: (0, 0))],
        out_specs=pl.BlockSpec((tb, GLA_V), lambda i: (i, 0)),
        out_shape=jax.ShapeDtypeStruct((S, GLA_V), BF16),
        scratch_shapes=[pltpu.VMEM((GLA_DV, GLA_QK), F32), pltpu.VMEM((tb, GLA_V), F32)],
        compiler_params=_cparams(("arbitrary",)),
        name="gla",
    )(p_gla, wa2p, ba.reshape(1, GLA_QK), norm_g.reshape(1, GLA_V))


def _rwkv_kernel(p_ref, mu_ref, w0_ref, w2_ref, a0_ref, a2_ref, g2_ref, kk_ref, ka_ref, rk_ref, lnw_ref, lnb_ref,
                 o_ref, prev_ref, h_ref, o_scr):
    C, H, N, D = RWKV_CHUNK, RWKV_HEADS, RWKV_HD, RWKV_D
    tb = p_ref.shape[0]
    first = pl.program_id(0) == 0

    @pl.when(first)
    def _():
        prev_ref[...] = jnp.zeros_like(prev_ref)
        h_ref[...] = jnp.zeros_like(h_ref)

    p = p_ref[...]
    shifted = jnp.where(_iota2(p.shape, 0) == 0, prev_ref[...], pltpu.roll(p, 1, 0))
    prev_ref[...] = p[tb - 1:tb]
    p = p + mu_ref[...] * (shifted - p)
    r = p[:, 0:D]
    k = p[:, D:2 * D]
    v = p[:, 2 * D:3 * D]
    xw = p[:, 3 * D:3 * D + 64]
    xa = p[:, 3 * D + 64:3 * D + 128]
    xg = p[:, 3 * D + 128:3 * D + 256]
    lw = -RWKV_DECAY_SCALE * _sigmoid(w0_ref[...] + _dot(jnp.tanh(xw), w2_ref[...]))
    a = _sigmoid(a0_ref[...] + _dot(xa, a2_ref[...]))
    g = _dot(_sigmoid(xg), g2_ref[...])
    head_ones = _head_block(D, N, 1.0)
    kk = k * kk_ref[...]
    kk = kk * lax.rsqrt(jnp.maximum(_dot_exact_rhs(kk * kk, head_ones), 1e-24))
    k = k * (1.0 + (a - 1.0) * ka_ref[...])
    pv = -kk * a
    cw = _dot_exact_lhs(_chunk_tril(tb, C), lw)
    cwx = cw - lw

    gi = _iota2((2 * C, 2 * C), 0)
    gj = _iota2((2 * C, 2 * C), 1) % C
    gram_mask = ((gi < C) & (gj < gi)) | ((gi >= C) & (gj <= gi - C))
    eye = _iota2((C, C), 0) == _iota2((C, C), 1)
    eye_f = jnp.where(eye, 1.0, 0.0)

    for c in range(tb // C):
        rows = slice(c * C, (c + 1) * C)
        cw_c = cw[rows]
        cw_end = cw_c[C - 1:C]
        e_pos = jnp.exp(cw_c)
        e_neg = jnp.exp(-cw_c)
        e_end = jnp.exp(cw_end - cw_c)
        r_t = r[rows] * e_pos
        b_t = kk[rows] * jnp.exp(cwx[rows])
        p_t = pv[rows] * e_neg
        k_t = k[rows] * e_neg
        p_e = pv[rows] * e_end
        k_e = k[rows] * e_end
        g_end = jnp.exp(cw_end)
        v_c = v[rows]
        for h in range(H):
            hs = slice(h * N, (h + 1) * N)
            gram = _dot_nt(jnp.concatenate([b_t[:, hs], r_t[:, hs]], axis=0),
                           jnp.concatenate([p_t[:, hs], k_t[:, hs]], axis=0))
            gram = jnp.where(gram_mask, gram, 0.0)
            l_p = gram[0:C, 0:C]
            l_k = gram[0:C, C:2 * C]
            m_pk = gram[C:2 * C, :]
            x = _dot(l_p, l_p)
            t = eye_f + l_p
            for _ in range(4):
                tx = _dot(jnp.concatenate([t, x], axis=0), x)
                t = t + tx[0:C]
                x = tx[C:2 * C]
            t = t + _dot(t, x)
            v_h = v_c[:, hs]
            wu = _dot(t, jnp.concatenate([b_t[:, hs], _dot(l_k, v_h)], axis=1))
            rhs = jnp.concatenate([wu, jnp.concatenate([jnp.zeros((C, N), F32), v_h], axis=1)], axis=0)
            az = _dot_tn(jnp.concatenate([p_e[:, hs], k_e[:, hs]], axis=0), rhs)
            qo = _dot(m_pk, rhs)
            a_mat = az[:, 0:N] + jnp.where(eye, g_end[:, hs], 0.0)
            q_mat = qo[:, 0:N] + r_t[:, hs]
            oh = _dot(jnp.concatenate([q_mat, a_mat], axis=0), h_ref[h])
            o_scr[rows, hs] = oh[0:C] + qo[:, N:2 * N]
            h_ref[h] = oh[C:C + N] + az[:, N:2 * N]

    o = o_scr[...]
    head_mean = _head_block(D, N, 1.0 / N)
    mean = _dot_exact_rhs(o, head_mean)
    d = o - mean
    var = _dot_exact_rhs(d * d, head_mean)
    o = d * lax.rsqrt(var + RWKV_GN_EPS) * lnw_ref[...] + lnb_ref[...]
    bonus = _dot_exact_rhs(r * k * rk_ref[...], head_ones) * v
    o_ref[...] = ((o + bonus) * g).astype(o_ref.dtype)


def _rwkv_call(p_rwkv, mu, w0, w2, a0, a2, g2, k_k, k_a, r_k, ln_w, ln_b, tb=128):
    S = p_rwkv.shape[0]
    D = RWKV_D
    row = lambda t: t.reshape(1, -1).astype(F32)
    full = lambda shape: pl.BlockSpec(shape, lambda i: tuple(0 for _ in shape))
    return pl.pallas_call(
        _rwkv_kernel,
        grid=(S // tb,),
        in_specs=[pl.BlockSpec((tb, RWKV_COLS), lambda i: (i, 0)),
                  full((1, RWKV_COLS)), full((1, D)), full((64, D)), full((1, D)), full((64, D)), full((128, D)),
                  full((1, D)), full((1, D)), full((1, D)), full((1, D)), full((1, D))],
        out_specs=pl.BlockSpec((tb, D), lambda i: (i, 0)),
        out_shape=jax.ShapeDtypeStruct((S, D), BF16),
        scratch_shapes=[pltpu.VMEM((1, RWKV_COLS), F32),
                        pltpu.VMEM((RWKV_HEADS, RWKV_HD, RWKV_HD), F32),
                        pltpu.VMEM((tb, D), F32)],
        compiler_params=_cparams(("arbitrary",)),
        name="rwkv7",
    )(p_rwkv, row(mu), row(w0), w2.astype(BF16), row(a0), a2.astype(BF16), g2.astype(BF16),
      row(k_k), row(k_a), row(r_k), row(ln_w), row(ln_b))


XATTN_HEADS, XATTN_HD = 4, 256


def _xattn_kernel(x_ref, wq_ref, k_ref, v_ref, wo_ref, g_ref, b_ref, o_ref):
    x = x_ref[...]
    q = _dot(x, wq_ref[...])
    outs = []
    for h in range(XATTN_HEADS):
        hs = slice(h * XATTN_HD, (h + 1) * XATTN_HD)
        s = _dot_nt(q[:, hs], k_ref[:, hs])
        e = jnp.exp(s - jnp.max(s, -1, keepdims=True))
        p = e / jnp.sum(e, -1, keepdims=True)
        outs.append(_dot(p, v_ref[:, hs]))
    o = jnp.concatenate(outs, axis=1)
    y = DN_ALPHA * x + _dot(o, wo_ref[...])
    o_ref[...] = _layer_norm_rows(y, g_ref[...], b_ref[...])


def _xattn_call(x, wq_scaled, k_mem, v_mem, wo, g, b, tm):
    S, D = x.shape
    M = k_mem.shape[0]
    const = lambda shape: pl.BlockSpec(shape, lambda i: (0, 0))
    return pl.pallas_call(
        _xattn_kernel,
        grid=(S // tm,),
        in_specs=[pl.BlockSpec((tm, D), lambda i: (i, 0)), const((D, D)), const((M, D)), const((M, D)),
                  const((D, D)), const((1, D)), const((1, D))],
        out_specs=pl.BlockSpec((tm, D), lambda i: (i, 0)),
        out_shape=jax.ShapeDtypeStruct((S, D), F32),
        compiler_params=_cparams(("parallel",)),
        name="xattn",
    )(x, wq_scaled, k_mem, v_mem, wo, g.reshape(1, D), b.reshape(1, D))


def _ffn_kernel(x_ref, wg_ref, wu_ref, wd_ref, g_ref, b_ref, o_ref, acc_ref):
    j = pl.program_id(1)
    x = x_ref[...]
    xb = x.astype(BF16)
    part = _dot(_silu(_dot(xb, wg_ref[...])) * _dot(xb, wu_ref[...]), wd_ref[...])

    @pl.when(j == 0)
    def _():
        acc_ref[...] = DN_ALPHA * x + part

    @pl.when(j > 0)
    def _():
        acc_ref[...] += part

    @pl.when(j == pl.num_programs(1) - 1)
    def _():
        o_ref[...] = _layer_norm_rows(acc_ref[...], g_ref[...], b_ref[...])


def _ffn_call(x, wg, wu, wd, g, b, tm, tf):
    S, D = x.shape
    F = wg.shape[1]
    return pl.pallas_call(
        _ffn_kernel,
        grid=(S // tm, F // tf),
        in_specs=[pl.BlockSpec((tm, D), lambda i, j: (i, 0)),
                  pl.BlockSpec((D, tf), lambda i, j: (0, j)),
                  pl.BlockSpec((D, tf), lambda i, j: (0, j)),
                  pl.BlockSpec((tf, D), lambda i, j: (j, 0)),
                  pl.BlockSpec((1, D), lambda i, j: (0, 0)),
                  pl.BlockSpec((1, D), lambda i, j: (0, 0))],
        out_specs=pl.BlockSpec((tm, D), lambda i, j: (i, 0)),
        out_shape=jax.ShapeDtypeStruct((S, D), F32),
        scratch_shapes=[pltpu.VMEM((tm, D), F32)],
        compiler_params=_cparams(("parallel", "arbitrary")),
        name="ffn",
    )(x, wg, wu, wd, g.reshape(1, D), b.reshape(1, D))


SSD_HD, SSD_HEADS, SSD_INNER, SSD_GROUPS, SSD_STATE = 64, 16, 1024, 2, 128
SSD_BC, SSD_CONV, SSD_CONV_CH, SSD_CHUNK = 256, 4, 1536, 128
SSD_COLS = SSD_INNER + SSD_CONV_CH
SSD_GW = SSD_INNER // SSD_GROUPS


def _softplus(x):
    return jnp.maximum(x, 0.0) + jnp.log(1.0 + jnp.exp(-jnp.abs(x)))


def _ssd_kernel(p_ref, dt_ref, dtt_ref, cw_ref, cb_ref, dtb_ref, dtbc_ref, a_ref, ac_ref, dsk_ref, ng_ref,
                o_ref, prev_ref, st_ref, y_scr):
    L, G, NS, HD = SSD_CHUNK, SSD_GROUPS, SSD_STATE, SSD_HD
    HG = SSD_HEADS // G

    @pl.when(pl.program_id(0) == 0)
    def _():
        prev_ref[...] = jnp.zeros_like(prev_ref)
        st_ref[...] = jnp.zeros_like(st_ref)

    cur = p_ref[:, SSD_INNER:]
    prev = prev_ref[...]
    row = _iota2(cur.shape, 0)
    conv = cur * cw_ref[SSD_CONV - 1:SSD_CONV, :] + cb_ref[...]
    for kk in range(1, SSD_CONV):
        shifted = jnp.where(row < kk, pltpu.roll(prev, kk, 0), pltpu.roll(cur, kk, 0))
        conv = conv + shifted * cw_ref[SSD_CONV - 1 - kk:SSD_CONV - kk, :]
    prev_ref[...] = cur
    xbc = _silu(conv)
    xs = xbc[:, :SSD_INNER]

    dt = _softplus(dt_ref[...] + dtb_ref[...])
    a_col = dt * a_ref[...]
    a_row = _softplus(dtt_ref[...] + dtbc_ref[...]) * ac_ref[...]
    li = _iota2((L, L), 0)
    lj = _iota2((L, L), 1)
    cs = _dot_exact_lhs(jnp.where(lj <= li, 1.0, 0.0), a_col)
    cs_row = _dot_exact_rhs(a_row, jnp.where(li <= lj, 1.0, 0.0))
    expand = jnp.where(_iota2((128, SSD_INNER), 1) // HD == _iota2((128, SSD_INNER), 0), 1.0, 0.0)
    dt_x = _dot_exact_rhs(dt, expand)
    cs_x = _dot_exact_rhs(cs, expand)
    cs_end = cs_x[L - 1:L]
    xd = xs * dt_x
    xd_dec = xd * jnp.exp(cs_end - cs_x)
    out_dec = jnp.exp(cs_x)
    chunk_dec = jnp.exp(cs_end)
    tril = lj <= li

    for g in range(G):
        gs = slice(g * SSD_GW, (g + 1) * SSD_GW)
        b_g = xbc[:, SSD_INNER + g * NS:SSD_INNER + (g + 1) * NS]
        c_g = xbc[:, SSD_INNER + SSD_BC + g * NS:SSD_INNER + SSD_BC + (g + 1) * NS]
        cb = _dot_nt(c_g, b_g)
        for j in range(HG):
            h = g * HG + j
            hs = slice(h * HD, (h + 1) * HD)
            seg = jnp.where(tril, jnp.exp(cs[:, h:h + 1] - cs_row[h:h + 1, :]), 0.0)
            y_scr[:, hs] = _dot(cb * seg, xd[:, hs])
        st = st_ref[g]
        y_off = _dot(c_g, st) * out_dec[:, gs]
        st_ref[g] = st * chunk_dec[:, gs] + _dot_tn(b_g, xd_dec[:, gs])
        y_scr[:, gs] = y_scr[:, gs] + y_off

    y = (y_scr[...] + dsk_ref[...] * xs) * _silu(p_ref[:, :SSD_INNER])
    for g in range(G):
        gs = slice(g * SSD_GW, (g + 1) * SSD_GW)
        y_g = y[:, gs]
        o_ref[:, gs] = (y_g * lax.rsqrt(jnp.mean(y_g * y_g, -1, keepdims=True) + 1e-5) * ng_ref[:, gs]).astype(o_ref.dtype)


def _ssd_call(p_ssd, p_dt, dtt, conv_w, conv_b, dt_bias, a_log, d_skip, norm_g):
    S = p_ssd.shape[0]
    L = SSD_CHUNK
    a_neg = -jnp.exp(a_log.astype(F32))
    pad = lambda t: jnp.pad(t.astype(F32), (0, 128 - SSD_HEADS)).reshape(1, 128)
    col = lambda t: t.astype(F32).reshape(SSD_HEADS, 1)
    const = lambda shape: pl.BlockSpec(shape, lambda i: (0, 0))
    return pl.pallas_call(
        _ssd_kernel,
        grid=(S // L,),
        in_specs=[pl.BlockSpec((L, SSD_COLS), lambda i: (i, 0)),
                  pl.BlockSpec((L, 128), lambda i: (i, 0)),
                  pl.BlockSpec((SSD_HEADS, L), lambda i: (0, i)),
                  const((SSD_CONV, SSD_CONV_CH)), const((1, SSD_CONV_CH)),
                  const((1, 128)), const((SSD_HEADS, 1)), const((1, 128)), const((SSD_HEADS, 1)),
                  const((1, SSD_INNER)), const((1, SSD_INNER))],
        out_specs=pl.BlockSpec((L, SSD_INNER), lambda i: (i, 0)),
        out_shape=jax.ShapeDtypeStruct((S, SSD_INNER), BF16),
        scratch_shapes=[pltpu.VMEM((L, SSD_CONV_CH), F32),
                        pltpu.VMEM((SSD_GROUPS, SSD_STATE, SSD_GW), F32),
                        pltpu.VMEM((L, SSD_INNER), F32)],
        compiler_params=_cparams(("arbitrary",)),
        name="ssd",
    )(p_ssd, p_dt, dtt, conv_w.astype(F32), conv_b.reshape(1, -1).astype(F32),
      pad(dt_bias), col(dt_bias), pad(a_neg), col(a_neg),
      jnp.repeat(d_skip.astype(F32), SSD_HD).reshape(1, -1), norm_g.reshape(1, -1).astype(F32))


MOBA_HD, MOBA_HEADS, MOBA_D, MOBA_BLOCK, MOBA_TOPK = 64, 8, 512, 256, 3
NEG_BIG = -1e30


def _kproj_kernel(x_ref, w_ref, k_ref, kmean_ref):
    k = _dot(x_ref[...], w_ref[...])
    kmean_ref[0] = jnp.mean(k, 0, keepdims=True)
    for h in range(MOBA_HEADS):
        k_ref[h, 0] = k[:, h * MOBA_HD:(h + 1) * MOBA_HD].astype(k_ref.dtype)


def _kproj_call(x, wk):
    S, D = x.shape
    nb = S // MOBA_BLOCK
    return pl.pallas_call(
        _kproj_kernel,
        grid=(nb,),
        in_specs=[pl.BlockSpec((MOBA_BLOCK, D), lambda i: (i, 0)), pl.BlockSpec((D, MOBA_D), lambda i: (0, 0))],
        out_specs=[pl.BlockSpec((MOBA_HEADS, 1, MOBA_BLOCK, MOBA_HD), lambda i: (0, i, 0, 0)),
                   pl.BlockSpec((1, 1, MOBA_D), lambda i: (i, 0, 0))],
        out_shape=[jax.ShapeDtypeStruct((MOBA_HEADS, nb, MOBA_BLOCK, MOBA_HD), BF16),
                   jax.ShapeDtypeStruct((nb, 1, MOBA_D), F32)],
        compiler_params=_cparams(("parallel",)),
        name="moba_kproj",
    )(x, wk)


def _vproj_kernel(wt_ref, x_ref, v_ref):
    vt = _dot_nt(wt_ref[...], x_ref[...])
    for h in range(MOBA_HEADS):
        v_ref[h, 0] = vt[h * MOBA_HD:(h + 1) * MOBA_HD].astype(v_ref.dtype)


def _vproj_call(wvt, x):
    S, D = x.shape
    nb = S // MOBA_BLOCK
    return pl.pallas_call(
        _vproj_kernel,
        grid=(nb,),
        in_specs=[pl.BlockSpec((MOBA_D, D), lambda i: (0, 0)), pl.BlockSpec((MOBA_BLOCK, D), lambda i: (i, 0))],
        out_specs=pl.BlockSpec((MOBA_HEADS, 1, MOBA_HD, MOBA_BLOCK), lambda i: (0, i, 0, 0)),
        out_shape=jax.ShapeDtypeStruct((MOBA_HEADS, nb, MOBA_HD, MOBA_BLOCK), BF16),
        compiler_params=_cparams(("parallel",)),
        name="moba_vproj",
    )(wvt, x)


def _moba_kernel(qt_ref, k_ref, vt_ref, kmean_ref, o_ref, sel_ref):
    BS = MOBA_BLOCK
    h = pl.program_id(0)
    i = pl.program_id(1)
    nb = k_ref.shape[0]
    qt = qt_ref[...]
    slope = jnp.exp2(jnp.zeros((1, BS), F32) - (h + 1).astype(F32) * (8.0 / MOBA_HEADS))

    gate = _dot(kmean_ref[...], qt)
    blk = _iota2((nb, BS), 0).astype(F32)
    cand = blk < i.astype(F32)
    sel = jnp.zeros((nb, BS), F32)
    for _ in range(MOBA_TOPK):
        best = jnp.max(jnp.where(cand, gate, -jnp.inf), 0, keepdims=True)
        idx = jnp.min(jnp.where(cand & (gate == best), blk, float(nb)), 0, keepdims=True)
        pick = blk == idx
        sel = jnp.where(pick, 1.0, sel)
        cand = cand & jnp.logical_not(pick)
    sel_ref[...] = sel

    rel = (_iota2((BS, BS), 1) - _iota2((BS, BS), 0)).astype(F32)
    s = _dot(k_ref[i], qt) - slope * rel
    s = jnp.where(rel >= 0.0, s, NEG_BIG)
    m = jnp.max(s, 0, keepdims=True)
    p = jnp.exp(s - m)
    l = jnp.sum(p, 0, keepdims=True)
    acc = _dot(vt_ref[i], p)

    def body(j, carry):
        m, l, acc = carry
        on = sel_ref[pl.ds(j, 1), :] > 0.0
        s = _dot(k_ref[j], qt) - slope * (rel + ((i - j) * BS).astype(F32))
        m_new = jnp.maximum(m, jnp.max(jnp.where(on, s, NEG_BIG), 0, keepdims=True))
        p = jnp.where(on, jnp.exp(s - m_new), 0.0)
        alpha = jnp.exp(m - m_new)
        return m_new, alpha * l + jnp.sum(p, 0, keepdims=True), alpha * acc + _dot(vt_ref[j], p)

    m, l, acc = lax.fori_loop(0, i, body, (m, l, acc))
    o_ref[...] = (acc / l).astype(o_ref.dtype)


def _moba_call(qt, k4, vt4, kmean):
    S = qt.shape[1]
    nb = S // MOBA_BLOCK
    return pl.pallas_call(
        _moba_kernel,
        grid=(MOBA_HEADS, nb),
        in_specs=[pl.BlockSpec((MOBA_HD, MOBA_BLOCK), lambda h, i: (h, i)),
                  pl.BlockSpec((None, nb, MOBA_BLOCK, MOBA_HD), lambda h, i: (h, 0, 0, 0)),
                  pl.BlockSpec((None, nb, MOBA_HD, MOBA_BLOCK), lambda h, i: (h, 0, 0, 0)),
                  pl.BlockSpec((None, nb, MOBA_HD), lambda h, i: (h, 0, 0))],
        out_specs=pl.BlockSpec((MOBA_HD, MOBA_BLOCK), lambda h, i: (h, i)),
        out_shape=jax.ShapeDtypeStruct((MOBA_D, S), BF16),
        scratch_shapes=[pltpu.VMEM((nb, MOBA_BLOCK), F32)],
        compiler_params=_cparams(("parallel", "arbitrary")),
        name="moba",
    )(qt, k4, vt4, kmean)


N_EXPERTS, TOP_K, EXPERT_FF = 8, 2, 2816
MOE_ROWS = 512


def _router_kernel(x_ref, whi_ref, wlo_ref, b_ref, o_ref):
    x = x_ref[...]
    xhi = x.astype(BF16)
    xlo = (x - xhi.astype(F32)).astype(BF16)
    logits = (jnp.dot(xhi, whi_ref[...], preferred_element_type=F32)
              + jnp.dot(xhi, wlo_ref[...], preferred_element_type=F32)
              + jnp.dot(xlo, whi_ref[...], preferred_element_type=F32)) + b_ref[...]
    lane = _iota2(logits.shape, 1).astype(F32)
    logits = jnp.where(lane < N_EXPERTS, logits, -jnp.inf)
    m1 = jnp.max(logits, -1, keepdims=True)
    i1 = jnp.min(jnp.where(logits == m1, lane, 128.0), -1, keepdims=True)
    rest = jnp.where(lane == i1, -jnp.inf, logits)
    m2 = jnp.max(rest, -1, keepdims=True)
    i2 = jnp.min(jnp.where(rest == m2, lane, 128.0), -1, keepdims=True)
    e = jnp.exp(m2 - m1)
    g1 = 1.0 / (1.0 + e)
    g2 = e / (1.0 + e)
    out = jnp.where(lane == 0, i1, 0.0)
    out = jnp.where(lane == 1, i2, out)
    out = jnp.where(lane == 2, g1, out)
    out = jnp.where(lane == 3, g2, out)
    o_ref[...] = out


def _router_call(x, w_router, b_router, tm):
    S, D = x.shape
    wp = jnp.pad(w_router.astype(F32), ((0, 0), (0, 128 - N_EXPERTS)))
    whi = wp.astype(BF16)
    wlo = (wp - whi.astype(F32)).astype(BF16)
    bp = jnp.pad(b_router.astype(F32), (0, 128 - N_EXPERTS)).reshape(1, 128)
    const = lambda shape: pl.BlockSpec(shape, lambda i: (0, 0))
    return pl.pallas_call(
        _router_kernel,
        grid=(S // tm,),
        in_specs=[pl.BlockSpec((tm, D), lambda i: (i, 0)), const((D, 128)), const((D, 128)), const((1, 128))],
        out_specs=pl.BlockSpec((tm, 128), lambda i: (i, 0)),
        out_shape=jax.ShapeDtypeStruct((S, 128), F32),
        compiler_params=_cparams(("parallel",)),
        name="router",
    )(x, whi, wlo, bp)


def _row_copy(src_hbm, dst_ref, src_row, dst_row, sem):
    return pltpu.make_async_copy(src_hbm.at[pl.ds(src_row, 1)], dst_ref.at[pl.ds(dst_row, 1)], sem)


def _gather_rows_kernel(idx_ref, x_hbm, o_ref, sem):
    rows = o_ref.shape[0]
    base = pl.program_id(0) * rows

    def start(r, c):
        _row_copy(x_hbm, o_ref, idx_ref[base + r], r, sem).start()
        return c

    def wait(r, c):
        _row_copy(x_hbm, o_ref, 0, r, sem).wait()
        return c

    lax.fori_loop(0, rows, start, 0)
    lax.fori_loop(0, rows, wait, 0)


def _gather_rows(x, idx, rows):
    n = idx.shape[0]
    D = x.shape[1]
    return pl.pallas_call(
        _gather_rows_kernel,
        grid_spec=pltpu.PrefetchScalarGridSpec(
            num_scalar_prefetch=1,
            grid=(n // rows,),
            in_specs=[pl.BlockSpec(memory_space=pl.ANY)],
            out_specs=pl.BlockSpec((rows, D), lambda i, idx_ref: (i, 0)),
            scratch_shapes=[pltpu.SemaphoreType.DMA(())]),
        out_shape=jax.ShapeDtypeStruct((n, D), x.dtype),
        compiler_params=_cparams(("arbitrary",)),
        name="moe_gather",
    )(idx, x)


def _moe_ffn_kernel(be_ref, nu_ref, x_ref, wg_ref, wu_ref, wd_ref, o_ref, acc_ref):
    i = pl.program_id(0)
    j = pl.program_id(1)

    @pl.when(i < nu_ref[0])
    def _():
        xb = x_ref[...].astype(BF16)
        part = _dot(_silu(_dot(xb, wg_ref[...])) * _dot(xb, wu_ref[...]), wd_ref[...])

        @pl.when(j == 0)
        def _():
            acc_ref[...] = part

        @pl.when(j > 0)
        def _():
            acc_ref[...] += part

    last = j == pl.num_programs(1) - 1

    @pl.when(last & (i < nu_ref[0]))
    def _():
        o_ref[...] = acc_ref[...]

    @pl.when(last & (i >= nu_ref[0]))
    def _():
        o_ref[...] = jnp.zeros_like(o_ref)


def _moe_ffn_call(x_rows, block_e, n_used, wg, wu, wd, tf):
    n, D = x_rows.shape
    F = wg.shape[2]
    R = MOE_ROWS
    return pl.pallas_call(
        _moe_ffn_kernel,
        grid_spec=pltpu.PrefetchScalarGridSpec(
            num_scalar_prefetch=2,
            grid=(n // R, F // tf),
            in_specs=[pl.BlockSpec((R, D), lambda i, j, be, nu: (i, 0)),
                      pl.BlockSpec((None, D, tf), lambda i, j, be, nu: (be[i], 0, j)),
                      pl.BlockSpec((None, D, tf), lambda i, j, be, nu: (be[i], 0, j)),
                      pl.BlockSpec((None, tf, D), lambda i, j, be, nu: (be[i], j, 0))],
            out_specs=pl.BlockSpec((R, D), lambda i, j, be, nu: (i, 0)),
            scratch_shapes=[pltpu.VMEM((R, D), F32)]),
        out_shape=jax.ShapeDtypeStruct((n, D), F32),
        compiler_params=_cparams(("arbitrary", "arbitrary")),
        name="moe_ffn",
    )(block_e, n_used, x_rows, wg, wu, wd)


def _moe_combine_kernel(d_ref, y_hbm, x_ref, r_ref, g_ref, b_ref, o_ref, y1_ref, y2_ref, sem):
    tm = x_ref.shape[0]
    base = pl.program_id(0) * tm

    def start(r, c):
        _row_copy(y_hbm, y1_ref, d_ref[2 * (base + r)], r, sem.at[0]).start()
        _row_copy(y_hbm, y2_ref, d_ref[2 * (base + r) + 1], r, sem.at[1]).start()
        return c

    def wait(r, c):
        _row_copy(y_hbm, y1_ref, 0, r, sem.at[0]).wait()
        _row_copy(y_hbm, y2_ref, 0, r, sem.at[1]).wait()
        return c

    lax.fori_loop(0, tm, start, 0)
    lax.fori_loop(0, tm, wait, 0)
    y = r_ref[:, 2:3] * y1_ref[...] + r_ref[:, 3:4] * y2_ref[...]
    o_ref[...] = _layer_norm_rows(DN_ALPHA * x_ref[...] + y, g_ref[...], b_ref[...])


def _moe_combine_call(dest, y_rows, x, routed, g, b, tm):
    S, D = x.shape
    return pl.pallas_call(
        _moe_combine_kernel,
        grid_spec=pltpu.PrefetchScalarGridSpec(
            num_scalar_prefetch=1,
            grid=(S // tm,),
            in_specs=[pl.BlockSpec(memory_space=pl.ANY),
                      pl.BlockSpec((tm, D), lambda i, d: (i, 0)),
                      pl.BlockSpec((tm, 128), lambda i, d: (i, 0)),
                      pl.BlockSpec((1, D), lambda i, d: (0, 0)),
                      pl.BlockSpec((1, D), lambda i, d: (0, 0))],
            out_specs=pl.BlockSpec((tm, D), lambda i, d: (i, 0)),
            scratch_shapes=[pltpu.VMEM((tm, D), F32), pltpu.VMEM((tm, D), F32), pltpu.SemaphoreType.DMA((2,))]),
        out_shape=jax.ShapeDtypeStruct((S, D), F32),
        compiler_params=_cparams(("arbitrary",)),
        name="moe_combine",
    )(dest, y_rows, x, routed, g.reshape(1, D), b.reshape(1, D))


def _moe_sublayer(x, w_router, b_router, wg, wu, wd, g, b):
    T = x.shape[0]
    R = MOE_ROWS
    routed = _router_call(x, w_router, b_router, min(512, T))
    top_e = routed[:, 0:TOP_K].astype(jnp.int32)
    tok_oh = jnp.sum((top_e[:, :, None] == jnp.arange(N_EXPERTS)[None, None, :]).astype(jnp.int32), axis=1)
    counts = jnp.sum(tok_oh, axis=0)
    rank = jnp.cumsum(tok_oh, axis=0) - tok_oh
    padded = (counts + R - 1) // R * R
    pend = jnp.cumsum(padded)
    pstart = pend - padded
    dest = pstart[top_e] + jnp.take_along_axis(rank, top_e, axis=1)
    n_rows = (T * TOP_K + N_EXPERTS * (R - 1)) // R * R
    n_blocks = n_rows // R
    row_tok = jnp.zeros((n_rows,), jnp.int32).at[dest.reshape(-1)].set(
        jnp.repeat(jnp.arange(T, dtype=jnp.int32), TOP_K))
    block_first_row = jnp.arange(n_blocks, dtype=jnp.int32) * R
    block_e = jnp.minimum(jnp.sum((pend[None, :] <= block_first_row[:, None]).astype(jnp.int32), axis=1),
                          N_EXPERTS - 1)
    n_used = (pend[-1] // R).astype(jnp.int32).reshape(1)
    x_rows = _gather_rows(x, row_tok, R)
    y_rows = _moe_ffn_call(x_rows, block_e, n_used, wg, wu, wd, EXPERT_FF // 2)
    return _moe_combine_call(dest.reshape(-1).astype(jnp.int32), y_rows, x, routed, g, b, min(256, T))


GLA_IN = 2 * GLA_QK + 2 * GLA_V + GLA_LR


def _gla_rwkv_sublayer(x, w_in, gla_wa2, gla_ba, gla_norm, mu, w0, w2, a0, a2, g2, k_k, k_a, r_k, ln_w, ln_b,
                       w_out, ln_g, ln_bias):
    S = x.shape[0]
    tm = min(512, S)
    lr_pad = 128 - GLA_LR
    w_gla = jnp.pad(w_in[:, :GLA_IN], ((0, 0), (0, lr_pad))).astype(BF16)
    w_rwkv = w_in[:, GLA_IN:].astype(BF16)
    p_gla = _matmul(x, w_gla, F32, tm, GLA_COLS)
    p_rwkv = _matmul(x, w_rwkv, F32, tm, RWKV_COLS)
    wa2p = jnp.pad(gla_wa2, ((0, lr_pad), (0, 0))).astype(BF16)
    o_gla = _gla_call(p_gla, wa2p, gla_ba, gla_norm)
    o_rwkv = _rwkv_call(p_rwkv, mu, w0, w2, a0, a2, g2, k_k, k_a, r_k, ln_w, ln_b)
    return _matmul_ln([o_gla, o_rwkv], [w_out[:GLA_V].astype(BF16), w_out[GLA_V:].astype(BF16)], x,
                      ln_g, ln_bias, tm)


def _xattn_sublayer(x, mem, wq, wk, wv, wo, ln_g, ln_bias):
    M = mem.shape[0]
    k_mem = _matmul(mem, wk.astype(BF16), BF16, M, D_MODEL)
    v_mem = _matmul(mem, wv.astype(BF16), BF16, M, D_MODEL)
    wq_scaled = (wq * XATTN_HD ** -0.5).astype(BF16)
    return _xattn_call(x, wq_scaled, k_mem, v_mem, wo.astype(BF16), ln_g, ln_bias, min(512, x.shape[0]))


def _ssd_moba_sublayer(x, w_in, conv_w, conv_b, dt_bias, a_log, d_skip, ssd_norm, w_out, ln_g, ln_bias):
    S = x.shape[0]
    tm = min(512, S)
    nb = S // MOBA_BLOCK
    o_dt = SSD_COLS
    o_q = o_dt + SSD_HEADS
    w_ssd = w_in[:, :o_dt].astype(BF16)
    w_dt = w_in[:, o_dt:o_q]
    w_q = w_in[:, o_q:o_q + MOBA_D]
    w_k = w_in[:, o_q + MOBA_D:o_q + 2 * MOBA_D]
    w_v = w_in[:, o_q + 2 * MOBA_D:]
    p_ssd = _matmul(x, w_ssd, F32, tm, SSD_COLS // 2)
    p_dt = _matmul(x, jnp.pad(w_dt, ((0, 0), (0, 128 - SSD_HEADS))).astype(BF16), F32, tm, 128)
    dtt = _matmul_t(w_dt.T.astype(BF16), x, F32, tm)
    o_ssd = _ssd_call(p_ssd, p_dt, dtt, conv_w, conv_b, dt_bias, a_log, d_skip, ssd_norm)
    qt = _matmul_t((w_q.T * MOBA_HD ** -0.5).astype(BF16), x, BF16, tm)
    k4, kmean = _kproj_call(x, w_k.astype(BF16))
    vt4 = _vproj_call(w_v.T.astype(BF16), x)
    kmean_h = kmean.reshape(nb, MOBA_HEADS, MOBA_HD).transpose(1, 0, 2)
    ot_moba = _moba_call(qt, k4, vt4, kmean_h)
    return _matmul_ln([o_ssd, ot_moba], [w_out[:SSD_INNER].astype(BF16), w_out[SSD_INNER:].astype(BF16)], x,
                      ln_g, ln_bias, tm, transposed=(False, True))


def kernel(x, mem, l0_w_in, l0_gla_wa2, l0_gla_ba, l0_gla_norm, l0_rwkv_mu, l0_rwkv_w0, l0_rwkv_w2, l0_rwkv_a0, l0_rwkv_a2, l0_rwkv_g2, l0_rwkv_kk, l0_rwkv_ka, l0_rwkv_rk, l0_rwkv_lnw, l0_rwkv_lnb, l0_w_out, l0_ln1_g, l0_ln1_b, l0_xq, l0_xk, l0_xv, l0_xo, l0_ln2_g, l0_ln2_b, l0_ffn_wg, l0_ffn_wu, l0_ffn_wd, l0_ln3_g, l0_ln3_b, l1_w_in, l1_conv_w, l1_conv_b, l1_dt_bias, l1_a_log, l1_d_skip, l1_ssd_norm, l1_w_out, l1_ln1_g, l1_ln1_b, l1_xq, l1_xk, l1_xv, l1_xo, l1_ln2_g, l1_ln2_b, l1_router, l1_router_b, l1_exp_wg, l1_exp_wu, l1_exp_wd, l1_ln3_g, l1_ln3_b):
    x2 = x.reshape(-1, D_MODEL)
    mem2 = mem.reshape(-1, D_MODEL)
    x2 = _gla_rwkv_sublayer(x2, l0_w_in, l0_gla_wa2, l0_gla_ba, l0_gla_norm, l0_rwkv_mu, l0_rwkv_w0, l0_rwkv_w2,
                            l0_rwkv_a0, l0_rwkv_a2, l0_rwkv_g2, l0_rwkv_kk, l0_rwkv_ka, l0_rwkv_rk, l0_rwkv_lnw,
                            l0_rwkv_lnb, l0_w_out, l0_ln1_g, l0_ln1_b)
    x2 = _xattn_sublayer(x2, mem2, l0_xq, l0_xk, l0_xv, l0_xo, l0_ln2_g, l0_ln2_b)
    tm = min(512, x2.shape[0])
    x2 = _ffn_call(x2, l0_ffn_wg.astype(BF16), l0_ffn_wu.astype(BF16), l0_ffn_wd.astype(BF16),
                   l0_ln3_g, l0_ln3_b, tm, l0_ffn_wg.shape[1] // 2)
    x2 = _ssd_moba_sublayer(x2, l1_w_in, l1_conv_w, l1_conv_b, l1_dt_bias, l1_a_log, l1_d_skip, l1_ssd_norm,
                            l1_w_out, l1_ln1_g, l1_ln1_b)
    x2 = _xattn_sublayer(x2, mem2, l1_xq, l1_xk, l1_xv, l1_xo, l1_ln2_g, l1_ln2_b)
    x2 = _moe_sublayer(x2, l1_router, l1_router_b, l1_exp_wg.astype(BF16), l1_exp_wu.astype(BF16),
                       l1_exp_wd.astype(BF16), l1_ln3_g, l1_ln3_b)
    return x2.reshape(x.shape)
```

```python
import functools
import math

import jax
import jax.numpy as jnp
from jax import lax
from jax.experimental import pallas as pl
from jax.experimental.pallas import tpu as pltpu

BF16 = jnp.bfloat16
F32 = jnp.float32

D_MODEL = 1024
LN_EPS = 1e-5
DEPTH = 2
DN_ALPHA = (2 * DEPTH) ** 0.25

GLA_HEADS, GLA_DK, GLA_DV, GLA_CHUNK = 4, 64, 128, 64
GLA_QK, GLA_V, GLA_LR, GLA_TAU = 256, 512, 16, 16.0
GLA_COLS = 2 * GLA_QK + 2 * GLA_V + 128

RWKV_HEADS, RWKV_HD, RWKV_D, RWKV_CHUNK = 8, 64, 512, 64
RWKV_COLS = 1792
RWKV_DECAY_SCALE = math.exp(-0.5)
RWKV_GN_EPS = 64e-5

VMEM_LIMIT = 56 * 1024 * 1024


def _cparams(sem):
    return pltpu.CompilerParams(dimension_semantics=sem, vmem_limit_bytes=VMEM_LIMIT)


def _dot(a, b):
    return jnp.dot(a.astype(BF16), b.astype(BF16), preferred_element_type=F32)


def _dot_nt(a, b):
    return lax.dot_general(a.astype(BF16), b.astype(BF16), (((1,), (1,)), ((), ())), preferred_element_type=F32)


def _dot_tn(a, b):
    return lax.dot_general(a.astype(BF16), b.astype(BF16), (((0,), (0,)), ((), ())), preferred_element_type=F32)


def _split3(x):
    hi = x.astype(BF16)
    r1 = x - hi.astype(F32)
    mid = r1.astype(BF16)
    lo = (r1 - mid.astype(F32)).astype(BF16)
    return hi, mid, lo


def _dot_exact_lhs(m, x):
    mb = m.astype(BF16)
    hi, mid, lo = _split3(x)
    return (jnp.dot(mb, hi, preferred_element_type=F32) + jnp.dot(mb, mid, preferred_element_type=F32)
            + jnp.dot(mb, lo, preferred_element_type=F32))


def _dot_exact_rhs(x, m):
    mb = m.astype(BF16)
    hi, mid, lo = _split3(x)
    return (jnp.dot(hi, mb, preferred_element_type=F32) + jnp.dot(mid, mb, preferred_element_type=F32)
            + jnp.dot(lo, mb, preferred_element_type=F32))


def _sigmoid(x):
    return 1.0 / (1.0 + jnp.exp(-x))


def _silu(x):
    return x * _sigmoid(x)


def _iota2(shape, axis):
    return lax.broadcasted_iota(jnp.int32, shape, axis)


def _chunk_tril(n, chunk):
    r = _iota2((n, n), 0)
    c = _iota2((n, n), 1)
    return jnp.where((c <= r) & ((r // chunk) == (c // chunk)), 1.0, 0.0)


def _head_block(n, width, value):
    r = _iota2((n, n), 0)
    c = _iota2((n, n), 1)
    return jnp.where((r // width) == (c // width), value, 0.0)


def _mm_kernel(x_ref, w_ref, o_ref):
    o_ref[...] = _dot(x_ref[...], w_ref[...]).astype(o_ref.dtype)


def _matmul(x, w, out_dtype, tm, tn):
    S, K = x.shape
    N = w.shape[1]
    return pl.pallas_call(
        _mm_kernel,
        grid=(S // tm, N // tn),
        in_specs=[pl.BlockSpec((tm, K), lambda i, j: (i, 0)),
                  pl.BlockSpec((K, tn), lambda i, j: (0, j))],
        out_specs=pl.BlockSpec((tm, tn), lambda i, j: (i, j)),
        out_shape=jax.ShapeDtypeStruct((S, N), out_dtype),
        compiler_params=_cparams(("parallel", "arbitrary")),
        name="matmul",
    )(x, w)


def _mm_t_kernel(wt_ref, x_ref, o_ref):
    o_ref[...] = _dot_nt(wt_ref[...], x_ref[...]).astype(o_ref.dtype)


def _matmul_t(wt, x, out_dtype, tm):
    S, K = x.shape
    N = wt.shape[0]
    return pl.pallas_call(
        _mm_t_kernel,
        grid=(S // tm,),
        in_specs=[pl.BlockSpec((N, K), lambda i: (0, 0)),
                  pl.BlockSpec((tm, K), lambda i: (i, 0))],
        out_specs=pl.BlockSpec((N, tm), lambda i: (0, i)),
        out_shape=jax.ShapeDtypeStruct((N, S), out_dtype),
        compiler_params=_cparams(("parallel",)),
        name="matmul_t",
    )(wt, x)


def _layer_norm_rows(y, g, b):
    mu = jnp.mean(y, -1, keepdims=True)
    d = y - mu
    var = jnp.mean(d * d, -1, keepdims=True)
    return d * lax.rsqrt(var + LN_EPS) * g + b


def _mm_ln_kernel(transposed, *refs):
    n_in = len(transposed)
    a_refs = refs[:n_in]
    w_refs = refs[n_in:2 * n_in]
    x_ref, g_ref, b_ref, o_ref = refs[2 * n_in:]
    acc = DN_ALPHA * x_ref[...]
    for a_ref, w_ref, tr in zip(a_refs, w_refs, transposed):
        acc = acc + (_dot_tn if tr else _dot)(a_ref[...], w_ref[...])
    o_ref[...] = _layer_norm_rows(acc, g_ref[...], b_ref[...])


def _matmul_ln(a_list, w_list, x, g, b, tm, transposed=None):
    S, D = x.shape
    transposed = tuple(transposed or (False,) * len(a_list))
    in_specs = ([pl.BlockSpec((a.shape[0], tm), lambda i: (0, i)) if tr else
                 pl.BlockSpec((tm, a.shape[1]), lambda i: (i, 0)) for a, tr in zip(a_list, transposed)]
                + [pl.BlockSpec(w.shape, lambda i: (0, 0)) for w in w_list]
                + [pl.BlockSpec((tm, D), lambda i: (i, 0)),
                   pl.BlockSpec((1, D), lambda i: (0, 0)),
                   pl.BlockSpec((1, D), lambda i: (0, 0))])
    return pl.pallas_call(
        functools.partial(_mm_ln_kernel, transposed),
        grid=(S // tm,),
        in_specs=in_specs,
        out_specs=pl.BlockSpec((tm, D), lambda i: (i, 0)),
        out_shape=jax.ShapeDtypeStruct((S, D), F32),
        compiler_params=_cparams(("parallel",)),
        name="matmul_ln",
    )(*a_list, *w_list, x, g.reshape(1, D), b.reshape(1, D))


def _gla_kernel(p_ref, wa2_ref, ba_ref, ng_ref, o_ref, st_ref, o_scr):
    C, H, dk, dv = GLA_CHUNK, GLA_HEADS, GLA_DK, GLA_DV
    tb = p_ref.shape[0]

    @pl.when(pl.program_id(0) == 0)
    def _():
        st_ref[...] = jnp.zeros_like(st_ref)

    z = _dot(p_ref[:, 2 * GLA_QK + 2 * GLA_V:], wa2_ref[...]) + ba_ref[...]
    log_a = -(jnp.maximum(-z, 0.0) + jnp.log(1.0 + jnp.exp(-jnp.abs(z)))) / GLA_TAU
    b = _dot_exact_lhs(_chunk_tril(tb, C), log_a)
    causal = _iota2((C, C), 1) <= _iota2((C, C), 0)

    nc = tb // C
    q_h, k_h, ke_h, v_h, dec_h = [], [], [], [], []
    for c in range(nc):
        rows = slice(c * C, (c + 1) * C)
        b_c = b[rows]
        b_last = b_c[C - 1:C]
        q_dec = p_ref[rows, 0:GLA_QK] * (dk ** -0.5) * jnp.exp(b_c)
        k_c = p_ref[rows, GLA_QK:2 * GLA_QK]
        k_dec = k_c * jnp.exp(-b_c)
        k_end = k_c * jnp.exp(b_last - b_c)
        decay = jnp.exp(b_last)
        for h in range(H):
            ks = slice(h * dk, (h + 1) * dk)
            q_h.append(q_dec[:, ks])
            k_h.append(k_dec[:, ks])
            ke_h.append(k_end[:, ks])
            dec_h.append(decay[:, ks])
            v_h.append(p_ref[rows, 2 * GLA_QK + h * dv:2 * GLA_QK + (h + 1) * dv])
    n = nc * H
    attn = [jnp.where(causal, _dot_nt(q_h[i], k_h[i]), 0.0) for i in range(n)]
    kv = [_dot_tn(v_h[i], ke_h[i]) for i in range(n)]
    intra = [_dot(attn[i], v_h[i]) for i in range(n)]
    state = [st_ref[:, h * dk:(h + 1) * dk] for h in range(H)]
    entering = []
    for i in range(n):
        entering.append(state[i % H])
        state[i % H] = state[i % H] * dec_h[i] + kv[i]
    for i in range(n):
        c, h = divmod(i, H)
        o_scr[c * C:(c + 1) * C, h * dv:(h + 1) * dv] = intra[i] + _dot_nt(q_h[i], entering[i])
    for h in range(H):
        st_ref[:, h * dk:(h + 1) * dk] = state[h]

    for h in range(H):
        vs = slice(h * dv, (h + 1) * dv)
        o_h = o_scr[:, vs]
        g_h = p_ref[:, 2 * GLA_QK + GLA_V + h * dv:2 * GLA_QK + GLA_V + (h + 1) * dv]
        o_h = o_h * lax.rsqrt(jnp.mean(o_h * o_h, -1, keepdims=True) + 1e-5) * ng_ref[:, vs]
        o_ref[:, vs] = (o_h * _silu(g_h)).astype(o_ref.dtype)


def _gla_call(p_gla, wa2p, ba, norm_g, tb=256):
    S = p_gla.shape[0]
    return pl.pallas_call(
        _gla_kernel,
        grid=(S // tb,),
        in_specs=[pl.BlockSpec((tb, GLA_COLS), lambda i: (i, 0)),
                  pl.BlockSpec((128, GLA_QK), lambda i: (0, 0)),
                  pl.BlockSpec((1, GLA_QK), lambda i: (0, 0)),
                  pl.BlockSpec((1, GLA_V), lambda i: (0, 0))],
        out_specs=pl.BlockSpec((tb, GLA_V), lambda i: (i, 0)),
        out_shape=jax.ShapeDtypeStruct((S, GLA_V), BF16),
        scratch_shapes=[pltpu.VMEM((GLA_DV, GLA_QK), F32), pltpu.VMEM((tb, GLA_V), F32)],
        compiler_params=_cparams(("arbitrary",)),
        name="gla",
    )(p_gla, wa2p, ba.reshape(1, GLA_QK), norm_g.reshape(1, GLA_V))


def _rwkv_kernel(p_ref, mu_ref, w0_ref, w2_ref, a0_ref, a2_ref, g2_ref, kk_ref, ka_ref, rk_ref, lnw_ref, lnb_ref,
                 o_ref, prev_ref, h_ref, o_scr):
    C, H, N, D = RWKV_CHUNK, RWKV_HEADS, RWKV_HD, RWKV_D
    tb = p_ref.shape[0]
    first = pl.program_id(0) == 0

    @pl.when(first)
    def _():
        prev_ref[...] = jnp.zeros_like(prev_ref)
        h_ref[...] = jnp.zeros_like(h_ref)

    p = p_ref[...]
    shifted = jnp.where(_iota2(p.shape, 0) == 0, prev_ref[...], pltpu.roll(p, 1, 0))
    prev_ref[...] = p[tb - 1:tb]
    p = p + mu_ref[...] * (shifted - p)
    r = p[:, 0:D]
    k = p[:, D:2 * D]
    v = p[:, 2 * D:3 * D]
    xw = p[:, 3 * D:3 * D + 64]
    xa = p[:, 3 * D + 64:3 * D + 128]
    xg = p[:, 3 * D + 128:3 * D + 256]
    lw = -RWKV_DECAY_SCALE * _sigmoid(w0_ref[...] + _dot(jnp.tanh(xw), w2_ref[...]))
    a = _sigmoid(a0_ref[...] + _dot(xa, a2_ref[...]))
    g = _dot(_sigmoid(xg), g2_ref[...])
    head_ones = _head_block(D, N, 1.0)
    kk = k * kk_ref[...]
    kk = kk * lax.rsqrt(jnp.maximum(_dot_exact_rhs(kk * kk, head_ones), 1e-24))
    k = k * (1.0 + (a - 1.0) * ka_ref[...])
    pv = -kk * a
    cw = _dot_exact_lhs(_chunk_tril(tb, C), lw)
    cwx = cw - lw

    gi = _iota2((2 * C, 2 * C), 0)
    gj = _iota2((2 * C, 2 * C), 1) % C
    gram_mask = ((gi < C) & (gj < gi)) | ((gi >= C) & (gj <= gi - C))
    eye = _iota2((C, C), 0) == _iota2((C, C), 1)
    eye_f = jnp.where(eye, 1.0, 0.0)

    nc = tb // C
    items = [(c, h) for c in range(nc) for h in range(H)]
    xs, ys, pk_e, b_h, v_h, r_h, g_h = [], [], [], [], [], [], []
    for c in range(nc):
        rows = slice(c * C, (c + 1) * C)
        cw_c = cw[rows]
        cw_end = cw_c[C - 1:C]
        e_pos = jnp.exp(cw_c)
        e_neg = jnp.exp(-cw_c)
        e_end = jnp.exp(cw_end - cw_c)
        r_t = r[rows] * e_pos
        b_t = kk[rows] * jnp.exp(cwx[rows])
        p_t = pv[rows] * e_neg
        k_t = k[rows] * e_neg
        p_e = pv[rows] * e_end
        k_e = k[rows] * e_end
        g_end = jnp.exp(cw_end)
        v_c = v[rows]
        for h in range(H):
            hs = slice(h * N, (h + 1) * N)
            xs.append(jnp.concatenate([b_t[:, hs], r_t[:, hs]], axis=0))
            ys.append(jnp.concatenate([p_t[:, hs], k_t[:, hs]], axis=0))
            pk_e.append(jnp.concatenate([p_e[:, hs], k_e[:, hs]], axis=0))
            b_h.append(b_t[:, hs])
            v_h.append(v_c[:, hs])
            r_h.append(r_t[:, hs])
            g_h.append(g_end[:, hs])
    n = len(items)
    grams = [jnp.where(gram_mask, _dot_nt(xs[i], ys[i]), 0.0) for i in range(n)]
    l_p = [g[0:C, 0:C] for g in grams]
    m_pk = [g[C:2 * C, :] for g in grams]
    lkv = [_dot(grams[i][0:C, C:2 * C], v_h[i]) for i in range(n)]
    x = [_dot(lp, lp) for lp in l_p]
    t = [eye_f + lp for lp in l_p]
    for _ in range(4):
        tx = [_dot(jnp.concatenate([t[i], x[i]], axis=0), x[i]) for i in range(n)]
        t = [t[i] + tx[i][0:C] for i in range(n)]
        x = [tx[i][C:2 * C] for i in range(n)]
    t = [t[i] + _dot(t[i], x[i]) for i in range(n)]
    wu = [_dot(t[i], jnp.concatenate([b_h[i], lkv[i]], axis=1)) for i in range(n)]
    rhs = [jnp.concatenate([wu[i], jnp.concatenate([jnp.zeros((C, N), F32), v_h[i]], axis=1)], axis=0)
           for i in range(n)]
    az = [_dot_tn(pk_e[i], rhs[i]) for i in range(n)]
    qo = [_dot(m_pk[i], rhs[i]) for i in range(n)]
    state = [h_ref[h] for h in range(H)]
    for i, (c, h) in enumerate(items):
        a_mat = az[i][:, 0:N] + jnp.where(eye, g_h[i], 0.0)
        q_mat = qo[i][:, 0:N] + r_h[i]
        oh = _dot(jnp.concatenate([q_mat, a_mat], axis=0), state[h])
        o_scr[c * C:(c + 1) * C, h * N:(h + 1) * N] = oh[0:C] + qo[i][:, N:2 * N]
        state[h] = oh[C:C + N] + az[i][:, N:2 * N]
    for h in range(H):
        h_ref[h] = state[h]

    o = o_scr[...]
    head_mean = _head_block(D, N, 1.0 / N)
    mean = _dot_exact_rhs(o, head_mean)
    d = o - mean
    var = _dot_exact_rhs(d * d, head_mean)
    o = d * lax.rsqrt(var + RWKV_GN_EPS) * lnw_ref[...] + lnb_ref[...]
    bonus = _dot_exact_rhs(r * k * rk_ref[...], head_ones) * v
    o_ref[...] = ((o + bonus) * g).astype(o_ref.dtype)


def _rwkv_call(p_rwkv, mu, w0, w2, a0, a2, g2, k_k, k_a, r_k, ln_w, ln_b, tb=128):
    S = p_rwkv.shape[0]
    D = RWKV_D
    row = lambda t: t.reshape(1, -1).astype(F32)
    full = lambda shape: pl.BlockSpec(shape, lambda i: tuple(0 for _ in shape))
    return pl.pallas_call(
        _rwkv_kernel,
        grid=(S // tb,),
        in_specs=[pl.BlockSpec((tb, RWKV_COLS), lambda i: (i, 0)),
                  full((1, RWKV_COLS)), full((1, D)), full((64, D)), full((1, D)), full((64, D)), full((128, D)),
                  full((1, D)), full((1, D)), full((1, D)), full((1, D)), full((1, D))],
        out_specs=pl.BlockSpec((tb, D), lambda i: (i, 0)),
        out_shape=jax.ShapeDtypeStruct((S, D), BF16),
        scratch_shapes=[pltpu.VMEM((1, RWKV_COLS), F32),
                        pltpu.VMEM((RWKV_HEADS, RWKV_HD, RWKV_HD), F32),
                        pltpu.VMEM((tb, D), F32)],
        compiler_params=_cparams(("arbitrary",)),
        name="rwkv7",
    )(p_rwkv, row(mu), row(w0), w2.astype(BF16), row(a0), a2.astype(BF16), g2.astype(BF16),
      row(k_k), row(k_a), row(r_k), row(ln_w), row(ln_b))


XATTN_HEADS, XATTN_HD = 4, 256


def _xattn_kernel(x_ref, wq_ref, k_ref, v_ref, wo_ref, g_ref, b_ref, o_ref):
    x = x_ref[...]
    q = _dot(x, wq_ref[...])
    outs = []
    for h in range(XATTN_HEADS):
        hs = slice(h * XATTN_HD, (h + 1) * XATTN_HD)
        s = _dot_nt(q[:, hs], k_ref[:, hs])
        e = jnp.exp(s - jnp.max(s, -1, keepdims=True))
        p = e / jnp.sum(e, -1, keepdims=True)
        outs.append(_dot(p, v_ref[:, hs]))
    o = jnp.concatenate(outs, axis=1)
    y = DN_ALPHA * x + _dot(o, wo_ref[...])
    o_ref[...] = _layer_norm_rows(y, g_ref[...], b_ref[...])


def _xattn_call(x, wq_scaled, k_mem, v_mem, wo, g, b, tm):
    S, D = x.shape
    M = k_mem.shape[0]
    const = lambda shape: pl.BlockSpec(shape, lambda i: (0, 0))
    return pl.pallas_call(
        _xattn_kernel,
        grid=(S // tm,),
        in_specs=[pl.BlockSpec((tm, D), lambda i: (i, 0)), const((D, D)), const((M, D)), const((M, D)),
                  const((D, D)), const((1, D)), const((1, D))],
        out_specs=pl.BlockSpec((tm, D), lambda i: (i, 0)),
        out_shape=jax.ShapeDtypeStruct((S, D), F32),
        compiler_params=_cparams(("parallel",)),
        name="xattn",
    )(x, wq_scaled, k_mem, v_mem, wo, g.reshape(1, D), b.reshape(1, D))


def _ffn_kernel(x_ref, wg_ref, wu_ref, wd_ref, g_ref, b_ref, o_ref, acc_ref):
    j = pl.program_id(1)
    x = x_ref[...]
    xb = x.astype(BF16)
    part = _dot(_silu(_dot(xb, wg_ref[...])) * _dot(xb, wu_ref[...]), wd_ref[...])

    @pl.when(j == 0)
    def _():
        acc_ref[...] = DN_ALPHA * x + part

    @pl.when(j > 0)
    def _():
        acc_ref[...] += part

    @pl.when(j == pl.num_programs(1) - 1)
    def _():
        o_ref[...] = _layer_norm_rows(acc_ref[...], g_ref[...], b_ref[...])


def _ffn_call(x, wg, wu, wd, g, b, tm, tf):
    S, D = x.shape
    F = wg.shape[1]
    return pl.pallas_call(
        _ffn_kernel,
        grid=(S // tm, F // tf),
        in_specs=[pl.BlockSpec((tm, D), lambda i, j: (i, 0)),
                  pl.BlockSpec((D, tf), lambda i, j: (0, j)),
                  pl.BlockSpec((D, tf), lambda i, j: (0, j)),
                  pl.BlockSpec((tf, D), lambda i, j: (j, 0)),
                  pl.BlockSpec((1, D), lambda i, j: (0, 0)),
                  pl.BlockSpec((1, D), lambda i, j: (0, 0))],
        out_specs=pl.BlockSpec((tm, D), lambda i, j: (i, 0)),
        out_shape=jax.ShapeDtypeStruct((S, D), F32),
        scratch_shapes=[pltpu.VMEM((tm, D), F32)],
        compiler_params=_cparams(("parallel", "arbitrary")),
        name="ffn",
    )(x, wg, wu, wd, g.reshape(1, D), b.reshape(1, D))


SSD_HD, SSD_HEADS, SSD_INNER, SSD_GROUPS, SSD_STATE = 64, 16, 1024, 2, 128
SSD_BC, SSD_CONV, SSD_CONV_CH, SSD_CHUNK = 256, 4, 1536, 128
SSD_COLS = SSD_INNER + SSD_CONV_CH
SSD_GW = SSD_INNER // SSD_GROUPS


def _softplus(x):
    return jnp.maximum(x, 0.0) + jnp.log(1.0 + jnp.exp(-jnp.abs(x)))


def _ssd_kernel(p_ref, dt_ref, dtt_ref, cw_ref, cb_ref, dtb_ref, dtbc_ref, a_ref, ac_ref, dsk_ref, ng_ref,
                o_ref, prev_ref, st_ref, y_scr):
    L, G, NS, HD = SSD_CHUNK, SSD_GROUPS, SSD_STATE, SSD_HD
    HG = SSD_HEADS // G

    @pl.when(pl.program_id(0) == 0)
    def _():
        prev_ref[...] = jnp.zeros_like(prev_ref)
        st_ref[...] = jnp.zeros_like(st_ref)

    cur = p_ref[:, SSD_INNER:]
    prev = prev_ref[...]
    row = _iota2(cur.shape, 0)
    conv = cur * cw_ref[SSD_CONV - 1:SSD_CONV, :] + cb_ref[...]
    for kk in range(1, SSD_CONV):
        shifted = jnp.where(row < kk, pltpu.roll(prev, kk, 0), pltpu.roll(cur, kk, 0))
        conv = conv + shifted * cw_ref[SSD_CONV - 1 - kk:SSD_CONV - kk, :]
    prev_ref[...] = cur
    xbc = _silu(conv)
    xs = xbc[:, :SSD_INNER]

    dt = _softplus(dt_ref[...] + dtb_ref[...])
    a_col = dt * a_ref[...]
    a_row = _softplus(dtt_ref[...] + dtbc_ref[...]) * ac_ref[...]
    li = _iota2((L, L), 0)
    lj = _iota2((L, L), 1)
    cs = _dot_exact_lhs(jnp.where(lj <= li, 1.0, 0.0), a_col)
    cs_row = _dot_exact_rhs(a_row, jnp.where(li <= lj, 1.0, 0.0))
    expand = jnp.where(_iota2((128, SSD_INNER), 1) // HD == _iota2((128, SSD_INNER), 0), 1.0, 0.0)
    dt_x = _dot_exact_rhs(dt, expand)
    cs_x = _dot_exact_rhs(cs, expand)
    cs_end = cs_x[L - 1:L]
    xd = xs * dt_x
    xd_dec = xd * jnp.exp(cs_end - cs_x)
    out_dec = jnp.exp(cs_x)
    chunk_dec = jnp.exp(cs_end)
    tril = lj <= li

    for g in range(G):
        gs = slice(g * SSD_GW, (g + 1) * SSD_GW)
        b_g = xbc[:, SSD_INNER + g * NS:SSD_INNER + (g + 1) * NS]
        c_g = xbc[:, SSD_INNER + SSD_BC + g * NS:SSD_INNER + SSD_BC + (g + 1) * NS]
        cb = _dot_nt(c_g, b_g)
        for j in range(HG):
            h = g * HG + j
            hs = slice(h * HD, (h + 1) * HD)
            seg = jnp.where(tril, jnp.exp(cs[:, h:h + 1] - cs_row[h:h + 1, :]), 0.0)
            y_scr[:, hs] = _dot(cb * seg, xd[:, hs])
        st = st_ref[g]
        y_off = _dot(c_g, st) * out_dec[:, gs]
        st_ref[g] = st * chunk_dec[:, gs] + _dot_tn(b_g, xd_dec[:, gs])
        y_scr[:, gs] = y_scr[:, gs] + y_off

    y = (y_scr[...] + dsk_ref[...] * xs) * _silu(p_ref[:, :SSD_INNER])
    for g in range(G):
        gs = slice(g * SSD_GW, (g + 1) * SSD_GW)
        y_g = y[:, gs]
        o_ref[:, gs] = (y_g * lax.rsqrt(jnp.mean(y_g * y_g, -1, keepdims=True) + 1e-5) * ng_ref[:, gs]).astype(o_ref.dtype)


def _ssd_call(p_ssd, p_dt, dtt, conv_w, conv_b, dt_bias, a_log, d_skip, norm_g):
    S = p_ssd.shape[0]
    L = SSD_CHUNK
    a_neg = -jnp.exp(a_log.astype(F32))
    pad = lambda t: jnp.pad(t.astype(F32), (0, 128 - SSD_HEADS)).reshape(1, 128)
    col = lambda t: t.astype(F32).reshape(SSD_HEADS, 1)
    const = lambda shape: pl.BlockSpec(shape, lambda i: (0, 0))
    return pl.pallas_call(
        _ssd_kernel,
        grid=(S // L,),
        in_specs=[pl.BlockSpec((L, SSD_COLS), lambda i: (i, 0)),
                  pl.BlockSpec((L, 128), lambda i: (i, 0)),
                  pl.BlockSpec((SSD_HEADS, L), lambda i: (0, i)),
                  const((SSD_CONV, SSD_CONV_CH)), const((1, SSD_CONV_CH)),
                  const((1, 128)), const((SSD_HEADS, 1)), const((1, 128)), const((SSD_HEADS, 1)),
                  const((1, SSD_INNER)), const((1, SSD_INNER))],
        out_specs=pl.BlockSpec((L, SSD_INNER), lambda i: (i, 0)),
        out_shape=jax.ShapeDtypeStruct((S, SSD_INNER), BF16),
        scratch_shapes=[pltpu.VMEM((L, SSD_CONV_CH), F32),
                        pltpu.VMEM((SSD_GROUPS, SSD_STATE, SSD_GW), F32),
                        pltpu.VMEM((L, SSD_INNER), F32)],
        compiler_params=_cparams(("arbitrary",)),
        name="ssd",
    )(p_ssd, p_dt, dtt, conv_w.astype(F32), conv_b.reshape(1, -1).astype(F32),
      pad(dt_bias), col(dt_bias), pad(a_neg), col(a_neg),
      jnp.repeat(d_skip.astype(F32), SSD_HD).reshape(1, -1), norm_g.reshape(1, -1).astype(F32))


MOBA_HD, MOBA_HEADS, MOBA_D, MOBA_BLOCK, MOBA_TOPK = 64, 8, 512, 256, 3
NEG_BIG = -1e30


def _kproj_kernel(x_ref, w_ref, k_ref, kmean_ref):
    k = _dot(x_ref[...], w_ref[...])
    kmean_ref[0] = jnp.mean(k, 0, keepdims=True)
    shape = (MOBA_BLOCK, 2 * MOBA_HD)
    pos = jnp.where(_iota2(shape, 1) == MOBA_HD, _iota2(shape, 0).astype(F32), 0.0).astype(k_ref.dtype)
    for h in range(MOBA_HEADS):
        k_ref[h, 0] = pos
        k_ref[h, 0, :, 0:MOBA_HD] = k[:, h * MOBA_HD:(h + 1) * MOBA_HD].astype(k_ref.dtype)


def _kproj_call(x, wk):
    S, D = x.shape
    nb = S // MOBA_BLOCK
    return pl.pallas_call(
        _kproj_kernel,
        grid=(nb,),
        in_specs=[pl.BlockSpec((MOBA_BLOCK, D), lambda i: (i, 0)), pl.BlockSpec((D, MOBA_D), lambda i: (0, 0))],
        out_specs=[pl.BlockSpec((MOBA_HEADS, 1, MOBA_BLOCK, 2 * MOBA_HD), lambda i: (0, i, 0, 0)),
                   pl.BlockSpec((1, 1, MOBA_D), lambda i: (i, 0, 0))],
        out_shape=[jax.ShapeDtypeStruct((MOBA_HEADS, nb, MOBA_BLOCK, 2 * MOBA_HD), BF16),
                   jax.ShapeDtypeStruct((nb, 1, MOBA_D), F32)],
        compiler_params=_cparams(("parallel",)),
        name="moba_kproj",
    )(x, wk)


def _vproj_kernel(wt_ref, x_ref, v_ref):
    vt = _dot_nt(wt_ref[...], x_ref[...])
    for h in range(MOBA_HEADS):
        v_ref[h, 0] = vt[h * MOBA_HD:(h + 1) * MOBA_HD].astype(v_ref.dtype)


def _vproj_call(wvt, x):
    S, D = x.shape
    nb = S // MOBA_BLOCK
    return pl.pallas_call(
        _vproj_kernel,
        grid=(nb,),
        in_specs=[pl.BlockSpec((MOBA_D, D), lambda i: (0, 0)), pl.BlockSpec((MOBA_BLOCK, D), lambda i: (i, 0))],
        out_specs=pl.BlockSpec((MOBA_HEADS, 1, MOBA_HD, MOBA_BLOCK), lambda i: (0, i, 0, 0)),
        out_shape=jax.ShapeDtypeStruct((MOBA_HEADS, nb, MOBA_HD, MOBA_BLOCK), BF16),
        compiler_params=_cparams(("parallel",)),
        name="moba_vproj",
    )(wvt, x)


def _moba_kernel(qt_ref, k_ref, vt_ref, kmean_ref, o_ref, sel_ref, s0_ref, s1_ref, p0_ref, p1_ref):
    BS, HD = MOBA_BLOCK, MOBA_HD
    h = pl.program_id(0)
    i = pl.program_id(1)
    nb = k_ref.shape[0]
    qt = qt_ref[...]
    slope = jnp.exp2(jnp.zeros((1, BS), F32) - (h + 1).astype(F32) * (8.0 / MOBA_HEADS))
    qt_ext = jnp.concatenate(
        [qt, jnp.where(_iota2((HD, BS), 0) == 0, slope, 0.0).astype(qt.dtype)], axis=0)

    gate = _dot(kmean_ref[...], qt)
    blk = _iota2((nb, BS), 0).astype(F32)
    cand = blk < i.astype(F32)
    sel = jnp.zeros((nb, BS), F32)
    for _ in range(MOBA_TOPK):
        best = jnp.max(jnp.where(cand, gate, -jnp.inf), 0, keepdims=True)
        idx = jnp.min(jnp.where(cand & (gate == best), blk, float(nb)), 0, keepdims=True)
        pick = blk == idx
        sel = jnp.where(pick, 1.0, sel)
        cand = cand & jnp.logical_not(pick)
    sel_ref[...] = sel

    last = nb - 1

    def pair(t):
        return jnp.clip(2 * t, 0, last), jnp.clip(2 * t + 1, 0, last)

    def issue_scores(t, s_ref):
        tops = []
        for x, j in enumerate(pair(t)):
            sc = _dot(k_ref[j], qt_ext)
            s_ref[x] = sc
            tops.append(jnp.max(sc, 0, keepdims=True))
        return tuple(tops)

    def weighted_values(t, p_ref):
        ja, jb = pair(t)
        return _dot(vt_ref[ja], p_ref[0]) + _dot(vt_ref[jb], p_ref[1])

    def step(t, tops, s_cur, s_nxt, p_cur, p_prv, alpha_prev, m, l, acc):
        acc = alpha_prev * acc + weighted_values(t - 1, p_prv)
        tops_next = issue_scores(t + 1, s_nxt)
        on, shift, top = [], [], []
        for x, j in enumerate(pair(t)):
            on.append(sel_ref[pl.ds(j, 1), :] > 0.0)
            shift.append(slope * ((j - i) * BS).astype(F32))
            top.append(jnp.where(on[x], tops[x] + shift[x], NEG_BIG))
        m_new = jnp.maximum(m, jnp.maximum(top[0], top[1]))
        alpha = jnp.exp(m - m_new)
        l = alpha * l
        for x in range(2):
            p = jnp.exp(s_cur[x] - (jnp.where(on[x], m_new, -NEG_BIG) - shift[x]))
            l = l + jnp.sum(p, 0, keepdims=True)
            p_cur[x] = p.astype(BF16)
        return tops_next, alpha, m_new, l, acc

    def body(u, carry):
        tops, alpha_prev, m, l, acc = carry
        tops, alpha_prev, m, l, acc = step(2 * u, tops, s0_ref, s1_ref, p0_ref, p1_ref, alpha_prev, m, l, acc)
        return step(2 * u + 1, tops, s1_ref, s0_ref, p1_ref, p0_ref, alpha_prev, m, l, acc)

    p1_ref[...] = jnp.zeros_like(p1_ref)
    init = (issue_scores(0, s0_ref), jnp.ones((1, BS), F32), jnp.full((1, BS), NEG_BIG, F32),
            jnp.zeros((1, BS), F32), jnp.zeros((HD, BS), F32))
    trips = (i + 3) // 4
    _, alpha_prev, m, l, acc = lax.fori_loop(0, trips, body, init)
    acc = alpha_prev * acc + weighted_values(2 * trips - 1, p1_ref)
    s = _dot(k_ref[i], qt_ext)
    s = jnp.where(_iota2((BS, BS), 1) >= _iota2((BS, BS), 0), s, NEG_BIG)
    m_new = jnp.maximum(m, jnp.max(s, 0, keepdims=True))
    p = jnp.exp(s - m_new)
    alpha = jnp.exp(m - m_new)
    l = alpha * l + jnp.sum(p, 0, keepdims=True)
    acc = alpha * acc + _dot(vt_ref[i], p)
    o_ref[...] = (acc / l).astype(o_ref.dtype)


def _moba_call(qt, k4, vt4, kmean):
    S = qt.shape[1]
    nb = S // MOBA_BLOCK
    return pl.pallas_call(
        _moba_kernel,
        grid=(MOBA_HEADS, nb),
        in_specs=[pl.BlockSpec((MOBA_HD, MOBA_BLOCK), lambda h, i: (h, i)),
                  pl.BlockSpec((None, nb, MOBA_BLOCK, 2 * MOBA_HD), lambda h, i: (h, 0, 0, 0)),
                  pl.BlockSpec((None, nb, MOBA_HD, MOBA_BLOCK), lambda h, i: (h, 0, 0, 0)),
                  pl.BlockSpec((None, nb, MOBA_HD), lambda h, i: (h, 0, 0))],
        out_specs=pl.BlockSpec((MOBA_HD, MOBA_BLOCK), lambda h, i: (h, i)),
        out_shape=jax.ShapeDtypeStruct((MOBA_D, S), BF16),
        scratch_shapes=[pltpu.VMEM((nb, MOBA_BLOCK), F32)]
        + [pltpu.VMEM((2, MOBA_BLOCK, MOBA_BLOCK), F32)] * 2 + [pltpu.VMEM((2, MOBA_BLOCK, MOBA_BLOCK), BF16)] * 2,
        compiler_params=_cparams(("parallel", "arbitrary")),
        name="moba",
    )(qt, k4, vt4, kmean)


N_EXPERTS, TOP_K, EXPERT_FF = 8, 2, 2816
MOE_ROWS = 512


def _router_kernel(x_ref, whi_ref, wlo_ref, b_ref, o_ref):
    x = x_ref[...]
    xhi = x.astype(BF16)
    xlo = (x - xhi.astype(F32)).astype(BF16)
    logits = (jnp.dot(xhi, whi_ref[...], preferred_element_type=F32)
              + jnp.dot(xhi, wlo_ref[...], preferred_element_type=F32)
              + jnp.dot(xlo, whi_ref[...], preferred_element_type=F32)) + b_ref[...]
    lane = _iota2(logits.shape, 1).astype(F32)
    logits = jnp.where(lane < N_EXPERTS, logits, -jnp.inf)
    m1 = jnp.max(logits, -1, keepdims=True)
    i1 = jnp.min(jnp.where(logits == m1, lane, 128.0), -1, keepdims=True)
    rest = jnp.where(lane == i1, -jnp.inf, logits)
    m2 = jnp.max(rest, -1, keepdims=True)
    i2 = jnp.min(jnp.where(rest == m2, lane, 128.0), -1, keepdims=True)
    e = jnp.exp(m2 - m1)
    g1 = 1.0 / (1.0 + e)
    g2 = e / (1.0 + e)
    out = jnp.where(lane == 0, i1, 0.0)
    out = jnp.where(lane == 1, i2, out)
    out = jnp.where(lane == 2, g1, out)
    out = jnp.where(lane == 3, g2, out)
    o_ref[...] = out


def _router_call(x, w_router, b_router, tm):
    S, D = x.shape
    wp = jnp.pad(w_router.astype(F32), ((0, 0), (0, 128 - N_EXPERTS)))
    whi = wp.astype(BF16)
    wlo = (wp - whi.astype(F32)).astype(BF16)
    bp = jnp.pad(b_router.astype(F32), (0, 128 - N_EXPERTS)).reshape(1, 128)
    const = lambda shape: pl.BlockSpec(shape, lambda i: (0, 0))
    return pl.pallas_call(
        _router_kernel,
        grid=(S // tm,),
        in_specs=[pl.BlockSpec((tm, D), lambda i: (i, 0)), const((D, 128)), const((D, 128)), const((1, 128))],
        out_specs=pl.BlockSpec((tm, 128), lambda i: (i, 0)),
        out_shape=jax.ShapeDtypeStruct((S, 128), F32),
        compiler_params=_cparams(("parallel",)),
        name="router",
    )(x, whi, wlo, bp)


def _row_copy(src_hbm, dst_ref, src_row, dst_row, sem):
    return pltpu.make_async_copy(src_hbm.at[pl.ds(src_row, 1)], dst_ref.at[pl.ds(dst_row, 1)], sem)


def _gather_rows_kernel(idx_ref, x_hbm, o_ref, sem):
    rows = o_ref.shape[0]
    base = pl.program_id(0) * rows

    def start(r, c):
        _row_copy(x_hbm, o_ref, idx_ref[base + r], r, sem).start()
        return c

    lax.fori_loop(0, rows, start, 0, unroll=8)
    pltpu.make_async_copy(x_hbm.at[pl.ds(0, rows)], o_ref, sem).wait()


def _gather_rows(x, idx, rows):
    n = idx.shape[0]
    D = x.shape[1]
    return pl.pallas_call(
        _gather_rows_kernel,
        grid_spec=pltpu.PrefetchScalarGridSpec(
            num_scalar_prefetch=1,
            grid=(n // rows,),
            in_specs=[pl.BlockSpec(memory_space=pl.ANY)],
            out_specs=pl.BlockSpec((rows, D), lambda i, idx_ref: (i, 0)),
            scratch_shapes=[pltpu.SemaphoreType.DMA(())]),
        out_shape=jax.ShapeDtypeStruct((n, D), x.dtype),
        compiler_params=_cparams(("arbitrary",)),
        name="moe_gather",
    )(idx, x)


def _moe_ffn_kernel(be_ref, nu_ref, x_ref, wg_ref, wu_ref, wd_ref, o_ref, acc_ref):
    i = pl.program_id(0)
    j = pl.program_id(1)

    @pl.when(i < nu_ref[0])
    def _():
        xb = x_ref[...].astype(BF16)
        part = _dot(_silu(_dot(xb, wg_ref[...])) * _dot(xb, wu_ref[...]), wd_ref[...])

        @pl.when(j == 0)
        def _():
            acc_ref[...] = part

        @pl.when(j > 0)
        def _():
            acc_ref[...] += part

    last = j == pl.num_programs(1) - 1

    @pl.when(last & (i < nu_ref[0]))
    def _():
        o_ref[...] = acc_ref[...]

    @pl.when(last & (i >= nu_ref[0]))
    def _():
        o_ref[...] = jnp.zeros_like(o_ref)


def _moe_ffn_call(x_rows, block_e, n_used, wg, wu, wd, tf):
    n, D = x_rows.shape
    F = wg.shape[2]
    R = MOE_ROWS
    return pl.pallas_call(
        _moe_ffn_kernel,
        grid_spec=pltpu.PrefetchScalarGridSpec(
            num_scalar_prefetch=2,
            grid=(n // R, F // tf),
            in_specs=[pl.BlockSpec((R, D), lambda i, j, be, nu: (i, 0)),
                      pl.BlockSpec((None, D, tf), lambda i, j, be, nu: (be[i], 0, j)),
                      pl.BlockSpec((None, D, tf), lambda i, j, be, nu: (be[i], 0, j)),
                      pl.BlockSpec((None, tf, D), lambda i, j, be, nu: (be[i], j, 0))],
            out_specs=pl.BlockSpec((R, D), lambda i, j, be, nu: (i, 0)),
            scratch_shapes=[pltpu.VMEM((R, D), F32)]),
        out_shape=jax.ShapeDtypeStruct((n, D), F32),
        compiler_params=_cparams(("arbitrary", "arbitrary")),
        name="moe_ffn",
    )(block_e, n_used, x_rows, wg, wu, wd)


def _moe_combine_kernel(d_ref, y_hbm, x_ref, r_ref, g_ref, b_ref, o_ref, y1_ref, y2_ref, sem):
    tm = x_ref.shape[0]
    base = pl.program_id(0) * tm

    def start(r, c):
        _row_copy(y_hbm, y1_ref, d_ref[2 * (base + r)], r, sem.at[0]).start()
        _row_copy(y_hbm, y2_ref, d_ref[2 * (base + r) + 1], r, sem.at[1]).start()
        return c

    lax.fori_loop(0, tm, start, 0, unroll=8)
    pltpu.make_async_copy(y_hbm.at[pl.ds(0, tm)], y1_ref, sem.at[0]).wait()
    pltpu.make_async_copy(y_hbm.at[pl.ds(0, tm)], y2_ref, sem.at[1]).wait()
    y = r_ref[:, 2:3] * y1_ref[...] + r_ref[:, 3:4] * y2_ref[...]
    o_ref[...] = _layer_norm_rows(DN_ALPHA * x_ref[...] + y, g_ref[...], b_ref[...])


def _moe_combine_call(dest, y_rows, x, routed, g, b, tm):
    S, D = x.shape
    return pl.pallas_call(
        _moe_combine_kernel,
        grid_spec=pltpu.PrefetchScalarGridSpec(
            num_scalar_prefetch=1,
            grid=(S // tm,),
            in_specs=[pl.BlockSpec(memory_space=pl.ANY),
                      pl.BlockSpec((tm, D), lambda i, d: (i, 0)),
                      pl.BlockSpec((tm, 128), lambda i, d: (i, 0)),
                      pl.BlockSpec((1, D), lambda i, d: (0, 0)),
                      pl.BlockSpec((1, D), lambda i, d: (0, 0))],
            out_specs=pl.BlockSpec((tm, D), lambda i, d: (i, 0)),
            scratch_shapes=[pltpu.VMEM((tm, D), F32), pltpu.VMEM((tm, D), F32), pltpu.SemaphoreType.DMA((2,))]),
        out_shape=jax.ShapeDtypeStruct((S, D), F32),
        compiler_params=_cparams(("arbitrary",)),
        name="moe_combine",
    )(dest, y_rows, x, routed, g.reshape(1, D), b.reshape(1, D))


def _moe_sublayer(x, w_router, b_router, wg, wu, wd, g, b):
    T = x.shape[0]
    R = MOE_ROWS
    routed = _router_call(x, w_router, b_router, min(512, T))
    top_e = routed[:, 0:TOP_K].astype(jnp.int32)
    tok_oh = jnp.sum((top_e[:, :, None] == jnp.arange(N_EXPERTS)[None, None, :]).astype(jnp.int32), axis=1)
    counts = jnp.sum(tok_oh, axis=0)
    rank = jnp.cumsum(tok_oh, axis=0) - tok_oh
    padded = (counts + R - 1) // R * R
    pend = jnp.cumsum(padded)
    pstart = pend - padded
    dest = pstart[top_e] + jnp.take_along_axis(rank, top_e, axis=1)
    n_rows = (T * TOP_K + N_EXPERTS * (R - 1)) // R * R
    n_blocks = n_rows // R
    row_tok = jnp.zeros((n_rows,), jnp.int32).at[dest.reshape(-1)].set(
        jnp.repeat(jnp.arange(T, dtype=jnp.int32), TOP_K))
    block_first_row = jnp.arange(n_blocks, dtype=jnp.int32) * R
    block_e = jnp.minimum(jnp.sum((pend[None, :] <= block_first_row[:, None]).astype(jnp.int32), axis=1),
                          N_EXPERTS - 1)
    n_used = (pend[-1] // R).astype(jnp.int32).reshape(1)
    x_rows = _gather_rows(x, row_tok, R)
    y_rows = _moe_ffn_call(x_rows, block_e, n_used, wg, wu, wd, EXPERT_FF // 2)
    return _moe_combine_call(dest.reshape(-1).astype(jnp.int32), y_rows, x, routed, g, b, min(256, T))


GLA_IN = 2 * GLA_QK + 2 * GLA_V + GLA_LR


def _gla_rwkv_sublayer(x, w_in, gla_wa2, gla_ba, gla_norm, mu, w0, w2, a0, a2, g2, k_k, k_a, r_k, ln_w, ln_b,
                       w_out, ln_g, ln_bias):
    S = x.shape[0]
    tm = min(512, S)
    lr_pad = 128 - GLA_LR
    w_gla = jnp.pad(w_in[:, :GLA_IN], ((0, 0), (0, lr_pad))).astype(BF16)
    w_rwkv = w_in[:, GLA_IN:].astype(BF16)
    p_gla = _matmul(x, w_gla, F32, tm, GLA_COLS)
    p_rwkv = _matmul(x, w_rwkv, F32, tm, RWKV_COLS)
    wa2p = jnp.pad(gla_wa2, ((0, lr_pad), (0, 0))).astype(BF16)
    o_gla = _gla_call(p_gla, wa2p, gla_ba, gla_norm)
    o_rwkv = _rwkv_call(p_rwkv, mu, w0, w2, a0, a2, g2, k_k, k_a, r_k, ln_w, ln_b)
    return _matmul_ln([o_gla, o_rwkv], [w_out[:GLA_V].astype(BF16), w_out[GLA_V:].astype(BF16)], x,
                      ln_g, ln_bias, tm)


def _xattn_sublayer(x, mem, wq, wk, wv, wo, ln_g, ln_bias):
    M = mem.shape[0]
    k_mem = _matmul(mem, wk.astype(BF16), BF16, M, D_MODEL)
    v_mem = _matmul(mem, wv.astype(BF16), BF16, M, D_MODEL)
    wq_scaled = (wq * XATTN_HD ** -0.5).astype(BF16)
    return _xattn_call(x, wq_scaled, k_mem, v_mem, wo.astype(BF16), ln_g, ln_bias, min(512, x.shape[0]))


def _ssd_moba_sublayer(x, w_in, conv_w, conv_b, dt_bias, a_log, d_skip, ssd_norm, w_out, ln_g, ln_bias):
    S = x.shape[0]
    tm = min(512, S)
    nb = S // MOBA_BLOCK
    o_dt = SSD_COLS
    o_q = o_dt + SSD_HEADS
    w_ssd = w_in[:, :o_dt].astype(BF16)
    w_dt = w_in[:, o_dt:o_q]
    w_q = w_in[:, o_q:o_q + MOBA_D]
    w_k = w_in[:, o_q + MOBA_D:o_q + 2 * MOBA_D]
    w_v = w_in[:, o_q + 2 * MOBA_D:]
    p_ssd = _matmul(x, w_ssd, F32, tm, SSD_COLS // 2)
    p_dt = _matmul(x, jnp.pad(w_dt, ((0, 0), (0, 128 - SSD_HEADS))).astype(BF16), F32, tm, 128)
    dtt = _matmul_t(w_dt.T.astype(BF16), x, F32, tm)
    o_ssd = _ssd_call(p_ssd, p_dt, dtt, conv_w, conv_b, dt_bias, a_log, d_skip, ssd_norm)
    qt = _matmul_t((w_q.T * MOBA_HD ** -0.5).astype(BF16), x, BF16, tm)
    k4, kmean = _kproj_call(x, w_k.astype(BF16))
    vt4 = _vproj_call(w_v.T.astype(BF16), x)
    kmean_h = kmean.reshape(nb, MOBA_HEADS, MOBA_HD).transpose(1, 0, 2)
    ot_moba = _moba_call(qt, k4, vt4, kmean_h)
    return _matmul_ln([o_ssd, ot_moba], [w_out[:SSD_INNER].astype(BF16), w_out[SSD_INNER:].astype(BF16)], x,
                      ln_g, ln_bias, tm, transposed=(False, True))


def kernel(x, mem, l0_w_in, l0_gla_wa2, l0_gla_ba, l0_gla_norm, l0_rwkv_mu, l0_rwkv_w0, l0_rwkv_w2, l0_rwkv_a0, l0_rwkv_a2, l0_rwkv_g2, l0_rwkv_kk, l0_rwkv_ka, l0_rwkv_rk, l0_rwkv_lnw, l0_rwkv_lnb, l0_w_out, l0_ln1_g, l0_ln1_b, l0_xq, l0_xk, l0_xv, l0_xo, l0_ln2_g, l0_ln2_b, l0_ffn_wg, l0_ffn_wu, l0_ffn_wd, l0_ln3_g, l0_ln3_b, l1_w_in, l1_conv_w, l1_conv_b, l1_dt_bias, l1_a_log, l1_d_skip, l1_ssd_norm, l1_w_out, l1_ln1_g, l1_ln1_b, l1_xq, l1_xk, l1_xv, l1_xo, l1_ln2_g, l1_ln2_b, l1_router, l1_router_b, l1_exp_wg, l1_exp_wu, l1_exp_wd, l1_ln3_g, l1_ln3_b):
    x2 = x.reshape(-1, D_MODEL)
    mem2 = mem.reshape(-1, D_MODEL)
    x2 = _gla_rwkv_sublayer(x2, l0_w_in, l0_gla_wa2, l0_gla_ba, l0_gla_norm, l0_rwkv_mu, l0_rwkv_w0, l0_rwkv_w2,
                            l0_rwkv_a0, l0_rwkv_a2, l0_rwkv_g2, l0_rwkv_kk, l0_rwkv_ka, l0_rwkv_rk, l0_rwkv_lnw,
                            l0_rwkv_lnb, l0_w_out, l0_ln1_g, l0_ln1_b)
    x2 = _xattn_sublayer(x2, mem2, l0_xq, l0_xk, l0_xv, l0_xo, l0_ln2_g, l0_ln2_b)
    tm = min(512, x2.shape[0])
    x2 = _ffn_call(x2, l0_ffn_wg.astype(BF16), l0_ffn_wu.astype(BF16), l0_ffn_wd.astype(BF16),
                   l0_ln3_g, l0_ln3_b, tm, l0_ffn_wg.shape[1] // 2)
    x2 = _ssd_moba_sublayer(x2, l1_w_in, l1_conv_w, l1_conv_b, l1_dt_bias, l1_a_log, l1_d_skip, l1_ssd_norm,
                            l1_w_out, l1_ln1_g, l1_ln1_b)
    x2 = _xattn_sublayer(x2, mem2, l1_xq, l1_xk, l1_xv, l1_xo, l1_ln2_g, l1_ln2_b)
    x2 = _moe_sublayer(x2, l1_router, l1_router_b, l1_exp_wg.astype(BF16), l1_exp_wu.astype(BF16),
                       l1_exp_wd.astype(BF16), l1_ln3_g, l1_ln3_b)
    return x2.reshape(x.shape)
```

```python
import functools
import math

import jax
import jax.numpy as jnp
from jax import lax
from jax.experimental import pallas as pl
from jax.experimental.pallas import tpu as pltpu

BF16 = jnp.bfloat16
F32 = jnp.float32

D_MODEL = 1024
LN_EPS = 1e-5
DEPTH = 2
DN_ALPHA = (2 * DEPTH) ** 0.25

GLA_HEADS, GLA_DK, GLA_DV, GLA_CHUNK = 4, 64, 128, 64
GLA_QK, GLA_V, GLA_LR, GLA_TAU = 256, 512, 16, 16.0
GLA_COLS = 2 * GLA_QK + 2 * GLA_V + 128

RWKV_HEADS, RWKV_HD, RWKV_D, RWKV_CHUNK = 8, 64, 512, 64
RWKV_COLS = 1792
RWKV_DECAY_SCALE = math.exp(-0.5)
RWKV_GN_EPS = 64e-5

VMEM_LIMIT = 56 * 1024 * 1024


def _cparams(sem):
    return pltpu.CompilerParams(dimension_semantics=sem, vmem_limit_bytes=VMEM_LIMIT)


def _dot(a, b):
    return jnp.dot(a.astype(BF16), b.astype(BF16), preferred_element_type=F32)


def _dot_nt(a, b):
    return lax.dot_general(a.astype(BF16), b.astype(BF16), (((1,), (1,)), ((), ())), preferred_element_type=F32)


def _dot_tn(a, b):
    return lax.dot_general(a.astype(BF16), b.astype(BF16), (((0,), (0,)), ((), ())), preferred_element_type=F32)


def _split3(x):
    hi = x.astype(BF16)
    r1 = x - hi.astype(F32)
    mid = r1.astype(BF16)
    lo = (r1 - mid.astype(F32)).astype(BF16)
    return hi, mid, lo


def _dot_exact_lhs(m, x):
    mb = m.astype(BF16)
    hi, mid, lo = _split3(x)
    return (jnp.dot(mb, hi, preferred_element_type=F32) + jnp.dot(mb, mid, preferred_element_type=F32)
            + jnp.dot(mb, lo, preferred_element_type=F32))


def _dot_exact_rhs(x, m):
    mb = m.astype(BF16)
    hi, mid, lo = _split3(x)
    return (jnp.dot(hi, mb, preferred_element_type=F32) + jnp.dot(mid, mb, preferred_element_type=F32)
            + jnp.dot(lo, mb, preferred_element_type=F32))


def _sigmoid(x):
    return 1.0 / (1.0 + jnp.exp(-x))


def _silu(x):
    return x * _sigmoid(x)


def _iota2(shape, axis):
    return lax.broadcasted_iota(jnp.int32, shape, axis)


def _chunk_tril(n, chunk):
    r = _iota2((n, n), 0)
    c = _iota2((n, n), 1)
    return jnp.where((c <= r) & ((r // chunk) == (c // chunk)), 1.0, 0.0)


def _head_block(n, width, value):
    r = _iota2((n, n), 0)
    c = _iota2((n, n), 1)
    return jnp.where((r // width) == (c // width), value, 0.0)


def _mm_kernel(x_ref, w_ref, o_ref):
    o_ref[...] = _dot(x_ref[...], w_ref[...]).astype(o_ref.dtype)


def _matmul(x, w, out_dtype, tm, tn):
    S, K = x.shape
    N = w.shape[1]
    return pl.pallas_call(
        _mm_kernel,
        grid=(S // tm, N // tn),
        in_specs=[pl.BlockSpec((tm, K), lambda i, j: (i, 0)),
                  pl.BlockSpec((K, tn), lambda i, j: (0, j))],
        out_specs=pl.BlockSpec((tm, tn), lambda i, j: (i, j)),
        out_shape=jax.ShapeDtypeStruct((S, N), out_dtype),
        compiler_params=_cparams(("parallel", "arbitrary")),
        name="matmul",
    )(x, w)


def _mm_t_kernel(wt_ref, x_ref, o_ref):
    o_ref[...] = _dot_nt(wt_ref[...], x_ref[...]).astype(o_ref.dtype)


def _matmul_t(wt, x, out_dtype, tm):
    S, K = x.shape
    N = wt.shape[0]
    return pl.pallas_call(
        _mm_t_kernel,
        grid=(S // tm,),
        in_specs=[pl.BlockSpec((N, K), lambda i: (0, 0)),
                  pl.BlockSpec((tm, K), lambda i: (i, 0))],
        out_specs=pl.BlockSpec((N, tm), lambda i: (0, i)),
        out_shape=jax.ShapeDtypeStruct((N, S), out_dtype),
        compiler_params=_cparams(("parallel",)),
        name="matmul_t",
    )(wt, x)


def _layer_norm_rows(y, g, b):
    mu = jnp.mean(y, -1, keepdims=True)
    d = y - mu
    var = jnp.mean(d * d, -1, keepdims=True)
    return d * lax.rsqrt(var + LN_EPS) * g + b


def _mm_ln_kernel(transposed, *refs):
    n_in = len(transposed)
    a_refs = refs[:n_in]
    w_refs = refs[n_in:2 * n_in]
    x_ref, g_ref, b_ref, o_ref = refs[2 * n_in:]
    acc = DN_ALPHA * x_ref[...]
    for a_ref, w_ref, tr in zip(a_refs, w_refs, transposed):
        acc = acc + (_dot_tn if tr else _dot)(a_ref[...], w_ref[...])
    o_ref[...] = _layer_norm_rows(acc, g_ref[...], b_ref[...])


def _matmul_ln(a_list, w_list, x, g, b, tm, transposed=None):
    S, D = x.shape
    transposed = tuple(transposed or (False,) * len(a_list))
    in_specs = ([pl.BlockSpec((a.shape[0], tm), lambda i: (0, i)) if tr else
                 pl.BlockSpec((tm, a.shape[1]), lambda i: (i, 0)) for a, tr in zip(a_list, transposed)]
                + [pl.BlockSpec(w.shape, lambda i: (0, 0)) for w in w_list]
                + [pl.BlockSpec((tm, D), lambda i: (i, 0)),
                   pl.BlockSpec((1, D), lambda i: (0, 0)),
                   pl.BlockSpec((1, D), lambda i: (0, 0))])
    return pl.pallas_call(
        functools.partial(_mm_ln_kernel, transposed),
        grid=(S // tm,),
        in_specs=in_specs,
        out_specs=pl.BlockSpec((tm, D), lambda i: (i, 0)),
        out_shape=jax.ShapeDtypeStruct((S, D), F32),
        compiler_params=_cparams(("parallel",)),
        name="matmul_ln",
    )(*a_list, *w_list, x, g.reshape(1, D), b.reshape(1, D))


def _gla_kernel(p_ref, wa2_ref, ba_ref, ng_ref, o_ref, st_ref, o_scr):
    C, H, dk, dv = GLA_CHUNK, GLA_HEADS, GLA_DK, GLA_DV
    tb = p_ref.shape[0]

    @pl.when(pl.program_id(0) == 0)
    def _():
        st_ref[...] = jnp.zeros_like(st_ref)

    z = _dot(p_ref[:, 2 * GLA_QK + 2 * GLA_V:], wa2_ref[...]) + ba_ref[...]
    log_a = -(jnp.maximum(-z, 0.0) + jnp.log(1.0 + jnp.exp(-jnp.abs(z)))) / GLA_TAU
    b = _dot_exact_lhs(_chunk_tril(tb, C), log_a)
    causal = _iota2((C, C), 1) <= _iota2((C, C), 0)

    nc = tb // C
    q_h, k_h, ke_h, v_h, dec_h = [], [], [], [], []
    for c in range(nc):
        rows = slice(c * C, (c + 1) * C)
        b_c = b[rows]
        b_last = b_c[C - 1:C]
        q_dec = p_ref[rows, 0:GLA_QK] * (dk ** -0.5) * jnp.exp(b_c)
        k_c = p_ref[rows, GLA_QK:2 * GLA_QK]
        k_dec = k_c * jnp.exp(-b_c)
        k_end = k_c * jnp.exp(b_last - b_c)
        decay = jnp.exp(b_last)
        for h in range(H):
            ks = slice(h * dk, (h + 1) * dk)
            q_h.append(q_dec[:, ks])
            k_h.append(k_dec[:, ks])
            ke_h.append(k_end[:, ks])
            dec_h.append(decay[:, ks])
            v_h.append(p_ref[rows, 2 * GLA_QK + h * dv:2 * GLA_QK + (h + 1) * dv])
    n = nc * H
    attn = [jnp.where(causal, _dot_nt(q_h[i], k_h[i]), 0.0) for i in range(n)]
    kv = [_dot_tn(v_h[i], ke_h[i]) for i in range(n)]
    intra = [_dot(attn[i], v_h[i]) for i in range(n)]
    state = [st_ref[:, h * dk:(h + 1) * dk] for h in range(H)]
    entering = []
    for i in range(n):
        entering.append(state[i % H])
        state[i % H] = state[i % H] * dec_h[i] + kv[i]
    for i in range(n):
        c, h = divmod(i, H)
        o_scr[c * C:(c + 1) * C, h * dv:(h + 1) * dv] = intra[i] + _dot_nt(q_h[i], entering[i])
    for h in range(H):
        st_ref[:, h * dk:(h + 1) * dk] = state[h]

    for h in range(H):
        vs = slice(h * dv, (h + 1) * dv)
        o_h = o_scr[:, vs]
        g_h = p_ref[:, 2 * GLA_QK + GLA_V + h * dv:2 * GLA_QK + GLA_V + (h + 1) * dv]
        o_h = o_h * lax.rsqrt(jnp.mean(o_h * o_h, -1, keepdims=True) + 1e-5) * ng_ref[:, vs]
        o_ref[:, vs] = (o_h * _silu(g_h)).astype(o_ref.dtype)


def _gla_call(p_gla, wa2p, ba, norm_g, tb=256):
    S = p_gla.shape[0]
    return pl.pallas_call(
        _gla_kernel,
        grid=(S // tb,),
        in_specs=[pl.BlockSpec((tb, GLA_COLS), lambda i: (i, 0)),
                  pl.BlockSpec((128, GLA_QK), lambda i: (0, 0)),
                  pl.BlockSpec((1, GLA_QK), lambda i: (0, 0)),
                  pl.BlockSpec((1, GLA_V), lambda i: (0, 0))],
        out_specs=pl.BlockSpec((tb, GLA_V), lambda i: (i, 0)),
        out_shape=jax.ShapeDtypeStruct((S, GLA_V), BF16),
        scratch_shapes=[pltpu.VMEM((GLA_DV, GLA_QK), F32), pltpu.VMEM((tb, GLA_V), F32)],
        compiler_params=_cparams(("arbitrary",)),
        name="gla",
    )(p_gla, wa2p, ba.reshape(1, GLA_QK), norm_g.reshape(1, GLA_V))


def _rwkv_kernel(p_ref, mu_ref, w0_ref, w2_ref, a0_ref, a2_ref, g2_ref, kk_ref, ka_ref, rk_ref, lnw_ref, lnb_ref,
                 o_ref, prev_ref, h_ref, o_scr):
    C, H, N, D = RWKV_CHUNK, RWKV_HEADS, RWKV_HD, RWKV_D
    tb = p_ref.shape[0]
    first = pl.program_id(0) == 0

    @pl.when(first)
    def _():
        prev_ref[...] = jnp.zeros_like(prev_ref)
        h_ref[...] = jnp.zeros_like(h_ref)

    p = p_ref[...]
    shifted = jnp.where(_iota2(p.shape, 0) == 0, prev_ref[...], pltpu.roll(p, 1, 0))
    prev_ref[...] = p[tb - 1:tb]
    p = p + mu_ref[...] * (shifted - p)
    r = p[:, 0:D]
    k = p[:, D:2 * D]
    v = p[:, 2 * D:3 * D]
    xw = p[:, 3 * D:3 * D + 64]
    xa = p[:, 3 * D + 64:3 * D + 128]
    xg = p[:, 3 * D + 128:3 * D + 256]
    lw = -RWKV_DECAY_SCALE * _sigmoid(w0_ref[...] + _dot(jnp.tanh(xw), w2_ref[...]))
    a = _sigmoid(a0_ref[...] + _dot(xa, a2_ref[...]))
    g = _dot(_sigmoid(xg), g2_ref[...])
    head_ones = _head_block(D, N, 1.0)
    kk = k * kk_ref[...]
    kk = kk * lax.rsqrt(jnp.maximum(_dot_exact_rhs(kk * kk, head_ones), 1e-24))
    k = k * (1.0 + (a - 1.0) * ka_ref[...])
    pv = -kk * a
    cw = _dot_exact_lhs(_chunk_tril(tb, C), lw)
    cwx = cw - lw

    gi = _iota2((2 * C, 2 * C), 0)
    gj = _iota2((2 * C, 2 * C), 1) % C
    gram_mask = ((gi < C) & (gj < gi)) | ((gi >= C) & (gj <= gi - C))
    eye = _iota2((C, C), 0) == _iota2((C, C), 1)
    eye_f = jnp.where(eye, 1.0, 0.0)

    nc = tb // C
    items = [(c, h) for c in range(nc) for h in range(H)]
    xs, ys, pk_e, b_h, v_h, r_h, g_h = [], [], [], [], [], [], []
    for c in range(nc):
        rows = slice(c * C, (c + 1) * C)
        cw_c = cw[rows]
        cw_end = cw_c[C - 1:C]
        e_pos = jnp.exp(cw_c)
        e_neg = jnp.exp(-cw_c)
        e_end = jnp.exp(cw_end - cw_c)
        r_t = r[rows] * e_pos
        b_t = kk[rows] * jnp.exp(cwx[rows])
        p_t = pv[rows] * e_neg
        k_t = k[rows] * e_neg
        p_e = pv[rows] * e_end
        k_e = k[rows] * e_end
        g_end = jnp.exp(cw_end)
        v_c = v[rows]
        for h in range(H):
            hs = slice(h * N, (h + 1) * N)
            xs.append(jnp.concatenate([b_t[:, hs], r_t[:, hs]], axis=0))
            ys.append(jnp.concatenate([p_t[:, hs], k_t[:, hs]], axis=0))
            pk_e.append(jnp.concatenate([p_e[:, hs], k_e[:, hs]], axis=0))
            b_h.append(b_t[:, hs])
            v_h.append(v_c[:, hs])
            r_h.append(r_t[:, hs])
            g_h.append(g_end[:, hs])
    n = len(items)
    grams = [jnp.where(gram_mask, _dot_nt(xs[i], ys[i]), 0.0) for i in range(n)]
    l_p = [g[0:C, 0:C] for g in grams]
    m_pk = [g[C:2 * C, :] for g in grams]
    lkv = [_dot(grams[i][0:C, C:2 * C], v_h[i]) for i in range(n)]
    x = [_dot(lp, lp) for lp in l_p]
    t = [eye_f + lp for lp in l_p]
    for _ in range(4):
        tx = [_dot(jnp.concatenate([t[i], x[i]], axis=0), x[i]) for i in range(n)]
        t = [t[i] + tx[i][0:C] for i in range(n)]
        x = [tx[i][C:2 * C] for i in range(n)]
    t = [t[i] + _dot(t[i], x[i]) for i in range(n)]
    wu = [_dot(t[i], jnp.concatenate([b_h[i], lkv[i]], axis=1)) for i in range(n)]
    rhs = [jnp.concatenate([wu[i], jnp.concatenate([jnp.zeros((C, N), F32), v_h[i]], axis=1)], axis=0)
           for i in range(n)]
    az = [_dot_tn(pk_e[i], rhs[i]) for i in range(n)]
    qo = [_dot(m_pk[i], rhs[i]) for i in range(n)]
    state = [h_ref[h] for h in range(H)]
    for i, (c, h) in enumerate(items):
        a_mat = az[i][:, 0:N] + jnp.where(eye, g_h[i], 0.0)
        q_mat = qo[i][:, 0:N] + r_h[i]
        oh = _dot(jnp.concatenate([q_mat, a_mat], axis=0), state[h])
        o_scr[c * C:(c + 1) * C, h * N:(h + 1) * N] = oh[0:C] + qo[i][:, N:2 * N]
        state[h] = oh[C:C + N] + az[i][:, N:2 * N]
    for h in range(H):
        h_ref[h] = state[h]

    o = o_scr[...]
    head_mean = _head_block(D, N, 1.0 / N)
    mean = _dot_exact_rhs(o, head_mean)
    d = o - mean
    var = _dot_exact_rhs(d * d, head_mean)
    o = d * lax.rsqrt(var + RWKV_GN_EPS) * lnw_ref[...] + lnb_ref[...]
    bonus = _dot_exact_rhs(r * k * rk_ref[...], head_ones) * v
    o_ref[...] = ((o + bonus) * g).astype(o_ref.dtype)


def _rwkv_call(p_rwkv, mu, w0, w2, a0, a2, g2, k_k, k_a, r_k, ln_w, ln_b, tb=256):
    S = p_rwkv.shape[0]
    D = RWKV_D
    row = lambda t: t.reshape(1, -1).astype(F32)
    full = lambda shape: pl.BlockSpec(shape, lambda i: tuple(0 for _ in shape))
    return pl.pallas_call(
        _rwkv_kernel,
        grid=(S // tb,),
        in_specs=[pl.BlockSpec((tb, RWKV_COLS), lambda i: (i, 0)),
                  full((1, RWKV_COLS)), full((1, D)), full((64, D)), full((1, D)), full((64, D)), full((128, D)),
                  full((1, D)), full((1, D)), full((1, D)), full((1, D)), full((1, D))],
        out_specs=pl.BlockSpec((tb, D), lambda i: (i, 0)),
        out_shape=jax.ShapeDtypeStruct((S, D), BF16),
        scratch_shapes=[pltpu.VMEM((1, RWKV_COLS), F32),
                        pltpu.VMEM((RWKV_HEADS, RWKV_HD, RWKV_HD), F32),
                        pltpu.VMEM((tb, D), F32)],
        compiler_params=_cparams(("arbitrary",)),
        name="rwkv7",
    )(p_rwkv, row(mu), row(w0), w2.astype(BF16), row(a0), a2.astype(BF16), g2.astype(BF16),
      row(k_k), row(k_a), row(r_k), row(ln_w), row(ln_b))


XATTN_HEADS, XATTN_HD = 4, 256


def _xattn_kernel(x_ref, wq_ref, k_ref, v_ref, wo_ref, g_ref, b_ref, o_ref):
    x = x_ref[...]
    q = _dot(x, wq_ref[...])
    outs = []
    for h in range(XATTN_HEADS):
        hs = slice(h * XATTN_HD, (h + 1) * XATTN_HD)
        s = _dot_nt(q[:, hs], k_ref[:, hs])
        e = jnp.exp(s - jnp.max(s, -1, keepdims=True))
        p = e / jnp.sum(e, -1, keepdims=True)
        outs.append(_dot(p, v_ref[:, hs]))
    o = jnp.concatenate(outs, axis=1)
    y = DN_ALPHA * x + _dot(o, wo_ref[...])
    o_ref[...] = _layer_norm_rows(y, g_ref[...], b_ref[...])


def _xattn_call(x, wq_scaled, k_mem, v_mem, wo, g, b, tm):
    S, D = x.shape
    M = k_mem.shape[0]
    const = lambda shape: pl.BlockSpec(shape, lambda i: (0, 0))
    return pl.pallas_call(
        _xattn_kernel,
        grid=(S // tm,),
        in_specs=[pl.BlockSpec((tm, D), lambda i: (i, 0)), const((D, D)), const((M, D)), const((M, D)),
                  const((D, D)), const((1, D)), const((1, D))],
        out_specs=pl.BlockSpec((tm, D), lambda i: (i, 0)),
        out_shape=jax.ShapeDtypeStruct((S, D), F32),
        compiler_params=_cparams(("parallel",)),
        name="xattn",
    )(x, wq_scaled, k_mem, v_mem, wo, g.reshape(1, D), b.reshape(1, D))


def _ffn_kernel(x_ref, wg_ref, wu_ref, wd_ref, g_ref, b_ref, o_ref, acc_ref):
    j = pl.program_id(1)
    x = x_ref[...]
    xb = x.astype(BF16)
    part = _dot(_silu(_dot(xb, wg_ref[...])) * _dot(xb, wu_ref[...]), wd_ref[...])

    @pl.when(j == 0)
    def _():
        acc_ref[...] = DN_ALPHA * x + part

    @pl.when(j > 0)
    def _():
        acc_ref[...] += part

    @pl.when(j == pl.num_programs(1) - 1)
    def _():
        o_ref[...] = _layer_norm_rows(acc_ref[...], g_ref[...], b_ref[...])


def _ffn_call(x, wg, wu, wd, g, b, tm, tf):
    S, D = x.shape
    F = wg.shape[1]
    return pl.pallas_call(
        _ffn_kernel,
        grid=(S // tm, F // tf),
        in_specs=[pl.BlockSpec((tm, D), lambda i, j: (i, 0)),
                  pl.BlockSpec((D, tf), lambda i, j: (0, j)),
                  pl.BlockSpec((D, tf), lambda i, j: (0, j)),
                  pl.BlockSpec((tf, D), lambda i, j: (j, 0)),
                  pl.BlockSpec((1, D), lambda i, j: (0, 0)),
                  pl.BlockSpec((1, D), lambda i, j: (0, 0))],
        out_specs=pl.BlockSpec((tm, D), lambda i, j: (i, 0)),
        out_shape=jax.ShapeDtypeStruct((S, D), F32),
        scratch_shapes=[pltpu.VMEM((tm, D), F32)],
        compiler_params=_cparams(("parallel", "arbitrary")),
        name="ffn",
    )(x, wg, wu, wd, g.reshape(1, D), b.reshape(1, D))


SSD_HD, SSD_HEADS, SSD_INNER, SSD_GROUPS, SSD_STATE = 64, 16, 1024, 2, 128
SSD_BC, SSD_CONV, SSD_CONV_CH, SSD_CHUNK = 256, 4, 1536, 128
SSD_COLS = SSD_INNER + SSD_CONV_CH
SSD_GW = SSD_INNER // SSD_GROUPS


def _softplus(x):
    return jnp.maximum(x, 0.0) + jnp.log(1.0 + jnp.exp(-jnp.abs(x)))


def _ssd_kernel(p_ref, dt_ref, dtt_ref, cw_ref, cb_ref, dtb_ref, dtbc_ref, a_ref, ac_ref, dsk_ref, ng_ref,
                o_ref, prev_ref, st_ref, y_scr):
    L, G, NS, HD = SSD_CHUNK, SSD_GROUPS, SSD_STATE, SSD_HD
    HG = SSD_HEADS // G

    @pl.when(pl.program_id(0) == 0)
    def _():
        prev_ref[...] = jnp.zeros_like(prev_ref)
        st_ref[...] = jnp.zeros_like(st_ref)

    cur = p_ref[:, SSD_INNER:]
    prev = prev_ref[...]
    row = _iota2(cur.shape, 0)
    conv = cur * cw_ref[SSD_CONV - 1:SSD_CONV, :] + cb_ref[...]
    for kk in range(1, SSD_CONV):
        shifted = jnp.where(row < kk, pltpu.roll(prev, kk, 0), pltpu.roll(cur, kk, 0))
        conv = conv + shifted * cw_ref[SSD_CONV - 1 - kk:SSD_CONV - kk, :]
    prev_ref[...] = cur
    xbc = _silu(conv)
    xs = xbc[:, :SSD_INNER]

    dt = _softplus(dt_ref[...] + dtb_ref[...])
    a_col = dt * a_ref[...]
    a_row = _softplus(dtt_ref[...] + dtbc_ref[...]) * ac_ref[...]
    li = _iota2((L, L), 0)
    lj = _iota2((L, L), 1)
    cs = _dot_exact_lhs(jnp.where(lj <= li, 1.0, 0.0), a_col)
    cs_row = _dot_exact_rhs(a_row, jnp.where(li <= lj, 1.0, 0.0))
    expand = jnp.where(_iota2((128, SSD_INNER), 1) // HD == _iota2((128, SSD_INNER), 0), 1.0, 0.0)
    dt_x = _dot_exact_rhs(dt, expand)
    cs_x = _dot_exact_rhs(cs, expand)
    cs_end = cs_x[L - 1:L]
    xd = xs * dt_x
    xd_dec = xd * jnp.exp(cs_end - cs_x)
    out_dec = jnp.exp(cs_x)
    chunk_dec = jnp.exp(cs_end)
    tril = lj <= li

    for g in range(G):
        gs = slice(g * SSD_GW, (g + 1) * SSD_GW)
        b_g = xbc[:, SSD_INNER + g * NS:SSD_INNER + (g + 1) * NS]
        c_g = xbc[:, SSD_INNER + SSD_BC + g * NS:SSD_INNER + SSD_BC + (g + 1) * NS]
        cb = _dot_nt(c_g, b_g)
        for j in range(HG):
            h = g * HG + j
            hs = slice(h * HD, (h + 1) * HD)
            seg = jnp.where(tril, jnp.exp(cs[:, h:h + 1] - cs_row[h:h + 1, :]), 0.0)
            y_scr[:, hs] = _dot(cb * seg, xd[:, hs])
        st = st_ref[g]
        y_off = _dot(c_g, st) * out_dec[:, gs]
        st_ref[g] = st * chunk_dec[:, gs] + _dot_tn(b_g, xd_dec[:, gs])
        y_scr[:, gs] = y_scr[:, gs] + y_off

    y = (y_scr[...] + dsk_ref[...] * xs) * _silu(p_ref[:, :SSD_INNER])
    for g in range(G):
        gs = slice(g * SSD_GW, (g + 1) * SSD_GW)
        y_g = y[:, gs]
        o_ref[:, gs] = (y_g * lax.rsqrt(jnp.mean(y_g * y_g, -1, keepdims=True) + 1e-5) * ng_ref[:, gs]).astype(o_ref.dtype)


def _ssd_call(p_ssd, p_dt, dtt, conv_w, conv_b, dt_bias, a_log, d_skip, norm_g):
    S = p_ssd.shape[0]
    L = SSD_CHUNK
    a_neg = -jnp.exp(a_log.astype(F32))
    pad = lambda t: jnp.pad(t.astype(F32), (0, 128 - SSD_HEADS)).reshape(1, 128)
    col = lambda t: t.astype(F32).reshape(SSD_HEADS, 1)
    const = lambda shape: pl.BlockSpec(shape, lambda i: (0, 0))
    return pl.pallas_call(
        _ssd_kernel,
        grid=(S // L,),
        in_specs=[pl.BlockSpec((L, SSD_COLS), lambda i: (i, 0)),
                  pl.BlockSpec((L, 128), lambda i: (i, 0)),
                  pl.BlockSpec((SSD_HEADS, L), lambda i: (0, i)),
                  const((SSD_CONV, SSD_CONV_CH)), const((1, SSD_CONV_CH)),
                  const((1, 128)), const((SSD_HEADS, 1)), const((1, 128)), const((SSD_HEADS, 1)),
                  const((1, SSD_INNER)), const((1, SSD_INNER))],
        out_specs=pl.BlockSpec((L, SSD_INNER), lambda i: (i, 0)),
        out_shape=jax.ShapeDtypeStruct((S, SSD_INNER), BF16),
        scratch_shapes=[pltpu.VMEM((L, SSD_CONV_CH), F32),
                        pltpu.VMEM((SSD_GROUPS, SSD_STATE, SSD_GW), F32),
                        pltpu.VMEM((L, SSD_INNER), F32)],
        compiler_params=_cparams(("arbitrary",)),
        name="ssd",
    )(p_ssd, p_dt, dtt, conv_w.astype(F32), conv_b.reshape(1, -1).astype(F32),
      pad(dt_bias), col(dt_bias), pad(a_neg), col(a_neg),
      jnp.repeat(d_skip.astype(F32), SSD_HD).reshape(1, -1), norm_g.reshape(1, -1).astype(F32))


MOBA_HD, MOBA_HEADS, MOBA_D, MOBA_BLOCK, MOBA_TOPK = 64, 8, 512, 256, 3
MOBA_GROUP = 4
NEG_BIG = -1e30
LOG2E = math.log2(math.e)


def _kproj_kernel(x_ref, w_ref, k_ref, kmean_ref):
    k = _dot(x_ref[...], w_ref[...])
    kmean_ref[0] = jnp.mean(k, 0, keepdims=True)
    shape = (MOBA_BLOCK, 2 * MOBA_HD)
    lane = _iota2(shape, 1)
    pos = jnp.where((lane == MOBA_HD) | (lane == MOBA_HD + 1), _iota2(shape, 0).astype(F32), 0.0).astype(k_ref.dtype)
    for h in range(MOBA_HEADS):
        k_ref[h, 0] = pos
        k_ref[h, 0, :, 0:MOBA_HD] = k[:, h * MOBA_HD:(h + 1) * MOBA_HD].astype(k_ref.dtype)


def _kproj_call(x, wk):
    S, D = x.shape
    nb = S // MOBA_BLOCK
    return pl.pallas_call(
        _kproj_kernel,
        grid=(nb,),
        in_specs=[pl.BlockSpec((MOBA_BLOCK, D), lambda i: (i, 0)), pl.BlockSpec((D, MOBA_D), lambda i: (0, 0))],
        out_specs=[pl.BlockSpec((MOBA_HEADS, 1, MOBA_BLOCK, 2 * MOBA_HD), lambda i: (0, i, 0, 0)),
                   pl.BlockSpec((1, 1, MOBA_D), lambda i: (i, 0, 0))],
        out_shape=[jax.ShapeDtypeStruct((MOBA_HEADS, nb, MOBA_BLOCK, 2 * MOBA_HD), BF16),
                   jax.ShapeDtypeStruct((nb, 1, MOBA_D), F32)],
        compiler_params=_cparams(("parallel",)),
        name="moba_kproj",
    )(x, wk)


MOBA_VROWS = MOBA_HD + 16


def _vproj_kernel(wt_ref, x_ref, v_ref):
    vt = _dot_nt(wt_ref[...], x_ref[...])
    extra = jnp.where(_iota2((MOBA_VROWS - MOBA_HD, MOBA_BLOCK), 0) == 0, 1.0, 0.0)
    for h in range(MOBA_HEADS):
        v_ref[h, 0] = jnp.concatenate([vt[h * MOBA_HD:(h + 1) * MOBA_HD], extra], axis=0).astype(v_ref.dtype)


def _vproj_call(wvt, x):
    S, D = x.shape
    nb = S // MOBA_BLOCK
    return pl.pallas_call(
        _vproj_kernel,
        grid=(nb,),
        in_specs=[pl.BlockSpec((MOBA_D, D), lambda i: (0, 0)), pl.BlockSpec((MOBA_BLOCK, D), lambda i: (i, 0))],
        out_specs=pl.BlockSpec((MOBA_HEADS, 1, MOBA_VROWS, MOBA_BLOCK), lambda i: (0, i, 0, 0)),
        out_shape=jax.ShapeDtypeStruct((MOBA_HEADS, nb, MOBA_VROWS, MOBA_BLOCK), BF16),
        compiler_params=_cparams(("parallel",)),
        name="moba_vproj",
    )(wvt, x)


def _moba_kernel(qt_ref, k_ref, vt_ref, kmean_ref, o_ref, sel_ref, s0_ref, s1_ref, p0_ref, p1_ref):
    BS, HD = MOBA_BLOCK, MOBA_HD
    h = pl.program_id(0)
    i = pl.program_id(1)
    nb = k_ref.shape[0]
    qt = qt_ref[...]
    slope = LOG2E * jnp.exp2(jnp.zeros((1, BS), F32) - (h + 1).astype(F32) * (8.0 / MOBA_HEADS))
    slope_hi = slope.astype(qt.dtype).astype(F32)
    slope_lo = slope - slope_hi
    row = _iota2((HD, BS), 0)
    qt_ext = jnp.concatenate(
        [qt, jnp.where(row == 0, slope_hi, jnp.where(row == 1, slope_lo, 0.0)).astype(qt.dtype)], axis=0)

    gate = _dot(kmean_ref[...], qt)
    blk = _iota2((nb, BS), 0).astype(F32)
    cand = blk < i.astype(F32)
    sel = jnp.zeros((nb, BS), F32)
    for _ in range(MOBA_TOPK):
        best = jnp.max(jnp.where(cand, gate, -jnp.inf), 0, keepdims=True)
        idx = jnp.min(jnp.where(cand & (gate == best), blk, float(nb)), 0, keepdims=True)
        pick = blk == idx
        sel = jnp.where(pick, 1.0, sel)
        cand = cand & jnp.logical_not(pick)
    sel_ref[...] = sel

    G = MOBA_GROUP
    last = nb - 1
    s_ref = (s0_ref, s1_ref)
    p_ref = (p0_ref, p1_ref)

    def group(u):
        return [jnp.clip(G * u + x, 0, last) for x in range(G)]

    def issue_scores(u, slot):
        tops = []
        for x, j in enumerate(group(u)):
            sc = _dot(k_ref[j], qt_ext)
            s_ref[slot][x] = sc
            tops.append(jnp.max(sc, 0, keepdims=True))
        return tuple(tops)

    def value_blocks(u):
        js = group(u)
        js[0] = jnp.where(u == -1, i, js[0])
        return js

    def weighted_values(u, slot):
        out = None
        for x, j in enumerate(value_blocks(u)):
            part = _dot(vt_ref[j], p_ref[slot][x])
            out = part if out is None else out + part
        return out

    def step(u, slot, tops, a_prev, m, acc):
        on, shift = [], []
        m_new = m
        for x, j in enumerate(group(u)):
            on.append(sel_ref[pl.ds(j, 1), :] > 0.0)
            shift.append(slope * ((j - i) * BS).astype(F32))
            m_new = jnp.maximum(m_new, jnp.where(on[x], tops[x] + shift[x], NEG_BIG))
        tops_next, pv = [], None
        for x, (j_prev, j_next) in enumerate(zip(value_blocks(u - 1), group(u + 1))):
            part = _dot(vt_ref[j_prev], p_ref[1 - slot][x])
            pv = part if pv is None else pv + part
            sc = _dot(k_ref[j_next], qt_ext)
            s_ref[1 - slot][x] = sc
            tops_next.append(jnp.max(sc, 0, keepdims=True))
            p = jnp.exp2(s_ref[slot][x] - (jnp.where(on[x], m_new, -NEG_BIG) - shift[x]))
            p_ref[slot][x] = p.astype(BF16)
        return tuple(tops_next), jnp.exp2(m - m_new), m_new, a_prev * acc + pv

    def body(w, carry):
        carry = step(2 * w, 0, *carry)
        return step(2 * w + 1, 1, *carry)

    tops0 = issue_scores(0, 0)
    s_own = jnp.where(_iota2((BS, BS), 1) >= _iota2((BS, BS), 0), _dot(k_ref[i], qt_ext), NEG_BIG)
    m_own = jnp.max(s_own, 0, keepdims=True)
    p1_ref[1:G] = jnp.zeros((G - 1, BS, BS), BF16)
    p1_ref[0] = jnp.exp2(s_own - m_own).astype(BF16)
    init = (tops0, jnp.ones((1, BS), F32), m_own, jnp.zeros((vt_ref.shape[1], BS), F32))
    trips = (i + 2 * G - 1) // (2 * G)
    _, a_prev, m, acc = lax.fori_loop(0, trips, body, init)
    acc = a_prev * acc + weighted_values(2 * trips - 1, 1)
    o_ref[...] = (acc[0:HD] / acc[HD:HD + 1]).astype(o_ref.dtype)


def _moba_call(qt, k4, vt4, kmean):
    S = qt.shape[1]
    nb = S // MOBA_BLOCK
    return pl.pallas_call(
        _moba_kernel,
        grid=(MOBA_HEADS, nb),
        in_specs=[pl.BlockSpec((MOBA_HD, MOBA_BLOCK), lambda h, i: (h, i)),
                  pl.BlockSpec((None, nb, MOBA_BLOCK, 2 * MOBA_HD), lambda h, i: (h, 0, 0, 0)),
                  pl.BlockSpec((None, nb, MOBA_VROWS, MOBA_BLOCK), lambda h, i: (h, 0, 0, 0)),
                  pl.BlockSpec((None, nb, MOBA_HD), lambda h, i: (h, 0, 0))],
        out_specs=pl.BlockSpec((MOBA_HD, MOBA_BLOCK), lambda h, i: (h, i)),
        out_shape=jax.ShapeDtypeStruct((MOBA_D, S), BF16),
        scratch_shapes=[pltpu.VMEM((nb, MOBA_BLOCK), F32)]
        + [pltpu.VMEM((MOBA_GROUP, MOBA_BLOCK, MOBA_BLOCK), F32)] * 2
        + [pltpu.VMEM((MOBA_GROUP, MOBA_BLOCK, MOBA_BLOCK), BF16)] * 2,
        compiler_params=_cparams(("parallel", "arbitrary")),
        name="moba",
    )(qt, k4, vt4, kmean)


N_EXPERTS, TOP_K, EXPERT_FF = 8, 2, 2816
MOE_ROWS = 512


def _router_kernel(x_ref, whi_ref, wlo_ref, b_ref, o_ref):
    x = x_ref[...]
    xhi = x.astype(BF16)
    xlo = (x - xhi.astype(F32)).astype(BF16)
    logits = (jnp.dot(xhi, whi_ref[...], preferred_element_type=F32)
              + jnp.dot(xhi, wlo_ref[...], preferred_element_type=F32)
              + jnp.dot(xlo, whi_ref[...], preferred_element_type=F32)) + b_ref[...]
    lane = _iota2(logits.shape, 1).astype(F32)
    logits = jnp.where(lane < N_EXPERTS, logits, -jnp.inf)
    m1 = jnp.max(logits, -1, keepdims=True)
    i1 = jnp.min(jnp.where(logits == m1, lane, 128.0), -1, keepdims=True)
    rest = jnp.where(lane == i1, -jnp.inf, logits)
    m2 = jnp.max(rest, -1, keepdims=True)
    i2 = jnp.min(jnp.where(rest == m2, lane, 128.0), -1, keepdims=True)
    e = jnp.exp(m2 - m1)
    g1 = 1.0 / (1.0 + e)
    g2 = e / (1.0 + e)
    out = jnp.where(lane == 0, i1, 0.0)
    out = jnp.where(lane == 1, i2, out)
    out = jnp.where(lane == 2, g1, out)
    out = jnp.where(lane == 3, g2, out)
    o_ref[...] = out


def _router_call(x, w_router, b_router, tm):
    S, D = x.shape
    wp = jnp.pad(w_router.astype(F32), ((0, 0), (0, 128 - N_EXPERTS)))
    whi = wp.astype(BF16)
    wlo = (wp - whi.astype(F32)).astype(BF16)
    bp = jnp.pad(b_router.astype(F32), (0, 128 - N_EXPERTS)).reshape(1, 128)
    const = lambda shape: pl.BlockSpec(shape, lambda i: (0, 0))
    return pl.pallas_call(
        _router_kernel,
        grid=(S // tm,),
        in_specs=[pl.BlockSpec((tm, D), lambda i: (i, 0)), const((D, 128)), const((D, 128)), const((1, 128))],
        out_specs=pl.BlockSpec((tm, 128), lambda i: (i, 0)),
        out_shape=jax.ShapeDtypeStruct((S, 128), F32),
        compiler_params=_cparams(("parallel",)),
        name="router",
    )(x, whi, wlo, bp)


def _row_copy(src_hbm, dst_ref, src_row, dst_row, sem):
    return pltpu.make_async_copy(src_hbm.at[pl.ds(src_row, 1)], dst_ref.at[pl.ds(dst_row, 1)], sem)


def _scatter_rows_kernel(dest_ref, lo_ref, hi_ref, x_ref, o_hbm, zero_ref, sem):
    step = pl.program_id(0)
    n_tiles = pl.num_programs(0) - 1
    tm = x_ref.shape[0]

    def row_out(src_ref, src_row, dst_row):
        return pltpu.make_async_copy(src_ref.at[pl.ds(src_row, 1)], o_hbm.at[pl.ds(dst_row, 1)], sem)

    @pl.when(step < n_tiles)
    def _():
        base = step * tm

        def start(r, c):
            for slot in range(TOP_K):
                row_out(x_ref, r, dest_ref[TOP_K * (base + r) + slot]).start()
            return c

        lax.fori_loop(0, tm, start, 0, unroll=8)
        for slot in range(TOP_K):
            pltpu.make_async_copy(x_ref, o_hbm.at[pl.ds(0, tm)], sem).wait()

    @pl.when(step == n_tiles)
    def _():
        zero_ref[...] = jnp.zeros_like(zero_ref)
        for e in range(N_EXPERTS):
            def start(r, c):
                row_out(zero_ref, 0, r).start()
                return c

            def wait(r, c):
                row_out(zero_ref, 0, r).wait()
                return c

            lax.fori_loop(lo_ref[e], hi_ref[e], start, 0)
            lax.fori_loop(lo_ref[e], hi_ref[e], wait, 0)


def _scatter_rows(x, dest, pad_lo, pad_hi, n_rows, tm):
    T, D = x.shape
    n_tiles = T // tm
    return pl.pallas_call(
        _scatter_rows_kernel,
        grid_spec=pltpu.PrefetchScalarGridSpec(
            num_scalar_prefetch=3,
            grid=(n_tiles + 1,),
            in_specs=[pl.BlockSpec((tm, D), lambda i, d, lo, hi: (jnp.minimum(i, n_tiles - 1), 0))],
            out_specs=pl.BlockSpec(memory_space=pl.ANY),
            scratch_shapes=[pltpu.VMEM((8, D), x.dtype), pltpu.SemaphoreType.DMA(())]),
        out_shape=jax.ShapeDtypeStruct((n_rows, D), x.dtype),
        compiler_params=_cparams(("arbitrary",)),
        name="moe_scatter",
    )(dest, pad_lo, pad_hi, x)


def _moe_ffn_kernel(be_ref, nu_ref, x_ref, wg_ref, wu_ref, wd_ref, o_ref, acc_ref):
    i = pl.program_id(0)
    j = pl.program_id(1)

    @pl.when(i < nu_ref[0])
    def _():
        xb = x_ref[...].astype(BF16)
        part = _dot(_silu(_dot(xb, wg_ref[...])) * _dot(xb, wu_ref[...]), wd_ref[...])

        @pl.when(j == 0)
        def _():
            acc_ref[...] = part

        @pl.when(j > 0)
        def _():
            acc_ref[...] += part

    last = j == pl.num_programs(1) - 1

    @pl.when(last & (i < nu_ref[0]))
    def _():
        o_ref[...] = acc_ref[...]

    @pl.when(last & (i >= nu_ref[0]))
    def _():
        o_ref[...] = jnp.zeros_like(o_ref)


def _moe_ffn_call(x_rows, block_e, n_used, wg, wu, wd, tf):
    n, D = x_rows.shape
    F = wg.shape[2]
    R = MOE_ROWS
    return pl.pallas_call(
        _moe_ffn_kernel,
        grid_spec=pltpu.PrefetchScalarGridSpec(
            num_scalar_prefetch=2,
            grid=(n // R, F // tf),
            in_specs=[pl.BlockSpec((R, D), lambda i, j, be, nu: (i, 0)),
                      pl.BlockSpec((None, D, tf), lambda i, j, be, nu: (be[i], 0, j)),
                      pl.BlockSpec((None, D, tf), lambda i, j, be, nu: (be[i], 0, j)),
                      pl.BlockSpec((None, tf, D), lambda i, j, be, nu: (be[i], j, 0))],
            out_specs=pl.BlockSpec((R, D), lambda i, j, be, nu: (i, 0)),
            scratch_shapes=[pltpu.VMEM((R, D), F32)]),
        out_shape=jax.ShapeDtypeStruct((n, D), F32),
        compiler_params=_cparams(("arbitrary", "arbitrary")),
        name="moe_ffn",
    )(block_e, n_used, x_rows, wg, wu, wd)


def _moe_combine_kernel(d_ref, y_hbm, x_ref, r_ref, g_ref, b_ref, o_ref, y1_ref, y2_ref, sem):
    tm = x_ref.shape[0]
    base = pl.program_id(0) * tm

    def start(r, c):
        _row_copy(y_hbm, y1_ref, d_ref[2 * (base + r)], r, sem.at[0]).start()
        _row_copy(y_hbm, y2_ref, d_ref[2 * (base + r) + 1], r, sem.at[1]).start()
        return c

    lax.fori_loop(0, tm, start, 0, unroll=8)
    pltpu.make_async_copy(y_hbm.at[pl.ds(0, tm)], y1_ref, sem.at[0]).wait()
    pltpu.make_async_copy(y_hbm.at[pl.ds(0, tm)], y2_ref, sem.at[1]).wait()
    y = r_ref[:, 2:3] * y1_ref[...] + r_ref[:, 3:4] * y2_ref[...]
    o_ref[...] = _layer_norm_rows(DN_ALPHA * x_ref[...] + y, g_ref[...], b_ref[...])


def _moe_combine_call(dest, y_rows, x, routed, g, b, tm):
    S, D = x.shape
    return pl.pallas_call(
        _moe_combine_kernel,
        grid_spec=pltpu.PrefetchScalarGridSpec(
            num_scalar_prefetch=1,
            grid=(S // tm,),
            in_specs=[pl.BlockSpec(memory_space=pl.ANY),
                      pl.BlockSpec((tm, D), lambda i, d: (i, 0)),
                      pl.BlockSpec((tm, 128), lambda i, d: (i, 0)),
                      pl.BlockSpec((1, D), lambda i, d: (0, 0)),
                      pl.BlockSpec((1, D), lambda i, d: (0, 0))],
            out_specs=pl.BlockSpec((tm, D), lambda i, d: (i, 0)),
            scratch_shapes=[pltpu.VMEM((tm, D), F32), pltpu.VMEM((tm, D), F32), pltpu.SemaphoreType.DMA((2,))]),
        out_shape=jax.ShapeDtypeStruct((S, D), F32),
        compiler_params=_cparams(("arbitrary",)),
        name="moe_combine",
    )(dest, y_rows, x, routed, g.reshape(1, D), b.reshape(1, D))


def _moe_sublayer(x, w_router, b_router, wg, wu, wd, g, b):
    T = x.shape[0]
    R = MOE_ROWS
    routed = _router_call(x, w_router, b_router, min(512, T))
    top_e = routed[:, 0:TOP_K].astype(jnp.int32)
    tok_oh = jnp.sum((top_e[:, :, None] == jnp.arange(N_EXPERTS)[None, None, :]).astype(jnp.int32), axis=1)
    counts = jnp.sum(tok_oh, axis=0)
    rank = jnp.cumsum(tok_oh, axis=0) - tok_oh
    padded = (counts + R - 1) // R * R
    pend = jnp.cumsum(padded)
    pstart = pend - padded
    dest = pstart[top_e] + jnp.take_along_axis(rank, top_e, axis=1)
    n_rows = (T * TOP_K + N_EXPERTS * (R - 1)) // R * R
    n_blocks = n_rows // R
    block_first_row = jnp.arange(n_blocks, dtype=jnp.int32) * R
    block_e = jnp.minimum(jnp.sum((pend[None, :] <= block_first_row[:, None]).astype(jnp.int32), axis=1),
                          N_EXPERTS - 1)
    n_used = (pend[-1] // R).astype(jnp.int32).reshape(1)
    dest = dest.reshape(-1).astype(jnp.int32)
    pad_lo = (pstart + counts).astype(jnp.int32)
    pad_hi = jnp.concatenate([pstart[1:], jnp.array([n_rows])]).astype(jnp.int32)
    x_rows = _scatter_rows(x, dest, pad_lo, pad_hi, n_rows, min(512, T))
    y_rows = _moe_ffn_call(x_rows, block_e, n_used, wg, wu, wd, EXPERT_FF // 2)
    return _moe_combine_call(dest, y_rows, x, routed, g, b, min(256, T))


GLA_IN = 2 * GLA_QK + 2 * GLA_V + GLA_LR


def _gla_rwkv_sublayer(x, w_in, gla_wa2, gla_ba, gla_norm, mu, w0, w2, a0, a2, g2, k_k, k_a, r_k, ln_w, ln_b,
                       w_out, ln_g, ln_bias):
    S = x.shape[0]
    tm = min(512, S)
    lr_pad = 128 - GLA_LR
    w_gla = jnp.pad(w_in[:, :GLA_IN], ((0, 0), (0, lr_pad))).astype(BF16)
    w_rwkv = w_in[:, GLA_IN:].astype(BF16)
    p_gla = _matmul(x, w_gla, F32, tm, GLA_COLS)
    p_rwkv = _matmul(x, w_rwkv, F32, tm, RWKV_COLS)
    wa2p = jnp.pad(gla_wa2, ((0, lr_pad), (0, 0))).astype(BF16)
    o_gla = _gla_call(p_gla, wa2p, gla_ba, gla_norm)
    o_rwkv = _rwkv_call(p_rwkv, mu, w0, w2, a0, a2, g2, k_k, k_a, r_k, ln_w, ln_b)
    return _matmul_ln([o_gla, o_rwkv], [w_out[:GLA_V].astype(BF16), w_out[GLA_V:].astype(BF16)], x,
                      ln_g, ln_bias, tm)


def _xattn_sublayer(x, mem, wq, wk, wv, wo, ln_g, ln_bias):
    M = mem.shape[0]
    k_mem = _matmul(mem, wk.astype(BF16), BF16, M, D_MODEL)
    v_mem = _matmul(mem, wv.astype(BF16), BF16, M, D_MODEL)
    wq_scaled = (wq * XATTN_HD ** -0.5).astype(BF16)
    return _xattn_call(x, wq_scaled, k_mem, v_mem, wo.astype(BF16), ln_g, ln_bias, min(512, x.shape[0]))


def _ssd_moba_sublayer(x, w_in, conv_w, conv_b, dt_bias, a_log, d_skip, ssd_norm, w_out, ln_g, ln_bias):
    S = x.shape[0]
    tm = min(512, S)
    nb = S // MOBA_BLOCK
    o_dt = SSD_COLS
    o_q = o_dt + SSD_HEADS
    w_ssd = w_in[:, :o_dt].astype(BF16)
    w_dt = w_in[:, o_dt:o_q]
    w_q = w_in[:, o_q:o_q + MOBA_D]
    w_k = w_in[:, o_q + MOBA_D:o_q + 2 * MOBA_D]
    w_v = w_in[:, o_q + 2 * MOBA_D:]
    p_ssd = _matmul(x, w_ssd, F32, tm, SSD_COLS // 2)
    p_dt = _matmul(x, jnp.pad(w_dt, ((0, 0), (0, 128 - SSD_HEADS))).astype(BF16), F32, tm, 128)
    dtt = _matmul_t(w_dt.T.astype(BF16), x, F32, tm)
    o_ssd = _ssd_call(p_ssd, p_dt, dtt, conv_w, conv_b, dt_bias, a_log, d_skip, ssd_norm)
    qt = _matmul_t((w_q.T * (MOBA_HD ** -0.5 * LOG2E)).astype(BF16), x, BF16, tm)
    k4, kmean = _kproj_call(x, w_k.astype(BF16))
    vt4 = _vproj_call(w_v.T.astype(BF16), x)
    kmean_h = kmean.reshape(nb, MOBA_HEADS, MOBA_HD).transpose(1, 0, 2)
    ot_moba = _moba_call(qt, k4, vt4, kmean_h)
    return _matmul_ln([o_ssd, ot_moba], [w_out[:SSD_INNER].astype(BF16), w_out[SSD_INNER:].astype(BF16)], x,
                      ln_g, ln_bias, tm, transposed=(False, True))


def kernel(x, mem, l0_w_in, l0_gla_wa2, l0_gla_ba, l0_gla_norm, l0_rwkv_mu, l0_rwkv_w0, l0_rwkv_w2, l0_rwkv_a0, l0_rwkv_a2, l0_rwkv_g2, l0_rwkv_kk, l0_rwkv_ka, l0_rwkv_rk, l0_rwkv_lnw, l0_rwkv_lnb, l0_w_out, l0_ln1_g, l0_ln1_b, l0_xq, l0_xk, l0_xv, l0_xo, l0_ln2_g, l0_ln2_b, l0_ffn_wg, l0_ffn_wu, l0_ffn_wd, l0_ln3_g, l0_ln3_b, l1_w_in, l1_conv_w, l1_conv_b, l1_dt_bias, l1_a_log, l1_d_skip, l1_ssd_norm, l1_w_out, l1_ln1_g, l1_ln1_b, l1_xq, l1_xk, l1_xv, l1_xo, l1_ln2_g, l1_ln2_b, l1_router, l1_router_b, l1_exp_wg, l1_exp_wu, l1_exp_wd, l1_ln3_g, l1_ln3_b):
    x2 = x.reshape(-1, D_MODEL)
    mem2 = mem.reshape(-1, D_MODEL)
    x2 = _gla_rwkv_sublayer(x2, l0_w_in, l0_gla_wa2, l0_gla_ba, l0_gla_norm, l0_rwkv_mu, l0_rwkv_w0, l0_rwkv_w2,
                            l0_rwkv_a0, l0_rwkv_a2, l0_rwkv_g2, l0_rwkv_kk, l0_rwkv_ka, l0_rwkv_rk, l0_rwkv_lnw,
                            l0_rwkv_lnb, l0_w_out, l0_ln1_g, l0_ln1_b)
    x2 = _xattn_sublayer(x2, mem2, l0_xq, l0_xk, l0_xv, l0_xo, l0_ln2_g, l0_ln2_b)
    tm = min(512, x2.shape[0])
    x2 = _ffn_call(x2, l0_ffn_wg.astype(BF16), l0_ffn_wu.astype(BF16), l0_ffn_wd.astype(BF16),
                   l0_ln3_g, l0_ln3_b, tm, l0_ffn_wg.shape[1] // 2)
    x2 = _ssd_moba_sublayer(x2, l1_w_in, l1_conv_w, l1_conv_b, l1_dt_bias, l1_a_log, l1_d_skip, l1_ssd_norm,
                            l1_w_out, l1_ln1_g, l1_ln1_b)
    x2 = _xattn_sublayer(x2, mem2, l1_xq, l1_xk, l1_xv, l1_xo, l1_ln2_g, l1_ln2_b)
    x2 = _moe_sublayer(x2, l1_router, l1_router_b, l1_exp_wg.astype(BF16), l1_exp_wu.astype(BF16),
                       l1_exp_wd.astype(BF16), l1_ln3_g, l1_ln3_b)
    return x2.reshape(x.shape)
```

```python
import functools
import math

import jax
import jax.numpy as jnp
from jax import lax
from jax.experimental import pallas as pl
from jax.experimental.pallas import tpu as pltpu

BF16 = jnp.bfloat16
F32 = jnp.float32

D_MODEL = 1024
LN_EPS = 1e-5
DEPTH = 2
DN_ALPHA = (2 * DEPTH) ** 0.25

GLA_HEADS, GLA_DK, GLA_DV, GLA_CHUNK = 4, 64, 128, 64
GLA_QK, GLA_V, GLA_LR, GLA_TAU = 256, 512, 16, 16.0
GLA_COLS = 2 * GLA_QK + 2 * GLA_V + 128

RWKV_HEADS, RWKV_HD, RWKV_D, RWKV_CHUNK = 8, 64, 512, 64
RWKV_COLS = 1792
RWKV_DECAY_SCALE = math.exp(-0.5)
RWKV_GN_EPS = 64e-5

VMEM_LIMIT = 56 * 1024 * 1024


def _cparams(sem):
    return pltpu.CompilerParams(dimension_semantics=sem, vmem_limit_bytes=VMEM_LIMIT)


def _dot(a, b):
    return jnp.dot(a.astype(BF16), b.astype(BF16), preferred_element_type=F32)


def _dot_nt(a, b):
    return lax.dot_general(a.astype(BF16), b.astype(BF16), (((1,), (1,)), ((), ())), preferred_element_type=F32)


def _dot_tn(a, b):
    return lax.dot_general(a.astype(BF16), b.astype(BF16), (((0,), (0,)), ((), ())), preferred_element_type=F32)


def _split3(x):
    hi = x.astype(BF16)
    r1 = x - hi.astype(F32)
    mid = r1.astype(BF16)
    lo = (r1 - mid.astype(F32)).astype(BF16)
    return hi, mid, lo


def _dot_exact_lhs(m, x):
    mb = m.astype(BF16)
    hi, mid, lo = _split3(x)
    return (jnp.dot(mb, hi, preferred_element_type=F32) + jnp.dot(mb, mid, preferred_element_type=F32)
            + jnp.dot(mb, lo, preferred_element_type=F32))


def _dot_exact_rhs(x, m):
    mb = m.astype(BF16)
    hi, mid, lo = _split3(x)
    return (jnp.dot(hi, mb, preferred_element_type=F32) + jnp.dot(mid, mb, preferred_element_type=F32)
            + jnp.dot(lo, mb, preferred_element_type=F32))


def _sigmoid(x):
    return 1.0 / (1.0 + jnp.exp(-x))


def _silu(x):
    return x * _sigmoid(x)


def _iota2(shape, axis):
    return lax.broadcasted_iota(jnp.int32, shape, axis)


def _chunk_tril(n, chunk):
    r = _iota2((n, n), 0)
    c = _iota2((n, n), 1)
    return jnp.where((c <= r) & ((r // chunk) == (c // chunk)), 1.0, 0.0)


def _head_block(n, width, value):
    r = _iota2((n, n), 0)
    c = _iota2((n, n), 1)
    return jnp.where((r // width) == (c // width), value, 0.0)


def _mm_kernel(x_ref, w_ref, o_ref):
    o_ref[...] = _dot(x_ref[...], w_ref[...]).astype(o_ref.dtype)


def _matmul(x, w, out_dtype, tm, tn):
    S, K = x.shape
    N = w.shape[1]
    return pl.pallas_call(
        _mm_kernel,
        grid=(S // tm, N // tn),
        in_specs=[pl.BlockSpec((tm, K), lambda i, j: (i, 0)),
                  pl.BlockSpec((K, tn), lambda i, j: (0, j))],
        out_specs=pl.BlockSpec((tm, tn), lambda i, j: (i, j)),
        out_shape=jax.ShapeDtypeStruct((S, N), out_dtype),
        compiler_params=_cparams(("parallel", "arbitrary")),
        name="matmul",
    )(x, w)


def _layer_norm_rows(y, g, b):
    mu = jnp.mean(y, -1, keepdims=True)
    d = y - mu
    var = jnp.mean(d * d, -1, keepdims=True)
    return d * lax.rsqrt(var + LN_EPS) * g + b


def _mm_ln_kernel(transposed, *refs):
    n_in = len(transposed)
    a_refs = refs[:n_in]
    w_refs = refs[n_in:2 * n_in]
    x_ref, g_ref, b_ref, o_ref = refs[2 * n_in:]
    acc = DN_ALPHA * x_ref[...]
    for a_ref, w_ref, tr in zip(a_refs, w_refs, transposed):
        acc = acc + (_dot_tn if tr else _dot)(a_ref[...], w_ref[...])
    o_ref[...] = _layer_norm_rows(acc, g_ref[...], b_ref[...])


def _matmul_ln(a_list, w_list, x, g, b, tm, transposed=None):
    S, D = x.shape
    transposed = tuple(transposed or (False,) * len(a_list))
    in_specs = ([pl.BlockSpec((a.shape[0], tm), lambda i: (0, i)) if tr else
                 pl.BlockSpec((tm, a.shape[1]), lambda i: (i, 0)) for a, tr in zip(a_list, transposed)]
                + [pl.BlockSpec(w.shape, lambda i: (0, 0)) for w in w_list]
                + [pl.BlockSpec((tm, D), lambda i: (i, 0)),
                   pl.BlockSpec((1, D), lambda i: (0, 0)),
                   pl.BlockSpec((1, D), lambda i: (0, 0))])
    return pl.pallas_call(
        functools.partial(_mm_ln_kernel, transposed),
        grid=(S // tm,),
        in_specs=in_specs,
        out_specs=pl.BlockSpec((tm, D), lambda i: (i, 0)),
        out_shape=jax.ShapeDtypeStruct((S, D), F32),
        compiler_params=_cparams(("parallel",)),
        name="matmul_ln",
    )(*a_list, *w_list, x, g.reshape(1, D), b.reshape(1, D))


def _gla_kernel(p_ref, wa2_ref, ba_ref, ng_ref, o_ref, st_ref, o_scr):
    C, H, dk, dv = GLA_CHUNK, GLA_HEADS, GLA_DK, GLA_DV
    tb = p_ref.shape[0]

    @pl.when(pl.program_id(0) == 0)
    def _():
        st_ref[...] = jnp.zeros_like(st_ref)

    z = _dot(p_ref[:, 2 * GLA_QK + 2 * GLA_V:], wa2_ref[...]) + ba_ref[...]
    log_a = -(jnp.maximum(-z, 0.0) + jnp.log(1.0 + jnp.exp(-jnp.abs(z)))) / GLA_TAU
    b = _dot_exact_lhs(_chunk_tril(tb, C), log_a)
    causal = _iota2((C, C), 1) <= _iota2((C, C), 0)

    nc = tb // C
    q_h, k_h, ke_h, v_h, dec_h = [], [], [], [], []
    for c in range(nc):
        rows = slice(c * C, (c + 1) * C)
        b_c = b[rows]
        b_last = b_c[C - 1:C]
        q_dec = p_ref[rows, 0:GLA_QK] * (dk ** -0.5) * jnp.exp(b_c)
        k_c = p_ref[rows, GLA_QK:2 * GLA_QK]
        k_dec = k_c * jnp.exp(-b_c)
        k_end = k_c * jnp.exp(b_last - b_c)
        decay = jnp.exp(b_last)
        for h in range(H):
            ks = slice(h * dk, (h + 1) * dk)
            q_h.append(q_dec[:, ks])
            k_h.append(k_dec[:, ks])
            ke_h.append(k_end[:, ks])
            dec_h.append(decay[:, ks])
            v_h.append(p_ref[rows, 2 * GLA_QK + h * dv:2 * GLA_QK + (h + 1) * dv])
    n = nc * H
    attn = [jnp.where(causal, _dot_nt(q_h[i], k_h[i]), 0.0) for i in range(n)]
    kv = [_dot_tn(v_h[i], ke_h[i]) for i in range(n)]
    intra = [_dot(attn[i], v_h[i]) for i in range(n)]
    state = [st_ref[:, h * dk:(h + 1) * dk] for h in range(H)]
    entering = []
    for i in range(n):
        entering.append(state[i % H])
        state[i % H] = state[i % H] * dec_h[i] + kv[i]
    for i in range(n):
        c, h = divmod(i, H)
        o_scr[c * C:(c + 1) * C, h * dv:(h + 1) * dv] = intra[i] + _dot_nt(q_h[i], entering[i])
    for h in range(H):
        st_ref[:, h * dk:(h + 1) * dk] = state[h]

    for h in range(H):
        vs = slice(h * dv, (h + 1) * dv)
        o_h = o_scr[:, vs]
        g_h = p_ref[:, 2 * GLA_QK + GLA_V + h * dv:2 * GLA_QK + GLA_V + (h + 1) * dv]
        o_h = o_h * lax.rsqrt(jnp.mean(o_h * o_h, -1, keepdims=True) + 1e-5) * ng_ref[:, vs]
        o_ref[:, vs] = (o_h * _silu(g_h)).astype(o_ref.dtype)


def _gla_call(p_gla, wa2p, ba, norm_g, tb=256):
    S = p_gla.shape[0]
    return pl.pallas_call(
        _gla_kernel,
        grid=(S // tb,),
        in_specs=[pl.BlockSpec((tb, GLA_COLS), lambda i: (i, 0)),
                  pl.BlockSpec((128, GLA_QK), lambda i: (0, 0)),
                  pl.BlockSpec((1, GLA_QK), lambda i: (0, 0)),
                  pl.BlockSpec((1, GLA_V), lambda i: (0, 0))],
        out_specs=pl.BlockSpec((tb, GLA_V), lambda i: (i, 0)),
        out_shape=jax.ShapeDtypeStruct((S, GLA_V), BF16),
        scratch_shapes=[pltpu.VMEM((GLA_DV, GLA_QK), F32), pltpu.VMEM((tb, GLA_V), F32)],
        compiler_params=_cparams(("arbitrary",)),
        name="gla",
    )(p_gla, wa2p, ba.reshape(1, GLA_QK), norm_g.reshape(1, GLA_V))


def _rwkv_kernel(p_ref, mu_ref, w0_ref, w2_ref, a0_ref, a2_ref, g2_ref, kk_ref, ka_ref, rk_ref, lnw_ref, lnb_ref,
                 o_ref, prev_ref, h_ref, o_scr):
    C, H, N, D = RWKV_CHUNK, RWKV_HEADS, RWKV_HD, RWKV_D
    tb = p_ref.shape[0]
    first = pl.program_id(0) == 0

    @pl.when(first)
    def _():
        prev_ref[...] = jnp.zeros_like(prev_ref)
        h_ref[...] = jnp.zeros_like(h_ref)

    p = p_ref[...]
    shifted = jnp.where(_iota2(p.shape, 0) == 0, prev_ref[...], pltpu.roll(p, 1, 0))
    prev_ref[...] = p[tb - 1:tb]
    p = p + mu_ref[...] * (shifted - p)
    r = p[:, 0:D]
    k = p[:, D:2 * D]
    v = p[:, 2 * D:3 * D]
    xw = p[:, 3 * D:3 * D + 64]
    xa = p[:, 3 * D + 64:3 * D + 128]
    xg = p[:, 3 * D + 128:3 * D + 256]
    lw = -RWKV_DECAY_SCALE * _sigmoid(w0_ref[...] + _dot(jnp.tanh(xw), w2_ref[...]))
    a = _sigmoid(a0_ref[...] + _dot(xa, a2_ref[...]))
    g = _dot(_sigmoid(xg), g2_ref[...])
    head_ones = _head_block(D, N, 1.0)
    kk = k * kk_ref[...]
    kk = kk * lax.rsqrt(jnp.maximum(_dot_exact_rhs(kk * kk, head_ones), 1e-24))
    k = k * (1.0 + (a - 1.0) * ka_ref[...])
    pv = -kk * a
    cw = _dot_exact_lhs(_chunk_tril(tb, C), lw)
    cwx = cw - lw

    gi = _iota2((2 * C, 2 * C), 0)
    gj = _iota2((2 * C, 2 * C), 1) % C
    gram_mask = ((gi < C) & (gj < gi)) | ((gi >= C) & (gj <= gi - C))
    eye = _iota2((C, C), 0) == _iota2((C, C), 1)
    eye_f = jnp.where(eye, 1.0, 0.0)

    nc = tb // C
    items = [(c, h) for c in range(nc) for h in range(H)]
    xs, ys, pk_e, b_h, v_h, r_h, g_h = [], [], [], [], [], [], []
    for c in range(nc):
        rows = slice(c * C, (c + 1) * C)
        cw_c = cw[rows]
        cw_end = cw_c[C - 1:C]
        e_pos = jnp.exp(cw_c)
        e_neg = jnp.exp(-cw_c)
        e_end = jnp.exp(cw_end - cw_c)
        r_t = r[rows] * e_pos
        b_t = kk[rows] * jnp.exp(cwx[rows])
        p_t = pv[rows] * e_neg
        k_t = k[rows] * e_neg
        p_e = pv[rows] * e_end
        k_e = k[rows] * e_end
        g_end = jnp.exp(cw_end)
        v_c = v[rows]
        for h in range(H):
            hs = slice(h * N, (h + 1) * N)
            xs.append(jnp.concatenate([b_t[:, hs], r_t[:, hs]], axis=0))
            ys.append(jnp.concatenate([p_t[:, hs], k_t[:, hs]], axis=0))
            pk_e.append(jnp.concatenate([p_e[:, hs], k_e[:, hs]], axis=0))
            b_h.append(b_t[:, hs])
            v_h.append(v_c[:, hs])
            r_h.append(r_t[:, hs])
            g_h.append(g_end[:, hs])
    n = len(items)
    grams = [jnp.where(gram_mask, _dot_nt(xs[i], ys[i]), 0.0) for i in range(n)]
    l_p = [g[0:C, 0:C] for g in grams]
    m_pk = [g[C:2 * C, :] for g in grams]
    lkv = [_dot(grams[i][0:C, C:2 * C], v_h[i]) for i in range(n)]
    x = [_dot(lp, lp) for lp in l_p]
    t = [eye_f + lp for lp in l_p]
    for _ in range(4):
        tx = [_dot(jnp.concatenate([t[i], x[i]], axis=0), x[i]) for i in range(n)]
        t = [t[i] + tx[i][0:C] for i in range(n)]
        x = [tx[i][C:2 * C] for i in range(n)]
    t = [t[i] + _dot(t[i], x[i]) for i in range(n)]
    wu = [_dot(t[i], jnp.concatenate([b_h[i], lkv[i]], axis=1)) for i in range(n)]
    rhs = [jnp.concatenate([wu[i], jnp.concatenate([jnp.zeros((C, N), F32), v_h[i]], axis=1)], axis=0)
           for i in range(n)]
    az = [_dot_tn(pk_e[i], rhs[i]) for i in range(n)]
    qo = [_dot(m_pk[i], rhs[i]) for i in range(n)]
    state = [h_ref[h] for h in range(H)]
    for i, (c, h) in enumerate(items):
        a_mat = az[i][:, 0:N] + jnp.where(eye, g_h[i], 0.0)
        q_mat = qo[i][:, 0:N] + r_h[i]
        oh = _dot(jnp.concatenate([q_mat, a_mat], axis=0), state[h])
        o_scr[c * C:(c + 1) * C, h * N:(h + 1) * N] = oh[0:C] + qo[i][:, N:2 * N]
        state[h] = oh[C:C + N] + az[i][:, N:2 * N]
    for h in range(H):
        h_ref[h] = state[h]

    o = o_scr[...]
    head_mean = _head_block(D, N, 1.0 / N)
    mean = _dot_exact_rhs(o, head_mean)
    d = o - mean
    var = _dot_exact_rhs(d * d, head_mean)
    o = d * lax.rsqrt(var + RWKV_GN_EPS) * lnw_ref[...] + lnb_ref[...]
    bonus = _dot_exact_rhs(r * k * rk_ref[...], head_ones) * v
    o_ref[...] = ((o + bonus) * g).astype(o_ref.dtype)


def _rwkv_call(p_rwkv, mu, w0, w2, a0, a2, g2, k_k, k_a, r_k, ln_w, ln_b, tb=256):
    S = p_rwkv.shape[0]
    D = RWKV_D
    row = lambda t: t.reshape(1, -1).astype(F32)
    full = lambda shape: pl.BlockSpec(shape, lambda i: tuple(0 for _ in shape))
    return pl.pallas_call(
        _rwkv_kernel,
        grid=(S // tb,),
        in_specs=[pl.BlockSpec((tb, RWKV_COLS), lambda i: (i, 0)),
                  full((1, RWKV_COLS)), full((1, D)), full((64, D)), full((1, D)), full((64, D)), full((128, D)),
                  full((1, D)), full((1, D)), full((1, D)), full((1, D)), full((1, D))],
        out_specs=pl.BlockSpec((tb, D), lambda i: (i, 0)),
        out_shape=jax.ShapeDtypeStruct((S, D), BF16),
        scratch_shapes=[pltpu.VMEM((1, RWKV_COLS), F32),
                        pltpu.VMEM((RWKV_HEADS, RWKV_HD, RWKV_HD), F32),
                        pltpu.VMEM((tb, D), F32)],
        compiler_params=_cparams(("arbitrary",)),
        name="rwkv7",
    )(p_rwkv, row(mu), row(w0), w2.astype(BF16), row(a0), a2.astype(BF16), g2.astype(BF16),
      row(k_k), row(k_a), row(r_k), row(ln_w), row(ln_b))


XATTN_HEADS, XATTN_HD = 4, 256


def _xattn_kernel(x_ref, wq_ref, k_ref, v_ref, wo_ref, g_ref, b_ref, o_ref):
    x = x_ref[...]
    q = _dot(x, wq_ref[...])
    outs = []
    for h in range(XATTN_HEADS):
        hs = slice(h * XATTN_HD, (h + 1) * XATTN_HD)
        s = _dot_nt(q[:, hs], k_ref[:, hs])
        e = jnp.exp(s - jnp.max(s, -1, keepdims=True))
        p = e / jnp.sum(e, -1, keepdims=True)
        outs.append(_dot(p, v_ref[:, hs]))
    o = jnp.concatenate(outs, axis=1)
    y = DN_ALPHA * x + _dot(o, wo_ref[...])
    o_ref[...] = _layer_norm_rows(y, g_ref[...], b_ref[...])


def _xattn_call(x, wq_scaled, k_mem, v_mem, wo, g, b, tm):
    S, D = x.shape
    M = k_mem.shape[0]
    const = lambda shape: pl.BlockSpec(shape, lambda i: (0, 0))
    return pl.pallas_call(
        _xattn_kernel,
        grid=(S // tm,),
        in_specs=[pl.BlockSpec((tm, D), lambda i: (i, 0)), const((D, D)), const((M, D)), const((M, D)),
                  const((D, D)), const((1, D)), const((1, D))],
        out_specs=pl.BlockSpec((tm, D), lambda i: (i, 0)),
        out_shape=jax.ShapeDtypeStruct((S, D), F32),
        compiler_params=_cparams(("parallel",)),
        name="xattn",
    )(x, wq_scaled, k_mem, v_mem, wo, g.reshape(1, D), b.reshape(1, D))


def _ffn_kernel(x_ref, wg_ref, wu_ref, wd_ref, g_ref, b_ref, o_ref, acc_ref):
    j = pl.program_id(1)
    x = x_ref[...]
    xb = x.astype(BF16)
    part = _dot(_silu(_dot(xb, wg_ref[...])) * _dot(xb, wu_ref[...]), wd_ref[...])

    @pl.when(j == 0)
    def _():
        acc_ref[...] = DN_ALPHA * x + part

    @pl.when(j > 0)
    def _():
        acc_ref[...] += part

    @pl.when(j == pl.num_programs(1) - 1)
    def _():
        o_ref[...] = _layer_norm_rows(acc_ref[...], g_ref[...], b_ref[...])


def _ffn_call(x, wg, wu, wd, g, b, tm, tf):
    S, D = x.shape
    F = wg.shape[1]
    return pl.pallas_call(
        _ffn_kernel,
        grid=(S // tm, F // tf),
        in_specs=[pl.BlockSpec((tm, D), lambda i, j: (i, 0)),
                  pl.BlockSpec((D, tf), lambda i, j: (0, j)),
                  pl.BlockSpec((D, tf), lambda i, j: (0, j)),
                  pl.BlockSpec((tf, D), lambda i, j: (j, 0)),
                  pl.BlockSpec((1, D), lambda i, j: (0, 0)),
                  pl.BlockSpec((1, D), lambda i, j: (0, 0))],
        out_specs=pl.BlockSpec((tm, D), lambda i, j: (i, 0)),
        out_shape=jax.ShapeDtypeStruct((S, D), F32),
        scratch_shapes=[pltpu.VMEM((tm, D), F32)],
        compiler_params=_cparams(("parallel", "arbitrary")),
        name="ffn",
    )(x, wg, wu, wd, g.reshape(1, D), b.reshape(1, D))


SSD_HD, SSD_HEADS, SSD_INNER, SSD_GROUPS, SSD_STATE = 64, 16, 1024, 2, 128
SSD_BC, SSD_CONV, SSD_CONV_CH, SSD_CHUNK = 256, 4, 1536, 128
SSD_COLS = SSD_INNER + SSD_CONV_CH
SSD_GW = SSD_INNER // SSD_GROUPS


def _softplus(x):
    return jnp.maximum(x, 0.0) + jnp.log(1.0 + jnp.exp(-jnp.abs(x)))


def _ssd_kernel(p_ref, cw_ref, cb_ref, dtb_ref, a_ref, dsk_ref, ng_ref, o_ref, prev_ref, st_ref, y_scr):
    L, G, NS, HD = SSD_CHUNK, SSD_GROUPS, SSD_STATE, SSD_HD
    HG = SSD_HEADS // G

    @pl.when(pl.program_id(0) == 0)
    def _():
        prev_ref[...] = jnp.zeros_like(prev_ref)
        st_ref[...] = jnp.zeros_like(st_ref)

    cur = p_ref[:, SSD_INNER:SSD_COLS]
    prev = prev_ref[...]
    row = _iota2(cur.shape, 0)
    conv = cur * cw_ref[SSD_CONV - 1:SSD_CONV, :] + cb_ref[...]
    for kk in range(1, SSD_CONV):
        shifted = jnp.where(row < kk, pltpu.roll(prev, kk, 0), pltpu.roll(cur, kk, 0))
        conv = conv + shifted * cw_ref[SSD_CONV - 1 - kk:SSD_CONV - kk, :]
    prev_ref[...] = cur
    xbc = _silu(conv)
    xs = xbc[:, :SSD_INNER]

    dt = _softplus(p_ref[:, SSD_COLS:] + dtb_ref[...])
    a_col = dt * a_ref[...]
    li = _iota2((L, L), 0)
    lj = _iota2((L, L), 1)
    cs = _dot_exact_lhs(jnp.where(lj <= li, 1.0, 0.0), a_col)
    cs_row = cs.T
    expand = jnp.where(_iota2((128, SSD_INNER), 1) // HD == _iota2((128, SSD_INNER), 0), 1.0, 0.0)
    dt_x = _dot_exact_rhs(dt, expand)
    cs_x = _dot_exact_rhs(cs, expand)
    cs_end = cs_x[L - 1:L]
    xd = xs * dt_x
    xd_dec = xd * jnp.exp(cs_end - cs_x)
    out_dec = jnp.exp(cs_x)
    chunk_dec = jnp.exp(cs_end)
    tril = lj <= li

    for g in range(G):
        gs = slice(g * SSD_GW, (g + 1) * SSD_GW)
        b_g = xbc[:, SSD_INNER + g * NS:SSD_INNER + (g + 1) * NS]
        c_g = xbc[:, SSD_INNER + SSD_BC + g * NS:SSD_INNER + SSD_BC + (g + 1) * NS]
        cb = _dot_nt(c_g, b_g)
        for j in range(HG):
            h = g * HG + j
            hs = slice(h * HD, (h + 1) * HD)
            seg = jnp.where(tril, jnp.exp(cs[:, h:h + 1] - cs_row[h:h + 1, :]), 0.0)
            y_scr[:, hs] = _dot(cb * seg, xd[:, hs])
        st = st_ref[g]
        y_off = _dot(c_g, st) * out_dec[:, gs]
        st_ref[g] = st * chunk_dec[:, gs] + _dot_tn(b_g, xd_dec[:, gs])
        y_scr[:, gs] = y_scr[:, gs] + y_off

    y = (y_scr[...] + dsk_ref[...] * xs) * _silu(p_ref[:, :SSD_INNER])
    for g in range(G):
        gs = slice(g * SSD_GW, (g + 1) * SSD_GW)
        y_g = y[:, gs]
        o_ref[:, gs] = (y_g * lax.rsqrt(jnp.mean(y_g * y_g, -1, keepdims=True) + 1e-5) * ng_ref[:, gs]).astype(o_ref.dtype)


def _ssd_call(p_ssd, conv_w, conv_b, dt_bias, a_log, d_skip, norm_g):
    S = p_ssd.shape[0]
    L = SSD_CHUNK
    a_neg = -jnp.exp(a_log.astype(F32))
    pad = lambda t: jnp.pad(t.astype(F32), (0, 128 - SSD_HEADS)).reshape(1, 128)
    const = lambda shape: pl.BlockSpec(shape, lambda i: (0, 0))
    return pl.pallas_call(
        _ssd_kernel,
        grid=(S // L,),
        in_specs=[pl.BlockSpec((L, SSD_COLS + 128), lambda i: (i, 0)),
                  const((SSD_CONV, SSD_CONV_CH)), const((1, SSD_CONV_CH)),
                  const((1, 128)), const((1, 128)),
                  const((1, SSD_INNER)), const((1, SSD_INNER))],
        out_specs=pl.BlockSpec((L, SSD_INNER), lambda i: (i, 0)),
        out_shape=jax.ShapeDtypeStruct((S, SSD_INNER), BF16),
        scratch_shapes=[pltpu.VMEM((L, SSD_CONV_CH), F32),
                        pltpu.VMEM((SSD_GROUPS, SSD_STATE, SSD_GW), F32),
                        pltpu.VMEM((L, SSD_INNER), F32)],
        compiler_params=_cparams(("arbitrary",)),
        name="ssd",
    )(p_ssd, conv_w.astype(F32), conv_b.reshape(1, -1).astype(F32), pad(dt_bias), pad(a_neg),
      jnp.repeat(d_skip.astype(F32), SSD_HD).reshape(1, -1), norm_g.reshape(1, -1).astype(F32))


MOBA_HD, MOBA_HEADS, MOBA_D, MOBA_BLOCK, MOBA_TOPK = 64, 8, 512, 256, 3
MOBA_GROUP = 4
NEG_BIG = -1e30
LOG2E = math.log2(math.e)


MOBA_VROWS = MOBA_HD + 16


def _moba_proj_kernel(x_ref, wqt_ref, wk_ref, wvt_ref, qt_ref, k_ref, kmean_ref, v_ref):
    xb = x_ref[...].astype(BF16)
    qt_ref[...] = _dot_nt(wqt_ref[...], xb).astype(qt_ref.dtype)
    vt = _dot_nt(wvt_ref[...], xb)
    extra = jnp.where(_iota2((MOBA_VROWS - MOBA_HD, MOBA_BLOCK), 0) == 0, 1.0, 0.0)
    for h in range(MOBA_HEADS):
        v_ref[h, 0] = jnp.concatenate([vt[h * MOBA_HD:(h + 1) * MOBA_HD], extra], axis=0).astype(v_ref.dtype)
    k = _dot(xb, wk_ref[...])
    kmean_ref[0] = jnp.mean(k, 0, keepdims=True)
    shape = (MOBA_BLOCK, 2 * MOBA_HD)
    lane = _iota2(shape, 1)
    pos = jnp.where((lane == MOBA_HD) | (lane == MOBA_HD + 1), _iota2(shape, 0).astype(F32), 0.0).astype(k_ref.dtype)
    for h in range(MOBA_HEADS):
        k_ref[h, 0] = pos
        k_ref[h, 0, :, 0:MOBA_HD] = k[:, h * MOBA_HD:(h + 1) * MOBA_HD].astype(k_ref.dtype)


def _moba_proj_call(x, wqt, wk, wvt):
    S, D = x.shape
    nb = S // MOBA_BLOCK
    const = lambda shape: pl.BlockSpec(shape, lambda i: (0, 0))
    return pl.pallas_call(
        _moba_proj_kernel,
        grid=(nb,),
        in_specs=[pl.BlockSpec((MOBA_BLOCK, D), lambda i: (i, 0)), const((MOBA_D, D)), const((D, MOBA_D)),
                  const((MOBA_D, D))],
        out_specs=[pl.BlockSpec((MOBA_D, MOBA_BLOCK), lambda i: (0, i)),
                   pl.BlockSpec((MOBA_HEADS, 1, MOBA_BLOCK, 2 * MOBA_HD), lambda i: (0, i, 0, 0)),
                   pl.BlockSpec((1, 1, MOBA_D), lambda i: (i, 0, 0)),
                   pl.BlockSpec((MOBA_HEADS, 1, MOBA_VROWS, MOBA_BLOCK), lambda i: (0, i, 0, 0))],
        out_shape=[jax.ShapeDtypeStruct((MOBA_D, S), BF16),
                   jax.ShapeDtypeStruct((MOBA_HEADS, nb, MOBA_BLOCK, 2 * MOBA_HD), BF16),
                   jax.ShapeDtypeStruct((nb, 1, MOBA_D), F32),
                   jax.ShapeDtypeStruct((MOBA_HEADS, nb, MOBA_VROWS, MOBA_BLOCK), BF16)],
        compiler_params=_cparams(("parallel",)),
        name="moba_proj",
    )(x, wqt, wk, wvt)


def _moba_kernel(qt_ref, k_ref, vt_ref, kmean_ref, o_ref, sel_ref, s0_ref, s1_ref, p0_ref, p1_ref):
    BS, HD = MOBA_BLOCK, MOBA_HD
    h = pl.program_id(0)
    i = pl.program_id(1)
    nb = k_ref.shape[0]
    qt = qt_ref[...]
    slope = LOG2E * jnp.exp2(jnp.zeros((1, BS), F32) - (h + 1).astype(F32) * (8.0 / MOBA_HEADS))
    slope_hi = slope.astype(qt.dtype).astype(F32)
    slope_lo = slope - slope_hi
    row = _iota2((HD, BS), 0)
    qt_ext = jnp.concatenate(
        [qt, jnp.where(row == 0, slope_hi, jnp.where(row == 1, slope_lo, 0.0)).astype(qt.dtype)], axis=0)

    gate = _dot(kmean_ref[...], qt)
    blk = _iota2((nb, BS), 0).astype(F32)
    cand = blk < i.astype(F32)
    sel = jnp.zeros((nb, BS), F32)
    for _ in range(MOBA_TOPK):
        best = jnp.max(jnp.where(cand, gate, -jnp.inf), 0, keepdims=True)
        idx = jnp.min(jnp.where(cand & (gate == best), blk, float(nb)), 0, keepdims=True)
        pick = blk == idx
        sel = jnp.where(pick, 1.0, sel)
        cand = cand & jnp.logical_not(pick)
    sel_ref[...] = sel

    G = MOBA_GROUP
    last = nb - 1
    s_ref = (s0_ref, s1_ref)
    p_ref = (p0_ref, p1_ref)

    def group(u):
        return [jnp.clip(G * u + x, 0, last) for x in range(G)]

    def issue_scores(u, slot):
        tops = []
        for x, j in enumerate(group(u)):
            sc = _dot(k_ref[j], qt_ext)
            s_ref[slot][x] = sc
            tops.append(jnp.max(sc, 0, keepdims=True))
        return tuple(tops)

    def value_blocks(u):
        js = group(u)
        js[0] = jnp.where(u == -1, i, js[0])
        return js

    def weighted_values(u, slot):
        out = None
        for x, j in enumerate(value_blocks(u)):
            part = _dot(vt_ref[j], p_ref[slot][x])
            out = part if out is None else out + part
        return out

    def step(u, slot, tops, a_prev, m, acc, issue_next=True):
        on, shift = [], []
        m_new = m
        for x, j in enumerate(group(u)):
            on.append(sel_ref[pl.ds(j, 1), :] > 0.0)
            shift.append(slope * ((j - i) * BS).astype(F32))
            m_new = jnp.maximum(m_new, jnp.where(on[x], tops[x] + shift[x], NEG_BIG))
        tops_next, pv = [], None
        for x, (j_prev, j_next) in enumerate(zip(value_blocks(u - 1), group(u + 1))):
            part = _dot(vt_ref[j_prev], p_ref[1 - slot][x])
            pv = part if pv is None else pv + part
            if issue_next:
                sc = _dot(k_ref[j_next], qt_ext)
                s_ref[1 - slot][x] = sc
                tops_next.append(jnp.max(sc, 0, keepdims=True))
            p = jnp.exp2(s_ref[slot][x] - (jnp.where(on[x], m_new, -NEG_BIG) - shift[x]))
            p_ref[slot][x] = p.astype(BF16)
        return tuple(tops_next), jnp.exp2(m - m_new), m_new, a_prev * acc + pv

    def body(w, carry):
        carry = step(2 * w, 0, *carry)
        return step(2 * w + 1, 1, *carry)

    tops0 = issue_scores(0, 0)
    s_own = jnp.where(_iota2((BS, BS), 1) >= _iota2((BS, BS), 0), _dot(k_ref[i], qt_ext), NEG_BIG)
    m_own = jnp.max(s_own, 0, keepdims=True)
    p1_ref[1:G] = jnp.zeros((G - 1, BS, BS), BF16)
    p1_ref[0] = jnp.exp2(s_own - m_own).astype(BF16)
    init = (tops0, jnp.ones((1, BS), F32), m_own, jnp.zeros((vt_ref.shape[1], BS), F32))
    steps = (i + G - 1) // G
    pairs = steps // 2
    carry = lax.fori_loop(0, pairs, body, init)

    def odd_tail(carry):
        _, a_last, _, acc = step(2 * pairs, 0, *carry, issue_next=False)
        return a_last * acc + weighted_values(2 * pairs, 0)

    def even_tail(carry):
        _, a_prev, _, acc = carry
        return a_prev * acc + weighted_values(2 * pairs - 1, 1)

    acc = lax.cond(steps % 2 == 1, odd_tail, even_tail, carry)
    o_ref[...] = (acc[0:HD] / acc[HD:HD + 1]).astype(o_ref.dtype)


def _moba_call(qt, k4, vt4, kmean):
    S = qt.shape[1]
    nb = S // MOBA_BLOCK
    return pl.pallas_call(
        _moba_kernel,
        grid=(MOBA_HEADS, nb),
        in_specs=[pl.BlockSpec((MOBA_HD, MOBA_BLOCK), lambda h, i: (h, i)),
                  pl.BlockSpec((None, nb, MOBA_BLOCK, 2 * MOBA_HD), lambda h, i: (h, 0, 0, 0)),
                  pl.BlockSpec((None, nb, MOBA_VROWS, MOBA_BLOCK), lambda h, i: (h, 0, 0, 0)),
                  pl.BlockSpec((None, nb, MOBA_HD), lambda h, i: (h, 0, 0))],
        out_specs=pl.BlockSpec((MOBA_HD, MOBA_BLOCK), lambda h, i: (h, i)),
        out_shape=jax.ShapeDtypeStruct((MOBA_D, S), BF16),
        scratch_shapes=[pltpu.VMEM((nb, MOBA_BLOCK), F32)]
        + [pltpu.VMEM((MOBA_GROUP, MOBA_BLOCK, MOBA_BLOCK), F32)] * 2
        + [pltpu.VMEM((MOBA_GROUP, MOBA_BLOCK, MOBA_BLOCK), BF16)] * 2,
        compiler_params=_cparams(("parallel", "arbitrary")),
        name="moba",
    )(qt, k4, vt4, kmean)


N_EXPERTS, TOP_K, EXPERT_FF = 8, 2, 2816
MOE_ROWS = 512


def _router_kernel(x_ref, whi_ref, wlo_ref, b_ref, o_ref):
    x = x_ref[...]
    xhi = x.astype(BF16)
    xlo = (x - xhi.astype(F32)).astype(BF16)
    logits = (jnp.dot(xhi, whi_ref[...], preferred_element_type=F32)
              + jnp.dot(xhi, wlo_ref[...], preferred_element_type=F32)
              + jnp.dot(xlo, whi_ref[...], preferred_element_type=F32)) + b_ref[...]
    lane = _iota2(logits.shape, 1).astype(F32)
    logits = jnp.where(lane < N_EXPERTS, logits, -jnp.inf)
    m1 = jnp.max(logits, -1, keepdims=True)
    i1 = jnp.min(jnp.where(logits == m1, lane, 128.0), -1, keepdims=True)
    rest = jnp.where(lane == i1, -jnp.inf, logits)
    m2 = jnp.max(rest, -1, keepdims=True)
    i2 = jnp.min(jnp.where(rest == m2, lane, 128.0), -1, keepdims=True)
    e = jnp.exp(m2 - m1)
    g1 = 1.0 / (1.0 + e)
    g2 = e / (1.0 + e)
    out = jnp.where(lane == 0, i1, 0.0)
    out = jnp.where(lane == 1, i2, out)
    out = jnp.where(lane == 2, g1, out)
    out = jnp.where(lane == 3, g2, out)
    o_ref[...] = out


def _router_call(x, w_router, b_router, tm):
    S, D = x.shape
    wp = jnp.pad(w_router.astype(F32), ((0, 0), (0, 128 - N_EXPERTS)))
    whi = wp.astype(BF16)
    wlo = (wp - whi.astype(F32)).astype(BF16)
    bp = jnp.pad(b_router.astype(F32), (0, 128 - N_EXPERTS)).reshape(1, 128)
    const = lambda shape: pl.BlockSpec(shape, lambda i: (0, 0))
    return pl.pallas_call(
        _router_kernel,
        grid=(S // tm,),
        in_specs=[pl.BlockSpec((tm, D), lambda i: (i, 0)), const((D, 128)), const((D, 128)), const((1, 128))],
        out_specs=pl.BlockSpec((tm, 128), lambda i: (i, 0)),
        out_shape=jax.ShapeDtypeStruct((S, 128), F32),
        compiler_params=_cparams(("parallel",)),
        name="router",
    )(x, whi, wlo, bp)


def _row_copy(src_hbm, dst_ref, src_row, dst_row, sem):
    return pltpu.make_async_copy(src_hbm.at[pl.ds(src_row, 1)], dst_ref.at[pl.ds(dst_row, 1)], sem)


def _scatter_rows_kernel(dest_ref, lo_ref, hi_ref, x_ref, o_hbm, zero_ref, sem):
    step = pl.program_id(0)
    n_tiles = pl.num_programs(0) - 1
    tm = x_ref.shape[0]

    def row_out(src_ref, src_row, dst_row):
        return pltpu.make_async_copy(src_ref.at[pl.ds(src_row, 1)], o_hbm.at[pl.ds(dst_row, 1)], sem)

    @pl.when(step < n_tiles)
    def _():
        base = step * tm

        def start(r, c):
            for slot in range(TOP_K):
                row_out(x_ref, r, dest_ref[TOP_K * (base + r) + slot]).start()
            return c

        lax.fori_loop(0, tm, start, 0, unroll=8)
        for slot in range(TOP_K):
            pltpu.make_async_copy(x_ref, o_hbm.at[pl.ds(0, tm)], sem).wait()

    @pl.when(step == n_tiles)
    def _():
        zero_ref[...] = jnp.zeros_like(zero_ref)
        for e in range(N_EXPERTS):
            def start(r, c):
                row_out(zero_ref, 0, r).start()
                return c

            def wait(r, c):
                row_out(zero_ref, 0, r).wait()
                return c

            lax.fori_loop(lo_ref[e], hi_ref[e], start, 0)
            lax.fori_loop(lo_ref[e], hi_ref[e], wait, 0)


def _scatter_rows(x, dest, pad_lo, pad_hi, n_rows, tm):
    T, D = x.shape
    n_tiles = T // tm
    return pl.pallas_call(
        _scatter_rows_kernel,
        grid_spec=pltpu.PrefetchScalarGridSpec(
            num_scalar_prefetch=3,
            grid=(n_tiles + 1,),
            in_specs=[pl.BlockSpec((tm, D), lambda i, d, lo, hi: (jnp.minimum(i, n_tiles - 1), 0))],
            out_specs=pl.BlockSpec(memory_space=pl.ANY),
            scratch_shapes=[pltpu.VMEM((8, D), x.dtype), pltpu.SemaphoreType.DMA(())]),
        out_shape=jax.ShapeDtypeStruct((n_rows, D), x.dtype),
        compiler_params=_cparams(("arbitrary",)),
        name="moe_scatter",
    )(dest, pad_lo, pad_hi, x)


def _moe_ffn_kernel(be_ref, nu_ref, x_ref, wg_ref, wu_ref, wd_ref, o_ref, acc_ref):
    i = pl.program_id(0)
    j = pl.program_id(1)

    @pl.when(i < nu_ref[0])
    def _():
        xb = x_ref[...].astype(BF16)
        part = _dot(_silu(_dot(xb, wg_ref[...])) * _dot(xb, wu_ref[...]), wd_ref[...])

        @pl.when(j == 0)
        def _():
            acc_ref[...] = part

        @pl.when(j > 0)
        def _():
            acc_ref[...] += part

    last = j == pl.num_programs(1) - 1

    @pl.when(last & (i < nu_ref[0]))
    def _():
        o_ref[...] = acc_ref[...]

    @pl.when(last & (i >= nu_ref[0]))
    def _():
        o_ref[...] = jnp.zeros_like(o_ref)


def _moe_ffn_call(x_rows, block_e, n_used, wg, wu, wd, tf):
    n, D = x_rows.shape
    F = wg.shape[2]
    R = MOE_ROWS
    return pl.pallas_call(
        _moe_ffn_kernel,
        grid_spec=pltpu.PrefetchScalarGridSpec(
            num_scalar_prefetch=2,
            grid=(n // R, F // tf),
            in_specs=[pl.BlockSpec((R, D), lambda i, j, be, nu: (i, 0)),
                      pl.BlockSpec((None, D, tf), lambda i, j, be, nu: (be[i], 0, j)),
                      pl.BlockSpec((None, D, tf), lambda i, j, be, nu: (be[i], 0, j)),
                      pl.BlockSpec((None, tf, D), lambda i, j, be, nu: (be[i], j, 0))],
            out_specs=pl.BlockSpec((R, D), lambda i, j, be, nu: (i, 0)),
            scratch_shapes=[pltpu.VMEM((R, D), F32)]),
        out_shape=jax.ShapeDtypeStruct((n, D), F32),
        compiler_params=_cparams(("arbitrary", "arbitrary")),
        name="moe_ffn",
    )(block_e, n_used, x_rows, wg, wu, wd)


def _moe_combine_kernel(d_ref, y_hbm, x_ref, r_ref, g_ref, b_ref, o_ref, y1_ref, y2_ref, sem):
    tm = x_ref.shape[0]
    base = pl.program_id(0) * tm

    def start(r, c):
        _row_copy(y_hbm, y1_ref, d_ref[2 * (base + r)], r, sem.at[0]).start()
        _row_copy(y_hbm, y2_ref, d_ref[2 * (base + r) + 1], r, sem.at[1]).start()
        return c

    lax.fori_loop(0, tm, start, 0, unroll=8)
    pltpu.make_async_copy(y_hbm.at[pl.ds(0, tm)], y1_ref, sem.at[0]).wait()
    pltpu.make_async_copy(y_hbm.at[pl.ds(0, tm)], y2_ref, sem.at[1]).wait()
    y = r_ref[:, 2:3] * y1_ref[...] + r_ref[:, 3:4] * y2_ref[...]
    o_ref[...] = _layer_norm_rows(DN_ALPHA * x_ref[...] + y, g_ref[...], b_ref[...])


def _moe_combine_call(dest, y_rows, x, routed, g, b, tm):
    S, D = x.shape
    return pl.pallas_call(
        _moe_combine_kernel,
        grid_spec=pltpu.PrefetchScalarGridSpec(
            num_scalar_prefetch=1,
            grid=(S // tm,),
            in_specs=[pl.BlockSpec(memory_space=pl.ANY),
                      pl.BlockSpec((tm, D), lambda i, d: (i, 0)),
                      pl.BlockSpec((tm, 128), lambda i, d: (i, 0)),
                      pl.BlockSpec((1, D), lambda i, d: (0, 0)),
                      pl.BlockSpec((1, D), lambda i, d: (0, 0))],
            out_specs=pl.BlockSpec((tm, D), lambda i, d: (i, 0)),
            scratch_shapes=[pltpu.VMEM((tm, D), F32), pltpu.VMEM((tm, D), F32), pltpu.SemaphoreType.DMA((2,))]),
        out_shape=jax.ShapeDtypeStruct((S, D), F32),
        compiler_params=_cparams(("arbitrary",)),
        name="moe_combine",
    )(dest, y_rows, x, routed, g.reshape(1, D), b.reshape(1, D))


def _moe_sublayer(x, w_router, b_router, wg, wu, wd, g, b):
    T = x.shape[0]
    R = MOE_ROWS
    routed = _router_call(x, w_router, b_router, min(512, T))
    top_e = routed[:, 0:TOP_K].astype(jnp.int32)
    tok_oh = jnp.sum((top_e[:, :, None] == jnp.arange(N_EXPERTS)[None, None, :]).astype(jnp.int32), axis=1)
    counts = jnp.sum(tok_oh, axis=0)
    rank = jnp.cumsum(tok_oh, axis=0) - tok_oh
    padded = (counts + R - 1) // R * R
    pend = jnp.cumsum(padded)
    pstart = pend - padded
    dest = pstart[top_e] + jnp.take_along_axis(rank, top_e, axis=1)
    n_rows = (T * TOP_K + N_EXPERTS * (R - 1)) // R * R
    n_blocks = n_rows // R
    block_first_row = jnp.arange(n_blocks, dtype=jnp.int32) * R
    block_e = jnp.minimum(jnp.sum((pend[None, :] <= block_first_row[:, None]).astype(jnp.int32), axis=1),
                          N_EXPERTS - 1)
    n_used = (pend[-1] // R).astype(jnp.int32).reshape(1)
    dest = dest.reshape(-1).astype(jnp.int32)
    pad_lo = (pstart + counts).astype(jnp.int32)
    pad_hi = jnp.concatenate([pstart[1:], jnp.array([n_rows])]).astype(jnp.int32)
    x_rows = _scatter_rows(x, dest, pad_lo, pad_hi, n_rows, min(512, T))
    y_rows = _moe_ffn_call(x_rows, block_e, n_used, wg, wu, wd, EXPERT_FF // 2)
    return _moe_combine_call(dest, y_rows, x, routed, g, b, min(256, T))


GLA_IN = 2 * GLA_QK + 2 * GLA_V + GLA_LR


def _gla_rwkv_sublayer(x, w_in, gla_wa2, gla_ba, gla_norm, mu, w0, w2, a0, a2, g2, k_k, k_a, r_k, ln_w, ln_b,
                       w_out, ln_g, ln_bias):
    S = x.shape[0]
    tm = min(512, S)
    lr_pad = 128 - GLA_LR
    w_gla = jnp.pad(w_in[:, :GLA_IN], ((0, 0), (0, lr_pad))).astype(BF16)
    w_rwkv = w_in[:, GLA_IN:].astype(BF16)
    p_gla = _matmul(x, w_gla, F32, tm, GLA_COLS)
    p_rwkv = _matmul(x, w_rwkv, F32, tm, RWKV_COLS)
    wa2p = jnp.pad(gla_wa2, ((0, lr_pad), (0, 0))).astype(BF16)
    o_gla = _gla_call(p_gla, wa2p, gla_ba, gla_norm)
    o_rwkv = _rwkv_call(p_rwkv, mu, w0, w2, a0, a2, g2, k_k, k_a, r_k, ln_w, ln_b)
    return _matmul_ln([o_gla, o_rwkv], [w_out[:GLA_V].astype(BF16), w_out[GLA_V:].astype(BF16)], x,
                      ln_g, ln_bias, tm)


def _xattn_sublayer(x, mem, wq, wk, wv, wo, ln_g, ln_bias):
    M = mem.shape[0]
    k_mem = _matmul(mem, wk.astype(BF16), BF16, M, D_MODEL)
    v_mem = _matmul(mem, wv.astype(BF16), BF16, M, D_MODEL)
    wq_scaled = (wq * XATTN_HD ** -0.5).astype(BF16)
    return _xattn_call(x, wq_scaled, k_mem, v_mem, wo.astype(BF16), ln_g, ln_bias, min(512, x.shape[0]))


def _ssd_moba_sublayer(x, w_in, conv_w, conv_b, dt_bias, a_log, d_skip, ssd_norm, w_out, ln_g, ln_bias):
    S = x.shape[0]
    tm = min(512, S)
    nb = S // MOBA_BLOCK
    o_dt = SSD_COLS
    o_q = o_dt + SSD_HEADS
    w_ssd = jnp.pad(w_in[:, :o_q], ((0, 0), (0, 128 - SSD_HEADS))).astype(BF16)
    w_q = w_in[:, o_q:o_q + MOBA_D]
    w_k = w_in[:, o_q + MOBA_D:o_q + 2 * MOBA_D]
    w_v = w_in[:, o_q + 2 * MOBA_D:]
    p_ssd = _matmul(x, w_ssd, F32, tm, (SSD_COLS + 128) // 3)
    o_ssd = _ssd_call(p_ssd, conv_w, conv_b, dt_bias, a_log, d_skip, ssd_norm)
    qt, k4, kmean, vt4 = _moba_proj_call(x, (w_q.T * (MOBA_HD ** -0.5 * LOG2E)).astype(BF16), w_k.astype(BF16),
                                         w_v.T.astype(BF16))
    kmean_h = kmean.reshape(nb, MOBA_HEADS, MOBA_HD).transpose(1, 0, 2)
    ot_moba = _moba_call(qt, k4, vt4, kmean_h)
    return _matmul_ln([o_ssd, ot_moba], [w_out[:SSD_INNER].astype(BF16), w_out[SSD_INNER:].astype(BF16)], x,
                      ln_g, ln_bias, tm, transposed=(False, True))


def kernel(x, mem, l0_w_in, l0_gla_wa2, l0_gla_ba, l0_gla_norm, l0_rwkv_mu, l0_rwkv_w0, l0_rwkv_w2, l0_rwkv_a0, l0_rwkv_a2, l0_rwkv_g2, l0_rwkv_kk, l0_rwkv_ka, l0_rwkv_rk, l0_rwkv_lnw, l0_rwkv_lnb, l0_w_out, l0_ln1_g, l0_ln1_b, l0_xq, l0_xk, l0_xv, l0_xo, l0_ln2_g, l0_ln2_b, l0_ffn_wg, l0_ffn_wu, l0_ffn_wd, l0_ln3_g, l0_ln3_b, l1_w_in, l1_conv_w, l1_conv_b, l1_dt_bias, l1_a_log, l1_d_skip, l1_ssd_norm, l1_w_out, l1_ln1_g, l1_ln1_b, l1_xq, l1_xk, l1_xv, l1_xo, l1_ln2_g, l1_ln2_b, l1_router, l1_router_b, l1_exp_wg, l1_exp_wu, l1_exp_wd, l1_ln3_g, l1_ln3_b):
    x2 = x.reshape(-1, D_MODEL)
    mem2 = mem.reshape(-1, D_MODEL)
    x2 = _gla_rwkv_sublayer(x2, l0_w_in, l0_gla_wa2, l0_gla_ba, l0_gla_norm, l0_rwkv_mu, l0_rwkv_w0, l0_rwkv_w2,
                            l0_rwkv_a0, l0_rwkv_a2, l0_rwkv_g2, l0_rwkv_kk, l0_rwkv_ka, l0_rwkv_rk, l0_rwkv_lnw,
                            l0_rwkv_lnb, l0_w_out, l0_ln1_g, l0_ln1_b)
    x2 = _xattn_sublayer(x2, mem2, l0_xq, l0_xk, l0_xv, l0_xo, l0_ln2_g, l0_ln2_b)
    tm = min(512, x2.shape[0])
    x2 = _ffn_call(x2, l0_ffn_wg.astype(BF16), l0_ffn_wu.astype(BF16), l0_ffn_wd.astype(BF16),
                   l0_ln3_g, l0_ln3_b, tm, l0_ffn_wg.shape[1] // 2)
    x2 = _ssd_moba_sublayer(x2, l1_w_in, l1_conv_w, l1_conv_b, l1_dt_bias, l1_a_log, l1_d_skip, l1_ssd_norm,
                            l1_w_out, l1_ln1_g, l1_ln1_b)
    x2 = _xattn_sublayer(x2, mem2, l1_xq, l1_xk, l1_xv, l1_xo, l1_ln2_g, l1_ln2_b)
    x2 = _moe_sublayer(x2, l1_router, l1_router_b, l1_exp_wg.astype(BF16), l1_exp_wu.astype(BF16),
                       l1_exp_wd.astype(BF16), l1_ln3_g, l1_ln3_b)
    return x2.reshape(x.shape)
```

```python
import functools
import math

import jax
import jax.numpy as jnp
from jax import lax
from jax.experimental import pallas as pl
from jax.experimental.pallas import tpu as pltpu

BF16 = jnp.bfloat16
F32 = jnp.float32

D_MODEL = 1024
LN_EPS = 1e-5
DEPTH = 2
DN_ALPHA = (2 * DEPTH) ** 0.25

GLA_HEADS, GLA_DK, GLA_DV, GLA_CHUNK = 4, 64, 128, 64
GLA_QK, GLA_V, GLA_LR, GLA_TAU = 256, 512, 16, 16.0
GLA_COLS = 2 * GLA_QK + 2 * GLA_V + 128

RWKV_HEADS, RWKV_HD, RWKV_D, RWKV_CHUNK = 8, 64, 512, 64
RWKV_COLS = 1792
RWKV_DECAY_SCALE = math.exp(-0.5)
RWKV_GN_EPS = 64e-5

VMEM_LIMIT = 56 * 1024 * 1024


def _cparams(sem):
    return pltpu.CompilerParams(dimension_semantics=sem, vmem_limit_bytes=VMEM_LIMIT)


def _dot(a, b):
    return jnp.dot(a.astype(BF16), b.astype(BF16), preferred_element_type=F32)


def _dot_nt(a, b):
    return lax.dot_general(a.astype(BF16), b.astype(BF16), (((1,), (1,)), ((), ())), preferred_element_type=F32)


def _dot_tn(a, b):
    return lax.dot_general(a.astype(BF16), b.astype(BF16), (((0,), (0,)), ((), ())), preferred_element_type=F32)


def _split3(x):
    hi = x.astype(BF16)
    r1 = x - hi.astype(F32)
    mid = r1.astype(BF16)
    lo = (r1 - mid.astype(F32)).astype(BF16)
    return hi, mid, lo


def _dot_exact_lhs(m, x):
    mb = m.astype(BF16)
    hi, mid, lo = _split3(x)
    return (jnp.dot(mb, hi, preferred_element_type=F32) + jnp.dot(mb, mid, preferred_element_type=F32)
            + jnp.dot(mb, lo, preferred_element_type=F32))


def _dot_exact_rhs(x, m):
    mb = m.astype(BF16)
    hi, mid, lo = _split3(x)
    return (jnp.dot(hi, mb, preferred_element_type=F32) + jnp.dot(mid, mb, preferred_element_type=F32)
            + jnp.dot(lo, mb, preferred_element_type=F32))


def _sigmoid(x):
    return 1.0 / (1.0 + jnp.exp(-x))


def _silu(x):
    return x * _sigmoid(x)


def _iota2(shape, axis):
    return lax.broadcasted_iota(jnp.int32, shape, axis)


def _chunk_tril(n, chunk):
    r = _iota2((n, n), 0)
    c = _iota2((n, n), 1)
    return jnp.where((c <= r) & ((r // chunk) == (c // chunk)), 1.0, 0.0)


def _head_block(n, width, value):
    r = _iota2((n, n), 0)
    c = _iota2((n, n), 1)
    return jnp.where((r // width) == (c // width), value, 0.0)


def _mm_kernel(x_ref, w_ref, o_ref):
    o_ref[...] = _dot(x_ref[...], w_ref[...]).astype(o_ref.dtype)


def _matmul(x, w, out_dtype, tm, tn):
    S, K = x.shape
    N = w.shape[1]
    return pl.pallas_call(
        _mm_kernel,
        grid=(S // tm, N // tn),
        in_specs=[pl.BlockSpec((tm, K), lambda i, j: (i, 0)),
                  pl.BlockSpec((K, tn), lambda i, j: (0, j))],
        out_specs=pl.BlockSpec((tm, tn), lambda i, j: (i, j)),
        out_shape=jax.ShapeDtypeStruct((S, N), out_dtype),
        compiler_params=_cparams(("parallel", "arbitrary")),
        name="matmul",
    )(x, w)


def _layer_norm_rows(y, g, b):
    mu = jnp.mean(y, -1, keepdims=True)
    d = y - mu
    var = jnp.mean(d * d, -1, keepdims=True)
    return d * lax.rsqrt(var + LN_EPS) * g + b


def _mm_ln_kernel(transposed, *refs):
    n_in = len(transposed)
    a_refs = refs[:n_in]
    w_refs = refs[n_in:2 * n_in]
    x_ref, g_ref, b_ref, o_ref = refs[2 * n_in:]
    acc = DN_ALPHA * x_ref[...]
    for a_ref, w_ref, tr in zip(a_refs, w_refs, transposed):
        acc = acc + (_dot_tn if tr else _dot)(a_ref[...], w_ref[...])
    o_ref[...] = _layer_norm_rows(acc, g_ref[...], b_ref[...])


def _matmul_ln(a_list, w_list, x, g, b, tm, transposed=None):
    S, D = x.shape
    transposed = tuple(transposed or (False,) * len(a_list))
    in_specs = ([pl.BlockSpec((a.shape[0], tm), lambda i: (0, i)) if tr else
                 pl.BlockSpec((tm, a.shape[1]), lambda i: (i, 0)) for a, tr in zip(a_list, transposed)]
                + [pl.BlockSpec(w.shape, lambda i: (0, 0)) for w in w_list]
                + [pl.BlockSpec((tm, D), lambda i: (i, 0)),
                   pl.BlockSpec((1, D), lambda i: (0, 0)),
                   pl.BlockSpec((1, D), lambda i: (0, 0))])
    return pl.pallas_call(
        functools.partial(_mm_ln_kernel, transposed),
        grid=(S // tm,),
        in_specs=in_specs,
        out_specs=pl.BlockSpec((tm, D), lambda i: (i, 0)),
        out_shape=jax.ShapeDtypeStruct((S, D), F32),
        compiler_params=_cparams(("parallel",)),
        name="matmul_ln",
    )(*a_list, *w_list, x, g.reshape(1, D), b.reshape(1, D))


def _gla_kernel(p_ref, wa2_ref, ba_ref, ng_ref, o_ref, st_ref, o_scr):
    C, H, dk, dv = GLA_CHUNK, GLA_HEADS, GLA_DK, GLA_DV
    tb = p_ref.shape[0]

    @pl.when(pl.program_id(0) == 0)
    def _():
        st_ref[...] = jnp.zeros_like(st_ref)

    z = _dot(p_ref[:, 2 * GLA_QK + 2 * GLA_V:], wa2_ref[...]) + ba_ref[...]
    log_a = -(jnp.maximum(-z, 0.0) + jnp.log(1.0 + jnp.exp(-jnp.abs(z)))) / GLA_TAU
    b = _dot_exact_lhs(_chunk_tril(tb, C), log_a)
    causal = _iota2((C, C), 1) <= _iota2((C, C), 0)

    nc = tb // C
    q_h, k_h, ke_h, v_h, dec_h = [], [], [], [], []
    for c in range(nc):
        rows = slice(c * C, (c + 1) * C)
        b_c = b[rows]
        b_last = b_c[C - 1:C]
        q_dec = p_ref[rows, 0:GLA_QK] * (dk ** -0.5) * jnp.exp(b_c)
        k_c = p_ref[rows, GLA_QK:2 * GLA_QK]
        k_dec = k_c * jnp.exp(-b_c)
        k_end = k_c * jnp.exp(b_last - b_c)
        decay = jnp.exp(b_last)
        for h in range(H):
            ks = slice(h * dk, (h + 1) * dk)
            q_h.append(q_dec[:, ks])
            k_h.append(k_dec[:, ks])
            ke_h.append(k_end[:, ks])
            dec_h.append(decay[:, ks])
            v_h.append(p_ref[rows, 2 * GLA_QK + h * dv:2 * GLA_QK + (h + 1) * dv])
    n = nc * H
    attn = [jnp.where(causal, _dot_nt(q_h[i], k_h[i]), 0.0) for i in range(n)]
    kv = [_dot_tn(v_h[i], ke_h[i]) for i in range(n)]
    intra = [_dot(attn[i], v_h[i]) for i in range(n)]
    state = [st_ref[:, h * dk:(h + 1) * dk] for h in range(H)]
    entering = []
    for i in range(n):
        entering.append(state[i % H])
        state[i % H] = state[i % H] * dec_h[i] + kv[i]
    for i in range(n):
        c, h = divmod(i, H)
        o_scr[c * C:(c + 1) * C, h * dv:(h + 1) * dv] = intra[i] + _dot_nt(q_h[i], entering[i])
    for h in range(H):
        st_ref[:, h * dk:(h + 1) * dk] = state[h]

    for h in range(H):
        vs = slice(h * dv, (h + 1) * dv)
        o_h = o_scr[:, vs]
        g_h = p_ref[:, 2 * GLA_QK + GLA_V + h * dv:2 * GLA_QK + GLA_V + (h + 1) * dv]
        o_h = o_h * lax.rsqrt(jnp.mean(o_h * o_h, -1, keepdims=True) + 1e-5) * ng_ref[:, vs]
        o_ref[:, vs] = (o_h * _silu(g_h)).astype(o_ref.dtype)


def _gla_call(p_gla, wa2p, ba, norm_g, tb=256):
    S = p_gla.shape[0]
    return pl.pallas_call(
        _gla_kernel,
        grid=(S // tb,),
        in_specs=[pl.BlockSpec((tb, GLA_COLS), lambda i: (i, 0)),
                  pl.BlockSpec((128, GLA_QK), lambda i: (0, 0)),
                  pl.BlockSpec((1, GLA_QK), lambda i: (0, 0)),
                  pl.BlockSpec((1, GLA_V), lambda i: (0, 0))],
        out_specs=pl.BlockSpec((tb, GLA_V), lambda i: (i, 0)),
        out_shape=jax.ShapeDtypeStruct((S, GLA_V), BF16),
        scratch_shapes=[pltpu.VMEM((GLA_DV, GLA_QK), F32), pltpu.VMEM((tb, GLA_V), F32)],
        compiler_params=_cparams(("arbitrary",)),
        name="gla",
    )(p_gla, wa2p, ba.reshape(1, GLA_QK), norm_g.reshape(1, GLA_V))


def _rwkv_kernel(p_ref, mu_ref, w0_ref, w2_ref, a0_ref, a2_ref, g2_ref, kk_ref, ka_ref, rk_ref, lnw_ref, lnb_ref,
                 o_ref, prev_ref, h_ref, o_scr):
    C, H, N, D = RWKV_CHUNK, RWKV_HEADS, RWKV_HD, RWKV_D
    tb = p_ref.shape[0]
    first = pl.program_id(0) == 0

    @pl.when(first)
    def _():
        prev_ref[...] = jnp.zeros_like(prev_ref)
        h_ref[...] = jnp.zeros_like(h_ref)

    p = p_ref[...]
    shifted = jnp.where(_iota2(p.shape, 0) == 0, prev_ref[...], pltpu.roll(p, 1, 0))
    prev_ref[...] = p[tb - 1:tb]
    p = p + mu_ref[...] * (shifted - p)
    r = p[:, 0:D]
    k = p[:, D:2 * D]
    v = p[:, 2 * D:3 * D]
    xw = p[:, 3 * D:3 * D + 64]
    xa = p[:, 3 * D + 64:3 * D + 128]
    xg = p[:, 3 * D + 128:3 * D + 256]
    lw = -RWKV_DECAY_SCALE * _sigmoid(w0_ref[...] + _dot(jnp.tanh(xw), w2_ref[...]))
    a = _sigmoid(a0_ref[...] + _dot(xa, a2_ref[...]))
    g = _dot(_sigmoid(xg), g2_ref[...])
    head_ones = _head_block(D, N, 1.0)
    kk = k * kk_ref[...]
    kk = kk * lax.rsqrt(jnp.maximum(_dot_exact_rhs(kk * kk, head_ones), 1e-24))
    k = k * (1.0 + (a - 1.0) * ka_ref[...])
    pv = -kk * a
    cw = _dot_exact_lhs(_chunk_tril(tb, C), lw)
    cwx = cw - lw

    gi = _iota2((2 * C, 2 * C), 0)
    gj = _iota2((2 * C, 2 * C), 1) % C
    gram_mask = ((gi < C) & (gj < gi)) | ((gi >= C) & (gj <= gi - C))
    eye = _iota2((C, C), 0) == _iota2((C, C), 1)
    eye_f = jnp.where(eye, 1.0, 0.0)

    nc = tb // C
    items = [(c, h) for c in range(nc) for h in range(H)]
    xs, ys, pk_e, b_h, v_h, r_h, g_h = [], [], [], [], [], [], []
    for c in range(nc):
        rows = slice(c * C, (c + 1) * C)
        cw_c = cw[rows]
        cw_end = cw_c[C - 1:C]
        e_pos = jnp.exp(cw_c)
        e_neg = jnp.exp(-cw_c)
        e_end = jnp.exp(cw_end - cw_c)
        r_t = r[rows] * e_pos
        b_t = kk[rows] * jnp.exp(cwx[rows])
        p_t = pv[rows] * e_neg
        k_t = k[rows] * e_neg
        p_e = pv[rows] * e_end
        k_e = k[rows] * e_end
        g_end = jnp.exp(cw_end)
        v_c = v[rows]
        for h in range(H):
            hs = slice(h * N, (h + 1) * N)
            xs.append(jnp.concatenate([b_t[:, hs], r_t[:, hs]], axis=0))
            ys.append(jnp.concatenate([p_t[:, hs], k_t[:, hs]], axis=0))
            pk_e.append(jnp.concatenate([p_e[:, hs], k_e[:, hs]], axis=0))
            b_h.append(b_t[:, hs])
            v_h.append(v_c[:, hs])
            r_h.append(r_t[:, hs])
            g_h.append(g_end[:, hs])
    n = len(items)
    grams = [jnp.where(gram_mask, _dot_nt(xs[i], ys[i]), 0.0) for i in range(n)]
    l_p = [g[0:C, 0:C] for g in grams]
    m_pk = [g[C:2 * C, :] for g in grams]
    lkv = [_dot(grams[i][0:C, C:2 * C], v_h[i]) for i in range(n)]
    x = [_dot(lp, lp) for lp in l_p]
    t = [eye_f + lp for lp in l_p]
    for _ in range(4):
        tx = [_dot(jnp.concatenate([t[i], x[i]], axis=0), x[i]) for i in range(n)]
        t = [t[i] + tx[i][0:C] for i in range(n)]
        x = [tx[i][C:2 * C] for i in range(n)]
    t = [t[i] + _dot(t[i], x[i]) for i in range(n)]
    wu = [_dot(t[i], jnp.concatenate([b_h[i], lkv[i]], axis=1)) for i in range(n)]
    rhs = [jnp.concatenate([wu[i], jnp.concatenate([jnp.zeros((C, N), F32), v_h[i]], axis=1)], axis=0)
           for i in range(n)]
    az = [_dot_tn(pk_e[i], rhs[i]) for i in range(n)]
    qo = [_dot(m_pk[i], rhs[i]) for i in range(n)]
    state = [h_ref[h] for h in range(H)]
    for i, (c, h) in enumerate(items):
        a_mat = az[i][:, 0:N] + jnp.where(eye, g_h[i], 0.0)
        q_mat = qo[i][:, 0:N] + r_h[i]
        oh = _dot(jnp.concatenate([q_mat, a_mat], axis=0), state[h])
        o_scr[c * C:(c + 1) * C, h * N:(h + 1) * N] = oh[0:C] + qo[i][:, N:2 * N]
        state[h] = oh[C:C + N] + az[i][:, N:2 * N]
    for h in range(H):
        h_ref[h] = state[h]

    o = o_scr[...]
    head_mean = _head_block(D, N, 1.0 / N)
    mean = _dot_exact_rhs(o, head_mean)
    d = o - mean
    var = _dot_exact_rhs(d * d, head_mean)
    o = d * lax.rsqrt(var + RWKV_GN_EPS) * lnw_ref[...] + lnb_ref[...]
    bonus = _dot_exact_rhs(r * k * rk_ref[...], head_ones) * v
    o_ref[...] = ((o + bonus) * g).astype(o_ref.dtype)


def _rwkv_call(p_rwkv, mu, w0, w2, a0, a2, g2, k_k, k_a, r_k, ln_w, ln_b, tb=256):
    S = p_rwkv.shape[0]
    D = RWKV_D
    row = lambda t: t.reshape(1, -1).astype(F32)
    full = lambda shape: pl.BlockSpec(shape, lambda i: tuple(0 for _ in shape))
    return pl.pallas_call(
        _rwkv_kernel,
        grid=(S // tb,),
        in_specs=[pl.BlockSpec((tb, RWKV_COLS), lambda i: (i, 0)),
                  full((1, RWKV_COLS)), full((1, D)), full((64, D)), full((1, D)), full((64, D)), full((128, D)),
                  full((1, D)), full((1, D)), full((1, D)), full((1, D)), full((1, D))],
        out_specs=pl.BlockSpec((tb, D), lambda i: (i, 0)),
        out_shape=jax.ShapeDtypeStruct((S, D), BF16),
        scratch_shapes=[pltpu.VMEM((1, RWKV_COLS), F32),
                        pltpu.VMEM((RWKV_HEADS, RWKV_HD, RWKV_HD), F32),
                        pltpu.VMEM((tb, D), F32)],
        compiler_params=_cparams(("arbitrary",)),
        name="rwkv7",
    )(p_rwkv, row(mu), row(w0), w2.astype(BF16), row(a0), a2.astype(BF16), g2.astype(BF16),
      row(k_k), row(k_a), row(r_k), row(ln_w), row(ln_b))


XATTN_HEADS, XATTN_HD = 4, 256


def _xattn_kernel(x_ref, wq_ref, k_ref, v_ref, wo_ref, g_ref, b_ref, o_ref):
    x = x_ref[...]
    q = _dot(x, wq_ref[...])
    outs = []
    for h in range(XATTN_HEADS):
        hs = slice(h * XATTN_HD, (h + 1) * XATTN_HD)
        s = _dot_nt(q[:, hs], k_ref[:, hs])
        e = jnp.exp(s - jnp.max(s, -1, keepdims=True))
        p = e / jnp.sum(e, -1, keepdims=True)
        outs.append(_dot(p, v_ref[:, hs]))
    o = jnp.concatenate(outs, axis=1)
    y = DN_ALPHA * x + _dot(o, wo_ref[...])
    o_ref[...] = _layer_norm_rows(y, g_ref[...], b_ref[...])


def _xattn_call(x, wq_scaled, k_mem, v_mem, wo, g, b, tm):
    S, D = x.shape
    M = k_mem.shape[0]
    const = lambda shape: pl.BlockSpec(shape, lambda i: (0, 0))
    return pl.pallas_call(
        _xattn_kernel,
        grid=(S // tm,),
        in_specs=[pl.BlockSpec((tm, D), lambda i: (i, 0)), const((D, D)), const((M, D)), const((M, D)),
                  const((D, D)), const((1, D)), const((1, D))],
        out_specs=pl.BlockSpec((tm, D), lambda i: (i, 0)),
        out_shape=jax.ShapeDtypeStruct((S, D), F32),
        compiler_params=_cparams(("parallel",)),
        name="xattn",
    )(x, wq_scaled, k_mem, v_mem, wo, g.reshape(1, D), b.reshape(1, D))


def _ffn_kernel(x_ref, wg_ref, wu_ref, wd_ref, g_ref, b_ref, o_ref, acc_ref):
    j = pl.program_id(1)
    x = x_ref[...]
    xb = x.astype(BF16)
    part = _dot(_silu(_dot(xb, wg_ref[...])) * _dot(xb, wu_ref[...]), wd_ref[...])

    @pl.when(j == 0)
    def _():
        acc_ref[...] = DN_ALPHA * x + part

    @pl.when(j > 0)
    def _():
        acc_ref[...] += part

    @pl.when(j == pl.num_programs(1) - 1)
    def _():
        o_ref[...] = _layer_norm_rows(acc_ref[...], g_ref[...], b_ref[...])


def _ffn_call(x, wg, wu, wd, g, b, tm, tf):
    S, D = x.shape
    F = wg.shape[1]
    return pl.pallas_call(
        _ffn_kernel,
        grid=(S // tm, F // tf),
        in_specs=[pl.BlockSpec((tm, D), lambda i, j: (i, 0)),
                  pl.BlockSpec((D, tf), lambda i, j: (0, j)),
                  pl.BlockSpec((D, tf), lambda i, j: (0, j)),
                  pl.BlockSpec((tf, D), lambda i, j: (j, 0)),
                  pl.BlockSpec((1, D), lambda i, j: (0, 0)),
                  pl.BlockSpec((1, D), lambda i, j: (0, 0))],
        out_specs=pl.BlockSpec((tm, D), lambda i, j: (i, 0)),
        out_shape=jax.ShapeDtypeStruct((S, D), F32),
        scratch_shapes=[pltpu.VMEM((tm, D), F32)],
        compiler_params=_cparams(("parallel", "arbitrary")),
        name="ffn",
    )(x, wg, wu, wd, g.reshape(1, D), b.reshape(1, D))


SSD_HD, SSD_HEADS, SSD_INNER, SSD_GROUPS, SSD_STATE = 64, 16, 1024, 2, 128
SSD_BC, SSD_CONV, SSD_CONV_CH, SSD_CHUNK = 256, 4, 1536, 128
SSD_COLS = SSD_INNER + SSD_CONV_CH
SSD_GW = SSD_INNER // SSD_GROUPS


def _softplus(x):
    return jnp.maximum(x, 0.0) + jnp.log(1.0 + jnp.exp(-jnp.abs(x)))


def _ssd_kernel(p_ref, cw_ref, cb_ref, dtb_ref, a_ref, dsk_ref, ng_ref, o_ref, prev_ref, st_ref, y_scr):
    L, G, NS, HD = SSD_CHUNK, SSD_GROUPS, SSD_STATE, SSD_HD
    HG = SSD_HEADS // G

    @pl.when(pl.program_id(0) == 0)
    def _():
        prev_ref[...] = jnp.zeros_like(prev_ref)
        st_ref[...] = jnp.zeros_like(st_ref)

    cur = p_ref[:, SSD_INNER:SSD_COLS]
    prev = prev_ref[...]
    row = _iota2(cur.shape, 0)
    conv = cur * cw_ref[SSD_CONV - 1:SSD_CONV, :] + cb_ref[...]
    for kk in range(1, SSD_CONV):
        shifted = jnp.where(row < kk, pltpu.roll(prev, kk, 0), pltpu.roll(cur, kk, 0))
        conv = conv + shifted * cw_ref[SSD_CONV - 1 - kk:SSD_CONV - kk, :]
    prev_ref[...] = cur
    xbc = _silu(conv)
    xs = xbc[:, :SSD_INNER]

    dt = _softplus(p_ref[:, SSD_COLS:] + dtb_ref[...])
    a_col = dt * a_ref[...]
    li = _iota2((L, L), 0)
    lj = _iota2((L, L), 1)
    cs = _dot_exact_lhs(jnp.where(lj <= li, 1.0, 0.0), a_col)
    cs_row = cs.T
    expand = jnp.where(_iota2((128, SSD_INNER), 1) // HD == _iota2((128, SSD_INNER), 0), 1.0, 0.0)
    dt_x = _dot_exact_rhs(dt, expand)
    cs_x = _dot_exact_rhs(cs, expand)
    cs_end = cs_x[L - 1:L]
    xd = xs * dt_x
    xd_dec = xd * jnp.exp(cs_end - cs_x)
    out_dec = jnp.exp(cs_x)
    chunk_dec = jnp.exp(cs_end)
    tril = lj <= li

    for g in range(G):
        gs = slice(g * SSD_GW, (g + 1) * SSD_GW)
        b_g = xbc[:, SSD_INNER + g * NS:SSD_INNER + (g + 1) * NS]
        c_g = xbc[:, SSD_INNER + SSD_BC + g * NS:SSD_INNER + SSD_BC + (g + 1) * NS]
        cb = _dot_nt(c_g, b_g)
        for j in range(HG):
            h = g * HG + j
            hs = slice(h * HD, (h + 1) * HD)
            seg = jnp.where(tril, jnp.exp(cs[:, h:h + 1] - cs_row[h:h + 1, :]), 0.0)
            y_scr[:, hs] = _dot(cb * seg, xd[:, hs])
        st = st_ref[g]
        y_off = _dot(c_g, st) * out_dec[:, gs]
        st_ref[g] = st * chunk_dec[:, gs] + _dot_tn(b_g, xd_dec[:, gs])
        y_scr[:, gs] = y_scr[:, gs] + y_off

    y = (y_scr[...] + dsk_ref[...] * xs) * _silu(p_ref[:, :SSD_INNER])
    for g in range(G):
        gs = slice(g * SSD_GW, (g + 1) * SSD_GW)
        y_g = y[:, gs]
        o_ref[:, gs] = (y_g * lax.rsqrt(jnp.mean(y_g * y_g, -1, keepdims=True) + 1e-5) * ng_ref[:, gs]).astype(o_ref.dtype)


def _ssd_call(p_ssd, conv_w, conv_b, dt_bias, a_log, d_skip, norm_g):
    S = p_ssd.shape[0]
    L = SSD_CHUNK
    a_neg = -jnp.exp(a_log.astype(F32))
    pad = lambda t: jnp.pad(t.astype(F32), (0, 128 - SSD_HEADS)).reshape(1, 128)
    const = lambda shape: pl.BlockSpec(shape, lambda i: (0, 0))
    return pl.pallas_call(
        _ssd_kernel,
        grid=(S // L,),
        in_specs=[pl.BlockSpec((L, SSD_COLS + 128), lambda i: (i, 0)),
                  const((SSD_CONV, SSD_CONV_CH)), const((1, SSD_CONV_CH)),
                  const((1, 128)), const((1, 128)),
                  const((1, SSD_INNER)), const((1, SSD_INNER))],
        out_specs=pl.BlockSpec((L, SSD_INNER), lambda i: (i, 0)),
        out_shape=jax.ShapeDtypeStruct((S, SSD_INNER), BF16),
        scratch_shapes=[pltpu.VMEM((L, SSD_CONV_CH), F32),
                        pltpu.VMEM((SSD_GROUPS, SSD_STATE, SSD_GW), F32),
                        pltpu.VMEM((L, SSD_INNER), F32)],
        compiler_params=_cparams(("arbitrary",)),
        name="ssd",
    )(p_ssd, conv_w.astype(F32), conv_b.reshape(1, -1).astype(F32), pad(dt_bias), pad(a_neg),
      jnp.repeat(d_skip.astype(F32), SSD_HD).reshape(1, -1), norm_g.reshape(1, -1).astype(F32))


MOBA_HD, MOBA_HEADS, MOBA_D, MOBA_BLOCK, MOBA_TOPK = 64, 8, 512, 256, 3
MOBA_GROUP = 4
MOBA_UNDERFLOW = -160.0
MOBA_BOUND_SLACK = 1.001
NEG_BIG = -1e30
LOG2E = math.log2(math.e)


MOBA_VROWS = MOBA_HD + 16


def _moba_proj_kernel(x_ref, wqt_ref, wk_ref, wvt_ref, qt_ref, k_ref, kmean_ref, v_ref, knorm_ref):
    xb = x_ref[...].astype(BF16)
    qt = _dot_nt(wqt_ref[...], xb).astype(qt_ref.dtype)
    qt_ref[...] = qt
    vt = _dot_nt(wvt_ref[...], xb)
    extra = jnp.where(_iota2((MOBA_VROWS - MOBA_HD, MOBA_BLOCK), 0) == 0, 1.0, 0.0)
    for h in range(MOBA_HEADS):
        v_ref[h, 0] = jnp.concatenate([vt[h * MOBA_HD:(h + 1) * MOBA_HD], extra], axis=0).astype(v_ref.dtype)
    k = _dot(xb, wk_ref[...])
    kmean_ref[0] = jnp.mean(k, 0, keepdims=True)
    shape = (MOBA_BLOCK, 2 * MOBA_HD)
    lane = _iota2(shape, 1)
    pos = jnp.where((lane == MOBA_HD) | (lane == MOBA_HD + 1), _iota2(shape, 0).astype(F32), 0.0).astype(k_ref.dtype)
    norm_lane = _iota2((1, 128), 1)
    knorm = jnp.zeros((1, 128), F32)
    for h in range(MOBA_HEADS):
        k_h = k[:, h * MOBA_HD:(h + 1) * MOBA_HD].astype(k_ref.dtype)
        k_ref[h, 0] = pos
        k_ref[h, 0, :, 0:MOBA_HD] = k_h
        k_f = k_h.astype(F32)
        biggest = jnp.max(jnp.sum(k_f * k_f, 1, keepdims=True), 0, keepdims=True)
        knorm = jnp.where(norm_lane == h, jnp.sqrt(biggest), knorm)
        q_f = qt[h * MOBA_HD:(h + 1) * MOBA_HD].astype(F32)
        biggest = jnp.max(jnp.sum(q_f * q_f, 0, keepdims=True), 1, keepdims=True)
        knorm = jnp.where(norm_lane == MOBA_HEADS + h, jnp.sqrt(biggest), knorm)
    knorm_ref[0] = knorm


def _moba_proj_call(x, wqt, wk, wvt):
    S, D = x.shape
    nb = S // MOBA_BLOCK
    const = lambda shape: pl.BlockSpec(shape, lambda i: (0, 0))
    return pl.pallas_call(
        _moba_proj_kernel,
        grid=(nb,),
        in_specs=[pl.BlockSpec((MOBA_BLOCK, D), lambda i: (i, 0)), const((MOBA_D, D)), const((D, MOBA_D)),
                  const((MOBA_D, D))],
        out_specs=[pl.BlockSpec((MOBA_D, MOBA_BLOCK), lambda i: (0, i)),
                   pl.BlockSpec((MOBA_HEADS, 1, MOBA_BLOCK, 2 * MOBA_HD), lambda i: (0, i, 0, 0)),
                   pl.BlockSpec((1, 1, MOBA_D), lambda i: (i, 0, 0)),
                   pl.BlockSpec((MOBA_HEADS, 1, MOBA_VROWS, MOBA_BLOCK), lambda i: (0, i, 0, 0)),
                   pl.BlockSpec((1, 1, 128), lambda i: (i, 0, 0))],
        out_shape=[jax.ShapeDtypeStruct((MOBA_D, S), BF16),
                   jax.ShapeDtypeStruct((MOBA_HEADS, nb, MOBA_BLOCK, 2 * MOBA_HD), BF16),
                   jax.ShapeDtypeStruct((nb, 1, MOBA_D), F32),
                   jax.ShapeDtypeStruct((MOBA_HEADS, nb, MOBA_VROWS, MOBA_BLOCK), BF16),
                   jax.ShapeDtypeStruct((nb, 1, 128), F32)],
        compiler_params=_cparams(("parallel",)),
        name="moba_proj",
    )(x, wqt, wk, wvt)


def _moba_kernel(u0_ref, qt_ref, k_ref, vt_ref, kmean_ref, o_ref, sel_ref, s0_ref, s1_ref, p0_ref, p1_ref):
    BS, HD = MOBA_BLOCK, MOBA_HD
    h = pl.program_id(0)
    i = pl.program_id(1)
    nb = k_ref.shape[0]
    qt = qt_ref[...]
    slope = LOG2E * jnp.exp2(jnp.zeros((1, BS), F32) - (h + 1).astype(F32) * (8.0 / MOBA_HEADS))
    slope_hi = slope.astype(qt.dtype).astype(F32)
    slope_lo = slope - slope_hi
    row = _iota2((HD, BS), 0)
    qt_ext = jnp.concatenate(
        [qt, jnp.where(row == 0, slope_hi, jnp.where(row == 1, slope_lo, 0.0)).astype(qt.dtype)], axis=0)

    gate = _dot(kmean_ref[...], qt)
    blk = _iota2((nb, BS), 0).astype(F32)
    cand = blk < i.astype(F32)
    sel = jnp.zeros((nb, BS), F32)
    for _ in range(MOBA_TOPK):
        best = jnp.max(jnp.where(cand, gate, -jnp.inf), 0, keepdims=True)
        idx = jnp.min(jnp.where(cand & (gate == best), blk, float(nb)), 0, keepdims=True)
        pick = blk == idx
        sel = jnp.where(pick, 1.0, sel)
        cand = cand & jnp.logical_not(pick)
    sel_ref[...] = sel

    G = MOBA_GROUP
    last = nb - 1
    s_ref = (s0_ref, s1_ref)
    p_ref = (p0_ref, p1_ref)

    u0 = u0_ref[h * nb + i]

    def group(u):
        return [jnp.clip(G * (u + u0) + x, 0, last) for x in range(G)]

    def issue_scores(u, slot):
        tops = []
        for x, j in enumerate(group(u)):
            sc = _dot(k_ref[j], qt_ext)
            s_ref[slot][x] = sc
            tops.append(jnp.max(sc, 0, keepdims=True))
        return tuple(tops)

    def value_blocks(u):
        js = group(u)
        js[0] = jnp.where(u == -1, i, js[0])
        return js

    def weighted_values(u, slot):
        out = None
        for x, j in enumerate(value_blocks(u)):
            part = _dot(vt_ref[j], p_ref[slot][x])
            out = part if out is None else out + part
        return out

    def step(u, slot, tops, a_prev, m, acc, issue_next=True):
        on, shift = [], []
        m_new = m
        for x, j in enumerate(group(u)):
            on.append(sel_ref[pl.ds(j, 1), :] > 0.0)
            shift.append(slope * ((j - i) * BS).astype(F32))
            m_new = jnp.maximum(m_new, jnp.where(on[x], tops[x] + shift[x], NEG_BIG))
        tops_next, pv = [], None
        for x, (j_prev, j_next) in enumerate(zip(value_blocks(u - 1), group(u + 1))):
            part = _dot(vt_ref[j_prev], p_ref[1 - slot][x])
            pv = part if pv is None else pv + part
            if issue_next:
                sc = _dot(k_ref[j_next], qt_ext)
                s_ref[1 - slot][x] = sc
                tops_next.append(jnp.max(sc, 0, keepdims=True))
            p = jnp.exp2(s_ref[slot][x] - (jnp.where(on[x], m_new, -NEG_BIG) - shift[x]))
            p_ref[slot][x] = p.astype(BF16)
        return tuple(tops_next), jnp.exp2(m - m_new), m_new, a_prev * acc + pv

    def body(w, carry):
        carry = step(2 * w, 0, *carry)
        return step(2 * w + 1, 1, *carry)

    tops0 = issue_scores(0, 0)
    s_own = jnp.where(_iota2((BS, BS), 1) >= _iota2((BS, BS), 0), _dot(k_ref[i], qt_ext), NEG_BIG)
    m_own = jnp.max(s_own, 0, keepdims=True)
    p1_ref[1:G] = jnp.zeros((G - 1, BS, BS), BF16)
    p1_ref[0] = jnp.exp2(s_own - m_own).astype(BF16)
    init = (tops0, jnp.ones((1, BS), F32), m_own, jnp.zeros((vt_ref.shape[1], BS), F32))
    steps = (i - G * u0 + G - 1) // G
    pairs = steps // 2
    carry = lax.fori_loop(0, pairs, body, init)

    def odd_tail(carry):
        _, a_last, _, acc = step(2 * pairs, 0, *carry, issue_next=False)
        return a_last * acc + weighted_values(2 * pairs, 0)

    def even_tail(carry):
        _, a_prev, _, acc = carry
        return a_prev * acc + weighted_values(2 * pairs - 1, 1)

    acc = lax.cond(steps % 2 == 1, odd_tail, even_tail, carry)
    o_ref[...] = (acc[0:HD] / acc[HD:HD + 1]).astype(o_ref.dtype)


def _moba_first_group(norms, nb):
    BS = MOBA_BLOCK
    k_norm = norms[:, 0, 0:MOBA_HEADS].T
    q_norm = norms[:, 0, MOBA_HEADS:2 * MOBA_HEADS].T
    slope = LOG2E * jnp.exp2(-(jnp.arange(MOBA_HEADS, dtype=F32) + 1.0) * (8.0 / MOBA_HEADS))
    i_idx = jnp.arange(nb, dtype=F32)[None, :, None]
    j_idx = jnp.arange(nb, dtype=F32)[None, None, :]
    reach = slope[:, None, None] * ((BS - 1.0) - BS * (i_idx - j_idx))
    bound = q_norm[:, :, None] * (k_norm[:, None, :] + k_norm[:, :, None]) * MOBA_BOUND_SLACK + reach
    matters = (bound >= MOBA_UNDERFLOW) & (j_idx < i_idx)
    first = jnp.min(jnp.where(matters, j_idx, i_idx), axis=2).astype(jnp.int32)
    return (first // MOBA_GROUP).reshape(-1)


def _moba_call(first_group, qt, k4, vt4, kmean):
    S = qt.shape[1]
    nb = S // MOBA_BLOCK
    return pl.pallas_call(
        _moba_kernel,
        grid_spec=pltpu.PrefetchScalarGridSpec(
            num_scalar_prefetch=1,
            grid=(MOBA_HEADS, nb),
            in_specs=[pl.BlockSpec((MOBA_HD, MOBA_BLOCK), lambda h, i, u0: (h, i)),
                      pl.BlockSpec((None, nb, MOBA_BLOCK, 2 * MOBA_HD), lambda h, i, u0: (h, 0, 0, 0)),
                      pl.BlockSpec((None, nb, MOBA_VROWS, MOBA_BLOCK), lambda h, i, u0: (h, 0, 0, 0)),
                      pl.BlockSpec((None, nb, MOBA_HD), lambda h, i, u0: (h, 0, 0))],
            out_specs=pl.BlockSpec((MOBA_HD, MOBA_BLOCK), lambda h, i, u0: (h, i)),
            scratch_shapes=[pltpu.VMEM((nb, MOBA_BLOCK), F32)]
            + [pltpu.VMEM((MOBA_GROUP, MOBA_BLOCK, MOBA_BLOCK), F32)] * 2
            + [pltpu.VMEM((MOBA_GROUP, MOBA_BLOCK, MOBA_BLOCK), BF16)] * 2),
        out_shape=jax.ShapeDtypeStruct((MOBA_D, S), BF16),
        compiler_params=_cparams(("parallel", "arbitrary")),
        name="moba",
    )(first_group, qt, k4, vt4, kmean)


N_EXPERTS, TOP_K, EXPERT_FF = 8, 2, 2816
MOE_ROWS = 512


def _router_kernel(x_ref, whi_ref, wlo_ref, b_ref, o_ref):
    x = x_ref[...]
    xhi = x.astype(BF16)
    xlo = (x - xhi.astype(F32)).astype(BF16)
    logits = (jnp.dot(xhi, whi_ref[...], preferred_element_type=F32)
              + jnp.dot(xhi, wlo_ref[...], preferred_element_type=F32)
              + jnp.dot(xlo, whi_ref[...], preferred_element_type=F32)) + b_ref[...]
    lane = _iota2(logits.shape, 1).astype(F32)
    logits = jnp.where(lane < N_EXPERTS, logits, -jnp.inf)
    m1 = jnp.max(logits, -1, keepdims=True)
    i1 = jnp.min(jnp.where(logits == m1, lane, 128.0), -1, keepdims=True)
    rest = jnp.where(lane == i1, -jnp.inf, logits)
    m2 = jnp.max(rest, -1, keepdims=True)
    i2 = jnp.min(jnp.where(rest == m2, lane, 128.0), -1, keepdims=True)
    e = jnp.exp(m2 - m1)
    g1 = 1.0 / (1.0 + e)
    g2 = e / (1.0 + e)
    out = jnp.where(lane == 0, i1, 0.0)
    out = jnp.where(lane == 1, i2, out)
    out = jnp.where(lane == 2, g1, out)
    out = jnp.where(lane == 3, g2, out)
    o_ref[...] = out


def _router_call(x, w_router, b_router, tm):
    S, D = x.shape
    wp = jnp.pad(w_router.astype(F32), ((0, 0), (0, 128 - N_EXPERTS)))
    whi = wp.astype(BF16)
    wlo = (wp - whi.astype(F32)).astype(BF16)
    bp = jnp.pad(b_router.astype(F32), (0, 128 - N_EXPERTS)).reshape(1, 128)
    const = lambda shape: pl.BlockSpec(shape, lambda i: (0, 0))
    return pl.pallas_call(
        _router_kernel,
        grid=(S // tm,),
        in_specs=[pl.BlockSpec((tm, D), lambda i: (i, 0)), const((D, 128)), const((D, 128)), const((1, 128))],
        out_specs=pl.BlockSpec((tm, 128), lambda i: (i, 0)),
        out_shape=jax.ShapeDtypeStruct((S, 128), F32),
        compiler_params=_cparams(("parallel",)),
        name="router",
    )(x, whi, wlo, bp)


def _row_copy(src_hbm, dst_ref, src_row, dst_row, sem):
    return pltpu.make_async_copy(src_hbm.at[pl.ds(src_row, 1)], dst_ref.at[pl.ds(dst_row, 1)], sem)


def _scatter_rows_kernel(dest_ref, lo_ref, hi_ref, x_ref, o_hbm, zero_ref, sem):
    step = pl.program_id(0)
    n_tiles = pl.num_programs(0) - 1
    tm = x_ref.shape[0]

    def row_out(src_ref, src_row, dst_row):
        return pltpu.make_async_copy(src_ref.at[pl.ds(src_row, 1)], o_hbm.at[pl.ds(dst_row, 1)], sem)

    @pl.when(step < n_tiles)
    def _():
        base = step * tm

        def start(r, c):
            for slot in range(TOP_K):
                row_out(x_ref, r, dest_ref[TOP_K * (base + r) + slot]).start()
            return c

        lax.fori_loop(0, tm, start, 0, unroll=8)
        for slot in range(TOP_K):
            pltpu.make_async_copy(x_ref, o_hbm.at[pl.ds(0, tm)], sem).wait()

    @pl.when(step == n_tiles)
    def _():
        zero_ref[...] = jnp.zeros_like(zero_ref)
        for e in range(N_EXPERTS):
            def start(r, c):
                row_out(zero_ref, 0, r).start()
                return c

            def wait(r, c):
                row_out(zero_ref, 0, r).wait()
                return c

            lax.fori_loop(lo_ref[e], hi_ref[e], start, 0)
            lax.fori_loop(lo_ref[e], hi_ref[e], wait, 0)


def _scatter_rows(x, dest, pad_lo, pad_hi, n_rows, tm):
    T, D = x.shape
    n_tiles = T // tm
    return pl.pallas_call(
        _scatter_rows_kernel,
        grid_spec=pltpu.PrefetchScalarGridSpec(
            num_scalar_prefetch=3,
            grid=(n_tiles + 1,),
            in_specs=[pl.BlockSpec((tm, D), lambda i, d, lo, hi: (jnp.minimum(i, n_tiles - 1), 0))],
            out_specs=pl.BlockSpec(memory_space=pl.ANY),
            scratch_shapes=[pltpu.VMEM((8, D), x.dtype), pltpu.SemaphoreType.DMA(())]),
        out_shape=jax.ShapeDtypeStruct((n_rows, D), x.dtype),
        compiler_params=_cparams(("arbitrary",)),
        name="moe_scatter",
    )(dest, pad_lo, pad_hi, x)


def _moe_ffn_kernel(be_ref, nu_ref, x_ref, wg_ref, wu_ref, wd_ref, o_ref, acc_ref):
    i = pl.program_id(0)
    j = pl.program_id(1)

    @pl.when(i < nu_ref[0])
    def _():
        xb = x_ref[...].astype(BF16)
        part = _dot(_silu(_dot(xb, wg_ref[...])) * _dot(xb, wu_ref[...]), wd_ref[...])

        @pl.when(j == 0)
        def _():
            acc_ref[...] = part

        @pl.when(j > 0)
        def _():
            acc_ref[...] += part

    last = j == pl.num_programs(1) - 1

    @pl.when(last & (i < nu_ref[0]))
    def _():
        o_ref[...] = acc_ref[...]

    @pl.when(last & (i >= nu_ref[0]))
    def _():
        o_ref[...] = jnp.zeros_like(o_ref)


def _moe_ffn_call(x_rows, block_e, n_used, wg, wu, wd, tf):
    n, D = x_rows.shape
    F = wg.shape[2]
    R = MOE_ROWS
    return pl.pallas_call(
        _moe_ffn_kernel,
        grid_spec=pltpu.PrefetchScalarGridSpec(
            num_scalar_prefetch=2,
            grid=(n // R, F // tf),
            in_specs=[pl.BlockSpec((R, D), lambda i, j, be, nu: (i, 0)),
                      pl.BlockSpec((None, D, tf), lambda i, j, be, nu: (be[i], 0, j)),
                      pl.BlockSpec((None, D, tf), lambda i, j, be, nu: (be[i], 0, j)),
                      pl.BlockSpec((None, tf, D), lambda i, j, be, nu: (be[i], j, 0))],
            out_specs=pl.BlockSpec((R, D), lambda i, j, be, nu: (i, 0)),
            scratch_shapes=[pltpu.VMEM((R, D), F32)]),
        out_shape=jax.ShapeDtypeStruct((n, D), F32),
        compiler_params=_cparams(("arbitrary", "arbitrary")),
        name="moe_ffn",
    )(block_e, n_used, x_rows, wg, wu, wd)


def _moe_combine_kernel(d_ref, y_hbm, x_ref, r_ref, g_ref, b_ref, o_ref, y1_ref, y2_ref, sem):
    tm = x_ref.shape[0]
    base = pl.program_id(0) * tm

    def start(r, c):
        _row_copy(y_hbm, y1_ref, d_ref[2 * (base + r)], r, sem.at[0]).start()
        _row_copy(y_hbm, y2_ref, d_ref[2 * (base + r) + 1], r, sem.at[1]).start()
        return c

    lax.fori_loop(0, tm, start, 0, unroll=8)
    pltpu.make_async_copy(y_hbm.at[pl.ds(0, tm)], y1_ref, sem.at[0]).wait()
    pltpu.make_async_copy(y_hbm.at[pl.ds(0, tm)], y2_ref, sem.at[1]).wait()
    y = r_ref[:, 2:3] * y1_ref[...] + r_ref[:, 3:4] * y2_ref[...]
    o_ref[...] = _layer_norm_rows(DN_ALPHA * x_ref[...] + y, g_ref[...], b_ref[...])


def _moe_combine_call(dest, y_rows, x, routed, g, b, tm):
    S, D = x.shape
    return pl.pallas_call(
        _moe_combine_kernel,
        grid_spec=pltpu.PrefetchScalarGridSpec(
            num_scalar_prefetch=1,
            grid=(S // tm,),
            in_specs=[pl.BlockSpec(memory_space=pl.ANY),
                      pl.BlockSpec((tm, D), lambda i, d: (i, 0)),
                      pl.BlockSpec((tm, 128), lambda i, d: (i, 0)),
                      pl.BlockSpec((1, D), lambda i, d: (0, 0)),
                      pl.BlockSpec((1, D), lambda i, d: (0, 0))],
            out_specs=pl.BlockSpec((tm, D), lambda i, d: (i, 0)),
            scratch_shapes=[pltpu.VMEM((tm, D), F32), pltpu.VMEM((tm, D), F32), pltpu.SemaphoreType.DMA((2,))]),
        out_shape=jax.ShapeDtypeStruct((S, D), F32),
        compiler_params=_cparams(("arbitrary",)),
        name="moe_combine",
    )(dest, y_rows, x, routed, g.reshape(1, D), b.reshape(1, D))


def _moe_sublayer(x, w_router, b_router, wg, wu, wd, g, b):
    T = x.shape[0]
    R = MOE_ROWS
    routed = _router_call(x, w_router, b_router, min(512, T))
    top_e = routed[:, 0:TOP_K].astype(jnp.int32)
    tok_oh = jnp.sum((top_e[:, :, None] == jnp.arange(N_EXPERTS)[None, None, :]).astype(jnp.int32), axis=1)
    counts = jnp.sum(tok_oh, axis=0)
    rank = jnp.cumsum(tok_oh, axis=0) - tok_oh
    padded = (counts + R - 1) // R * R
    pend = jnp.cumsum(padded)
    pstart = pend - padded
    dest = pstart[top_e] + jnp.take_along_axis(rank, top_e, axis=1)
    n_rows = (T * TOP_K + N_EXPERTS * (R - 1)) // R * R
    n_blocks = n_rows // R
    block_first_row = jnp.arange(n_blocks, dtype=jnp.int32) * R
    block_e = jnp.minimum(jnp.sum((pend[None, :] <= block_first_row[:, None]).astype(jnp.int32), axis=1),
                          N_EXPERTS - 1)
    n_used = (pend[-1] // R).astype(jnp.int32).reshape(1)
    dest = dest.reshape(-1).astype(jnp.int32)
    pad_lo = (pstart + counts).astype(jnp.int32)
    pad_hi = jnp.concatenate([pstart[1:], jnp.array([n_rows])]).astype(jnp.int32)
    x_rows = _scatter_rows(x, dest, pad_lo, pad_hi, n_rows, min(512, T))
    y_rows = _moe_ffn_call(x_rows, block_e, n_used, wg, wu, wd, EXPERT_FF // 2)
    return _moe_combine_call(dest, y_rows, x, routed, g, b, min(256, T))


GLA_IN = 2 * GLA_QK + 2 * GLA_V + GLA_LR


def _gla_rwkv_sublayer(x, w_in, gla_wa2, gla_ba, gla_norm, mu, w0, w2, a0, a2, g2, k_k, k_a, r_k, ln_w, ln_b,
                       w_out, ln_g, ln_bias):
    S = x.shape[0]
    tm = min(512, S)
    lr_pad = 128 - GLA_LR
    w_gla = jnp.pad(w_in[:, :GLA_IN], ((0, 0), (0, lr_pad))).astype(BF16)
    w_rwkv = w_in[:, GLA_IN:].astype(BF16)
    p_gla = _matmul(x, w_gla, F32, tm, GLA_COLS)
    p_rwkv = _matmul(x, w_rwkv, F32, tm, RWKV_COLS)
    wa2p = jnp.pad(gla_wa2, ((0, lr_pad), (0, 0))).astype(BF16)
    o_gla = _gla_call(p_gla, wa2p, gla_ba, gla_norm)
    o_rwkv = _rwkv_call(p_rwkv, mu, w0, w2, a0, a2, g2, k_k, k_a, r_k, ln_w, ln_b)
    return _matmul_ln([o_gla, o_rwkv], [w_out[:GLA_V].astype(BF16), w_out[GLA_V:].astype(BF16)], x,
                      ln_g, ln_bias, tm)


def _xattn_sublayer(x, mem, wq, wk, wv, wo, ln_g, ln_bias):
    M = mem.shape[0]
    k_mem = _matmul(mem, wk.astype(BF16), BF16, M, D_MODEL)
    v_mem = _matmul(mem, wv.astype(BF16), BF16, M, D_MODEL)
    wq_scaled = (wq * XATTN_HD ** -0.5).astype(BF16)
    return _xattn_call(x, wq_scaled, k_mem, v_mem, wo.astype(BF16), ln_g, ln_bias, min(512, x.shape[0]))


def _ssd_moba_sublayer(x, w_in, conv_w, conv_b, dt_bias, a_log, d_skip, ssd_norm, w_out, ln_g, ln_bias):
    S = x.shape[0]
    tm = min(512, S)
    nb = S // MOBA_BLOCK
    o_dt = SSD_COLS
    o_q = o_dt + SSD_HEADS
    w_ssd = jnp.pad(w_in[:, :o_q], ((0, 0), (0, 128 - SSD_HEADS))).astype(BF16)
    w_q = w_in[:, o_q:o_q + MOBA_D]
    w_k = w_in[:, o_q + MOBA_D:o_q + 2 * MOBA_D]
    w_v = w_in[:, o_q + 2 * MOBA_D:]
    p_ssd = _matmul(x, w_ssd, F32, tm, (SSD_COLS + 128) // 3)
    o_ssd = _ssd_call(p_ssd, conv_w, conv_b, dt_bias, a_log, d_skip, ssd_norm)
    qt, k4, kmean, vt4, knorm = _moba_proj_call(x, (w_q.T * (MOBA_HD ** -0.5 * LOG2E)).astype(BF16), w_k.astype(BF16),
                                         w_v.T.astype(BF16))
    kmean_h = kmean.reshape(nb, MOBA_HEADS, MOBA_HD).transpose(1, 0, 2)
    ot_moba = _moba_call(_moba_first_group(knorm, nb), qt, k4, vt4, kmean_h)
    return _matmul_ln([o_ssd, ot_moba], [w_out[:SSD_INNER].astype(BF16), w_out[SSD_INNER:].astype(BF16)], x,
                      ln_g, ln_bias, tm, transposed=(False, True))


def kernel(x, mem, l0_w_in, l0_gla_wa2, l0_gla_ba, l0_gla_norm, l0_rwkv_mu, l0_rwkv_w0, l0_rwkv_w2, l0_rwkv_a0, l0_rwkv_a2, l0_rwkv_g2, l0_rwkv_kk, l0_rwkv_ka, l0_rwkv_rk, l0_rwkv_lnw, l0_rwkv_lnb, l0_w_out, l0_ln1_g, l0_ln1_b, l0_xq, l0_xk, l0_xv, l0_xo, l0_ln2_g, l0_ln2_b, l0_ffn_wg, l0_ffn_wu, l0_ffn_wd, l0_ln3_g, l0_ln3_b, l1_w_in, l1_conv_w, l1_conv_b, l1_dt_bias, l1_a_log, l1_d_skip, l1_ssd_norm, l1_w_out, l1_ln1_g, l1_ln1_b, l1_xq, l1_xk, l1_xv, l1_xo, l1_ln2_g, l1_ln2_b, l1_router, l1_router_b, l1_exp_wg, l1_exp_wu, l1_exp_wd, l1_ln3_g, l1_ln3_b):
    x2 = x.reshape(-1, D_MODEL)
    mem2 = mem.reshape(-1, D_MODEL)
    x2 = _gla_rwkv_sublayer(x2, l0_w_in, l0_gla_wa2, l0_gla_ba, l0_gla_norm, l0_rwkv_mu, l0_rwkv_w0, l0_rwkv_w2,
                            l0_rwkv_a0, l0_rwkv_a2, l0_rwkv_g2, l0_rwkv_kk, l0_rwkv_ka, l0_rwkv_rk, l0_rwkv_lnw,
                            l0_rwkv_lnb, l0_w_out, l0_ln1_g, l0_ln1_b)
    x2 = _xattn_sublayer(x2, mem2, l0_xq, l0_xk, l0_xv, l0_xo, l0_ln2_g, l0_ln2_b)
    tm = min(512, x2.shape[0])
    x2 = _ffn_call(x2, l0_ffn_wg.astype(BF16), l0_ffn_wu.astype(BF16), l0_ffn_wd.astype(BF16),
                   l0_ln3_g, l0_ln3_b, tm, l0_ffn_wg.shape[1] // 2)
    x2 = _ssd_moba_sublayer(x2, l1_w_in, l1_conv_w, l1_conv_b, l1_dt_bias, l1_a_log, l1_d_skip, l1_ssd_norm,
                            l1_w_out, l1_ln1_g, l1_ln1_b)
    x2 = _xattn_sublayer(x2, mem2, l1_xq, l1_xk, l1_xv, l1_xo, l1_ln2_g, l1_ln2_b)
    x2 = _moe_sublayer(x2, l1_router, l1_router_b, l1_exp_wg.astype(BF16), l1_exp_wu.astype(BF16),
                       l1_exp_wd.astype(BF16), l1_ln3_g, l1_ln3_b)
    return x2.reshape(x.shape)
```

```python
import functools
import math

import jax
import jax.numpy as jnp
from jax import lax
from jax.experimental import pallas as pl
from jax.experimental.pallas import tpu as pltpu

BF16 = jnp.bfloat16
F32 = jnp.float32

D_MODEL = 1024
LN_EPS = 1e-5
DEPTH = 2
DN_ALPHA = (2 * DEPTH) ** 0.25

GLA_HEADS, GLA_DK, GLA_DV, GLA_CHUNK = 4, 64, 128, 64
GLA_QK, GLA_V, GLA_LR, GLA_TAU = 256, 512, 16, 16.0
GLA_COLS = 2 * GLA_QK + 2 * GLA_V + 128

RWKV_HEADS, RWKV_HD, RWKV_D, RWKV_CHUNK = 8, 64, 512, 64
RWKV_COLS = 1792
RWKV_DECAY_SCALE = math.exp(-0.5)
RWKV_GN_EPS = 64e-5

VMEM_LIMIT = 56 * 1024 * 1024
ROW_TILE = 512
SCAN_TILE = 256
COMBINE_TILE = 256


def _cparams(sem):
    return pltpu.CompilerParams(dimension_semantics=sem, vmem_limit_bytes=VMEM_LIMIT)


def _dot(a, b):
    return jnp.dot(a.astype(BF16), b.astype(BF16), preferred_element_type=F32)


def _dot_nt(a, b):
    return lax.dot_general(a.astype(BF16), b.astype(BF16), (((1,), (1,)), ((), ())), preferred_element_type=F32)


def _dot_tn(a, b):
    return lax.dot_general(a.astype(BF16), b.astype(BF16), (((0,), (0,)), ((), ())), preferred_element_type=F32)


def _split3(x):
    hi = x.astype(BF16)
    r1 = x - hi.astype(F32)
    mid = r1.astype(BF16)
    lo = (r1 - mid.astype(F32)).astype(BF16)
    return hi, mid, lo


def _dot_exact_lhs(m, x):
    mb = m.astype(BF16)
    hi, mid, lo = _split3(x)
    return (jnp.dot(mb, hi, preferred_element_type=F32) + jnp.dot(mb, mid, preferred_element_type=F32)
            + jnp.dot(mb, lo, preferred_element_type=F32))


def _dot_exact_rhs(x, m):
    mb = m.astype(BF16)
    hi, mid, lo = _split3(x)
    return (jnp.dot(hi, mb, preferred_element_type=F32) + jnp.dot(mid, mb, preferred_element_type=F32)
            + jnp.dot(lo, mb, preferred_element_type=F32))


def _sigmoid(x):
    return 1.0 / (1.0 + jnp.exp(-x))


def _silu(x):
    return x * _sigmoid(x)


def _iota2(shape, axis):
    return lax.broadcasted_iota(jnp.int32, shape, axis)


def _chunk_tril(n, chunk):
    r = _iota2((n, n), 0)
    c = _iota2((n, n), 1)
    return jnp.where((c <= r) & ((r // chunk) == (c // chunk)), 1.0, 0.0)


def _head_block(n, width, value):
    r = _iota2((n, n), 0)
    c = _iota2((n, n), 1)
    return jnp.where((r // width) == (c // width), value, 0.0)


def _mm_kernel(x_ref, w_ref, o_ref):
    o_ref[...] = _dot(x_ref[...], w_ref[...]).astype(o_ref.dtype)


def _matmul(x, w, out_dtype, tm, tn):
    S, K = x.shape
    N = w.shape[1]
    return pl.pallas_call(
        _mm_kernel,
        grid=(S // tm, N // tn),
        in_specs=[pl.BlockSpec((tm, K), lambda i, j: (i, 0)),
                  pl.BlockSpec((K, tn), lambda i, j: (0, j))],
        out_specs=pl.BlockSpec((tm, tn), lambda i, j: (i, j)),
        out_shape=jax.ShapeDtypeStruct((S, N), out_dtype),
        compiler_params=_cparams(("parallel", "arbitrary")),
        name="matmul",
    )(x, w)


def _layer_norm_rows(y, g, b):
    mu = jnp.mean(y, -1, keepdims=True)
    d = y - mu
    var = jnp.mean(d * d, -1, keepdims=True)
    return d * lax.rsqrt(var + LN_EPS) * g + b


def _mm_ln_kernel(transposed, *refs):
    n_in = len(transposed)
    a_refs = refs[:n_in]
    w_refs = refs[n_in:2 * n_in]
    x_ref, g_ref, b_ref, o_ref = refs[2 * n_in:]
    acc = DN_ALPHA * x_ref[...]
    for a_ref, w_ref, tr in zip(a_refs, w_refs, transposed):
        acc = acc + (_dot_tn if tr else _dot)(a_ref[...], w_ref[...])
    o_ref[...] = _layer_norm_rows(acc, g_ref[...], b_ref[...])


def _matmul_ln(a_list, w_list, x, g, b, tm, transposed=None):
    S, D = x.shape
    transposed = tuple(transposed or (False,) * len(a_list))
    in_specs = ([pl.BlockSpec((a.shape[0], tm), lambda i: (0, i)) if tr else
                 pl.BlockSpec((tm, a.shape[1]), lambda i: (i, 0)) for a, tr in zip(a_list, transposed)]
                + [pl.BlockSpec(w.shape, lambda i: (0, 0)) for w in w_list]
                + [pl.BlockSpec((tm, D), lambda i: (i, 0)),
                   pl.BlockSpec((1, D), lambda i: (0, 0)),
                   pl.BlockSpec((1, D), lambda i: (0, 0))])
    return pl.pallas_call(
        functools.partial(_mm_ln_kernel, transposed),
        grid=(S // tm,),
        in_specs=in_specs,
        out_specs=pl.BlockSpec((tm, D), lambda i: (i, 0)),
        out_shape=jax.ShapeDtypeStruct((S, D), F32),
        compiler_params=_cparams(("parallel",)),
        name="matmul_ln",
    )(*a_list, *w_list, x, g.reshape(1, D), b.reshape(1, D))


def _gla_kernel(p_ref, wa2_ref, ba_ref, ng_ref, o_ref, st_ref, o_scr):
    C, H, dk, dv = GLA_CHUNK, GLA_HEADS, GLA_DK, GLA_DV
    tb = p_ref.shape[0]

    @pl.when(pl.program_id(0) == 0)
    def _():
        st_ref[...] = jnp.zeros_like(st_ref)

    z = _dot(p_ref[:, 2 * GLA_QK + 2 * GLA_V:], wa2_ref[...]) + ba_ref[...]
    log_a = -(jnp.maximum(-z, 0.0) + jnp.log(1.0 + jnp.exp(-jnp.abs(z)))) / GLA_TAU
    b = _dot_exact_lhs(_chunk_tril(tb, C), log_a)
    causal = _iota2((C, C), 1) <= _iota2((C, C), 0)

    nc = tb // C
    q_h, k_h, ke_h, v_h, dec_h = [], [], [], [], []
    for c in range(nc):
        rows = slice(c * C, (c + 1) * C)
        b_c = b[rows]
        b_last = b_c[C - 1:C]
        q_dec = p_ref[rows, 0:GLA_QK] * (dk ** -0.5) * jnp.exp(b_c)
        k_c = p_ref[rows, GLA_QK:2 * GLA_QK]
        k_dec = k_c * jnp.exp(-b_c)
        k_end = k_c * jnp.exp(b_last - b_c)
        decay = jnp.exp(b_last)
        for h in range(H):
            ks = slice(h * dk, (h + 1) * dk)
            q_h.append(q_dec[:, ks])
            k_h.append(k_dec[:, ks])
            ke_h.append(k_end[:, ks])
            dec_h.append(decay[:, ks])
            v_h.append(p_ref[rows, 2 * GLA_QK + h * dv:2 * GLA_QK + (h + 1) * dv])
    n = nc * H
    attn = [jnp.where(causal, _dot_nt(q_h[i], k_h[i]), 0.0) for i in range(n)]
    kv = [_dot_tn(v_h[i], ke_h[i]) for i in range(n)]
    intra = [_dot(attn[i], v_h[i]) for i in range(n)]
    state = [st_ref[:, h * dk:(h + 1) * dk] for h in range(H)]
    entering = []
    for i in range(n):
        entering.append(state[i % H])
        state[i % H] = state[i % H] * dec_h[i] + kv[i]
    for i in range(n):
        c, h = divmod(i, H)
        o_scr[c * C:(c + 1) * C, h * dv:(h + 1) * dv] = intra[i] + _dot_nt(q_h[i], entering[i])
    for h in range(H):
        st_ref[:, h * dk:(h + 1) * dk] = state[h]

    for h in range(H):
        vs = slice(h * dv, (h + 1) * dv)
        o_h = o_scr[:, vs]
        g_h = p_ref[:, 2 * GLA_QK + GLA_V + h * dv:2 * GLA_QK + GLA_V + (h + 1) * dv]
        o_h = o_h * lax.rsqrt(jnp.mean(o_h * o_h, -1, keepdims=True) + 1e-5) * ng_ref[:, vs]
        o_ref[:, vs] = (o_h * _silu(g_h)).astype(o_ref.dtype)


def _gla_call(p_gla, wa2p, ba, norm_g, tb=SCAN_TILE):
    S = p_gla.shape[0]
    return pl.pallas_call(
        _gla_kernel,
        grid=(S // tb,),
        in_specs=[pl.BlockSpec((tb, GLA_COLS), lambda i: (i, 0)),
                  pl.BlockSpec((128, GLA_QK), lambda i: (0, 0)),
                  pl.BlockSpec((1, GLA_QK), lambda i: (0, 0)),
                  pl.BlockSpec((1, GLA_V), lambda i: (0, 0))],
        out_specs=pl.BlockSpec((tb, GLA_V), lambda i: (i, 0)),
        out_shape=jax.ShapeDtypeStruct((S, GLA_V), BF16),
        scratch_shapes=[pltpu.VMEM((GLA_DV, GLA_QK), F32), pltpu.VMEM((tb, GLA_V), F32)],
        compiler_params=_cparams(("arbitrary",)),
        name="gla",
    )(p_gla, wa2p, ba.reshape(1, GLA_QK), norm_g.reshape(1, GLA_V))


def _rwkv_kernel(p_ref, mu_ref, w0_ref, w2_ref, a0_ref, a2_ref, g2_ref, kk_ref, ka_ref, rk_ref, lnw_ref, lnb_ref,
                 o_ref, prev_ref, h_ref, o_scr):
    C, H, N, D = RWKV_CHUNK, RWKV_HEADS, RWKV_HD, RWKV_D
    tb = p_ref.shape[0]
    first = pl.program_id(0) == 0

    @pl.when(first)
    def _():
        prev_ref[...] = jnp.zeros_like(prev_ref)
        h_ref[...] = jnp.zeros_like(h_ref)

    p = p_ref[...]
    shifted = jnp.where(_iota2(p.shape, 0) == 0, prev_ref[...], pltpu.roll(p, 1, 0))
    prev_ref[...] = p[tb - 1:tb]
    p = p + mu_ref[...] * (shifted - p)
    r = p[:, 0:D]
    k = p[:, D:2 * D]
    v = p[:, 2 * D:3 * D]
    xw = p[:, 3 * D:3 * D + 64]
    xa = p[:, 3 * D + 64:3 * D + 128]
    xg = p[:, 3 * D + 128:3 * D + 256]
    lw = -RWKV_DECAY_SCALE * _sigmoid(w0_ref[...] + _dot(jnp.tanh(xw), w2_ref[...]))
    a = _sigmoid(a0_ref[...] + _dot(xa, a2_ref[...]))
    g = _dot(_sigmoid(xg), g2_ref[...])
    head_ones = _head_block(D, N, 1.0)
    kk = k * kk_ref[...]
    kk = kk * lax.rsqrt(jnp.maximum(_dot_exact_rhs(kk * kk, head_ones), 1e-24))
    k = k * (1.0 + (a - 1.0) * ka_ref[...])
    pv = -kk * a
    cw = _dot_exact_lhs(_chunk_tril(tb, C), lw)
    cwx = cw - lw

    gi = _iota2((2 * C, 2 * C), 0)
    gj = _iota2((2 * C, 2 * C), 1) % C
    gram_mask = ((gi < C) & (gj < gi)) | ((gi >= C) & (gj <= gi - C))
    eye = _iota2((C, C), 0) == _iota2((C, C), 1)
    eye_f = jnp.where(eye, 1.0, 0.0)

    nc = tb // C
    items = [(c, h) for c in range(nc) for h in range(H)]
    xs, ys, pk_e, b_h, v_h, r_h, g_h = [], [], [], [], [], [], []
    for c in range(nc):
        rows = slice(c * C, (c + 1) * C)
        cw_c = cw[rows]
        cw_end = cw_c[C - 1:C]
        e_pos = jnp.exp(cw_c)
        e_neg = jnp.exp(-cw_c)
        e_end = jnp.exp(cw_end - cw_c)
        r_t = r[rows] * e_pos
        b_t = kk[rows] * jnp.exp(cwx[rows])
        p_t = pv[rows] * e_neg
        k_t = k[rows] * e_neg
        p_e = pv[rows] * e_end
        k_e = k[rows] * e_end
        g_end = jnp.exp(cw_end)
        v_c = v[rows]
        for h in range(H):
            hs = slice(h * N, (h + 1) * N)
            xs.append(jnp.concatenate([b_t[:, hs], r_t[:, hs]], axis=0))
            ys.append(jnp.concatenate([p_t[:, hs], k_t[:, hs]], axis=0))
            pk_e.append(jnp.concatenate([p_e[:, hs], k_e[:, hs]], axis=0))
            b_h.append(b_t[:, hs])
            v_h.append(v_c[:, hs])
            r_h.append(r_t[:, hs])
            g_h.append(g_end[:, hs])
    n = len(items)
    grams = [jnp.where(gram_mask, _dot_nt(xs[i], ys[i]), 0.0) for i in range(n)]
    l_p = [g[0:C, 0:C] for g in grams]
    m_pk = [g[C:2 * C, :] for g in grams]
    lkv = [_dot(grams[i][0:C, C:2 * C], v_h[i]) for i in range(n)]
    x = [_dot(lp, lp) for lp in l_p]
    t = [eye_f + lp for lp in l_p]
    for _ in range(4):
        tx = [_dot(jnp.concatenate([t[i], x[i]], axis=0), x[i]) for i in range(n)]
        t = [t[i] + tx[i][0:C] for i in range(n)]
        x = [tx[i][C:2 * C] for i in range(n)]
    t = [t[i] + _dot(t[i], x[i]) for i in range(n)]
    wu = [_dot(t[i], jnp.concatenate([b_h[i], lkv[i]], axis=1)) for i in range(n)]
    rhs = [jnp.concatenate([wu[i], jnp.concatenate([jnp.zeros((C, N), F32), v_h[i]], axis=1)], axis=0)
           for i in range(n)]
    az = [_dot_tn(pk_e[i], rhs[i]) for i in range(n)]
    qo = [_dot(m_pk[i], rhs[i]) for i in range(n)]
    state = [h_ref[h] for h in range(H)]
    for i, (c, h) in enumerate(items):
        a_mat = az[i][:, 0:N] + jnp.where(eye, g_h[i], 0.0)
        q_mat = qo[i][:, 0:N] + r_h[i]
        oh = _dot(jnp.concatenate([q_mat, a_mat], axis=0), state[h])
        o_scr[c * C:(c + 1) * C, h * N:(h + 1) * N] = oh[0:C] + qo[i][:, N:2 * N]
        state[h] = oh[C:C + N] + az[i][:, N:2 * N]
    for h in range(H):
        h_ref[h] = state[h]

    o = o_scr[...]
    head_mean = _head_block(D, N, 1.0 / N)
    mean = _dot_exact_rhs(o, head_mean)
    d = o - mean
    var = _dot_exact_rhs(d * d, head_mean)
    o = d * lax.rsqrt(var + RWKV_GN_EPS) * lnw_ref[...] + lnb_ref[...]
    bonus = _dot_exact_rhs(r * k * rk_ref[...], head_ones) * v
    o_ref[...] = ((o + bonus) * g).astype(o_ref.dtype)


def _rwkv_call(p_rwkv, mu, w0, w2, a0, a2, g2, k_k, k_a, r_k, ln_w, ln_b, tb=SCAN_TILE):
    S = p_rwkv.shape[0]
    D = RWKV_D
    row = lambda t: t.reshape(1, -1).astype(F32)
    full = lambda shape: pl.BlockSpec(shape, lambda i: tuple(0 for _ in shape))
    return pl.pallas_call(
        _rwkv_kernel,
        grid=(S // tb,),
        in_specs=[pl.BlockSpec((tb, RWKV_COLS), lambda i: (i, 0)),
                  full((1, RWKV_COLS)), full((1, D)), full((64, D)), full((1, D)), full((64, D)), full((128, D)),
                  full((1, D)), full((1, D)), full((1, D)), full((1, D)), full((1, D))],
        out_specs=pl.BlockSpec((tb, D), lambda i: (i, 0)),
        out_shape=jax.ShapeDtypeStruct((S, D), BF16),
        scratch_shapes=[pltpu.VMEM((1, RWKV_COLS), F32),
                        pltpu.VMEM((RWKV_HEADS, RWKV_HD, RWKV_HD), F32),
                        pltpu.VMEM((tb, D), F32)],
        compiler_params=_cparams(("arbitrary",)),
        name="rwkv7",
    )(p_rwkv, row(mu), row(w0), w2.astype(BF16), row(a0), a2.astype(BF16), g2.astype(BF16),
      row(k_k), row(k_a), row(r_k), row(ln_w), row(ln_b))


XATTN_HEADS, XATTN_HD = 4, 256


def _xattn_kernel(x_ref, wq_ref, k_ref, v_ref, wo_ref, g_ref, b_ref, o_ref):
    x = x_ref[...]
    q = _dot(x, wq_ref[...])
    outs = []
    for h in range(XATTN_HEADS):
        hs = slice(h * XATTN_HD, (h + 1) * XATTN_HD)
        s = _dot_nt(q[:, hs], k_ref[:, hs])
        e = jnp.exp(s - jnp.max(s, -1, keepdims=True))
        p = e / jnp.sum(e, -1, keepdims=True)
        outs.append(_dot(p, v_ref[:, hs]))
    o = jnp.concatenate(outs, axis=1)
    y = DN_ALPHA * x + _dot(o, wo_ref[...])
    o_ref[...] = _layer_norm_rows(y, g_ref[...], b_ref[...])


def _xattn_call(x, wq_scaled, k_mem, v_mem, wo, g, b, tm):
    S, D = x.shape
    M = k_mem.shape[0]
    const = lambda shape: pl.BlockSpec(shape, lambda i: (0, 0))
    return pl.pallas_call(
        _xattn_kernel,
        grid=(S // tm,),
        in_specs=[pl.BlockSpec((tm, D), lambda i: (i, 0)), const((D, D)), const((M, D)), const((M, D)),
                  const((D, D)), const((1, D)), const((1, D))],
        out_specs=pl.BlockSpec((tm, D), lambda i: (i, 0)),
        out_shape=jax.ShapeDtypeStruct((S, D), F32),
        compiler_params=_cparams(("parallel",)),
        name="xattn",
    )(x, wq_scaled, k_mem, v_mem, wo, g.reshape(1, D), b.reshape(1, D))


def _ffn_kernel(x_ref, wg_ref, wu_ref, wd_ref, g_ref, b_ref, o_ref, acc_ref):
    j = pl.program_id(1)
    x = x_ref[...]
    xb = x.astype(BF16)
    part = _dot(_silu(_dot(xb, wg_ref[...])) * _dot(xb, wu_ref[...]), wd_ref[...])

    @pl.when(j == 0)
    def _():
        acc_ref[...] = DN_ALPHA * x + part

    @pl.when(j > 0)
    def _():
        acc_ref[...] += part

    @pl.when(j == pl.num_programs(1) - 1)
    def _():
        o_ref[...] = _layer_norm_rows(acc_ref[...], g_ref[...], b_ref[...])


def _ffn_call(x, wg, wu, wd, g, b, tm, tf):
    S, D = x.shape
    F = wg.shape[1]
    return pl.pallas_call(
        _ffn_kernel,
        grid=(S // tm, F // tf),
        in_specs=[pl.BlockSpec((tm, D), lambda i, j: (i, 0)),
                  pl.BlockSpec((D, tf), lambda i, j: (0, j)),
                  pl.BlockSpec((D, tf), lambda i, j: (0, j)),
                  pl.BlockSpec((tf, D), lambda i, j: (j, 0)),
                  pl.BlockSpec((1, D), lambda i, j: (0, 0)),
                  pl.BlockSpec((1, D), lambda i, j: (0, 0))],
        out_specs=pl.BlockSpec((tm, D), lambda i, j: (i, 0)),
        out_shape=jax.ShapeDtypeStruct((S, D), F32),
        scratch_shapes=[pltpu.VMEM((tm, D), F32)],
        compiler_params=_cparams(("parallel", "arbitrary")),
        name="ffn",
    )(x, wg, wu, wd, g.reshape(1, D), b.reshape(1, D))


SSD_HD, SSD_HEADS, SSD_INNER, SSD_GROUPS, SSD_STATE = 64, 16, 1024, 2, 128
SSD_BC, SSD_CONV, SSD_CONV_CH, SSD_CHUNK = 256, 4, 1536, 128
SSD_COLS = SSD_INNER + SSD_CONV_CH
SSD_GW = SSD_INNER // SSD_GROUPS


def _softplus(x):
    return jnp.maximum(x, 0.0) + jnp.log(1.0 + jnp.exp(-jnp.abs(x)))


def _ssd_kernel(p_ref, cw_ref, cb_ref, dtb_ref, a_ref, dsk_ref, ng_ref, o_ref, prev_ref, st_ref, y_scr):
    L, G, NS, HD = SSD_CHUNK, SSD_GROUPS, SSD_STATE, SSD_HD
    HG = SSD_HEADS // G

    @pl.when(pl.program_id(0) == 0)
    def _():
        prev_ref[...] = jnp.zeros_like(prev_ref)
        st_ref[...] = jnp.zeros_like(st_ref)

    cur = p_ref[:, SSD_INNER:SSD_COLS]
    prev = prev_ref[...]
    row = _iota2(cur.shape, 0)
    conv = cur * cw_ref[SSD_CONV - 1:SSD_CONV, :] + cb_ref[...]
    for kk in range(1, SSD_CONV):
        shifted = jnp.where(row < kk, pltpu.roll(prev, kk, 0), pltpu.roll(cur, kk, 0))
        conv = conv + shifted * cw_ref[SSD_CONV - 1 - kk:SSD_CONV - kk, :]
    prev_ref[...] = cur
    xbc = _silu(conv)
    xs = xbc[:, :SSD_INNER]

    dt = _softplus(p_ref[:, SSD_COLS:] + dtb_ref[...])
    a_col = dt * a_ref[...]
    li = _iota2((L, L), 0)
    lj = _iota2((L, L), 1)
    cs = _dot_exact_lhs(jnp.where(lj <= li, 1.0, 0.0), a_col)
    cs_row = cs.T
    expand = jnp.where(_iota2((128, SSD_INNER), 1) // HD == _iota2((128, SSD_INNER), 0), 1.0, 0.0)
    dt_x = _dot_exact_rhs(dt, expand)
    cs_x = _dot_exact_rhs(cs, expand)
    cs_end = cs_x[L - 1:L]
    xd = xs * dt_x
    xd_dec = xd * jnp.exp(cs_end - cs_x)
    out_dec = jnp.exp(cs_x)
    chunk_dec = jnp.exp(cs_end)
    tril = lj <= li

    for g in range(G):
        gs = slice(g * SSD_GW, (g + 1) * SSD_GW)
        b_g = xbc[:, SSD_INNER + g * NS:SSD_INNER + (g + 1) * NS]
        c_g = xbc[:, SSD_INNER + SSD_BC + g * NS:SSD_INNER + SSD_BC + (g + 1) * NS]
        cb = _dot_nt(c_g, b_g)
        for j in range(HG):
            h = g * HG + j
            hs = slice(h * HD, (h + 1) * HD)
            seg = jnp.where(tril, jnp.exp(cs[:, h:h + 1] - cs_row[h:h + 1, :]), 0.0)
            y_scr[:, hs] = _dot(cb * seg, xd[:, hs])
        st = st_ref[g]
        y_off = _dot(c_g, st) * out_dec[:, gs]
        st_ref[g] = st * chunk_dec[:, gs] + _dot_tn(b_g, xd_dec[:, gs])
        y_scr[:, gs] = y_scr[:, gs] + y_off

    y = (y_scr[...] + dsk_ref[...] * xs) * _silu(p_ref[:, :SSD_INNER])
    for g in range(G):
        gs = slice(g * SSD_GW, (g + 1) * SSD_GW)
        y_g = y[:, gs]
        o_ref[:, gs] = (y_g * lax.rsqrt(jnp.mean(y_g * y_g, -1, keepdims=True) + 1e-5) * ng_ref[:, gs]).astype(o_ref.dtype)


def _ssd_call(p_ssd, conv_w, conv_b, dt_bias, a_log, d_skip, norm_g):
    S = p_ssd.shape[0]
    L = SSD_CHUNK
    a_neg = -jnp.exp(a_log.astype(F32))
    pad = lambda t: jnp.pad(t.astype(F32), (0, 128 - SSD_HEADS)).reshape(1, 128)
    const = lambda shape: pl.BlockSpec(shape, lambda i: (0, 0))
    return pl.pallas_call(
        _ssd_kernel,
        grid=(S // L,),
        in_specs=[pl.BlockSpec((L, SSD_COLS + 128), lambda i: (i, 0)),
                  const((SSD_CONV, SSD_CONV_CH)), const((1, SSD_CONV_CH)),
                  const((1, 128)), const((1, 128)),
                  const((1, SSD_INNER)), const((1, SSD_INNER))],
        out_specs=pl.BlockSpec((L, SSD_INNER), lambda i: (i, 0)),
        out_shape=jax.ShapeDtypeStruct((S, SSD_INNER), BF16),
        scratch_shapes=[pltpu.VMEM((L, SSD_CONV_CH), F32),
                        pltpu.VMEM((SSD_GROUPS, SSD_STATE, SSD_GW), F32),
                        pltpu.VMEM((L, SSD_INNER), F32)],
        compiler_params=_cparams(("arbitrary",)),
        name="ssd",
    )(p_ssd, conv_w.astype(F32), conv_b.reshape(1, -1).astype(F32), pad(dt_bias), pad(a_neg),
      jnp.repeat(d_skip.astype(F32), SSD_HD).reshape(1, -1), norm_g.reshape(1, -1).astype(F32))


MOBA_HD, MOBA_HEADS, MOBA_D, MOBA_BLOCK, MOBA_TOPK = 64, 8, 512, 256, 3
MOBA_GROUP = 4
MOBA_UNDERFLOW = -160.0
MOBA_BOUND_SLACK = 1.001
NEG_BIG = -1e30
LOG2E = math.log2(math.e)


MOBA_VROWS = MOBA_HD + 16


def _moba_proj_kernel(x_ref, wqt_ref, wk_ref, wvt_ref, qt_ref, k_ref, kmean_ref, v_ref, knorm_ref, qnorm_ref):
    xb = x_ref[...].astype(BF16)
    qt = _dot_nt(wqt_ref[...], xb).astype(qt_ref.dtype)
    qt_ref[...] = qt
    vt = _dot_nt(wvt_ref[...], xb)
    extra = jnp.where(_iota2((MOBA_VROWS - MOBA_HD, MOBA_BLOCK), 0) == 0, 1.0, 0.0)
    for h in range(MOBA_HEADS):
        v_ref[h, 0] = jnp.concatenate([vt[h * MOBA_HD:(h + 1) * MOBA_HD], extra], axis=0).astype(v_ref.dtype)
    k = _dot(xb, wk_ref[...])
    kmean_ref[0] = jnp.mean(k, 0, keepdims=True)
    shape = (MOBA_BLOCK, 2 * MOBA_HD)
    lane = _iota2(shape, 1)
    pos = jnp.where((lane == MOBA_HD) | (lane == MOBA_HD + 1), _iota2(shape, 0).astype(F32), 0.0).astype(k_ref.dtype)
    k_b = k.astype(k_ref.dtype)
    for h in range(MOBA_HEADS):
        k_ref[h, 0] = pos
        k_ref[h, 0, :, 0:MOBA_HD] = k_b[:, h * MOBA_HD:(h + 1) * MOBA_HD]
    k_f = k_b.astype(F32)
    q_f = qt.astype(F32)
    head_cols = jnp.where(_iota2((MOBA_D, 128), 0) // MOBA_HD == _iota2((MOBA_D, 128), 1), 1.0, 0.0)
    head_rows = jnp.where(_iota2((128, MOBA_D), 1) // MOBA_HD == _iota2((128, MOBA_D), 0), 1.0, 0.0)
    knorm_ref[0] = jnp.sqrt(jnp.max(_dot_exact_rhs(k_f * k_f, head_cols), 0, keepdims=True))
    qnorm_ref[0] = jnp.sqrt(jnp.max(_dot_exact_lhs(head_rows, q_f * q_f), 1, keepdims=True))


def _moba_proj_call(x, wqt, wk, wvt):
    S, D = x.shape
    nb = S // MOBA_BLOCK
    const = lambda shape: pl.BlockSpec(shape, lambda i: (0, 0))
    return pl.pallas_call(
        _moba_proj_kernel,
        grid=(nb,),
        in_specs=[pl.BlockSpec((MOBA_BLOCK, D), lambda i: (i, 0)), const((MOBA_D, D)), const((D, MOBA_D)),
                  const((MOBA_D, D))],
        out_specs=[pl.BlockSpec((MOBA_D, MOBA_BLOCK), lambda i: (0, i)),
                   pl.BlockSpec((MOBA_HEADS, 1, MOBA_BLOCK, 2 * MOBA_HD), lambda i: (0, i, 0, 0)),
                   pl.BlockSpec((1, 1, MOBA_D), lambda i: (i, 0, 0)),
                   pl.BlockSpec((MOBA_HEADS, 1, MOBA_VROWS, MOBA_BLOCK), lambda i: (0, i, 0, 0)),
                   pl.BlockSpec((1, 1, 128), lambda i: (i, 0, 0)),
                   pl.BlockSpec((1, 128, 1), lambda i: (i, 0, 0))],
        out_shape=[jax.ShapeDtypeStruct((MOBA_D, S), BF16),
                   jax.ShapeDtypeStruct((MOBA_HEADS, nb, MOBA_BLOCK, 2 * MOBA_HD), BF16),
                   jax.ShapeDtypeStruct((nb, 1, MOBA_D), F32),
                   jax.ShapeDtypeStruct((MOBA_HEADS, nb, MOBA_VROWS, MOBA_BLOCK), BF16),
                   jax.ShapeDtypeStruct((nb, 1, 128), F32),
                   jax.ShapeDtypeStruct((nb, 128, 1), F32)],
        compiler_params=_cparams(("parallel",)),
        name="moba_proj",
    )(x, wqt, wk, wvt)


def _moba_kernel(first_ref, qt_ref, k_ref, vt_ref, kmean_ref, o_ref, sel_ref, s0_ref, s1_ref, p0_ref, p1_ref):
    BS, HD = MOBA_BLOCK, MOBA_HD
    h = pl.program_id(0)
    i = pl.program_id(1)
    nb = k_ref.shape[0]
    qt = qt_ref[...]
    slope = LOG2E * jnp.exp2(jnp.zeros((1, BS), F32) - (h + 1).astype(F32) * (8.0 / MOBA_HEADS))
    slope_hi = slope.astype(qt.dtype).astype(F32)
    slope_lo = slope - slope_hi
    row = _iota2((HD, BS), 0)
    qt_ext = jnp.concatenate(
        [qt, jnp.where(row == 0, slope_hi, jnp.where(row == 1, slope_lo, 0.0)).astype(qt.dtype)], axis=0)

    gate = _dot(kmean_ref[...], qt)
    blk = _iota2((nb, BS), 0).astype(F32)
    cand = blk < i.astype(F32)
    sel = jnp.zeros((nb, BS), F32)
    for _ in range(MOBA_TOPK):
        best = jnp.max(jnp.where(cand, gate, -jnp.inf), 0, keepdims=True)
        idx = jnp.min(jnp.where(cand & (gate == best), blk, float(nb)), 0, keepdims=True)
        pick = blk == idx
        sel = jnp.where(pick, 1.0, sel)
        cand = cand & jnp.logical_not(pick)
    sel_ref[...] = sel

    G = MOBA_GROUP
    last = nb - 1
    s_ref = (s0_ref, s1_ref)
    p_ref = (p0_ref, p1_ref)

    first = first_ref[h * nb + i]

    def group(u):
        return [jnp.clip(first + G * u + x, 0, last) for x in range(G)]

    def issue_scores(u, slot):
        tops = []
        for x, j in enumerate(group(u)):
            sc = _dot(k_ref[j], qt_ext)
            s_ref[slot][x] = sc
            tops.append(jnp.max(sc, 0, keepdims=True))
        return tuple(tops)

    def value_blocks(u):
        js = group(u)
        js[0] = jnp.where(u == -1, i, js[0])
        return js

    def weighted_values(u, slot):
        out = None
        for x, j in enumerate(value_blocks(u)):
            part = _dot(vt_ref[j], p_ref[slot][x])
            out = part if out is None else out + part
        return out

    def step(u, slot, tops, a_prev, m, acc, issue_next=True):
        on, shift = [], []
        m_new = m
        for x, j in enumerate(group(u)):
            on.append(sel_ref[pl.ds(j, 1), :] > 0.0)
            shift.append(slope * ((j - i) * BS).astype(F32))
            m_new = jnp.maximum(m_new, jnp.where(on[x], tops[x] + shift[x], NEG_BIG))
        tops_next, pv = [], None
        for x, (j_prev, j_next) in enumerate(zip(value_blocks(u - 1), group(u + 1))):
            part = _dot(vt_ref[j_prev], p_ref[1 - slot][x])
            pv = part if pv is None else pv + part
            if issue_next:
                sc = _dot(k_ref[j_next], qt_ext)
                s_ref[1 - slot][x] = sc
                tops_next.append(jnp.max(sc, 0, keepdims=True))
            p = jnp.exp2(s_ref[slot][x] - (jnp.where(on[x], m_new, -NEG_BIG) - shift[x]))
            p_ref[slot][x] = p.astype(BF16)
        return tuple(tops_next), jnp.exp2(m - m_new), m_new, a_prev * acc + pv

    def body(w, carry):
        carry = step(2 * w, 0, *carry)
        return step(2 * w + 1, 1, *carry)

    tops0 = issue_scores(0, 0)
    s_own = jnp.where(_iota2((BS, BS), 1) >= _iota2((BS, BS), 0), _dot(k_ref[i], qt_ext), NEG_BIG)
    m_own = jnp.max(s_own, 0, keepdims=True)
    p1_ref[1:G] = jnp.zeros((G - 1, BS, BS), BF16)
    p1_ref[0] = jnp.exp2(s_own - m_own).astype(BF16)
    init = (tops0, jnp.ones((1, BS), F32), m_own, jnp.zeros((vt_ref.shape[1], BS), F32))
    steps = (i - first + G - 1) // G
    pairs = steps // 2
    carry = lax.fori_loop(0, pairs, body, init)

    def odd_tail(carry):
        _, a_last, _, acc = step(2 * pairs, 0, *carry, issue_next=False)
        return a_last * acc + weighted_values(2 * pairs, 0)

    def even_tail(carry):
        _, a_prev, _, acc = carry
        return a_prev * acc + weighted_values(2 * pairs - 1, 1)

    acc = lax.cond(steps % 2 == 1, odd_tail, even_tail, carry)
    o_ref[...] = (acc[0:HD] / acc[HD:HD + 1]).astype(o_ref.dtype)


def _moba_first_block(knorm, qnorm, nb):
    BS = MOBA_BLOCK
    k_norm = knorm[:, 0, 0:MOBA_HEADS].T
    q_norm = qnorm[:, 0:MOBA_HEADS, 0].T
    slope = LOG2E * jnp.exp2(-(jnp.arange(MOBA_HEADS, dtype=F32) + 1.0) * (8.0 / MOBA_HEADS))
    i_idx = jnp.arange(nb, dtype=F32)[None, :, None]
    j_idx = jnp.arange(nb, dtype=F32)[None, None, :]
    reach = slope[:, None, None] * ((BS - 1.0) - BS * (i_idx - j_idx))
    bound = q_norm[:, :, None] * (k_norm[:, None, :] + k_norm[:, :, None]) * MOBA_BOUND_SLACK + reach
    matters = (bound >= MOBA_UNDERFLOW) & (j_idx < i_idx)
    first = jnp.min(jnp.where(matters, j_idx, i_idx), axis=2).astype(jnp.int32)
    return first.reshape(-1)


def _moba_call(first_block, qt, k4, vt4, kmean):
    S = qt.shape[1]
    nb = S // MOBA_BLOCK
    return pl.pallas_call(
        _moba_kernel,
        grid_spec=pltpu.PrefetchScalarGridSpec(
            num_scalar_prefetch=1,
            grid=(MOBA_HEADS, nb),
            in_specs=[pl.BlockSpec((MOBA_HD, MOBA_BLOCK), lambda h, i, first: (h, i)),
                      pl.BlockSpec((None, nb, MOBA_BLOCK, 2 * MOBA_HD), lambda h, i, first: (h, 0, 0, 0)),
                      pl.BlockSpec((None, nb, MOBA_VROWS, MOBA_BLOCK), lambda h, i, first: (h, 0, 0, 0)),
                      pl.BlockSpec((None, nb, MOBA_HD), lambda h, i, first: (h, 0, 0))],
            out_specs=pl.BlockSpec((MOBA_HD, MOBA_BLOCK), lambda h, i, first: (h, i)),
            scratch_shapes=[pltpu.VMEM((nb, MOBA_BLOCK), F32)]
            + [pltpu.VMEM((MOBA_GROUP, MOBA_BLOCK, MOBA_BLOCK), F32)] * 2
            + [pltpu.VMEM((MOBA_GROUP, MOBA_BLOCK, MOBA_BLOCK), BF16)] * 2),
        out_shape=jax.ShapeDtypeStruct((MOBA_D, S), BF16),
        compiler_params=_cparams(("parallel", "arbitrary")),
        name="moba",
    )(first_block, qt, k4, vt4, kmean)


N_EXPERTS, TOP_K, EXPERT_FF = 8, 2, 2816
MOE_ROWS = 512


def _router_kernel(x_ref, whi_ref, wlo_ref, b_ref, o_ref):
    x = x_ref[...]
    xhi = x.astype(BF16)
    xlo = (x - xhi.astype(F32)).astype(BF16)
    logits = (jnp.dot(xhi, whi_ref[...], preferred_element_type=F32)
              + jnp.dot(xhi, wlo_ref[...], preferred_element_type=F32)
              + jnp.dot(xlo, whi_ref[...], preferred_element_type=F32)) + b_ref[...]
    lane = _iota2(logits.shape, 1).astype(F32)
    logits = jnp.where(lane < N_EXPERTS, logits, -jnp.inf)
    m1 = jnp.max(logits, -1, keepdims=True)
    i1 = jnp.min(jnp.where(logits == m1, lane, 128.0), -1, keepdims=True)
    rest = jnp.where(lane == i1, -jnp.inf, logits)
    m2 = jnp.max(rest, -1, keepdims=True)
    i2 = jnp.min(jnp.where(rest == m2, lane, 128.0), -1, keepdims=True)
    e = jnp.exp(m2 - m1)
    g1 = 1.0 / (1.0 + e)
    g2 = e / (1.0 + e)
    out = jnp.where(lane == 0, i1, 0.0)
    out = jnp.where(lane == 1, i2, out)
    out = jnp.where(lane == 2, g1, out)
    out = jnp.where(lane == 3, g2, out)
    o_ref[...] = out


def _router_call(x, w_router, b_router, tm):
    S, D = x.shape
    wp = jnp.pad(w_router.astype(F32), ((0, 0), (0, 128 - N_EXPERTS)))
    whi = wp.astype(BF16)
    wlo = (wp - whi.astype(F32)).astype(BF16)
    bp = jnp.pad(b_router.astype(F32), (0, 128 - N_EXPERTS)).reshape(1, 128)
    const = lambda shape: pl.BlockSpec(shape, lambda i: (0, 0))
    return pl.pallas_call(
        _router_kernel,
        grid=(S // tm,),
        in_specs=[pl.BlockSpec((tm, D), lambda i: (i, 0)), const((D, 128)), const((D, 128)), const((1, 128))],
        out_specs=pl.BlockSpec((tm, 128), lambda i: (i, 0)),
        out_shape=jax.ShapeDtypeStruct((S, 128), F32),
        compiler_params=_cparams(("parallel",)),
        name="router",
    )(x, whi, wlo, bp)


def _row_copy(src_hbm, dst_ref, src_row, dst_row, sem):
    return pltpu.make_async_copy(src_hbm.at[pl.ds(src_row, 1)], dst_ref.at[pl.ds(dst_row, 1)], sem)


def _scatter_rows_kernel(dest_ref, lo_ref, hi_ref, x_ref, o_hbm, zero_ref, sem):
    step = pl.program_id(0)
    n_tiles = pl.num_programs(0) - 1
    tm = x_ref.shape[0]

    def row_out(src_ref, src_row, dst_row):
        return pltpu.make_async_copy(src_ref.at[pl.ds(src_row, 1)], o_hbm.at[pl.ds(dst_row, 1)], sem)

    @pl.when(step < n_tiles)
    def _():
        base = step * tm

        def start(r, c):
            for slot in range(TOP_K):
                row_out(x_ref, r, dest_ref[TOP_K * (base + r) + slot]).start()
            return c

        lax.fori_loop(0, tm, start, 0, unroll=8)
        for slot in range(TOP_K):
            pltpu.make_async_copy(x_ref, o_hbm.at[pl.ds(0, tm)], sem).wait()

    @pl.when(step == n_tiles)
    def _():
        zero_ref[...] = jnp.zeros_like(zero_ref)
        for e in range(N_EXPERTS):
            def start(r, c):
                row_out(zero_ref, 0, r).start()
                return c

            def wait(r, c):
                row_out(zero_ref, 0, r).wait()
                return c

            lax.fori_loop(lo_ref[e], hi_ref[e], start, 0)
            lax.fori_loop(lo_ref[e], hi_ref[e], wait, 0)


def _scatter_rows(x, dest, pad_lo, pad_hi, n_rows, tm):
    T, D = x.shape
    n_tiles = T // tm
    return pl.pallas_call(
        _scatter_rows_kernel,
        grid_spec=pltpu.PrefetchScalarGridSpec(
            num_scalar_prefetch=3,
            grid=(n_tiles + 1,),
            in_specs=[pl.BlockSpec((tm, D), lambda i, d, lo, hi: (jnp.minimum(i, n_tiles - 1), 0))],
            out_specs=pl.BlockSpec(memory_space=pl.ANY),
            scratch_shapes=[pltpu.VMEM((8, D), x.dtype), pltpu.SemaphoreType.DMA(())]),
        out_shape=jax.ShapeDtypeStruct((n_rows, D), x.dtype),
        compiler_params=_cparams(("arbitrary",)),
        name="moe_scatter",
    )(dest, pad_lo, pad_hi, x)


def _moe_ffn_kernel(be_ref, nu_ref, x_ref, wg_ref, wu_ref, wd_ref, o_ref, acc_ref):
    i = pl.program_id(0)
    j = pl.program_id(1)

    @pl.when(i < nu_ref[0])
    def _():
        xb = x_ref[...].astype(BF16)
        part = _dot(_silu(_dot(xb, wg_ref[...])) * _dot(xb, wu_ref[...]), wd_ref[...])

        @pl.when(j == 0)
        def _():
            acc_ref[...] = part

        @pl.when(j > 0)
        def _():
            acc_ref[...] += part

    last = j == pl.num_programs(1) - 1

    @pl.when(last & (i < nu_ref[0]))
    def _():
        o_ref[...] = acc_ref[...]

    @pl.when(last & (i >= nu_ref[0]))
    def _():
        o_ref[...] = jnp.zeros_like(o_ref)


def _moe_ffn_call(x_rows, block_e, n_used, wg, wu, wd, tf):
    n, D = x_rows.shape
    F = wg.shape[2]
    R = MOE_ROWS
    return pl.pallas_call(
        _moe_ffn_kernel,
        grid_spec=pltpu.PrefetchScalarGridSpec(
            num_scalar_prefetch=2,
            grid=(n // R, F // tf),
            in_specs=[pl.BlockSpec((R, D), lambda i, j, be, nu: (i, 0)),
                      pl.BlockSpec((None, D, tf), lambda i, j, be, nu: (be[i], 0, j)),
                      pl.BlockSpec((None, D, tf), lambda i, j, be, nu: (be[i], 0, j)),
                      pl.BlockSpec((None, tf, D), lambda i, j, be, nu: (be[i], j, 0))],
            out_specs=pl.BlockSpec((R, D), lambda i, j, be, nu: (i, 0)),
            scratch_shapes=[pltpu.VMEM((R, D), F32)]),
        out_shape=jax.ShapeDtypeStruct((n, D), F32),
        compiler_params=_cparams(("arbitrary", "arbitrary")),
        name="moe_ffn",
    )(block_e, n_used, x_rows, wg, wu, wd)


def _moe_combine_kernel(d_ref, y_hbm, x_ref, r_ref, g_ref, b_ref, o_ref, y1_ref, y2_ref, sem):
    tm = x_ref.shape[0]
    base = pl.program_id(0) * tm

    def start(r, c):
        _row_copy(y_hbm, y1_ref, d_ref[2 * (base + r)], r, sem.at[0]).start()
        _row_copy(y_hbm, y2_ref, d_ref[2 * (base + r) + 1], r, sem.at[1]).start()
        return c

    lax.fori_loop(0, tm, start, 0, unroll=8)
    pltpu.make_async_copy(y_hbm.at[pl.ds(0, tm)], y1_ref, sem.at[0]).wait()
    pltpu.make_async_copy(y_hbm.at[pl.ds(0, tm)], y2_ref, sem.at[1]).wait()
    y = r_ref[:, 2:3] * y1_ref[...] + r_ref[:, 3:4] * y2_ref[...]
    o_ref[...] = _layer_norm_rows(DN_ALPHA * x_ref[...] + y, g_ref[...], b_ref[...])


def _moe_combine_call(dest, y_rows, x, routed, g, b, tm):
    S, D = x.shape
    return pl.pallas_call(
        _moe_combine_kernel,
        grid_spec=pltpu.PrefetchScalarGridSpec(
            num_scalar_prefetch=1,
            grid=(S // tm,),
            in_specs=[pl.BlockSpec(memory_space=pl.ANY),
                      pl.BlockSpec((tm, D), lambda i, d: (i, 0)),
                      pl.BlockSpec((tm, 128), lambda i, d: (i, 0)),
                      pl.BlockSpec((1, D), lambda i, d: (0, 0)),
                      pl.BlockSpec((1, D), lambda i, d: (0, 0))],
            out_specs=pl.BlockSpec((tm, D), lambda i, d: (i, 0)),
            scratch_shapes=[pltpu.VMEM((tm, D), F32), pltpu.VMEM((tm, D), F32), pltpu.SemaphoreType.DMA((2,))]),
        out_shape=jax.ShapeDtypeStruct((S, D), F32),
        compiler_params=_cparams(("arbitrary",)),
        name="moe_combine",
    )(dest, y_rows, x, routed, g.reshape(1, D), b.reshape(1, D))


def _moe_sublayer(x, w_router, b_router, wg, wu, wd, g, b):
    T = x.shape[0]
    R = MOE_ROWS
    routed = _router_call(x, w_router, b_router, min(ROW_TILE, T))
    top_e = routed[:, 0:TOP_K].astype(jnp.int32)
    tok_oh = jnp.sum((top_e[:, :, None] == jnp.arange(N_EXPERTS)[None, None, :]).astype(jnp.int32), axis=1)
    counts = jnp.sum(tok_oh, axis=0)
    rank = jnp.cumsum(tok_oh, axis=0) - tok_oh
    padded = (counts + R - 1) // R * R
    pend = jnp.cumsum(padded)
    pstart = pend - padded
    dest = pstart[top_e] + jnp.take_along_axis(rank, top_e, axis=1)
    n_rows = (T * TOP_K + N_EXPERTS * (R - 1)) // R * R
    n_blocks = n_rows // R
    block_first_row = jnp.arange(n_blocks, dtype=jnp.int32) * R
    block_e = jnp.minimum(jnp.sum((pend[None, :] <= block_first_row[:, None]).astype(jnp.int32), axis=1),
                          N_EXPERTS - 1)
    n_used = (pend[-1] // R).astype(jnp.int32).reshape(1)
    dest = dest.reshape(-1).astype(jnp.int32)
    pad_lo = (pstart + counts).astype(jnp.int32)
    pad_hi = jnp.concatenate([pstart[1:], jnp.array([n_rows])]).astype(jnp.int32)
    x_rows = _scatter_rows(x, dest, pad_lo, pad_hi, n_rows, min(ROW_TILE, T))
    y_rows = _moe_ffn_call(x_rows, block_e, n_used, wg, wu, wd, EXPERT_FF // 2)
    return _moe_combine_call(dest, y_rows, x, routed, g, b, min(COMBINE_TILE, T))


GLA_IN = 2 * GLA_QK + 2 * GLA_V + GLA_LR


def _gla_rwkv_sublayer(x, w_in, gla_wa2, gla_ba, gla_norm, mu, w0, w2, a0, a2, g2, k_k, k_a, r_k, ln_w, ln_b,
                       w_out, ln_g, ln_bias):
    S = x.shape[0]
    tm = min(ROW_TILE, S)
    lr_pad = 128 - GLA_LR
    w_gla = jnp.pad(w_in[:, :GLA_IN], ((0, 0), (0, lr_pad))).astype(BF16)
    w_rwkv = w_in[:, GLA_IN:].astype(BF16)
    p_gla = _matmul(x, w_gla, F32, tm, GLA_COLS)
    p_rwkv = _matmul(x, w_rwkv, F32, tm, RWKV_COLS)
    wa2p = jnp.pad(gla_wa2, ((0, lr_pad), (0, 0))).astype(BF16)
    o_gla = _gla_call(p_gla, wa2p, gla_ba, gla_norm)
    o_rwkv = _rwkv_call(p_rwkv, mu, w0, w2, a0, a2, g2, k_k, k_a, r_k, ln_w, ln_b)
    return _matmul_ln([o_gla, o_rwkv], [w_out[:GLA_V].astype(BF16), w_out[GLA_V:].astype(BF16)], x,
                      ln_g, ln_bias, tm)


def _xattn_sublayer(x, mem, wq, wk, wv, wo, ln_g, ln_bias):
    M = mem.shape[0]
    k_mem = _matmul(mem, wk.astype(BF16), BF16, M, D_MODEL)
    v_mem = _matmul(mem, wv.astype(BF16), BF16, M, D_MODEL)
    wq_scaled = (wq * XATTN_HD ** -0.5).astype(BF16)
    return _xattn_call(x, wq_scaled, k_mem, v_mem, wo.astype(BF16), ln_g, ln_bias, min(ROW_TILE, x.shape[0]))


def _ssd_moba_sublayer(x, w_in, conv_w, conv_b, dt_bias, a_log, d_skip, ssd_norm, w_out, ln_g, ln_bias):
    S = x.shape[0]
    tm = min(ROW_TILE, S)
    nb = S // MOBA_BLOCK
    o_dt = SSD_COLS
    o_q = o_dt + SSD_HEADS
    w_ssd = jnp.pad(w_in[:, :o_q], ((0, 0), (0, 128 - SSD_HEADS))).astype(BF16)
    w_q = w_in[:, o_q:o_q + MOBA_D]
    w_k = w_in[:, o_q + MOBA_D:o_q + 2 * MOBA_D]
    w_v = w_in[:, o_q + 2 * MOBA_D:]
    p_ssd = _matmul(x, w_ssd, F32, tm, SSD_COLS + 128)
    o_ssd = _ssd_call(p_ssd, conv_w, conv_b, dt_bias, a_log, d_skip, ssd_norm)
    qt, k4, kmean, vt4, knorm, qnorm = _moba_proj_call(x, (w_q.T * (MOBA_HD ** -0.5 * LOG2E)).astype(BF16), w_k.astype(BF16),
                                         w_v.T.astype(BF16))
    kmean_h = kmean.reshape(nb, MOBA_HEADS, MOBA_HD).transpose(1, 0, 2)
    ot_moba = _moba_call(_moba_first_block(knorm, qnorm, nb), qt, k4, vt4, kmean_h)
    return _matmul_ln([o_ssd, ot_moba], [w_out[:SSD_INNER].astype(BF16), w_out[SSD_INNER:].astype(BF16)], x,
                      ln_g, ln_bias, tm, transposed=(False, True))


def kernel(x, mem, l0_w_in, l0_gla_wa2, l0_gla_ba, l0_gla_norm, l0_rwkv_mu, l0_rwkv_w0, l0_rwkv_w2, l0_rwkv_a0, l0_rwkv_a2, l0_rwkv_g2, l0_rwkv_kk, l0_rwkv_ka, l0_rwkv_rk, l0_rwkv_lnw, l0_rwkv_lnb, l0_w_out, l0_ln1_g, l0_ln1_b, l0_xq, l0_xk, l0_xv, l0_xo, l0_ln2_g, l0_ln2_b, l0_ffn_wg, l0_ffn_wu, l0_ffn_wd, l0_ln3_g, l0_ln3_b, l1_w_in, l1_conv_w, l1_conv_b, l1_dt_bias, l1_a_log, l1_d_skip, l1_ssd_norm, l1_w_out, l1_ln1_g, l1_ln1_b, l1_xq, l1_xk, l1_xv, l1_xo, l1_ln2_g, l1_ln2_b, l1_router, l1_router_b, l1_exp_wg, l1_exp_wu, l1_exp_wd, l1_ln3_g, l1_ln3_b):
    x2 = x.reshape(-1, D_MODEL)
    mem2 = mem.reshape(-1, D_MODEL)
    x2 = _gla_rwkv_sublayer(x2, l0_w_in, l0_gla_wa2, l0_gla_ba, l0_gla_norm, l0_rwkv_mu, l0_rwkv_w0, l0_rwkv_w2,
                            l0_rwkv_a0, l0_rwkv_a2, l0_rwkv_g2, l0_rwkv_kk, l0_rwkv_ka, l0_rwkv_rk, l0_rwkv_lnw,
                            l0_rwkv_lnb, l0_w_out, l0_ln1_g, l0_ln1_b)
    x2 = _xattn_sublayer(x2, mem2, l0_xq, l0_xk, l0_xv, l0_xo, l0_ln2_g, l0_ln2_b)
    tm = min(ROW_TILE, x2.shape[0])
    x2 = _ffn_call(x2, l0_ffn_wg.astype(BF16), l0_ffn_wu.astype(BF16), l0_ffn_wd.astype(BF16),
                   l0_ln3_g, l0_ln3_b, tm, l0_ffn_wg.shape[1] // 2)
    x2 = _ssd_moba_sublayer(x2, l1_w_in, l1_conv_w, l1_conv_b, l1_dt_bias, l1_a_log, l1_d_skip, l1_ssd_norm,
                            l1_w_out, l1_ln1_g, l1_ln1_b)
    x2 = _xattn_sublayer(x2, mem2, l1_xq, l1_xk, l1_xv, l1_xo, l1_ln2_g, l1_ln2_b)
    x2 = _moe_sublayer(x2, l1_router, l1_router_b, l1_exp_wg.astype(BF16), l1_exp_wu.astype(BF16),
                       l1_exp_wd.astype(BF16), l1_ln3_g, l1_ln3_b)
    return x2.reshape(x.shape)
```

```python
import functools
import math

import jax
import jax.numpy as jnp
from jax import lax
from jax.experimental import pallas as pl
from jax.experimental.pallas import tpu as pltpu

BF16 = jnp.bfloat16
F32 = jnp.float32

D_MODEL = 1024
LN_EPS = 1e-5
DEPTH = 2
DN_ALPHA = (2 * DEPTH) ** 0.25

GLA_HEADS, GLA_DK, GLA_DV, GLA_CHUNK = 4, 64, 128, 64
GLA_QK, GLA_V, GLA_LR, GLA_TAU = 256, 512, 16, 16.0
GLA_COLS = 2 * GLA_QK + 2 * GLA_V + 128

RWKV_HEADS, RWKV_HD, RWKV_D, RWKV_CHUNK = 8, 64, 512, 64
RWKV_COLS = 1792
RWKV_DECAY_SCALE = math.exp(-0.5)
RWKV_GN_EPS = 64e-5

VMEM_LIMIT = 56 * 1024 * 1024
ROW_TILE = 512
SCAN_TILE = 256
COMBINE_TILE = 256


def _cparams(sem):
    return pltpu.CompilerParams(dimension_semantics=sem, vmem_limit_bytes=VMEM_LIMIT)


def _dot(a, b):
    return jnp.dot(a.astype(BF16), b.astype(BF16), preferred_element_type=F32)


def _dot_nt(a, b):
    return lax.dot_general(a.astype(BF16), b.astype(BF16), (((1,), (1,)), ((), ())), preferred_element_type=F32)


def _dot_tn(a, b):
    return lax.dot_general(a.astype(BF16), b.astype(BF16), (((0,), (0,)), ((), ())), preferred_element_type=F32)


def _split3(x):
    hi = x.astype(BF16)
    r1 = x - hi.astype(F32)
    mid = r1.astype(BF16)
    lo = (r1 - mid.astype(F32)).astype(BF16)
    return hi, mid, lo


def _dot_exact_lhs(m, x):
    mb = m.astype(BF16)
    hi, mid, lo = _split3(x)
    return (jnp.dot(mb, hi, preferred_element_type=F32) + jnp.dot(mb, mid, preferred_element_type=F32)
            + jnp.dot(mb, lo, preferred_element_type=F32))


def _dot_exact_rhs(x, m):
    mb = m.astype(BF16)
    hi, mid, lo = _split3(x)
    return (jnp.dot(hi, mb, preferred_element_type=F32) + jnp.dot(mid, mb, preferred_element_type=F32)
            + jnp.dot(lo, mb, preferred_element_type=F32))


def _dot_stat_rhs(x, m):
    mb = m.astype(BF16)
    hi = x.astype(BF16)
    lo = (x - hi.astype(F32)).astype(BF16)
    return jnp.dot(hi, mb, preferred_element_type=F32) + jnp.dot(lo, mb, preferred_element_type=F32)


def _sigmoid(x):
    return 1.0 / (1.0 + jnp.exp(-x))


def _silu(x):
    return x * _sigmoid(x)


def _iota2(shape, axis):
    return lax.broadcasted_iota(jnp.int32, shape, axis)


def _chunk_tril(n, chunk):
    r = _iota2((n, n), 0)
    c = _iota2((n, n), 1)
    return jnp.where((c <= r) & ((r // chunk) == (c // chunk)), 1.0, 0.0)


def _head_block(n, width, value):
    r = _iota2((n, n), 0)
    c = _iota2((n, n), 1)
    return jnp.where((r // width) == (c // width), value, 0.0)


def _mm_kernel(x_ref, w_ref, o_ref):
    o_ref[...] = _dot(x_ref[...], w_ref[...]).astype(o_ref.dtype)


def _matmul(x, w, out_dtype, tm, tn):
    S, K = x.shape
    N = w.shape[1]
    return pl.pallas_call(
        _mm_kernel,
        grid=(S // tm, N // tn),
        in_specs=[pl.BlockSpec((tm, K), lambda i, j: (i, 0)),
                  pl.BlockSpec((K, tn), lambda i, j: (0, j))],
        out_specs=pl.BlockSpec((tm, tn), lambda i, j: (i, j)),
        out_shape=jax.ShapeDtypeStruct((S, N), out_dtype),
        compiler_params=_cparams(("parallel", "arbitrary")),
        name="matmul",
    )(x, w)


def _layer_norm_rows(y, g, b):
    mu = jnp.mean(y, -1, keepdims=True)
    d = y - mu
    var = jnp.mean(d * d, -1, keepdims=True)
    return d * lax.rsqrt(var + LN_EPS) * g + b


def _mm_ln_kernel(transposed, *refs):
    n_in = len(transposed)
    a_refs = refs[:n_in]
    w_refs = refs[n_in:2 * n_in]
    x_ref, g_ref, b_ref, o_ref = refs[2 * n_in:]
    acc = DN_ALPHA * x_ref[...]
    for a_ref, w_ref, tr in zip(a_refs, w_refs, transposed):
        acc = acc + (_dot_tn if tr else _dot)(a_ref[...], w_ref[...])
    o_ref[...] = _layer_norm_rows(acc, g_ref[...], b_ref[...])


def _matmul_ln(a_list, w_list, x, g, b, tm, transposed=None):
    S, D = x.shape
    transposed = tuple(transposed or (False,) * len(a_list))
    in_specs = ([pl.BlockSpec((a.shape[0], tm), lambda i: (0, i)) if tr else
                 pl.BlockSpec((tm, a.shape[1]), lambda i: (i, 0)) for a, tr in zip(a_list, transposed)]
                + [pl.BlockSpec(w.shape, lambda i: (0, 0)) for w in w_list]
                + [pl.BlockSpec((tm, D), lambda i: (i, 0)),
                   pl.BlockSpec((1, D), lambda i: (0, 0)),
                   pl.BlockSpec((1, D), lambda i: (0, 0))])
    return pl.pallas_call(
        functools.partial(_mm_ln_kernel, transposed),
        grid=(S // tm,),
        in_specs=in_specs,
        out_specs=pl.BlockSpec((tm, D), lambda i: (i, 0)),
        out_shape=jax.ShapeDtypeStruct((S, D), F32),
        compiler_params=_cparams(("parallel",)),
        name="matmul_ln",
    )(*a_list, *w_list, x, g.reshape(1, D), b.reshape(1, D))


def _gla_kernel(p_ref, wa2_ref, ba_ref, ng_ref, o_ref, st_ref, o_scr):
    C, H, dk, dv = GLA_CHUNK, GLA_HEADS, GLA_DK, GLA_DV
    tb = p_ref.shape[0]

    @pl.when(pl.program_id(0) == 0)
    def _():
        st_ref[...] = jnp.zeros_like(st_ref)

    z = _dot(p_ref[:, 2 * GLA_QK + 2 * GLA_V:], wa2_ref[...]) + ba_ref[...]
    log_a = -(jnp.maximum(-z, 0.0) + jnp.log(1.0 + jnp.exp(-jnp.abs(z)))) / GLA_TAU
    b = _dot_exact_lhs(_chunk_tril(tb, C), log_a)
    causal = _iota2((C, C), 1) <= _iota2((C, C), 0)

    nc = tb // C
    q_h, k_h, ke_h, v_h, dec_h = [], [], [], [], []
    for c in range(nc):
        rows = slice(c * C, (c + 1) * C)
        b_c = b[rows]
        b_last = b_c[C - 1:C]
        q_dec = p_ref[rows, 0:GLA_QK] * (dk ** -0.5) * jnp.exp(b_c)
        k_c = p_ref[rows, GLA_QK:2 * GLA_QK]
        k_dec = k_c * jnp.exp(-b_c)
        k_end = k_c * jnp.exp(b_last - b_c)
        decay = jnp.exp(b_last)
        for h in range(H):
            ks = slice(h * dk, (h + 1) * dk)
            q_h.append(q_dec[:, ks])
            k_h.append(k_dec[:, ks])
            ke_h.append(k_end[:, ks])
            dec_h.append(decay[:, ks])
            v_h.append(p_ref[rows, 2 * GLA_QK + h * dv:2 * GLA_QK + (h + 1) * dv])
    n = nc * H
    attn = [jnp.where(causal, _dot_nt(q_h[i], k_h[i]), 0.0) for i in range(n)]
    kv = [_dot_tn(v_h[i], ke_h[i]) for i in range(n)]
    intra = [_dot(attn[i], v_h[i]) for i in range(n)]
    state = [st_ref[:, h * dk:(h + 1) * dk] for h in range(H)]
    entering = []
    for i in range(n):
        entering.append(state[i % H])
        state[i % H] = state[i % H] * dec_h[i] + kv[i]
    for i in range(n):
        c, h = divmod(i, H)
        o_scr[c * C:(c + 1) * C, h * dv:(h + 1) * dv] = intra[i] + _dot_nt(q_h[i], entering[i])
    for h in range(H):
        st_ref[:, h * dk:(h + 1) * dk] = state[h]

    for h in range(H):
        vs = slice(h * dv, (h + 1) * dv)
        o_h = o_scr[:, vs]
        g_h = p_ref[:, 2 * GLA_QK + GLA_V + h * dv:2 * GLA_QK + GLA_V + (h + 1) * dv]
        o_h = o_h * lax.rsqrt(jnp.mean(o_h * o_h, -1, keepdims=True) + 1e-5) * ng_ref[:, vs]
        o_ref[:, vs] = (o_h * _silu(g_h)).astype(o_ref.dtype)


def _gla_call(p_gla, wa2p, ba, norm_g, tb=SCAN_TILE):
    S = p_gla.shape[0]
    return pl.pallas_call(
        _gla_kernel,
        grid=(S // tb,),
        in_specs=[pl.BlockSpec((tb, GLA_COLS), lambda i: (i, 0)),
                  pl.BlockSpec((128, GLA_QK), lambda i: (0, 0)),
                  pl.BlockSpec((1, GLA_QK), lambda i: (0, 0)),
                  pl.BlockSpec((1, GLA_V), lambda i: (0, 0))],
        out_specs=pl.BlockSpec((tb, GLA_V), lambda i: (i, 0)),
        out_shape=jax.ShapeDtypeStruct((S, GLA_V), BF16),
        scratch_shapes=[pltpu.VMEM((GLA_DV, GLA_QK), F32), pltpu.VMEM((tb, GLA_V), F32)],
        compiler_params=_cparams(("arbitrary",)),
        name="gla",
    )(p_gla, wa2p, ba.reshape(1, GLA_QK), norm_g.reshape(1, GLA_V))


def _rwkv_kernel(p_ref, mu_ref, w0_ref, w2_ref, a0_ref, a2_ref, g2_ref, kk_ref, ka_ref, rk_ref, lnw_ref, lnb_ref,
                 o_ref, prev_ref, h_ref, o_scr):
    C, H, N, D = RWKV_CHUNK, RWKV_HEADS, RWKV_HD, RWKV_D
    tb = p_ref.shape[0]
    first = pl.program_id(0) == 0

    @pl.when(first)
    def _():
        prev_ref[...] = jnp.zeros_like(prev_ref)
        h_ref[...] = jnp.zeros_like(h_ref)

    p = p_ref[...]
    shifted = jnp.where(_iota2(p.shape, 0) == 0, prev_ref[...], pltpu.roll(p, 1, 0))
    prev_ref[...] = p[tb - 1:tb]
    p = p + mu_ref[...] * (shifted - p)
    r = p[:, 0:D]
    k = p[:, D:2 * D]
    v = p[:, 2 * D:3 * D]
    xw = p[:, 3 * D:3 * D + 64]
    xa = p[:, 3 * D + 64:3 * D + 128]
    xg = p[:, 3 * D + 128:3 * D + 256]
    lw = -RWKV_DECAY_SCALE * _sigmoid(w0_ref[...] + _dot(jnp.tanh(xw), w2_ref[...]))
    a = _sigmoid(a0_ref[...] + _dot(xa, a2_ref[...]))
    g = _dot(_sigmoid(xg), g2_ref[...])
    head_ones = _head_block(D, N, 1.0)
    kk = k * kk_ref[...]
    kk = kk * lax.rsqrt(jnp.maximum(_dot_stat_rhs(kk * kk, head_ones), 1e-24))
    k = k * (1.0 + (a - 1.0) * ka_ref[...])
    pv = -kk * a
    cw = _dot_exact_lhs(_chunk_tril(tb, C), lw)
    cwx = cw - lw

    gi = _iota2((2 * C, 2 * C), 0)
    gj = _iota2((2 * C, 2 * C), 1) % C
    gram_mask = ((gi < C) & (gj < gi)) | ((gi >= C) & (gj <= gi - C))
    eye = _iota2((C, C), 0) == _iota2((C, C), 1)
    eye_f = jnp.where(eye, 1.0, 0.0)

    nc = tb // C
    items = [(c, h) for c in range(nc) for h in range(H)]
    xs, ys, pk_e, b_h, v_h, r_h, g_h = [], [], [], [], [], [], []
    for c in range(nc):
        rows = slice(c * C, (c + 1) * C)
        cw_c = cw[rows]
        cw_end = cw_c[C - 1:C]
        e_pos = jnp.exp(cw_c)
        e_neg = jnp.exp(-cw_c)
        e_end = jnp.exp(cw_end - cw_c)
        r_t = r[rows] * e_pos
        b_t = kk[rows] * jnp.exp(cwx[rows])
        p_t = pv[rows] * e_neg
        k_t = k[rows] * e_neg
        p_e = pv[rows] * e_end
        k_e = k[rows] * e_end
        g_end = jnp.exp(cw_end)
        v_c = v[rows]
        for h in range(H):
            hs = slice(h * N, (h + 1) * N)
            xs.append(jnp.concatenate([b_t[:, hs], r_t[:, hs]], axis=0))
            ys.append(jnp.concatenate([p_t[:, hs], k_t[:, hs]], axis=0))
            pk_e.append(jnp.concatenate([p_e[:, hs], k_e[:, hs]], axis=0))
            b_h.append(b_t[:, hs])
            v_h.append(v_c[:, hs])
            r_h.append(r_t[:, hs])
            g_h.append(g_end[:, hs])
    n = len(items)
    grams = [jnp.where(gram_mask, _dot_nt(xs[i], ys[i]), 0.0) for i in range(n)]
    l_p = [g[0:C, 0:C] for g in grams]
    m_pk = [g[C:2 * C, :] for g in grams]
    lkv = [_dot(grams[i][0:C, C:2 * C], v_h[i]) for i in range(n)]
    x = [_dot(lp, lp) for lp in l_p]
    t = [eye_f + lp for lp in l_p]
    for _ in range(4):
        tx = [_dot(jnp.concatenate([t[i], x[i]], axis=0), x[i]) for i in range(n)]
        t = [t[i] + tx[i][0:C] for i in range(n)]
        x = [tx[i][C:2 * C] for i in range(n)]
    t = [t[i] + _dot(t[i], x[i]) for i in range(n)]
    wu = [_dot(t[i], jnp.concatenate([b_h[i], lkv[i]], axis=1)) for i in range(n)]
    rhs = [jnp.concatenate([wu[i], jnp.concatenate([jnp.zeros((C, N), F32), v_h[i]], axis=1)], axis=0)
           for i in range(n)]
    az = [_dot_tn(pk_e[i], rhs[i]) for i in range(n)]
    qo = [_dot(m_pk[i], rhs[i]) for i in range(n)]
    state = [h_ref[h] for h in range(H)]
    for i, (c, h) in enumerate(items):
        a_mat = az[i][:, 0:N] + jnp.where(eye, g_h[i], 0.0)
        q_mat = qo[i][:, 0:N] + r_h[i]
        oh = _dot(jnp.concatenate([q_mat, a_mat], axis=0), state[h])
        o_scr[c * C:(c + 1) * C, h * N:(h + 1) * N] = oh[0:C] + qo[i][:, N:2 * N]
        state[h] = oh[C:C + N] + az[i][:, N:2 * N]
    for h in range(H):
        h_ref[h] = state[h]

    o = o_scr[...]
    head_mean = _head_block(D, N, 1.0 / N)
    mean = _dot_stat_rhs(o, head_mean)
    d = o - mean
    var = _dot_stat_rhs(d * d, head_mean)
    o = d * lax.rsqrt(var + RWKV_GN_EPS) * lnw_ref[...] + lnb_ref[...]
    bonus = _dot_stat_rhs(r * k * rk_ref[...], head_ones) * v
    o_ref[...] = ((o + bonus) * g).astype(o_ref.dtype)


def _rwkv_call(p_rwkv, mu, w0, w2, a0, a2, g2, k_k, k_a, r_k, ln_w, ln_b, tb=SCAN_TILE):
    S = p_rwkv.shape[0]
    D = RWKV_D
    row = lambda t: t.reshape(1, -1).astype(F32)
    full = lambda shape: pl.BlockSpec(shape, lambda i: tuple(0 for _ in shape))
    return pl.pallas_call(
        _rwkv_kernel,
        grid=(S // tb,),
        in_specs=[pl.BlockSpec((tb, RWKV_COLS), lambda i: (i, 0)),
                  full((1, RWKV_COLS)), full((1, D)), full((64, D)), full((1, D)), full((64, D)), full((128, D)),
                  full((1, D)), full((1, D)), full((1, D)), full((1, D)), full((1, D))],
        out_specs=pl.BlockSpec((tb, D), lambda i: (i, 0)),
        out_shape=jax.ShapeDtypeStruct((S, D), BF16),
        scratch_shapes=[pltpu.VMEM((1, RWKV_COLS), F32),
                        pltpu.VMEM((RWKV_HEADS, RWKV_HD, RWKV_HD), F32),
                        pltpu.VMEM((tb, D), F32)],
        compiler_params=_cparams(("arbitrary",)),
        name="rwkv7",
    )(p_rwkv, row(mu), row(w0), w2.astype(BF16), row(a0), a2.astype(BF16), g2.astype(BF16),
      row(k_k), row(k_a), row(r_k), row(ln_w), row(ln_b))


XATTN_HEADS, XATTN_HD = 4, 256


def _xattn_kernel(x_ref, wq_ref, k_ref, v_ref, wo_ref, g_ref, b_ref, o_ref):
    x = x_ref[...]
    q = _dot(x, wq_ref[...])
    outs = []
    for h in range(XATTN_HEADS):
        hs = slice(h * XATTN_HD, (h + 1) * XATTN_HD)
        s = _dot_nt(q[:, hs], k_ref[:, hs])
        e = jnp.exp(s - jnp.max(s, -1, keepdims=True))
        p = e / jnp.sum(e, -1, keepdims=True)
        outs.append(_dot(p, v_ref[:, hs]))
    o = jnp.concatenate(outs, axis=1)
    y = DN_ALPHA * x + _dot(o, wo_ref[...])
    o_ref[...] = _layer_norm_rows(y, g_ref[...], b_ref[...])


def _xattn_call(x, wq_scaled, k_mem, v_mem, wo, g, b, tm):
    S, D = x.shape
    M = k_mem.shape[0]
    const = lambda shape: pl.BlockSpec(shape, lambda i: (0, 0))
    return pl.pallas_call(
        _xattn_kernel,
        grid=(S // tm,),
        in_specs=[pl.BlockSpec((tm, D), lambda i: (i, 0)), const((D, D)), const((M, D)), const((M, D)),
                  const((D, D)), const((1, D)), const((1, D))],
        out_specs=pl.BlockSpec((tm, D), lambda i: (i, 0)),
        out_shape=jax.ShapeDtypeStruct((S, D), F32),
        compiler_params=_cparams(("parallel",)),
        name="xattn",
    )(x, wq_scaled, k_mem, v_mem, wo, g.reshape(1, D), b.reshape(1, D))


def _ffn_kernel(x_ref, wg_ref, wu_ref, wd_ref, g_ref, b_ref, o_ref, acc_ref):
    j = pl.program_id(1)
    x = x_ref[...]
    xb = x.astype(BF16)
    part = _dot(_silu(_dot(xb, wg_ref[...])) * _dot(xb, wu_ref[...]), wd_ref[...])

    last = pl.num_programs(1) - 1

    @pl.when(j == 0)
    def _():
        acc_ref[...] = DN_ALPHA * x + part

    @pl.when((j > 0) & (j < last))
    def _():
        acc_ref[...] += part

    @pl.when(j == last)
    def _():
        o_ref[...] = _layer_norm_rows(acc_ref[...] + part, g_ref[...], b_ref[...])


def _ffn_call(x, wg, wu, wd, g, b, tm, tf):
    S, D = x.shape
    F = wg.shape[1]
    assert F % tf == 0 and F // tf >= 2
    return pl.pallas_call(
        _ffn_kernel,
        grid=(S // tm, F // tf),
        in_specs=[pl.BlockSpec((tm, D), lambda i, j: (i, 0)),
                  pl.BlockSpec((D, tf), lambda i, j: (0, j)),
                  pl.BlockSpec((D, tf), lambda i, j: (0, j)),
                  pl.BlockSpec((tf, D), lambda i, j: (j, 0)),
                  pl.BlockSpec((1, D), lambda i, j: (0, 0)),
                  pl.BlockSpec((1, D), lambda i, j: (0, 0))],
        out_specs=pl.BlockSpec((tm, D), lambda i, j: (i, 0)),
        out_shape=jax.ShapeDtypeStruct((S, D), F32),
        scratch_shapes=[pltpu.VMEM((tm, D), F32)],
        compiler_params=_cparams(("parallel", "arbitrary")),
        name="ffn",
    )(x, wg, wu, wd, g.reshape(1, D), b.reshape(1, D))


SSD_HD, SSD_HEADS, SSD_INNER, SSD_GROUPS, SSD_STATE = 64, 16, 1024, 2, 128
SSD_BC, SSD_CONV, SSD_CONV_CH, SSD_CHUNK = 256, 4, 1536, 128
SSD_COLS = SSD_INNER + SSD_CONV_CH
SSD_GW = SSD_INNER // SSD_GROUPS


def _softplus(x):
    return jnp.maximum(x, 0.0) + jnp.log(1.0 + jnp.exp(-jnp.abs(x)))


def _ssd_kernel(p_ref, cw_ref, cb_ref, dtb_ref, a_ref, dsk_ref, ng_ref, o_ref, prev_ref, st_ref, y_scr):
    L, G, NS, HD = SSD_CHUNK, SSD_GROUPS, SSD_STATE, SSD_HD
    HG = SSD_HEADS // G

    @pl.when(pl.program_id(0) == 0)
    def _():
        prev_ref[...] = jnp.zeros_like(prev_ref)
        st_ref[...] = jnp.zeros_like(st_ref)

    cur = p_ref[:, SSD_INNER:SSD_COLS]
    prev = prev_ref[...]
    row = _iota2(cur.shape, 0)
    conv = cur * cw_ref[SSD_CONV - 1:SSD_CONV, :] + cb_ref[...]
    for kk in range(1, SSD_CONV):
        shifted = jnp.where(row < kk, pltpu.roll(prev, kk, 0), pltpu.roll(cur, kk, 0))
        conv = conv + shifted * cw_ref[SSD_CONV - 1 - kk:SSD_CONV - kk, :]
    prev_ref[...] = cur
    xbc = _silu(conv)
    xs = xbc[:, :SSD_INNER]

    dt = _softplus(p_ref[:, SSD_COLS:] + dtb_ref[...])
    a_col = dt * a_ref[...]
    li = _iota2((L, L), 0)
    lj = _iota2((L, L), 1)
    cs = _dot_exact_lhs(jnp.where(lj <= li, 1.0, 0.0), a_col)
    cs_row = cs.T
    expand = jnp.where(_iota2((128, SSD_INNER), 1) // HD == _iota2((128, SSD_INNER), 0), 1.0, 0.0)
    dt_x = _dot_stat_rhs(dt, expand)
    cs_x = _dot_exact_rhs(cs, expand)
    cs_end = cs_x[L - 1:L]
    xd = xs * dt_x
    xd_dec = xd * jnp.exp(cs_end - cs_x)
    out_dec = jnp.exp(cs_x)
    chunk_dec = jnp.exp(cs_end)
    tril = lj <= li

    for g in range(G):
        gs = slice(g * SSD_GW, (g + 1) * SSD_GW)
        b_g = xbc[:, SSD_INNER + g * NS:SSD_INNER + (g + 1) * NS]
        c_g = xbc[:, SSD_INNER + SSD_BC + g * NS:SSD_INNER + SSD_BC + (g + 1) * NS]
        cb = _dot_nt(c_g, b_g)
        for j in range(HG):
            h = g * HG + j
            hs = slice(h * HD, (h + 1) * HD)
            seg = jnp.where(tril, jnp.exp(cs[:, h:h + 1] - cs_row[h:h + 1, :]), 0.0)
            y_scr[:, hs] = _dot(cb * seg, xd[:, hs])
        st = st_ref[g]
        y_off = _dot(c_g, st) * out_dec[:, gs]
        st_ref[g] = st * chunk_dec[:, gs] + _dot_tn(b_g, xd_dec[:, gs])
        y_scr[:, gs] = y_scr[:, gs] + y_off

    y = (y_scr[...] + dsk_ref[...] * xs) * _silu(p_ref[:, :SSD_INNER])
    for g in range(G):
        gs = slice(g * SSD_GW, (g + 1) * SSD_GW)
        y_g = y[:, gs]
        o_ref[:, gs] = (y_g * lax.rsqrt(jnp.mean(y_g * y_g, -1, keepdims=True) + 1e-5) * ng_ref[:, gs]).astype(o_ref.dtype)


def _ssd_call(p_ssd, conv_w, conv_b, dt_bias, a_log, d_skip, norm_g):
    S = p_ssd.shape[0]
    L = SSD_CHUNK
    a_neg = -jnp.exp(a_log.astype(F32))
    pad = lambda t: jnp.pad(t.astype(F32), (0, 128 - SSD_HEADS)).reshape(1, 128)
    const = lambda shape: pl.BlockSpec(shape, lambda i: (0, 0))
    return pl.pallas_call(
        _ssd_kernel,
        grid=(S // L,),
        in_specs=[pl.BlockSpec((L, SSD_COLS + 128), lambda i: (i, 0)),
                  const((SSD_CONV, SSD_CONV_CH)), const((1, SSD_CONV_CH)),
                  const((1, 128)), const((1, 128)),
                  const((1, SSD_INNER)), const((1, SSD_INNER))],
        out_specs=pl.BlockSpec((L, SSD_INNER), lambda i: (i, 0)),
        out_shape=jax.ShapeDtypeStruct((S, SSD_INNER), BF16),
        scratch_shapes=[pltpu.VMEM((L, SSD_CONV_CH), F32),
                        pltpu.VMEM((SSD_GROUPS, SSD_STATE, SSD_GW), F32),
                        pltpu.VMEM((L, SSD_INNER), F32)],
        compiler_params=_cparams(("arbitrary",)),
        name="ssd",
    )(p_ssd, conv_w.astype(F32), conv_b.reshape(1, -1).astype(F32), pad(dt_bias), pad(a_neg),
      jnp.repeat(d_skip.astype(F32), SSD_HD).reshape(1, -1), norm_g.reshape(1, -1).astype(F32))


MOBA_HD, MOBA_HEADS, MOBA_D, MOBA_BLOCK, MOBA_TOPK = 64, 8, 512, 256, 3
MOBA_GROUP = 4
MOBA_UNDERFLOW = -160.0
MOBA_BOUND_SLACK = 1.001
NEG_BIG = -1e30
LOG2E = math.log2(math.e)


MOBA_VROWS = MOBA_HD + 16


def _moba_proj_kernel(x_ref, wqt_ref, wk_ref, wvt_ref, qt_ref, k_ref, kmean_ref, v_ref, knorm_ref, qnorm_ref):
    xb = x_ref[...].astype(BF16)
    qt = _dot_nt(wqt_ref[...], xb).astype(qt_ref.dtype)
    qt_ref[...] = qt
    vt = _dot_nt(wvt_ref[...], xb)
    extra = jnp.where(_iota2((MOBA_VROWS - MOBA_HD, MOBA_BLOCK), 0) == 0, 1.0, 0.0)
    for h in range(MOBA_HEADS):
        v_ref[h, 0] = jnp.concatenate([vt[h * MOBA_HD:(h + 1) * MOBA_HD], extra], axis=0).astype(v_ref.dtype)
    k = _dot(xb, wk_ref[...])
    kmean_ref[0] = jnp.mean(k, 0, keepdims=True)
    shape = (MOBA_BLOCK, 2 * MOBA_HD)
    lane = _iota2(shape, 1)
    pos = jnp.where((lane == MOBA_HD) | (lane == MOBA_HD + 1), _iota2(shape, 0).astype(F32), 0.0).astype(k_ref.dtype)
    k_b = k.astype(k_ref.dtype)
    for h in range(MOBA_HEADS):
        k_ref[h, 0] = pos
        k_ref[h, 0, :, 0:MOBA_HD] = k_b[:, h * MOBA_HD:(h + 1) * MOBA_HD]
    k_f = k_b.astype(F32)
    q_f = qt.astype(F32)
    head_cols = jnp.where(_iota2((MOBA_D, 128), 0) // MOBA_HD == _iota2((MOBA_D, 128), 1), 1.0, 0.0)
    head_rows = jnp.where(_iota2((128, MOBA_D), 1) // MOBA_HD == _iota2((128, MOBA_D), 0), 1.0, 0.0)
    knorm_ref[0] = jnp.sqrt(jnp.max(_dot_stat_rhs(k_f * k_f, head_cols), 0, keepdims=True))
    qnorm_ref[0] = jnp.sqrt(jnp.max(_dot_exact_lhs(head_rows, q_f * q_f), 1, keepdims=True))


def _moba_proj_call(x, wqt, wk, wvt):
    S, D = x.shape
    nb = S // MOBA_BLOCK
    const = lambda shape: pl.BlockSpec(shape, lambda i: (0, 0))
    return pl.pallas_call(
        _moba_proj_kernel,
        grid=(nb,),
        in_specs=[pl.BlockSpec((MOBA_BLOCK, D), lambda i: (i, 0)), const((MOBA_D, D)), const((D, MOBA_D)),
                  const((MOBA_D, D))],
        out_specs=[pl.BlockSpec((MOBA_D, MOBA_BLOCK), lambda i: (0, i)),
                   pl.BlockSpec((MOBA_HEADS, 1, MOBA_BLOCK, 2 * MOBA_HD), lambda i: (0, i, 0, 0)),
                   pl.BlockSpec((1, 1, MOBA_D), lambda i: (i, 0, 0)),
                   pl.BlockSpec((MOBA_HEADS, 1, MOBA_VROWS, MOBA_BLOCK), lambda i: (0, i, 0, 0)),
                   pl.BlockSpec((1, 1, 128), lambda i: (i, 0, 0)),
                   pl.BlockSpec((1, 128, 1), lambda i: (i, 0, 0))],
        out_shape=[jax.ShapeDtypeStruct((MOBA_D, S), BF16),
                   jax.ShapeDtypeStruct((MOBA_HEADS, nb, MOBA_BLOCK, 2 * MOBA_HD), BF16),
                   jax.ShapeDtypeStruct((nb, 1, MOBA_D), F32),
                   jax.ShapeDtypeStruct((MOBA_HEADS, nb, MOBA_VROWS, MOBA_BLOCK), BF16),
                   jax.ShapeDtypeStruct((nb, 1, 128), F32),
                   jax.ShapeDtypeStruct((nb, 128, 1), F32)],
        compiler_params=_cparams(("parallel",)),
        name="moba_proj",
    )(x, wqt, wk, wvt)


def _moba_kernel(first_ref, qt_ref, k_ref, vt_ref, kmean_ref, o_ref, sel_ref, s0_ref, s1_ref, p0_ref, p1_ref):
    BS, HD = MOBA_BLOCK, MOBA_HD
    h = pl.program_id(0)
    i = pl.program_id(1)
    nb = k_ref.shape[0]
    qt = qt_ref[...]
    slope = LOG2E * jnp.exp2(jnp.zeros((1, BS), F32) - (h + 1).astype(F32) * (8.0 / MOBA_HEADS))
    slope_hi = slope.astype(qt.dtype).astype(F32)
    slope_lo = slope - slope_hi
    row = _iota2((HD, BS), 0)
    qt_ext = jnp.concatenate(
        [qt, jnp.where(row == 0, slope_hi, jnp.where(row == 1, slope_lo, 0.0)).astype(qt.dtype)], axis=0)

    gate = _dot(kmean_ref[...], qt)
    blk = _iota2((nb, BS), 0).astype(F32)
    cand = blk < i.astype(F32)
    sel = jnp.zeros((nb, BS), F32)
    for _ in range(MOBA_TOPK):
        best = jnp.max(jnp.where(cand, gate, -jnp.inf), 0, keepdims=True)
        idx = jnp.min(jnp.where(cand & (gate == best), blk, float(nb)), 0, keepdims=True)
        pick = blk == idx
        sel = jnp.where(pick, 1.0, sel)
        cand = cand & jnp.logical_not(pick)
    sel_ref[...] = sel

    G = MOBA_GROUP
    last = nb - 1
    s_ref = (s0_ref, s1_ref)
    p_ref = (p0_ref, p1_ref)

    first = first_ref[h * nb + i]

    def group(u):
        return [jnp.clip(first + G * u + x, 0, last) for x in range(G)]

    def issue_scores(u, slot):
        tops = []
        for x, j in enumerate(group(u)):
            sc = _dot(k_ref[j], qt_ext)
            s_ref[slot][x] = sc
            tops.append(jnp.max(sc, 0, keepdims=True))
        return tuple(tops)

    def value_blocks(u):
        js = group(u)
        js[0] = jnp.where(u == -1, i, js[0])
        return js

    def weighted_values(u, slot):
        out = None
        for x, j in enumerate(value_blocks(u)):
            part = _dot(vt_ref[j], p_ref[slot][x])
            out = part if out is None else out + part
        return out

    def step(u, slot, tops, a_prev, m, acc, issue_next=True):
        on, shift = [], []
        m_new = m
        for x, j in enumerate(group(u)):
            on.append(sel_ref[pl.ds(j, 1), :] > 0.0)
            shift.append(slope * ((j - i) * BS).astype(F32))
            m_new = jnp.maximum(m_new, jnp.where(on[x], tops[x] + shift[x], NEG_BIG))
        tops_next, pv = [], None
        for x, (j_prev, j_next) in enumerate(zip(value_blocks(u - 1), group(u + 1))):
            part = _dot(vt_ref[j_prev], p_ref[1 - slot][x])
            pv = part if pv is None else pv + part
            if issue_next:
                sc = _dot(k_ref[j_next], qt_ext)
                s_ref[1 - slot][x] = sc
                tops_next.append(jnp.max(sc, 0, keepdims=True))
            p = jnp.exp2(s_ref[slot][x] - (jnp.where(on[x], m_new, -NEG_BIG) - shift[x]))
            p_ref[slot][x] = p.astype(BF16)
        return tuple(tops_next), jnp.exp2(m - m_new), m_new, a_prev * acc + pv

    def body(w, carry):
        carry = step(2 * w, 0, *carry)
        return step(2 * w + 1, 1, *carry)

    tops0 = issue_scores(0, 0)
    s_own = jnp.where(_iota2((BS, BS), 1) >= _iota2((BS, BS), 0), _dot(k_ref[i], qt_ext), NEG_BIG)
    m_own = jnp.max(s_own, 0, keepdims=True)
    p1_ref[1:G] = jnp.zeros((G - 1, BS, BS), BF16)
    p1_ref[0] = jnp.exp2(s_own - m_own).astype(BF16)
    init = (tops0, jnp.ones((1, BS), F32), m_own, jnp.zeros((vt_ref.shape[1], BS), F32))
    steps = (i - first + G - 1) // G
    pairs = steps // 2
    carry = lax.fori_loop(0, pairs, body, init)

    def odd_tail(carry):
        _, a_last, _, acc = step(2 * pairs, 0, *carry, issue_next=False)
        return a_last * acc + weighted_values(2 * pairs, 0)

    def even_tail(carry):
        _, a_prev, _, acc = carry
        return a_prev * acc + weighted_values(2 * pairs - 1, 1)

    acc = lax.cond(steps % 2 == 1, odd_tail, even_tail, carry)
    o_ref[...] = (acc[0:HD] / acc[HD:HD + 1]).astype(o_ref.dtype)


def _moba_first_block(knorm, qnorm, nb):
    BS = MOBA_BLOCK
    k_norm = knorm[:, 0, 0:MOBA_HEADS].T
    q_norm = qnorm[:, 0:MOBA_HEADS, 0].T
    slope = LOG2E * jnp.exp2(-(jnp.arange(MOBA_HEADS, dtype=F32) + 1.0) * (8.0 / MOBA_HEADS))
    i_idx = jnp.arange(nb, dtype=F32)[None, :, None]
    j_idx = jnp.arange(nb, dtype=F32)[None, None, :]
    reach = slope[:, None, None] * ((BS - 1.0) - BS * (i_idx - j_idx))
    bound = q_norm[:, :, None] * (k_norm[:, None, :] + k_norm[:, :, None]) * MOBA_BOUND_SLACK + reach
    matters = (bound >= MOBA_UNDERFLOW) & (j_idx < i_idx)
    first = jnp.min(jnp.where(matters, j_idx, i_idx), axis=2).astype(jnp.int32)
    return first.reshape(-1)


def _moba_call(first_block, qt, k4, vt4, kmean):
    S = qt.shape[1]
    nb = S // MOBA_BLOCK
    return pl.pallas_call(
        _moba_kernel,
        grid_spec=pltpu.PrefetchScalarGridSpec(
            num_scalar_prefetch=1,
            grid=(MOBA_HEADS, nb),
            in_specs=[pl.BlockSpec((MOBA_HD, MOBA_BLOCK), lambda h, i, first: (h, i)),
                      pl.BlockSpec((None, nb, MOBA_BLOCK, 2 * MOBA_HD), lambda h, i, first: (h, 0, 0, 0)),
                      pl.BlockSpec((None, nb, MOBA_VROWS, MOBA_BLOCK), lambda h, i, first: (h, 0, 0, 0)),
                      pl.BlockSpec((None, nb, MOBA_HD), lambda h, i, first: (h, 0, 0))],
            out_specs=pl.BlockSpec((MOBA_HD, MOBA_BLOCK), lambda h, i, first: (h, i)),
            scratch_shapes=[pltpu.VMEM((nb, MOBA_BLOCK), F32)]
            + [pltpu.VMEM((MOBA_GROUP, MOBA_BLOCK, MOBA_BLOCK), F32)] * 2
            + [pltpu.VMEM((MOBA_GROUP, MOBA_BLOCK, MOBA_BLOCK), BF16)] * 2),
        out_shape=jax.ShapeDtypeStruct((MOBA_D, S), BF16),
        compiler_params=_cparams(("parallel", "arbitrary")),
        name="moba",
    )(first_block, qt, k4, vt4, kmean)


N_EXPERTS, TOP_K, EXPERT_FF = 8, 2, 2816
MOE_ROWS = 512


def _router_kernel(x_ref, whi_ref, wlo_ref, b_ref, o_ref):
    x = x_ref[...]
    xhi = x.astype(BF16)
    xlo = (x - xhi.astype(F32)).astype(BF16)
    logits = (jnp.dot(xhi, whi_ref[...], preferred_element_type=F32)
              + jnp.dot(xhi, wlo_ref[...], preferred_element_type=F32)
              + jnp.dot(xlo, whi_ref[...], preferred_element_type=F32)) + b_ref[...]
    lane = _iota2(logits.shape, 1).astype(F32)
    logits = jnp.where(lane < N_EXPERTS, logits, -jnp.inf)
    m1 = jnp.max(logits, -1, keepdims=True)
    i1 = jnp.min(jnp.where(logits == m1, lane, 128.0), -1, keepdims=True)
    rest = jnp.where(lane == i1, -jnp.inf, logits)
    m2 = jnp.max(rest, -1, keepdims=True)
    i2 = jnp.min(jnp.where(rest == m2, lane, 128.0), -1, keepdims=True)
    e = jnp.exp(m2 - m1)
    g1 = 1.0 / (1.0 + e)
    g2 = e / (1.0 + e)
    out = jnp.where(lane == 0, i1, 0.0)
    out = jnp.where(lane == 1, i2, out)
    out = jnp.where(lane == 2, g1, out)
    out = jnp.where(lane == 3, g2, out)
    o_ref[...] = out


def _router_call(x, w_router, b_router, tm):
    S, D = x.shape
    wp = jnp.pad(w_router.astype(F32), ((0, 0), (0, 128 - N_EXPERTS)))
    whi = wp.astype(BF16)
    wlo = (wp - whi.astype(F32)).astype(BF16)
    bp = jnp.pad(b_router.astype(F32), (0, 128 - N_EXPERTS)).reshape(1, 128)
    const = lambda shape: pl.BlockSpec(shape, lambda i: (0, 0))
    return pl.pallas_call(
        _router_kernel,
        grid=(S // tm,),
        in_specs=[pl.BlockSpec((tm, D), lambda i: (i, 0)), const((D, 128)), const((D, 128)), const((1, 128))],
        out_specs=pl.BlockSpec((tm, 128), lambda i: (i, 0)),
        out_shape=jax.ShapeDtypeStruct((S, 128), F32),
        compiler_params=_cparams(("parallel",)),
        name="router",
    )(x, whi, wlo, bp)


def _row_copy(src_hbm, dst_ref, src_row, dst_row, sem):
    return pltpu.make_async_copy(src_hbm.at[pl.ds(src_row, 1)], dst_ref.at[pl.ds(dst_row, 1)], sem)


def _scatter_rows_kernel(dest_ref, lo_ref, hi_ref, x_ref, o_hbm, zero_ref, sem):
    step = pl.program_id(0)
    n_tiles = pl.num_programs(0) - 1
    tm = x_ref.shape[0]

    def row_out(src_ref, src_row, dst_row):
        return pltpu.make_async_copy(src_ref.at[pl.ds(src_row, 1)], o_hbm.at[pl.ds(dst_row, 1)], sem)

    @pl.when(step < n_tiles)
    def _():
        base = step * tm

        def start(r, c):
            for slot in range(TOP_K):
                row_out(x_ref, r, dest_ref[TOP_K * (base + r) + slot]).start()
            return c

        lax.fori_loop(0, tm, start, 0, unroll=8)
        for slot in range(TOP_K):
            pltpu.make_async_copy(x_ref, o_hbm.at[pl.ds(0, tm)], sem).wait()

    @pl.when(step == n_tiles)
    def _():
        zero_ref[...] = jnp.zeros_like(zero_ref)
        for e in range(N_EXPERTS):
            def start(r, c):
                row_out(zero_ref, 0, r).start()
                return c

            def wait(r, c):
                row_out(zero_ref, 0, r).wait()
                return c

            lax.fori_loop(lo_ref[e], hi_ref[e], start, 0)
            lax.fori_loop(lo_ref[e], hi_ref[e], wait, 0)


def _scatter_rows(x, dest, pad_lo, pad_hi, n_rows, tm):
    T, D = x.shape
    n_tiles = T // tm
    return pl.pallas_call(
        _scatter_rows_kernel,
        grid_spec=pltpu.PrefetchScalarGridSpec(
            num_scalar_prefetch=3,
            grid=(n_tiles + 1,),
            in_specs=[pl.BlockSpec((tm, D), lambda i, d, lo, hi: (jnp.minimum(i, n_tiles - 1), 0))],
            out_specs=pl.BlockSpec(memory_space=pl.ANY),
            scratch_shapes=[pltpu.VMEM((8, D), x.dtype), pltpu.SemaphoreType.DMA(())]),
        out_shape=jax.ShapeDtypeStruct((n_rows, D), x.dtype),
        compiler_params=_cparams(("arbitrary",)),
        name="moe_scatter",
    )(dest, pad_lo, pad_hi, x)


def _moe_ffn_kernel(be_ref, nu_ref, x_ref, wg_ref, wu_ref, wd_ref, o_ref, acc_ref):
    i = pl.program_id(0)
    j = pl.program_id(1)

    last = pl.num_programs(1) - 1

    @pl.when(i < nu_ref[0])
    def _():
        xb = x_ref[...].astype(BF16)
        part = _dot(_silu(_dot(xb, wg_ref[...])) * _dot(xb, wu_ref[...]), wd_ref[...])

        @pl.when(j == 0)
        def _():
            acc_ref[...] = part

        @pl.when((j > 0) & (j < last))
        def _():
            acc_ref[...] += part

        @pl.when(j == last)
        def _():
            o_ref[...] = acc_ref[...] + part

    @pl.when((j == last) & (i >= nu_ref[0]))
    def _():
        o_ref[...] = jnp.zeros_like(o_ref)


def _moe_ffn_call(x_rows, block_e, n_used, wg, wu, wd, tf):
    n, D = x_rows.shape
    F = wg.shape[2]
    R = MOE_ROWS
    assert F % tf == 0 and F // tf >= 2
    return pl.pallas_call(
        _moe_ffn_kernel,
        grid_spec=pltpu.PrefetchScalarGridSpec(
            num_scalar_prefetch=2,
            grid=(n // R, F // tf),
            in_specs=[pl.BlockSpec((R, D), lambda i, j, be, nu: (i, 0)),
                      pl.BlockSpec((None, D, tf), lambda i, j, be, nu: (be[i], 0, j)),
                      pl.BlockSpec((None, D, tf), lambda i, j, be, nu: (be[i], 0, j)),
                      pl.BlockSpec((None, tf, D), lambda i, j, be, nu: (be[i], j, 0))],
            out_specs=pl.BlockSpec((R, D), lambda i, j, be, nu: (i, 0)),
            scratch_shapes=[pltpu.VMEM((R, D), F32)]),
        out_shape=jax.ShapeDtypeStruct((n, D), F32),
        compiler_params=_cparams(("arbitrary", "arbitrary")),
        name="moe_ffn",
    )(block_e, n_used, x_rows, wg, wu, wd)


def _moe_combine_kernel(d_ref, y_hbm, x_ref, r_ref, g_ref, b_ref, o_ref, y1_ref, y2_ref, sem):
    tm = x_ref.shape[0]
    base = pl.program_id(0) * tm

    def start(r, c):
        _row_copy(y_hbm, y1_ref, d_ref[2 * (base + r)], r, sem.at[0]).start()
        _row_copy(y_hbm, y2_ref, d_ref[2 * (base + r) + 1], r, sem.at[1]).start()
        return c

    lax.fori_loop(0, tm, start, 0, unroll=8)
    pltpu.make_async_copy(y_hbm.at[pl.ds(0, tm)], y1_ref, sem.at[0]).wait()
    pltpu.make_async_copy(y_hbm.at[pl.ds(0, tm)], y2_ref, sem.at[1]).wait()
    y = r_ref[:, 2:3] * y1_ref[...] + r_ref[:, 3:4] * y2_ref[...]
    o_ref[...] = _layer_norm_rows(DN_ALPHA * x_ref[...] + y, g_ref[...], b_ref[...])


def _moe_combine_call(dest, y_rows, x, routed, g, b, tm):
    S, D = x.shape
    return pl.pallas_call(
        _moe_combine_kernel,
        grid_spec=pltpu.PrefetchScalarGridSpec(
            num_scalar_prefetch=1,
            grid=(S // tm,),
            in_specs=[pl.BlockSpec(memory_space=pl.ANY),
                      pl.BlockSpec((tm, D), lambda i, d: (i, 0)),
                      pl.BlockSpec((tm, 128), lambda i, d: (i, 0)),
                      pl.BlockSpec((1, D), lambda i, d: (0, 0)),
                      pl.BlockSpec((1, D), lambda i, d: (0, 0))],
            out_specs=pl.BlockSpec((tm, D), lambda i, d: (i, 0)),
            scratch_shapes=[pltpu.VMEM((tm, D), F32), pltpu.VMEM((tm, D), F32), pltpu.SemaphoreType.DMA((2,))]),
        out_shape=jax.ShapeDtypeStruct((S, D), F32),
        compiler_params=_cparams(("arbitrary",)),
        name="moe_combine",
    )(dest, y_rows, x, routed, g.reshape(1, D), b.reshape(1, D))


def _moe_sublayer(x, w_router, b_router, wg, wu, wd, g, b):
    T = x.shape[0]
    R = MOE_ROWS
    routed = _router_call(x, w_router, b_router, min(ROW_TILE, T))
    top_e = routed[:, 0:TOP_K].astype(jnp.int32)
    tok_oh = jnp.sum((top_e[:, :, None] == jnp.arange(N_EXPERTS)[None, None, :]).astype(jnp.int32), axis=1)
    counts = jnp.sum(tok_oh, axis=0)
    rank = jnp.cumsum(tok_oh, axis=0) - tok_oh
    padded = (counts + R - 1) // R * R
    pend = jnp.cumsum(padded)
    pstart = pend - padded
    dest = pstart[top_e] + jnp.take_along_axis(rank, top_e, axis=1)
    n_rows = (T * TOP_K + N_EXPERTS * (R - 1)) // R * R
    n_blocks = n_rows // R
    block_first_row = jnp.arange(n_blocks, dtype=jnp.int32) * R
    block_e = jnp.minimum(jnp.sum((pend[None, :] <= block_first_row[:, None]).astype(jnp.int32), axis=1),
                          N_EXPERTS - 1)
    n_used = (pend[-1] // R).astype(jnp.int32).reshape(1)
    dest = dest.reshape(-1).astype(jnp.int32)
    pad_lo = (pstart + counts).astype(jnp.int32)
    pad_hi = jnp.concatenate([pstart[1:], jnp.array([n_rows])]).astype(jnp.int32)
    x_rows = _scatter_rows(x, dest, pad_lo, pad_hi, n_rows, min(ROW_TILE, T))
    y_rows = _moe_ffn_call(x_rows, block_e, n_used, wg, wu, wd, EXPERT_FF // 2)
    return _moe_combine_call(dest, y_rows, x, routed, g, b, min(COMBINE_TILE, T))


GLA_IN = 2 * GLA_QK + 2 * GLA_V + GLA_LR


def _gla_rwkv_sublayer(x, w_in, gla_wa2, gla_ba, gla_norm, mu, w0, w2, a0, a2, g2, k_k, k_a, r_k, ln_w, ln_b,
                       w_out, ln_g, ln_bias):
    S = x.shape[0]
    tm = min(ROW_TILE, S)
    lr_pad = 128 - GLA_LR
    w_gla = jnp.pad(w_in[:, :GLA_IN], ((0, 0), (0, lr_pad))).astype(BF16)
    w_rwkv = w_in[:, GLA_IN:].astype(BF16)
    p_gla = _matmul(x, w_gla, F32, tm, GLA_COLS)
    p_rwkv = _matmul(x, w_rwkv, F32, tm, RWKV_COLS)
    wa2p = jnp.pad(gla_wa2, ((0, lr_pad), (0, 0))).astype(BF16)
    o_gla = _gla_call(p_gla, wa2p, gla_ba, gla_norm)
    o_rwkv = _rwkv_call(p_rwkv, mu, w0, w2, a0, a2, g2, k_k, k_a, r_k, ln_w, ln_b)
    return _matmul_ln([o_gla, o_rwkv], [w_out[:GLA_V].astype(BF16), w_out[GLA_V:].astype(BF16)], x,
                      ln_g, ln_bias, tm)


def _xattn_sublayer(x, mem, wq, wk, wv, wo, ln_g, ln_bias):
    M = mem.shape[0]
    k_mem = _matmul(mem, wk.astype(BF16), BF16, M, D_MODEL)
    v_mem = _matmul(mem, wv.astype(BF16), BF16, M, D_MODEL)
    wq_scaled = (wq * XATTN_HD ** -0.5).astype(BF16)
    return _xattn_call(x, wq_scaled, k_mem, v_mem, wo.astype(BF16), ln_g, ln_bias, min(ROW_TILE, x.shape[0]))


def _ssd_moba_sublayer(x, w_in, conv_w, conv_b, dt_bias, a_log, d_skip, ssd_norm, w_out, ln_g, ln_bias):
    S = x.shape[0]
    tm = min(ROW_TILE, S)
    nb = S // MOBA_BLOCK
    o_dt = SSD_COLS
    o_q = o_dt + SSD_HEADS
    w_ssd = jnp.pad(w_in[:, :o_q], ((0, 0), (0, 128 - SSD_HEADS))).astype(BF16)
    w_q = w_in[:, o_q:o_q + MOBA_D]
    w_k = w_in[:, o_q + MOBA_D:o_q + 2 * MOBA_D]
    w_v = w_in[:, o_q + 2 * MOBA_D:]
    p_ssd = _matmul(x, w_ssd, F32, tm, SSD_COLS + 128)
    o_ssd = _ssd_call(p_ssd, conv_w, conv_b, dt_bias, a_log, d_skip, ssd_norm)
    qt, k4, kmean, vt4, knorm, qnorm = _moba_proj_call(x, (w_q.T * (MOBA_HD ** -0.5 * LOG2E)).astype(BF16), w_k.astype(BF16),
                                         w_v.T.astype(BF16))
    kmean_h = kmean.reshape(nb, MOBA_HEADS, MOBA_HD).transpose(1, 0, 2)
    ot_moba = _moba_call(_moba_first_block(knorm, qnorm, nb), qt, k4, vt4, kmean_h)
    return _matmul_ln([o_ssd, ot_moba], [w_out[:SSD_INNER].astype(BF16), w_out[SSD_INNER:].astype(BF16)], x,
                      ln_g, ln_bias, tm, transposed=(False, True))


def kernel(x, mem, l0_w_in, l0_gla_wa2, l0_gla_ba, l0_gla_norm, l0_rwkv_mu, l0_rwkv_w0, l0_rwkv_w2, l0_rwkv_a0, l0_rwkv_a2, l0_rwkv_g2, l0_rwkv_kk, l0_rwkv_ka, l0_rwkv_rk, l0_rwkv_lnw, l0_rwkv_lnb, l0_w_out, l0_ln1_g, l0_ln1_b, l0_xq, l0_xk, l0_xv, l0_xo, l0_ln2_g, l0_ln2_b, l0_ffn_wg, l0_ffn_wu, l0_ffn_wd, l0_ln3_g, l0_ln3_b, l1_w_in, l1_conv_w, l1_conv_b, l1_dt_bias, l1_a_log, l1_d_skip, l1_ssd_norm, l1_w_out, l1_ln1_g, l1_ln1_b, l1_xq, l1_xk, l1_xv, l1_xo, l1_ln2_g, l1_ln2_b, l1_router, l1_router_b, l1_exp_wg, l1_exp_wu, l1_exp_wd, l1_ln3_g, l1_ln3_b):
    x2 = x.reshape(-1, D_MODEL)
    mem2 = mem.reshape(-1, D_MODEL)
    x2 = _gla_rwkv_sublayer(x2, l0_w_in, l0_gla_wa2, l0_gla_ba, l0_gla_norm, l0_rwkv_mu, l0_rwkv_w0, l0_rwkv_w2,
                            l0_rwkv_a0, l0_rwkv_a2, l0_rwkv_g2, l0_rwkv_kk, l0_rwkv_ka, l0_rwkv_rk, l0_rwkv_lnw,
                            l0_rwkv_lnb, l0_w_out, l0_ln1_g, l0_ln1_b)
    x2 = _xattn_sublayer(x2, mem2, l0_xq, l0_xk, l0_xv, l0_xo, l0_ln2_g, l0_ln2_b)
    tm = min(ROW_TILE, x2.shape[0])
    x2 = _ffn_call(x2, l0_ffn_wg.astype(BF16), l0_ffn_wu.astype(BF16), l0_ffn_wd.astype(BF16),
                   l0_ln3_g, l0_ln3_b, tm, l0_ffn_wg.shape[1] // 2)
    x2 = _ssd_moba_sublayer(x2, l1_w_in, l1_conv_w, l1_conv_b, l1_dt_bias, l1_a_log, l1_d_skip, l1_ssd_norm,
                            l1_w_out, l1_ln1_g, l1_ln1_b)
    x2 = _xattn_sublayer(x2, mem2, l1_xq, l1_xk, l1_xv, l1_xo, l1_ln2_g, l1_ln2_b)
    x2 = _moe_sublayer(x2, l1_router, l1_router_b, l1_exp_wg.astype(BF16), l1_exp_wu.astype(BF16),
                       l1_exp_wd.astype(BF16), l1_ln3_g, l1_ln3_b)
    return x2.reshape(x.shape)
```

```python
import functools
import math

import jax
import jax.numpy as jnp
from jax import lax
from jax.experimental import pallas as pl
from jax.experimental.pallas import tpu as pltpu

BF16 = jnp.bfloat16
F32 = jnp.float32

D_MODEL = 1024
LN_EPS = 1e-5
DEPTH = 2
DN_ALPHA = (2 * DEPTH) ** 0.25

GLA_HEADS, GLA_DK, GLA_DV, GLA_CHUNK = 4, 64, 128, 64
GLA_QK, GLA_V, GLA_LR, GLA_TAU = 256, 512, 16, 16.0
GLA_COLS = 2 * GLA_QK + 2 * GLA_V + 128

RWKV_HEADS, RWKV_HD, RWKV_D, RWKV_CHUNK = 8, 64, 512, 64
RWKV_COLS = 1792
RWKV_DECAY_SCALE = math.exp(-0.5)
RWKV_GN_EPS = 64e-5

VMEM_LIMIT = 56 * 1024 * 1024
ROW_TILE = 512
SCAN_TILE = 256
COMBINE_TILE = 256


def _cparams(sem):
    return pltpu.CompilerParams(dimension_semantics=sem, vmem_limit_bytes=VMEM_LIMIT)


def _dot(a, b):
    return jnp.dot(a.astype(BF16), b.astype(BF16), preferred_element_type=F32)


def _dot_nt(a, b):
    return lax.dot_general(a.astype(BF16), b.astype(BF16), (((1,), (1,)), ((), ())), preferred_element_type=F32)


def _dot_tn(a, b):
    return lax.dot_general(a.astype(BF16), b.astype(BF16), (((0,), (0,)), ((), ())), preferred_element_type=F32)


def _split3(x):
    hi = x.astype(BF16)
    r1 = x - hi.astype(F32)
    mid = r1.astype(BF16)
    lo = (r1 - mid.astype(F32)).astype(BF16)
    return hi, mid, lo


def _dot_exact_lhs(m, x):
    mb = m.astype(BF16)
    hi, mid, lo = _split3(x)
    return (jnp.dot(mb, hi, preferred_element_type=F32) + jnp.dot(mb, mid, preferred_element_type=F32)
            + jnp.dot(mb, lo, preferred_element_type=F32))


def _dot_exact_rhs(x, m):
    mb = m.astype(BF16)
    hi, mid, lo = _split3(x)
    return (jnp.dot(hi, mb, preferred_element_type=F32) + jnp.dot(mid, mb, preferred_element_type=F32)
            + jnp.dot(lo, mb, preferred_element_type=F32))


def _dot_stat_rhs(x, m):
    mb = m.astype(BF16)
    hi = x.astype(BF16)
    lo = (x - hi.astype(F32)).astype(BF16)
    return jnp.dot(hi, mb, preferred_element_type=F32) + jnp.dot(lo, mb, preferred_element_type=F32)


def _sigmoid(x):
    return 1.0 / (1.0 + jnp.exp(-x))


def _silu(x):
    return x * _sigmoid(x)


def _iota2(shape, axis):
    return lax.broadcasted_iota(jnp.int32, shape, axis)


def _chunk_tril(n, chunk):
    r = _iota2((n, n), 0)
    c = _iota2((n, n), 1)
    return jnp.where((c <= r) & ((r // chunk) == (c // chunk)), 1.0, 0.0)


def _head_block(n, width, value):
    r = _iota2((n, n), 0)
    c = _iota2((n, n), 1)
    return jnp.where((r // width) == (c // width), value, 0.0)


def _mm_kernel(x_ref, w_ref, o_ref):
    o_ref[...] = _dot(x_ref[...], w_ref[...]).astype(o_ref.dtype)


def _matmul(x, w, out_dtype, tm, tn):
    S, K = x.shape
    N = w.shape[1]
    return pl.pallas_call(
        _mm_kernel,
        grid=(S // tm, N // tn),
        in_specs=[pl.BlockSpec((tm, K), lambda i, j: (i, 0)),
                  pl.BlockSpec((K, tn), lambda i, j: (0, j))],
        out_specs=pl.BlockSpec((tm, tn), lambda i, j: (i, j)),
        out_shape=jax.ShapeDtypeStruct((S, N), out_dtype),
        compiler_params=_cparams(("parallel", "arbitrary")),
        name="matmul",
    )(x, w)


def _layer_norm_rows(y, g, b):
    mu = jnp.mean(y, -1, keepdims=True)
    d = y - mu
    var = jnp.mean(d * d, -1, keepdims=True)
    return d * lax.rsqrt(var + LN_EPS) * g + b


def _mm_ln_kernel(transposed, *refs):
    n_in = len(transposed)
    a_refs = refs[:n_in]
    w_refs = refs[n_in:2 * n_in]
    x_ref, g_ref, b_ref, o_ref = refs[2 * n_in:]
    acc = DN_ALPHA * x_ref[...]
    for a_ref, w_ref, tr in zip(a_refs, w_refs, transposed):
        acc = acc + (_dot_tn if tr else _dot)(a_ref[...], w_ref[...])
    o_ref[...] = _layer_norm_rows(acc, g_ref[...], b_ref[...])


def _matmul_ln(a_list, w_list, x, g, b, tm, transposed=None):
    S, D = x.shape
    transposed = tuple(transposed or (False,) * len(a_list))
    in_specs = ([pl.BlockSpec((a.shape[0], tm), lambda i: (0, i)) if tr else
                 pl.BlockSpec((tm, a.shape[1]), lambda i: (i, 0)) for a, tr in zip(a_list, transposed)]
                + [pl.BlockSpec(w.shape, lambda i: (0, 0)) for w in w_list]
                + [pl.BlockSpec((tm, D), lambda i: (i, 0)),
                   pl.BlockSpec((1, D), lambda i: (0, 0)),
                   pl.BlockSpec((1, D), lambda i: (0, 0))])
    return pl.pallas_call(
        functools.partial(_mm_ln_kernel, transposed),
        grid=(S // tm,),
        in_specs=in_specs,
        out_specs=pl.BlockSpec((tm, D), lambda i: (i, 0)),
        out_shape=jax.ShapeDtypeStruct((S, D), F32),
        compiler_params=_cparams(("parallel",)),
        name="matmul_ln",
    )(*a_list, *w_list, x, g.reshape(1, D), b.reshape(1, D))


def _gla_kernel(p_ref, wa2_ref, ba_ref, ng_ref, o_ref, st_ref, o_scr):
    C, H, dk, dv = GLA_CHUNK, GLA_HEADS, GLA_DK, GLA_DV
    tb = p_ref.shape[0]

    @pl.when(pl.program_id(0) == 0)
    def _():
        st_ref[...] = jnp.zeros_like(st_ref)

    z = _dot(p_ref[:, 2 * GLA_QK + 2 * GLA_V:], wa2_ref[...]) + ba_ref[...]
    log_a = -(jnp.maximum(-z, 0.0) + jnp.log(1.0 + jnp.exp(-jnp.abs(z)))) / GLA_TAU
    b = _dot_exact_lhs(_chunk_tril(tb, C), log_a)
    causal = _iota2((C, C), 1) <= _iota2((C, C), 0)

    nc = tb // C
    q_h, k_h, ke_h, v_h, dec_h = [], [], [], [], []
    for c in range(nc):
        rows = slice(c * C, (c + 1) * C)
        b_c = b[rows]
        b_last = b_c[C - 1:C]
        q_dec = p_ref[rows, 0:GLA_QK] * (dk ** -0.5) * jnp.exp(b_c)
        k_c = p_ref[rows, GLA_QK:2 * GLA_QK]
        k_dec = k_c * jnp.exp(-b_c)
        k_end = k_c * jnp.exp(b_last - b_c)
        decay = jnp.exp(b_last)
        for h in range(H):
            ks = slice(h * dk, (h + 1) * dk)
            q_h.append(q_dec[:, ks])
            k_h.append(k_dec[:, ks])
            ke_h.append(k_end[:, ks])
            dec_h.append(decay[:, ks])
            v_h.append(p_ref[rows, 2 * GLA_QK + h * dv:2 * GLA_QK + (h + 1) * dv])
    n = nc * H
    attn = [jnp.where(causal, _dot_nt(q_h[i], k_h[i]), 0.0) for i in range(n)]
    kv = [_dot_tn(v_h[i], ke_h[i]) for i in range(n)]
    intra = [_dot(attn[i], v_h[i]) for i in range(n)]
    state = [st_ref[:, h * dk:(h + 1) * dk] for h in range(H)]
    entering = []
    for i in range(n):
        entering.append(state[i % H])
        state[i % H] = state[i % H] * dec_h[i] + kv[i]
    for i in range(n):
        c, h = divmod(i, H)
        o_scr[c * C:(c + 1) * C, h * dv:(h + 1) * dv] = intra[i] + _dot_nt(q_h[i], entering[i])
    for h in range(H):
        st_ref[:, h * dk:(h + 1) * dk] = state[h]

    for h in range(H):
        vs = slice(h * dv, (h + 1) * dv)
        o_h = o_scr[:, vs]
        g_h = p_ref[:, 2 * GLA_QK + GLA_V + h * dv:2 * GLA_QK + GLA_V + (h + 1) * dv]
        o_h = o_h * lax.rsqrt(jnp.mean(o_h * o_h, -1, keepdims=True) + 1e-5) * ng_ref[:, vs]
        o_ref[:, vs] = (o_h * _silu(g_h)).astype(o_ref.dtype)


def _gla_call(p_gla, wa2p, ba, norm_g, tb=SCAN_TILE):
    S = p_gla.shape[0]
    return pl.pallas_call(
        _gla_kernel,
        grid=(S // tb,),
        in_specs=[pl.BlockSpec((tb, GLA_COLS), lambda i: (i, 0)),
                  pl.BlockSpec((128, GLA_QK), lambda i: (0, 0)),
                  pl.BlockSpec((1, GLA_QK), lambda i: (0, 0)),
                  pl.BlockSpec((1, GLA_V), lambda i: (0, 0))],
        out_specs=pl.BlockSpec((tb, GLA_V), lambda i: (i, 0)),
        out_shape=jax.ShapeDtypeStruct((S, GLA_V), BF16),
        scratch_shapes=[pltpu.VMEM((GLA_DV, GLA_QK), F32), pltpu.VMEM((tb, GLA_V), F32)],
        compiler_params=_cparams(("arbitrary",)),
        name="gla",
    )(p_gla, wa2p, ba.reshape(1, GLA_QK), norm_g.reshape(1, GLA_V))


def _rwkv_kernel(p_ref, mu_ref, w0_ref, w2_ref, a0_ref, a2_ref, g2_ref, kk_ref, ka_ref, rk_ref, lnw_ref, lnb_ref,
                 o_ref, prev_ref, h_ref, o_scr):
    C, H, N, D = RWKV_CHUNK, RWKV_HEADS, RWKV_HD, RWKV_D
    tb = p_ref.shape[0]
    first = pl.program_id(0) == 0

    @pl.when(first)
    def _():
        prev_ref[...] = jnp.zeros_like(prev_ref)
        h_ref[...] = jnp.zeros_like(h_ref)

    p = p_ref[...]
    shifted = jnp.where(_iota2(p.shape, 0) == 0, prev_ref[...], pltpu.roll(p, 1, 0))
    prev_ref[...] = p[tb - 1:tb]
    p = p + mu_ref[...] * (shifted - p)
    r = p[:, 0:D]
    k = p[:, D:2 * D]
    v = p[:, 2 * D:3 * D]
    xw = p[:, 3 * D:3 * D + 64]
    xa = p[:, 3 * D + 64:3 * D + 128]
    xg = p[:, 3 * D + 128:3 * D + 256]
    lw = -RWKV_DECAY_SCALE * _sigmoid(w0_ref[...] + _dot(jnp.tanh(xw), w2_ref[...]))
    a = _sigmoid(a0_ref[...] + _dot(xa, a2_ref[...]))
    g = _dot(_sigmoid(xg), g2_ref[...])
    head_ones = _head_block(D, N, 1.0)
    kk = k * kk_ref[...]
    kk = kk * lax.rsqrt(jnp.maximum(_dot_stat_rhs(kk * kk, head_ones), 1e-24))
    k = k * (1.0 + (a - 1.0) * ka_ref[...])
    pv = -kk * a
    cw = _dot_exact_lhs(_chunk_tril(tb, C), lw)
    cwx = cw - lw

    W2 = 2 * N
    left = _iota2((C, W2), 1) < N

    def blockdiag(a):
        return jnp.concatenate([jnp.where(left, a, 0.0), jnp.where(left, 0.0, a)], axis=0)

    def diag_blocks(full):
        return jnp.where(left, full[0:C], full[C:2 * C])

    gi = _iota2((2 * C, 2 * W2), 0)
    gj = _iota2((2 * C, 2 * W2), 1) % N
    gram_mask = ((gi < C) & (gj < gi)) | ((gi >= C) & (gj <= gi - C))
    eye = _iota2((C, W2), 0) == _iota2((C, W2), 1) % N
    eye_f = jnp.where(eye, 1.0, 0.0)
    zeros_c = jnp.zeros((C, W2), F32)
    zeros_w = jnp.zeros((W2, W2), F32)

    nc = tb // C
    items = [(c, p) for c in range(nc) for p in range(H // 2)]
    xs, ys, pk_e, b_h, v_h, r_h, g_h = [], [], [], [], [], [], []
    for c in range(nc):
        rows = slice(c * C, (c + 1) * C)
        cw_c = cw[rows]
        cw_end = cw_c[C - 1:C]
        e_pos = jnp.exp(cw_c)
        e_neg = jnp.exp(-cw_c)
        e_end = jnp.exp(cw_end - cw_c)
        r_t = r[rows] * e_pos
        b_t = kk[rows] * jnp.exp(cwx[rows])
        p_t = pv[rows] * e_neg
        k_t = k[rows] * e_neg
        p_e = pv[rows] * e_end
        k_e = k[rows] * e_end
        g_end = jnp.exp(cw_end)
        v_c = v[rows]
        for p in range(H // 2):
            ps = slice(p * W2, (p + 1) * W2)
            xs.append(jnp.concatenate([b_t[:, ps], r_t[:, ps]], axis=0))
            ys.append(jnp.concatenate([blockdiag(p_t[:, ps]), blockdiag(k_t[:, ps])], axis=0))
            pk_e.append(jnp.concatenate([p_e[:, ps], k_e[:, ps]], axis=0))
            b_h.append(b_t[:, ps])
            v_h.append(v_c[:, ps])
            r_h.append(r_t[:, ps])
            g_h.append(g_end[:, ps])
    n = len(items)
    grams = [jnp.where(gram_mask, _dot_nt(xs[i], ys[i]), 0.0) for i in range(n)]
    l_p = [g[0:C, 0:W2] for g in grams]
    m_pk = [g[C:2 * C, :] for g in grams]
    v_bd = [blockdiag(v_h[i]) for i in range(n)]
    lkv = [_dot(grams[i][0:C, W2:2 * W2], v_bd[i]) for i in range(n)]
    x = [_dot(lp, blockdiag(lp)) for lp in l_p]
    t = [eye_f + lp for lp in l_p]
    for _ in range(4):
        tx = [_dot(jnp.concatenate([t[i], x[i]], axis=0), blockdiag(x[i])) for i in range(n)]
        t = [t[i] + tx[i][0:C] for i in range(n)]
        x = [tx[i][C:2 * C] for i in range(n)]
    t = [t[i] + _dot(t[i], blockdiag(x[i])) for i in range(n)]
    wu = [_dot(t[i], jnp.concatenate([blockdiag(b_h[i]), blockdiag(lkv[i])], axis=1)) for i in range(n)]
    az = [_dot_tn(pk_e[i], jnp.concatenate([wu[i], jnp.concatenate([zeros_c, v_h[i]], axis=1)], axis=0))
          for i in range(n)]
    qo = [_dot(m_pk[i], jnp.concatenate(
        [jnp.concatenate([blockdiag(wu[i][:, 0:W2]), blockdiag(wu[i][:, W2:2 * W2])], axis=1),
         jnp.concatenate([zeros_w, v_bd[i]], axis=1)], axis=0)) for i in range(n)]
    state = [h_ref[p] for p in range(H // 2)]
    for i, (c, p) in enumerate(items):
        a_mat = diag_blocks(az[i][:, 0:W2]) + jnp.where(eye, g_h[i], 0.0)
        q_mat = qo[i][:, 0:W2] + r_h[i]
        oh = _dot(jnp.concatenate([q_mat, a_mat], axis=0), blockdiag(state[p]))
        o_scr[c * C:(c + 1) * C, p * W2:(p + 1) * W2] = oh[0:C] + qo[i][:, W2:2 * W2]
        state[p] = oh[C:C + N] + diag_blocks(az[i][:, W2:2 * W2])
    for p in range(H // 2):
        h_ref[p] = state[p]

    o = o_scr[...]
    head_mean = _head_block(D, N, 1.0 / N)
    mean = _dot_stat_rhs(o, head_mean)
    d = o - mean
    var = _dot_stat_rhs(d * d, head_mean)
    o = d * lax.rsqrt(var + RWKV_GN_EPS) * lnw_ref[...] + lnb_ref[...]
    bonus = _dot_stat_rhs(r * k * rk_ref[...], head_ones) * v
    o_ref[...] = ((o + bonus) * g).astype(o_ref.dtype)


def _rwkv_call(p_rwkv, mu, w0, w2, a0, a2, g2, k_k, k_a, r_k, ln_w, ln_b, tb=SCAN_TILE):
    S = p_rwkv.shape[0]
    D = RWKV_D
    row = lambda t: t.reshape(1, -1).astype(F32)
    full = lambda shape: pl.BlockSpec(shape, lambda i: tuple(0 for _ in shape))
    return pl.pallas_call(
        _rwkv_kernel,
        grid=(S // tb,),
        in_specs=[pl.BlockSpec((tb, RWKV_COLS), lambda i: (i, 0)),
                  full((1, RWKV_COLS)), full((1, D)), full((64, D)), full((1, D)), full((64, D)), full((128, D)),
                  full((1, D)), full((1, D)), full((1, D)), full((1, D)), full((1, D))],
        out_specs=pl.BlockSpec((tb, D), lambda i: (i, 0)),
        out_shape=jax.ShapeDtypeStruct((S, D), BF16),
        scratch_shapes=[pltpu.VMEM((1, RWKV_COLS), F32),
                        pltpu.VMEM((RWKV_HEADS // 2, RWKV_HD, 2 * RWKV_HD), F32),
                        pltpu.VMEM((tb, D), F32)],
        compiler_params=_cparams(("arbitrary",)),
        name="rwkv7",
    )(p_rwkv, row(mu), row(w0), w2.astype(BF16), row(a0), a2.astype(BF16), g2.astype(BF16),
      row(k_k), row(k_a), row(r_k), row(ln_w), row(ln_b))


XATTN_HEADS, XATTN_HD = 4, 256


def _xattn_kernel(x_ref, wq_ref, k_ref, v_ref, wo_ref, g_ref, b_ref, o_ref):
    x = x_ref[...]
    q = _dot(x, wq_ref[...])
    outs = []
    for h in range(XATTN_HEADS):
        hs = slice(h * XATTN_HD, (h + 1) * XATTN_HD)
        s = _dot_nt(q[:, hs], k_ref[:, hs])
        e = jnp.exp(s - jnp.max(s, -1, keepdims=True))
        p = e / jnp.sum(e, -1, keepdims=True)
        outs.append(_dot(p, v_ref[:, hs]))
    o = jnp.concatenate(outs, axis=1)
    y = DN_ALPHA * x + _dot(o, wo_ref[...])
    o_ref[...] = _layer_norm_rows(y, g_ref[...], b_ref[...])


def _xattn_call(x, wq_scaled, k_mem, v_mem, wo, g, b, tm):
    S, D = x.shape
    M = k_mem.shape[0]
    const = lambda shape: pl.BlockSpec(shape, lambda i: (0, 0))
    return pl.pallas_call(
        _xattn_kernel,
        grid=(S // tm,),
        in_specs=[pl.BlockSpec((tm, D), lambda i: (i, 0)), const((D, D)), const((M, D)), const((M, D)),
                  const((D, D)), const((1, D)), const((1, D))],
        out_specs=pl.BlockSpec((tm, D), lambda i: (i, 0)),
        out_shape=jax.ShapeDtypeStruct((S, D), F32),
        compiler_params=_cparams(("parallel",)),
        name="xattn",
    )(x, wq_scaled, k_mem, v_mem, wo, g.reshape(1, D), b.reshape(1, D))


def _ffn_kernel(x_ref, wg_ref, wu_ref, wd_ref, g_ref, b_ref, o_ref, acc_ref):
    j = pl.program_id(1)
    x = x_ref[...]
    xb = x.astype(BF16)
    part = _dot(_silu(_dot(xb, wg_ref[...])) * _dot(xb, wu_ref[...]), wd_ref[...])

    last = pl.num_programs(1) - 1

    @pl.when(j == 0)
    def _():
        acc_ref[...] = DN_ALPHA * x + part

    @pl.when((j > 0) & (j < last))
    def _():
        acc_ref[...] += part

    @pl.when(j == last)
    def _():
        o_ref[...] = _layer_norm_rows(acc_ref[...] + part, g_ref[...], b_ref[...])


def _ffn_call(x, wg, wu, wd, g, b, tm, tf):
    S, D = x.shape
    F = wg.shape[1]
    assert F % tf == 0 and F // tf >= 2
    return pl.pallas_call(
        _ffn_kernel,
        grid=(S // tm, F // tf),
        in_specs=[pl.BlockSpec((tm, D), lambda i, j: (i, 0)),
                  pl.BlockSpec((D, tf), lambda i, j: (0, j)),
                  pl.BlockSpec((D, tf), lambda i, j: (0, j)),
                  pl.BlockSpec((tf, D), lambda i, j: (j, 0)),
                  pl.BlockSpec((1, D), lambda i, j: (0, 0)),
                  pl.BlockSpec((1, D), lambda i, j: (0, 0))],
        out_specs=pl.BlockSpec((tm, D), lambda i, j: (i, 0)),
        out_shape=jax.ShapeDtypeStruct((S, D), F32),
        scratch_shapes=[pltpu.VMEM((tm, D), F32)],
        compiler_params=_cparams(("parallel", "arbitrary")),
        name="ffn",
    )(x, wg, wu, wd, g.reshape(1, D), b.reshape(1, D))


SSD_HD, SSD_HEADS, SSD_INNER, SSD_GROUPS, SSD_STATE = 64, 16, 1024, 2, 128
SSD_BC, SSD_CONV, SSD_CONV_CH, SSD_CHUNK = 256, 4, 1536, 128
SSD_COLS = SSD_INNER + SSD_CONV_CH
SSD_GW = SSD_INNER // SSD_GROUPS


def _softplus(x):
    return jnp.maximum(x, 0.0) + jnp.log(1.0 + jnp.exp(-jnp.abs(x)))


def _ssd_kernel(p_ref, cw_ref, cb_ref, dtb_ref, a_ref, dsk_ref, ng_ref, o_ref, prev_ref, st_ref, y_scr):
    L, G, NS, HD = SSD_CHUNK, SSD_GROUPS, SSD_STATE, SSD_HD
    HG = SSD_HEADS // G

    @pl.when(pl.program_id(0) == 0)
    def _():
        prev_ref[...] = jnp.zeros_like(prev_ref)
        st_ref[...] = jnp.zeros_like(st_ref)

    cur = p_ref[:, SSD_INNER:SSD_COLS]
    tail = prev_ref[...]
    row = _iota2(tail.shape, 0)
    conv = cur * cw_ref[SSD_CONV - 1:SSD_CONV, :] + cb_ref[...]
    for kk in range(1, SSD_CONV):
        rolled = pltpu.roll(cur, kk, 0)
        head = jnp.where(row < kk, pltpu.roll(tail, kk, 0), rolled[0:8])
        shifted = jnp.concatenate([head, rolled[8:]], axis=0)
        conv = conv + shifted * cw_ref[SSD_CONV - 1 - kk:SSD_CONV - kk, :]
    prev_ref[...] = cur[L - 8:L]
    xbc = _silu(conv)
    xs = xbc[:, :SSD_INNER]

    dt = _softplus(p_ref[:, SSD_COLS:] + dtb_ref[...])
    a_col = dt * a_ref[...]
    li = _iota2((L, L), 0)
    lj = _iota2((L, L), 1)
    cs = _dot_exact_lhs(jnp.where(lj <= li, 1.0, 0.0), a_col)
    cs_row = cs.T
    expand = jnp.where(_iota2((128, SSD_INNER), 1) // HD == _iota2((128, SSD_INNER), 0), 1.0, 0.0)
    dt_x = _dot_stat_rhs(dt, expand)
    cs_x = _dot_exact_rhs(cs, expand)
    cs_end = cs_x[L - 1:L]
    xd = xs * dt_x
    xd_dec = xd * jnp.exp(cs_end - cs_x)
    out_dec = jnp.exp(cs_x)
    chunk_dec = jnp.exp(cs_end)
    tril = lj <= li

    for g in range(G):
        gs = slice(g * SSD_GW, (g + 1) * SSD_GW)
        b_g = xbc[:, SSD_INNER + g * NS:SSD_INNER + (g + 1) * NS]
        c_g = xbc[:, SSD_INNER + SSD_BC + g * NS:SSD_INNER + SSD_BC + (g + 1) * NS]
        cb = _dot_nt(c_g, b_g)
        for j in range(HG):
            h = g * HG + j
            hs = slice(h * HD, (h + 1) * HD)
            seg = jnp.where(tril, jnp.exp(cs[:, h:h + 1] - cs_row[h:h + 1, :]), 0.0)
            y_scr[:, hs] = _dot(cb * seg, xd[:, hs])
        st = st_ref[g]
        y_off = _dot(c_g, st) * out_dec[:, gs]
        st_ref[g] = st * chunk_dec[:, gs] + _dot_tn(b_g, xd_dec[:, gs])
        y_scr[:, gs] = y_scr[:, gs] + y_off

    y = (y_scr[...] + dsk_ref[...] * xs) * _silu(p_ref[:, :SSD_INNER])
    for g in range(G):
        gs = slice(g * SSD_GW, (g + 1) * SSD_GW)
        y_g = y[:, gs]
        o_ref[:, gs] = (y_g * lax.rsqrt(jnp.mean(y_g * y_g, -1, keepdims=True) + 1e-5) * ng_ref[:, gs]).astype(o_ref.dtype)


def _ssd_call(p_ssd, conv_w, conv_b, dt_bias, a_log, d_skip, norm_g):
    S = p_ssd.shape[0]
    L = SSD_CHUNK
    a_neg = -jnp.exp(a_log.astype(F32))
    pad = lambda t: jnp.pad(t.astype(F32), (0, 128 - SSD_HEADS)).reshape(1, 128)
    const = lambda shape: pl.BlockSpec(shape, lambda i: (0, 0))
    return pl.pallas_call(
        _ssd_kernel,
        grid=(S // L,),
        in_specs=[pl.BlockSpec((L, SSD_COLS + 128), lambda i: (i, 0)),
                  const((SSD_CONV, SSD_CONV_CH)), const((1, SSD_CONV_CH)),
                  const((1, 128)), const((1, 128)),
                  const((1, SSD_INNER)), const((1, SSD_INNER))],
        out_specs=pl.BlockSpec((L, SSD_INNER), lambda i: (i, 0)),
        out_shape=jax.ShapeDtypeStruct((S, SSD_INNER), BF16),
        scratch_shapes=[pltpu.VMEM((8, SSD_CONV_CH), F32),
                        pltpu.VMEM((SSD_GROUPS, SSD_STATE, SSD_GW), F32),
                        pltpu.VMEM((L, SSD_INNER), F32)],
        compiler_params=_cparams(("arbitrary",)),
        name="ssd",
    )(p_ssd, conv_w.astype(F32), conv_b.reshape(1, -1).astype(F32), pad(dt_bias), pad(a_neg),
      jnp.repeat(d_skip.astype(F32), SSD_HD).reshape(1, -1), norm_g.reshape(1, -1).astype(F32))


MOBA_HD, MOBA_HEADS, MOBA_D, MOBA_BLOCK, MOBA_TOPK = 64, 8, 512, 256, 3
MOBA_GROUP = 4
MOBA_UNDERFLOW = -160.0
MOBA_BOUND_SLACK = 1.001
NEG_BIG = -1e30
LOG2E = math.log2(math.e)


MOBA_VROWS = MOBA_HD + 16


def _moba_proj_kernel(x_ref, wqt_ref, wk_ref, wvt_ref, qt_ref, k_ref, kmean_ref, v_ref, knorm_ref, qnorm_ref):
    xb = x_ref[...].astype(BF16)
    qt = _dot_nt(wqt_ref[...], xb).astype(qt_ref.dtype)
    qt_ref[...] = qt
    vt = _dot_nt(wvt_ref[...], xb)
    extra = jnp.where(_iota2((MOBA_VROWS - MOBA_HD, MOBA_BLOCK), 0) == 0, 1.0, 0.0)
    for h in range(MOBA_HEADS):
        v_ref[h, 0] = jnp.concatenate([vt[h * MOBA_HD:(h + 1) * MOBA_HD], extra], axis=0).astype(v_ref.dtype)
    k = _dot(xb, wk_ref[...])
    kmean_ref[0] = jnp.mean(k, 0, keepdims=True)
    shape = (MOBA_BLOCK, 2 * MOBA_HD)
    lane = _iota2(shape, 1)
    pos = jnp.where((lane == MOBA_HD) | (lane == MOBA_HD + 1), _iota2(shape, 0).astype(F32), 0.0).astype(k_ref.dtype)
    k_b = k.astype(k_ref.dtype)
    for h in range(MOBA_HEADS):
        k_ref[h, 0] = pos
        k_ref[h, 0, :, 0:MOBA_HD] = k_b[:, h * MOBA_HD:(h + 1) * MOBA_HD]
    k_f = k_b.astype(F32)
    q_f = qt.astype(F32)
    head_cols = jnp.where(_iota2((MOBA_D, 128), 0) // MOBA_HD == _iota2((MOBA_D, 128), 1), 1.0, 0.0)
    head_rows = jnp.where(_iota2((128, MOBA_D), 1) // MOBA_HD == _iota2((128, MOBA_D), 0), 1.0, 0.0)
    knorm_ref[0] = jnp.sqrt(jnp.max(_dot_stat_rhs(k_f * k_f, head_cols), 0, keepdims=True))
    qnorm_ref[0] = jnp.sqrt(jnp.max(_dot_exact_lhs(head_rows, q_f * q_f), 1, keepdims=True))


def _moba_proj_call(x, wqt, wk, wvt):
    S, D = x.shape
    nb = S // MOBA_BLOCK
    const = lambda shape: pl.BlockSpec(shape, lambda i: (0, 0))
    return pl.pallas_call(
        _moba_proj_kernel,
        grid=(nb,),
        in_specs=[pl.BlockSpec((MOBA_BLOCK, D), lambda i: (i, 0)), const((MOBA_D, D)), const((D, MOBA_D)),
                  const((MOBA_D, D))],
        out_specs=[pl.BlockSpec((MOBA_D, MOBA_BLOCK), lambda i: (0, i)),
                   pl.BlockSpec((MOBA_HEADS, 1, MOBA_BLOCK, 2 * MOBA_HD), lambda i: (0, i, 0, 0)),
                   pl.BlockSpec((1, 1, MOBA_D), lambda i: (i, 0, 0)),
                   pl.BlockSpec((MOBA_HEADS, 1, MOBA_VROWS, MOBA_BLOCK), lambda i: (0, i, 0, 0)),
                   pl.BlockSpec((1, 1, 128), lambda i: (i, 0, 0)),
                   pl.BlockSpec((1, 128, 1), lambda i: (i, 0, 0))],
        out_shape=[jax.ShapeDtypeStruct((MOBA_D, S), BF16),
                   jax.ShapeDtypeStruct((MOBA_HEADS, nb, MOBA_BLOCK, 2 * MOBA_HD), BF16),
                   jax.ShapeDtypeStruct((nb, 1, MOBA_D), F32),
                   jax.ShapeDtypeStruct((MOBA_HEADS, nb, MOBA_VROWS, MOBA_BLOCK), BF16),
                   jax.ShapeDtypeStruct((nb, 1, 128), F32),
                   jax.ShapeDtypeStruct((nb, 128, 1), F32)],
        compiler_params=_cparams(("parallel",)),
        name="moba_proj",
    )(x, wqt, wk, wvt)


def _moba_kernel(first_ref, qt_ref, k_ref, vt_ref, kmean_ref, o_ref, sel_ref, s0_ref, s1_ref, p0_ref, p1_ref):
    BS, HD = MOBA_BLOCK, MOBA_HD
    h = pl.program_id(0)
    i = pl.program_id(1)
    nb = k_ref.shape[0]
    qt = qt_ref[...]
    slope = LOG2E * jnp.exp2(jnp.zeros((1, BS), F32) - (h + 1).astype(F32) * (8.0 / MOBA_HEADS))
    slope_hi = slope.astype(qt.dtype).astype(F32)
    slope_lo = slope - slope_hi
    row = _iota2((HD, BS), 0)
    qt_ext = jnp.concatenate(
        [qt, jnp.where(row == 0, slope_hi, jnp.where(row == 1, slope_lo, 0.0)).astype(qt.dtype)], axis=0)

    gate = _dot(kmean_ref[...], qt)
    blk = _iota2((nb, BS), 0).astype(F32)
    cand = blk < i.astype(F32)
    sel = jnp.zeros((nb, BS), F32)
    for _ in range(MOBA_TOPK):
        best = jnp.max(jnp.where(cand, gate, -jnp.inf), 0, keepdims=True)
        idx = jnp.min(jnp.where(cand & (gate == best), blk, float(nb)), 0, keepdims=True)
        pick = blk == idx
        sel = jnp.where(pick, 1.0, sel)
        cand = cand & jnp.logical_not(pick)
    sel_ref[...] = sel

    G = MOBA_GROUP
    last = nb - 1
    s_ref = (s0_ref, s1_ref)
    p_ref = (p0_ref, p1_ref)

    first = first_ref[h * nb + i]

    def group(u):
        return [jnp.clip(first + G * u + x, 0, last) for x in range(G)]

    def issue_scores(u, slot):
        tops = []
        for x, j in enumerate(group(u)):
            sc = _dot(k_ref[j], qt_ext)
            s_ref[slot][x] = sc
            tops.append(jnp.max(sc, 0, keepdims=True))
        return tuple(tops)

    def value_blocks(u):
        js = group(u)
        js[0] = jnp.where(u == -1, i, js[0])
        return js

    def weighted_values(u, slot):
        out = None
        for x, j in enumerate(value_blocks(u)):
            part = _dot(vt_ref[j], p_ref[slot][x])
            out = part if out is None else out + part
        return out

    def step(u, slot, tops, a_prev, m, acc, issue_next=True):
        on, shift = [], []
        m_new = m
        for x, j in enumerate(group(u)):
            on.append(sel_ref[pl.ds(j, 1), :] > 0.0)
            shift.append(slope * ((j - i) * BS).astype(F32))
            m_new = jnp.maximum(m_new, jnp.where(on[x], tops[x] + shift[x], NEG_BIG))
        tops_next, pv = [], None
        for x, (j_prev, j_next) in enumerate(zip(value_blocks(u - 1), group(u + 1))):
            part = _dot(vt_ref[j_prev], p_ref[1 - slot][x])
            pv = part if pv is None else pv + part
            if issue_next:
                sc = _dot(k_ref[j_next], qt_ext)
                s_ref[1 - slot][x] = sc
                tops_next.append(jnp.max(sc, 0, keepdims=True))
            p = jnp.exp2(s_ref[slot][x] - (jnp.where(on[x], m_new, -NEG_BIG) - shift[x]))
            p_ref[slot][x] = p.astype(BF16)
        return tuple(tops_next), jnp.exp2(m - m_new), m_new, a_prev * acc + pv

    def body(w, carry):
        carry = step(2 * w, 0, *carry)
        return step(2 * w + 1, 1, *carry)

    tops0 = issue_scores(0, 0)
    s_own = jnp.where(_iota2((BS, BS), 1) >= _iota2((BS, BS), 0), _dot(k_ref[i], qt_ext), NEG_BIG)
    m_own = jnp.max(s_own, 0, keepdims=True)
    p1_ref[1:G] = jnp.zeros((G - 1, BS, BS), BF16)
    p1_ref[0] = jnp.exp2(s_own - m_own).astype(BF16)
    init = (tops0, jnp.ones((1, BS), F32), m_own, jnp.zeros((vt_ref.shape[1], BS), F32))
    steps = (i - first + G - 1) // G
    pairs = steps // 2
    carry = lax.fori_loop(0, pairs, body, init)

    def odd_tail(carry):
        _, a_last, _, acc = step(2 * pairs, 0, *carry, issue_next=False)
        return a_last * acc + weighted_values(2 * pairs, 0)

    def even_tail(carry):
        _, a_prev, _, acc = carry
        return a_prev * acc + weighted_values(2 * pairs - 1, 1)

    acc = lax.cond(steps % 2 == 1, odd_tail, even_tail, carry)
    o_ref[...] = (acc[0:HD] / acc[HD:HD + 1]).astype(o_ref.dtype)


def _moba_first_block(knorm, qnorm, nb):
    BS = MOBA_BLOCK
    k_norm = knorm[:, 0, 0:MOBA_HEADS].T
    q_norm = qnorm[:, 0:MOBA_HEADS, 0].T
    slope = LOG2E * jnp.exp2(-(jnp.arange(MOBA_HEADS, dtype=F32) + 1.0) * (8.0 / MOBA_HEADS))
    i_idx = jnp.arange(nb, dtype=F32)[None, :, None]
    j_idx = jnp.arange(nb, dtype=F32)[None, None, :]
    reach = slope[:, None, None] * ((BS - 1.0) - BS * (i_idx - j_idx))
    bound = q_norm[:, :, None] * (k_norm[:, None, :] + k_norm[:, :, None]) * MOBA_BOUND_SLACK + reach
    matters = (bound >= MOBA_UNDERFLOW) & (j_idx < i_idx)
    first = jnp.min(jnp.where(matters, j_idx, i_idx), axis=2).astype(jnp.int32)
    return first.reshape(-1)


def _moba_call(first_block, qt, k4, vt4, kmean):
    S = qt.shape[1]
    nb = S // MOBA_BLOCK
    return pl.pallas_call(
        _moba_kernel,
        grid_spec=pltpu.PrefetchScalarGridSpec(
            num_scalar_prefetch=1,
            grid=(MOBA_HEADS, nb),
            in_specs=[pl.BlockSpec((MOBA_HD, MOBA_BLOCK), lambda h, i, first: (h, i)),
                      pl.BlockSpec((None, nb, MOBA_BLOCK, 2 * MOBA_HD), lambda h, i, first: (h, 0, 0, 0)),
                      pl.BlockSpec((None, nb, MOBA_VROWS, MOBA_BLOCK), lambda h, i, first: (h, 0, 0, 0)),
                      pl.BlockSpec((None, nb, MOBA_HD), lambda h, i, first: (h, 0, 0))],
            out_specs=pl.BlockSpec((MOBA_HD, MOBA_BLOCK), lambda h, i, first: (h, i)),
            scratch_shapes=[pltpu.VMEM((nb, MOBA_BLOCK), F32)]
            + [pltpu.VMEM((MOBA_GROUP, MOBA_BLOCK, MOBA_BLOCK), F32)] * 2
            + [pltpu.VMEM((MOBA_GROUP, MOBA_BLOCK, MOBA_BLOCK), BF16)] * 2),
        out_shape=jax.ShapeDtypeStruct((MOBA_D, S), BF16),
        compiler_params=_cparams(("parallel", "arbitrary")),
        name="moba",
    )(first_block, qt, k4, vt4, kmean)


N_EXPERTS, TOP_K, EXPERT_FF = 8, 2, 2816
MOE_ROWS = 512


def _router_kernel(x_ref, whi_ref, wlo_ref, b_ref, o_ref):
    x = x_ref[...]
    xhi = x.astype(BF16)
    xlo = (x - xhi.astype(F32)).astype(BF16)
    logits = (jnp.dot(xhi, whi_ref[...], preferred_element_type=F32)
              + jnp.dot(xhi, wlo_ref[...], preferred_element_type=F32)
              + jnp.dot(xlo, whi_ref[...], preferred_element_type=F32)) + b_ref[...]
    lane = _iota2(logits.shape, 1).astype(F32)
    logits = jnp.where(lane < N_EXPERTS, logits, -jnp.inf)
    m1 = jnp.max(logits, -1, keepdims=True)
    i1 = jnp.min(jnp.where(logits == m1, lane, 128.0), -1, keepdims=True)
    rest = jnp.where(lane == i1, -jnp.inf, logits)
    m2 = jnp.max(rest, -1, keepdims=True)
    i2 = jnp.min(jnp.where(rest == m2, lane, 128.0), -1, keepdims=True)
    e = jnp.exp(m2 - m1)
    g1 = 1.0 / (1.0 + e)
    g2 = e / (1.0 + e)
    out = jnp.where(lane == 0, i1, 0.0)
    out = jnp.where(lane == 1, i2, out)
    out = jnp.where(lane == 2, g1, out)
    out = jnp.where(lane == 3, g2, out)
    o_ref[...] = out


def _router_call(x, w_router, b_router, tm):
    S, D = x.shape
    wp = jnp.pad(w_router.astype(F32), ((0, 0), (0, 128 - N_EXPERTS)))
    whi = wp.astype(BF16)
    wlo = (wp - whi.astype(F32)).astype(BF16)
    bp = jnp.pad(b_router.astype(F32), (0, 128 - N_EXPERTS)).reshape(1, 128)
    const = lambda shape: pl.BlockSpec(shape, lambda i: (0, 0))
    return pl.pallas_call(
        _router_kernel,
        grid=(S // tm,),
        in_specs=[pl.BlockSpec((tm, D), lambda i: (i, 0)), const((D, 128)), const((D, 128)), const((1, 128))],
        out_specs=pl.BlockSpec((tm, 128), lambda i: (i, 0)),
        out_shape=jax.ShapeDtypeStruct((S, 128), F32),
        compiler_params=_cparams(("parallel",)),
        name="router",
    )(x, whi, wlo, bp)


def _row_copy(src_hbm, dst_ref, src_row, dst_row, sem):
    return pltpu.make_async_copy(src_hbm.at[pl.ds(src_row, 1)], dst_ref.at[pl.ds(dst_row, 1)], sem)


def _scatter_rows_kernel(dest_ref, lo_ref, hi_ref, x_ref, o_hbm, zero_ref, sem):
    step = pl.program_id(0)
    n_tiles = pl.num_programs(0) - 1
    tm = x_ref.shape[0]

    def row_out(src_ref, src_row, dst_row):
        return pltpu.make_async_copy(src_ref.at[pl.ds(src_row, 1)], o_hbm.at[pl.ds(dst_row, 1)], sem)

    @pl.when(step < n_tiles)
    def _():
        base = step * tm

        def start(r, c):
            for slot in range(TOP_K):
                row_out(x_ref, r, dest_ref[TOP_K * (base + r) + slot]).start()
            return c

        lax.fori_loop(0, tm, start, 0, unroll=8)
        for slot in range(TOP_K):
            pltpu.make_async_copy(x_ref, o_hbm.at[pl.ds(0, tm)], sem).wait()

    @pl.when(step == n_tiles)
    def _():
        zero_ref[...] = jnp.zeros_like(zero_ref)
        for e in range(N_EXPERTS):
            def start(r, c):
                row_out(zero_ref, 0, r).start()
                return c

            def wait(r, c):
                row_out(zero_ref, 0, r).wait()
                return c

            lax.fori_loop(lo_ref[e], hi_ref[e], start, 0)
            lax.fori_loop(lo_ref[e], hi_ref[e], wait, 0)


def _scatter_rows(x, dest, pad_lo, pad_hi, n_rows, tm):
    T, D = x.shape
    n_tiles = T // tm
    return pl.pallas_call(
        _scatter_rows_kernel,
        grid_spec=pltpu.PrefetchScalarGridSpec(
            num_scalar_prefetch=3,
            grid=(n_tiles + 1,),
            in_specs=[pl.BlockSpec((tm, D), lambda i, d, lo, hi: (jnp.minimum(i, n_tiles - 1), 0))],
            out_specs=pl.BlockSpec(memory_space=pl.ANY),
            scratch_shapes=[pltpu.VMEM((8, D), x.dtype), pltpu.SemaphoreType.DMA(())]),
        out_shape=jax.ShapeDtypeStruct((n_rows, D), x.dtype),
        compiler_params=_cparams(("arbitrary",)),
        name="moe_scatter",
    )(dest, pad_lo, pad_hi, x)


def _moe_ffn_kernel(be_ref, nu_ref, x_ref, wg_ref, wu_ref, wd_ref, o_ref, acc_ref):
    i = pl.program_id(0)
    j = pl.program_id(1)

    last = pl.num_programs(1) - 1

    @pl.when(i < nu_ref[0])
    def _():
        xb = x_ref[...].astype(BF16)
        part = _dot(_silu(_dot(xb, wg_ref[...])) * _dot(xb, wu_ref[...]), wd_ref[...])

        @pl.when(j == 0)
        def _():
            acc_ref[...] = part

        @pl.when((j > 0) & (j < last))
        def _():
            acc_ref[...] += part

        @pl.when(j == last)
        def _():
            o_ref[...] = acc_ref[...] + part

    @pl.when((j == last) & (i >= nu_ref[0]))
    def _():
        o_ref[...] = jnp.zeros_like(o_ref)


def _moe_ffn_call(x_rows, block_e, n_used, wg, wu, wd, tf):
    n, D = x_rows.shape
    F = wg.shape[2]
    R = MOE_ROWS
    assert F % tf == 0 and F // tf >= 2
    return pl.pallas_call(
        _moe_ffn_kernel,
        grid_spec=pltpu.PrefetchScalarGridSpec(
            num_scalar_prefetch=2,
            grid=(n // R, F // tf),
            in_specs=[pl.BlockSpec((R, D), lambda i, j, be, nu: (i, 0)),
                      pl.BlockSpec((None, D, tf), lambda i, j, be, nu: (be[i], 0, j)),
                      pl.BlockSpec((None, D, tf), lambda i, j, be, nu: (be[i], 0, j)),
                      pl.BlockSpec((None, tf, D), lambda i, j, be, nu: (be[i], j, 0))],
            out_specs=pl.BlockSpec((R, D), lambda i, j, be, nu: (i, 0)),
            scratch_shapes=[pltpu.VMEM((R, D), F32)]),
        out_shape=jax.ShapeDtypeStruct((n, D), F32),
        compiler_params=_cparams(("arbitrary", "arbitrary")),
        name="moe_ffn",
    )(block_e, n_used, x_rows, wg, wu, wd)


def _moe_combine_kernel(d_ref, y_hbm, x_ref, r_ref, g_ref, b_ref, o_ref, y1_ref, y2_ref, sem):
    tm = x_ref.shape[0]
    base = pl.program_id(0) * tm

    def start(r, c):
        _row_copy(y_hbm, y1_ref, d_ref[2 * (base + r)], r, sem.at[0]).start()
        _row_copy(y_hbm, y2_ref, d_ref[2 * (base + r) + 1], r, sem.at[1]).start()
        return c

    lax.fori_loop(0, tm, start, 0, unroll=8)
    pltpu.make_async_copy(y_hbm.at[pl.ds(0, tm)], y1_ref, sem.at[0]).wait()
    pltpu.make_async_copy(y_hbm.at[pl.ds(0, tm)], y2_ref, sem.at[1]).wait()
    y = r_ref[:, 2:3] * y1_ref[...] + r_ref[:, 3:4] * y2_ref[...]
    o_ref[...] = _layer_norm_rows(DN_ALPHA * x_ref[...] + y, g_ref[...], b_ref[...])


def _moe_combine_call(dest, y_rows, x, routed, g, b, tm):
    S, D = x.shape
    return pl.pallas_call(
        _moe_combine_kernel,
        grid_spec=pltpu.PrefetchScalarGridSpec(
            num_scalar_prefetch=1,
            grid=(S // tm,),
            in_specs=[pl.BlockSpec(memory_space=pl.ANY),
                      pl.BlockSpec((tm, D), lambda i, d: (i, 0)),
                      pl.BlockSpec((tm, 128), lambda i, d: (i, 0)),
                      pl.BlockSpec((1, D), lambda i, d: (0, 0)),
                      pl.BlockSpec((1, D), lambda i, d: (0, 0))],
            out_specs=pl.BlockSpec((tm, D), lambda i, d: (i, 0)),
            scratch_shapes=[pltpu.VMEM((tm, D), F32), pltpu.VMEM((tm, D), F32), pltpu.SemaphoreType.DMA((2,))]),
        out_shape=jax.ShapeDtypeStruct((S, D), F32),
        compiler_params=_cparams(("arbitrary",)),
        name="moe_combine",
    )(dest, y_rows, x, routed, g.reshape(1, D), b.reshape(1, D))


def _moe_sublayer(x, w_router, b_router, wg, wu, wd, g, b):
    T = x.shape[0]
    R = MOE_ROWS
    routed = _router_call(x, w_router, b_router, min(ROW_TILE, T))
    top_e = routed[:, 0:TOP_K].astype(jnp.int32)
    tok_oh = jnp.sum((top_e[:, :, None] == jnp.arange(N_EXPERTS)[None, None, :]).astype(jnp.int32), axis=1)
    counts = jnp.sum(tok_oh, axis=0)
    rank = jnp.cumsum(tok_oh, axis=0) - tok_oh
    padded = (counts + R - 1) // R * R
    pend = jnp.cumsum(padded)
    pstart = pend - padded
    dest = pstart[top_e] + jnp.take_along_axis(rank, top_e, axis=1)
    n_rows = (T * TOP_K + N_EXPERTS * (R - 1)) // R * R
    n_blocks = n_rows // R
    block_first_row = jnp.arange(n_blocks, dtype=jnp.int32) * R
    block_e = jnp.minimum(jnp.sum((pend[None, :] <= block_first_row[:, None]).astype(jnp.int32), axis=1),
                          N_EXPERTS - 1)
    n_used = (pend[-1] // R).astype(jnp.int32).reshape(1)
    dest = dest.reshape(-1).astype(jnp.int32)
    pad_lo = (pstart + counts).astype(jnp.int32)
    pad_hi = jnp.concatenate([pstart[1:], jnp.array([n_rows])]).astype(jnp.int32)
    x_rows = _scatter_rows(x, dest, pad_lo, pad_hi, n_rows, min(ROW_TILE, T))
    y_rows = _moe_ffn_call(x_rows, block_e, n_used, wg, wu, wd, EXPERT_FF // 2)
    return _moe_combine_call(dest, y_rows, x, routed, g, b, min(COMBINE_TILE, T))


GLA_IN = 2 * GLA_QK + 2 * GLA_V + GLA_LR


def _gla_rwkv_sublayer(x, w_in, gla_wa2, gla_ba, gla_norm, mu, w0, w2, a0, a2, g2, k_k, k_a, r_k, ln_w, ln_b,
                       w_out, ln_g, ln_bias):
    S = x.shape[0]
    tm = min(ROW_TILE, S)
    lr_pad = 128 - GLA_LR
    w_gla = jnp.pad(w_in[:, :GLA_IN], ((0, 0), (0, lr_pad))).astype(BF16)
    w_rwkv = w_in[:, GLA_IN:].astype(BF16)
    p_gla = _matmul(x, w_gla, F32, tm, GLA_COLS)
    p_rwkv = _matmul(x, w_rwkv, F32, tm, RWKV_COLS)
    wa2p = jnp.pad(gla_wa2, ((0, lr_pad), (0, 0))).astype(BF16)
    o_gla = _gla_call(p_gla, wa2p, gla_ba, gla_norm)
    o_rwkv = _rwkv_call(p_rwkv, mu, w0, w2, a0, a2, g2, k_k, k_a, r_k, ln_w, ln_b)
    return _matmul_ln([o_gla, o_rwkv], [w_out[:GLA_V].astype(BF16), w_out[GLA_V:].astype(BF16)], x,
                      ln_g, ln_bias, tm)


def _xattn_sublayer(x, mem, wq, wk, wv, wo, ln_g, ln_bias):
    M = mem.shape[0]
    k_mem = _matmul(mem, wk.astype(BF16), BF16, M, D_MODEL)
    v_mem = _matmul(mem, wv.astype(BF16), BF16, M, D_MODEL)
    wq_scaled = (wq * XATTN_HD ** -0.5).astype(BF16)
    return _xattn_call(x, wq_scaled, k_mem, v_mem, wo.astype(BF16), ln_g, ln_bias, min(ROW_TILE, x.shape[0]))


def _ssd_moba_sublayer(x, w_in, conv_w, conv_b, dt_bias, a_log, d_skip, ssd_norm, w_out, ln_g, ln_bias):
    S = x.shape[0]
    tm = min(ROW_TILE, S)
    nb = S // MOBA_BLOCK
    o_dt = SSD_COLS
    o_q = o_dt + SSD_HEADS
    w_ssd = jnp.pad(w_in[:, :o_q], ((0, 0), (0, 128 - SSD_HEADS))).astype(BF16)
    w_q = w_in[:, o_q:o_q + MOBA_D]
    w_k = w_in[:, o_q + MOBA_D:o_q + 2 * MOBA_D]
    w_v = w_in[:, o_q + 2 * MOBA_D:]
    p_ssd = _matmul(x, w_ssd, F32, tm, SSD_COLS + 128)
    o_ssd = _ssd_call(p_ssd, conv_w, conv_b, dt_bias, a_log, d_skip, ssd_norm)
    qt, k4, kmean, vt4, knorm, qnorm = _moba_proj_call(x, (w_q.T * (MOBA_HD ** -0.5 * LOG2E)).astype(BF16), w_k.astype(BF16),
                                         w_v.T.astype(BF16))
    kmean_h = kmean.reshape(nb, MOBA_HEADS, MOBA_HD).transpose(1, 0, 2)
    ot_moba = _moba_call(_moba_first_block(knorm, qnorm, nb), qt, k4, vt4, kmean_h)
    return _matmul_ln([o_ssd, ot_moba], [w_out[:SSD_INNER].astype(BF16), w_out[SSD_INNER:].astype(BF16)], x,
                      ln_g, ln_bias, tm, transposed=(False, True))


def kernel(x, mem, l0_w_in, l0_gla_wa2, l0_gla_ba, l0_gla_norm, l0_rwkv_mu, l0_rwkv_w0, l0_rwkv_w2, l0_rwkv_a0, l0_rwkv_a2, l0_rwkv_g2, l0_rwkv_kk, l0_rwkv_ka, l0_rwkv_rk, l0_rwkv_lnw, l0_rwkv_lnb, l0_w_out, l0_ln1_g, l0_ln1_b, l0_xq, l0_xk, l0_xv, l0_xo, l0_ln2_g, l0_ln2_b, l0_ffn_wg, l0_ffn_wu, l0_ffn_wd, l0_ln3_g, l0_ln3_b, l1_w_in, l1_conv_w, l1_conv_b, l1_dt_bias, l1_a_log, l1_d_skip, l1_ssd_norm, l1_w_out, l1_ln1_g, l1_ln1_b, l1_xq, l1_xk, l1_xv, l1_xo, l1_ln2_g, l1_ln2_b, l1_router, l1_router_b, l1_exp_wg, l1_exp_wu, l1_exp_wd, l1_ln3_g, l1_ln3_b):
    x2 = x.reshape(-1, D_MODEL)
    mem2 = mem.reshape(-1, D_MODEL)
    x2 = _gla_rwkv_sublayer(x2, l0_w_in, l0_gla_wa2, l0_gla_ba, l0_gla_norm, l0_rwkv_mu, l0_rwkv_w0, l0_rwkv_w2,
                            l0_rwkv_a0, l0_rwkv_a2, l0_rwkv_g2, l0_rwkv_kk, l0_rwkv_ka, l0_rwkv_rk, l0_rwkv_lnw,
                            l0_rwkv_lnb, l0_w_out, l0_ln1_g, l0_ln1_b)
    x2 = _xattn_sublayer(x2, mem2, l0_xq, l0_xk, l0_xv, l0_xo, l0_ln2_g, l0_ln2_b)
    tm = min(ROW_TILE, x2.shape[0])
    x2 = _ffn_call(x2, l0_ffn_wg.astype(BF16), l0_ffn_wu.astype(BF16), l0_ffn_wd.astype(BF16),
                   l0_ln3_g, l0_ln3_b, tm, l0_ffn_wg.shape[1] // 2)
    x2 = _ssd_moba_sublayer(x2, l1_w_in, l1_conv_w, l1_conv_b, l1_dt_bias, l1_a_log, l1_d_skip, l1_ssd_norm,
                            l1_w_out, l1_ln1_g, l1_ln1_b)
    x2 = _xattn_sublayer(x2, mem2, l1_xq, l1_xk, l1_xv, l1_xo, l1_ln2_g, l1_ln2_b)
    x2 = _moe_sublayer(x2, l1_router, l1_router_b, l1_exp_wg.astype(BF16), l1_exp_wu.astype(BF16),
                       l1_exp_wd.astype(BF16), l1_ln3_g, l1_ln3_b)
    return x2.reshape(x.shape)
```

```python
import functools
import math

import jax
import jax.numpy as jnp
from jax import lax
from jax.experimental import pallas as pl
from jax.experimental.pallas import tpu as pltpu

BF16 = jnp.bfloat16
F32 = jnp.float32

D_MODEL = 1024
LN_EPS = 1e-5
DEPTH = 2
DN_ALPHA = (2 * DEPTH) ** 0.25

GLA_HEADS, GLA_DK, GLA_DV, GLA_CHUNK = 4, 64, 128, 64
GLA_QK, GLA_V, GLA_LR, GLA_TAU = 256, 512, 16, 16.0
GLA_COLS = 2 * GLA_QK + 2 * GLA_V + 128

RWKV_HEADS, RWKV_HD, RWKV_D, RWKV_CHUNK = 8, 64, 512, 64
RWKV_COLS = 1792
RWKV_DECAY_SCALE = math.exp(-0.5)
RWKV_GN_EPS = 64e-5

VMEM_LIMIT = 56 * 1024 * 1024
ROW_TILE = 512
SCAN_TILE = 256
COMBINE_TILE = 256


def _cparams(sem):
    return pltpu.CompilerParams(dimension_semantics=sem, vmem_limit_bytes=VMEM_LIMIT)


def _dot(a, b):
    return jnp.dot(a.astype(BF16), b.astype(BF16), preferred_element_type=F32)


def _dot_nt(a, b):
    return lax.dot_general(a.astype(BF16), b.astype(BF16), (((1,), (1,)), ((), ())), preferred_element_type=F32)


def _dot_tn(a, b):
    return lax.dot_general(a.astype(BF16), b.astype(BF16), (((0,), (0,)), ((), ())), preferred_element_type=F32)


def _split3(x):
    hi = x.astype(BF16)
    r1 = x - hi.astype(F32)
    mid = r1.astype(BF16)
    lo = (r1 - mid.astype(F32)).astype(BF16)
    return hi, mid, lo


def _dot_exact_lhs(m, x):
    mb = m.astype(BF16)
    hi, mid, lo = _split3(x)
    return (jnp.dot(mb, hi, preferred_element_type=F32) + jnp.dot(mb, mid, preferred_element_type=F32)
            + jnp.dot(mb, lo, preferred_element_type=F32))


def _dot_exact_rhs(x, m):
    mb = m.astype(BF16)
    hi, mid, lo = _split3(x)
    return (jnp.dot(hi, mb, preferred_element_type=F32) + jnp.dot(mid, mb, preferred_element_type=F32)
            + jnp.dot(lo, mb, preferred_element_type=F32))


def _dot_stat_rhs(x, m):
    mb = m.astype(BF16)
    hi = x.astype(BF16)
    lo = (x - hi.astype(F32)).astype(BF16)
    return jnp.dot(hi, mb, preferred_element_type=F32) + jnp.dot(lo, mb, preferred_element_type=F32)


def _sigmoid(x):
    return 1.0 / (1.0 + jnp.exp(-x))


def _silu(x):
    return x * _sigmoid(x)


def _iota2(shape, axis):
    return lax.broadcasted_iota(jnp.int32, shape, axis)


def _chunk_tril(n, chunk):
    r = _iota2((n, n), 0)
    c = _iota2((n, n), 1)
    return jnp.where((c <= r) & ((r // chunk) == (c // chunk)), 1.0, 0.0)


def _head_block(n, width, value):
    r = _iota2((n, n), 0)
    c = _iota2((n, n), 1)
    return jnp.where((r // width) == (c // width), value, 0.0)


def _mm_kernel(x_ref, w_ref, o_ref):
    o_ref[...] = _dot(x_ref[...], w_ref[...]).astype(o_ref.dtype)


def _matmul(x, w, out_dtype, tm, tn):
    S, K = x.shape
    N = w.shape[1]
    return pl.pallas_call(
        _mm_kernel,
        grid=(S // tm, N // tn),
        in_specs=[pl.BlockSpec((tm, K), lambda i, j: (i, 0)),
                  pl.BlockSpec((K, tn), lambda i, j: (0, j))],
        out_specs=pl.BlockSpec((tm, tn), lambda i, j: (i, j)),
        out_shape=jax.ShapeDtypeStruct((S, N), out_dtype),
        compiler_params=_cparams(("parallel", "arbitrary")),
        name="matmul",
    )(x, w)


def _layer_norm_rows(y, g, b):
    mu = jnp.mean(y, -1, keepdims=True)
    d = y - mu
    var = jnp.mean(d * d, -1, keepdims=True)
    return d * lax.rsqrt(var + LN_EPS) * g + b


def _mm_ln_kernel(transposed, *refs):
    n_in = len(transposed)
    a_refs = refs[:n_in]
    w_refs = refs[n_in:2 * n_in]
    x_ref, g_ref, b_ref, o_ref = refs[2 * n_in:]
    acc = DN_ALPHA * x_ref[...]
    for a_ref, w_ref, tr in zip(a_refs, w_refs, transposed):
        acc = acc + (_dot_tn if tr else _dot)(a_ref[...], w_ref[...])
    o_ref[...] = _layer_norm_rows(acc, g_ref[...], b_ref[...])


def _matmul_ln(a_list, w_list, x, g, b, tm, transposed=None):
    S, D = x.shape
    transposed = tuple(transposed or (False,) * len(a_list))
    in_specs = ([pl.BlockSpec((a.shape[0], tm), lambda i: (0, i)) if tr else
                 pl.BlockSpec((tm, a.shape[1]), lambda i: (i, 0)) for a, tr in zip(a_list, transposed)]
                + [pl.BlockSpec(w.shape, lambda i: (0, 0)) for w in w_list]
                + [pl.BlockSpec((tm, D), lambda i: (i, 0)),
                   pl.BlockSpec((1, D), lambda i: (0, 0)),
                   pl.BlockSpec((1, D), lambda i: (0, 0))])
    return pl.pallas_call(
        functools.partial(_mm_ln_kernel, transposed),
        grid=(S // tm,),
        in_specs=in_specs,
        out_specs=pl.BlockSpec((tm, D), lambda i: (i, 0)),
        out_shape=jax.ShapeDtypeStruct((S, D), F32),
        compiler_params=_cparams(("parallel",)),
        name="matmul_ln",
    )(*a_list, *w_list, x, g.reshape(1, D), b.reshape(1, D))


def _gla_kernel(p_ref, wa2_ref, ba_ref, ng_ref, o_ref, st_ref, o_scr):
    C, H, dk, dv = GLA_CHUNK, GLA_HEADS, GLA_DK, GLA_DV
    tb = p_ref.shape[0]

    @pl.when(pl.program_id(0) == 0)
    def _():
        st_ref[...] = jnp.zeros_like(st_ref)

    z = _dot(p_ref[:, 2 * GLA_QK + 2 * GLA_V:], wa2_ref[...]) + ba_ref[...]
    log_a = -(jnp.maximum(-z, 0.0) + jnp.log(1.0 + jnp.exp(-jnp.abs(z)))) / GLA_TAU
    b = _dot_exact_lhs(_chunk_tril(tb, C), log_a)
    causal = _iota2((C, C), 1) <= _iota2((C, C), 0)

    nc = tb // C
    q_h, k_h, ke_h, v_h, dec_h = [], [], [], [], []
    for c in range(nc):
        rows = slice(c * C, (c + 1) * C)
        b_c = b[rows]
        b_last = b_c[C - 1:C]
        q_dec = p_ref[rows, 0:GLA_QK] * (dk ** -0.5) * jnp.exp(b_c)
        k_c = p_ref[rows, GLA_QK:2 * GLA_QK]
        k_dec = k_c * jnp.exp(-b_c)
        k_end = k_c * jnp.exp(b_last - b_c)
        decay = jnp.exp(b_last)
        for h in range(H):
            ks = slice(h * dk, (h + 1) * dk)
            q_h.append(q_dec[:, ks])
            k_h.append(k_dec[:, ks])
            ke_h.append(k_end[:, ks])
            dec_h.append(decay[:, ks])
            v_h.append(p_ref[rows, 2 * GLA_QK + h * dv:2 * GLA_QK + (h + 1) * dv])
    n = nc * H
    attn = [jnp.where(causal, _dot_nt(q_h[i], k_h[i]), 0.0) for i in range(n)]
    kv = [_dot_tn(v_h[i], ke_h[i]) for i in range(n)]
    intra = [_dot(attn[i], v_h[i]) for i in range(n)]
    state = [st_ref[:, h * dk:(h + 1) * dk] for h in range(H)]
    entering = []
    for i in range(n):
        entering.append(state[i % H])
        state[i % H] = state[i % H] * dec_h[i] + kv[i]
    for i in range(n):
        c, h = divmod(i, H)
        o_scr[c * C:(c + 1) * C, h * dv:(h + 1) * dv] = intra[i] + _dot_nt(q_h[i], entering[i])
    for h in range(H):
        st_ref[:, h * dk:(h + 1) * dk] = state[h]

    for h in range(H):
        vs = slice(h * dv, (h + 1) * dv)
        o_h = o_scr[:, vs]
        g_h = p_ref[:, 2 * GLA_QK + GLA_V + h * dv:2 * GLA_QK + GLA_V + (h + 1) * dv]
        o_h = o_h * lax.rsqrt(jnp.mean(o_h * o_h, -1, keepdims=True) + 1e-5) * ng_ref[:, vs]
        o_ref[:, vs] = (o_h * _silu(g_h)).astype(o_ref.dtype)


def _gla_call(p_gla, wa2p, ba, norm_g, tb=SCAN_TILE):
    S = p_gla.shape[0]
    return pl.pallas_call(
        _gla_kernel,
        grid=(S // tb,),
        in_specs=[pl.BlockSpec((tb, GLA_COLS), lambda i: (i, 0)),
                  pl.BlockSpec((128, GLA_QK), lambda i: (0, 0)),
                  pl.BlockSpec((1, GLA_QK), lambda i: (0, 0)),
                  pl.BlockSpec((1, GLA_V), lambda i: (0, 0))],
        out_specs=pl.BlockSpec((tb, GLA_V), lambda i: (i, 0)),
        out_shape=jax.ShapeDtypeStruct((S, GLA_V), BF16),
        scratch_shapes=[pltpu.VMEM((GLA_DV, GLA_QK), F32), pltpu.VMEM((tb, GLA_V), F32)],
        compiler_params=_cparams(("arbitrary",)),
        name="gla",
    )(p_gla, wa2p, ba.reshape(1, GLA_QK), norm_g.reshape(1, GLA_V))


def _rwkv_kernel(p_ref, mu_ref, w0_ref, w2_ref, a0_ref, a2_ref, g2_ref, kk_ref, ka_ref, rk_ref, lnw_ref, lnb_ref,
                 o_ref, prev_ref, h_ref, o_scr):
    C, H, N, D = RWKV_CHUNK, RWKV_HEADS, RWKV_HD, RWKV_D
    tb = p_ref.shape[0]
    first = pl.program_id(0) == 0

    @pl.when(first)
    def _():
        prev_ref[...] = jnp.zeros_like(prev_ref)
        h_ref[...] = jnp.zeros_like(h_ref)

    p = p_ref[...]
    shifted = jnp.where(_iota2(p.shape, 0) == 0, prev_ref[...], pltpu.roll(p, 1, 0))
    prev_ref[...] = p[tb - 1:tb]
    p = p + mu_ref[...] * (shifted - p)
    r = p[:, 0:D]
    k = p[:, D:2 * D]
    v = p[:, 2 * D:3 * D]
    xw = p[:, 3 * D:3 * D + 64]
    xa = p[:, 3 * D + 64:3 * D + 128]
    xg = p[:, 3 * D + 128:3 * D + 256]
    lw = -RWKV_DECAY_SCALE * _sigmoid(w0_ref[...] + _dot(jnp.tanh(xw), w2_ref[...]))
    a = _sigmoid(a0_ref[...] + _dot(xa, a2_ref[...]))
    g = _dot(_sigmoid(xg), g2_ref[...])
    head_ones = _head_block(D, N, 1.0)
    kk = k * kk_ref[...]
    kk = kk * lax.rsqrt(jnp.maximum(_dot_stat_rhs(kk * kk, head_ones), 1e-24))
    k = k * (1.0 + (a - 1.0) * ka_ref[...])
    pv = -kk * a
    cw = _dot_exact_lhs(_chunk_tril(tb, C), lw)
    cwx = cw - lw

    W2 = 2 * N
    left = _iota2((C, W2), 1) < N

    def blockdiag(a):
        return jnp.concatenate([jnp.where(left, a, 0.0), jnp.where(left, 0.0, a)], axis=0)

    def diag_blocks(full):
        return jnp.where(left, full[0:C], full[C:2 * C])

    gi = _iota2((2 * C, 2 * W2), 0)
    gj = _iota2((2 * C, 2 * W2), 1) % N
    gram_mask = ((gi < C) & (gj < gi)) | ((gi >= C) & (gj <= gi - C))
    eye = _iota2((C, W2), 0) == _iota2((C, W2), 1) % N
    eye_f = jnp.where(eye, 1.0, 0.0)
    zeros_c = jnp.zeros((C, W2), F32)
    zeros_w = jnp.zeros((W2, W2), F32)

    nc = tb // C
    items = [(c, p) for c in range(nc) for p in range(H // 2)]
    xs, ys, pk_e, b_h, v_h, r_h, g_h = [], [], [], [], [], [], []
    for c in range(nc):
        rows = slice(c * C, (c + 1) * C)
        cw_c = cw[rows]
        cw_end = cw_c[C - 1:C]
        e_pos = jnp.exp(cw_c)
        e_neg = jnp.exp(-cw_c)
        e_end = jnp.exp(cw_end - cw_c)
        r_t = r[rows] * e_pos
        b_t = kk[rows] * jnp.exp(cwx[rows])
        p_t = pv[rows] * e_neg
        k_t = k[rows] * e_neg
        p_e = pv[rows] * e_end
        k_e = k[rows] * e_end
        g_end = jnp.exp(cw_end)
        v_c = v[rows]
        for p in range(H // 2):
            ps = slice(p * W2, (p + 1) * W2)
            xs.append(jnp.concatenate([b_t[:, ps], r_t[:, ps]], axis=0))
            ys.append(jnp.concatenate([blockdiag(p_t[:, ps]), blockdiag(k_t[:, ps])], axis=0))
            pk_e.append(jnp.concatenate([p_e[:, ps], k_e[:, ps]], axis=0))
            b_h.append(b_t[:, ps])
            v_h.append(v_c[:, ps])
            r_h.append(r_t[:, ps])
            g_h.append(g_end[:, ps])
    n = len(items)
    grams = [jnp.where(gram_mask, _dot_nt(xs[i], ys[i]), 0.0) for i in range(n)]
    l_p = [g[0:C, 0:W2] for g in grams]
    m_pk = [g[C:2 * C, :] for g in grams]
    v_bd = [blockdiag(v_h[i]) for i in range(n)]
    lkv = [_dot(grams[i][0:C, W2:2 * W2], v_bd[i]) for i in range(n)]
    x = [_dot(lp, blockdiag(lp)) for lp in l_p]
    t = [eye_f + lp for lp in l_p]
    for _ in range(4):
        tx = [_dot(jnp.concatenate([t[i], x[i]], axis=0), blockdiag(x[i])) for i in range(n)]
        t = [t[i] + tx[i][0:C] for i in range(n)]
        x = [tx[i][C:2 * C] for i in range(n)]
    t = [t[i] + _dot(t[i], blockdiag(x[i])) for i in range(n)]
    wu = [_dot(t[i], jnp.concatenate([blockdiag(b_h[i]), blockdiag(lkv[i])], axis=1)) for i in range(n)]
    az = [_dot_tn(pk_e[i], jnp.concatenate([wu[i], jnp.concatenate([zeros_c, v_h[i]], axis=1)], axis=0))
          for i in range(n)]
    qo = [_dot(m_pk[i], jnp.concatenate(
        [jnp.concatenate([blockdiag(wu[i][:, 0:W2]), blockdiag(wu[i][:, W2:2 * W2])], axis=1),
         jnp.concatenate([zeros_w, v_bd[i]], axis=1)], axis=0)) for i in range(n)]
    state = [h_ref[p] for p in range(H // 2)]
    for i, (c, p) in enumerate(items):
        a_mat = diag_blocks(az[i][:, 0:W2]) + jnp.where(eye, g_h[i], 0.0)
        q_mat = qo[i][:, 0:W2] + r_h[i]
        oh = _dot(jnp.concatenate([q_mat, a_mat], axis=0), blockdiag(state[p]))
        o_scr[c * C:(c + 1) * C, p * W2:(p + 1) * W2] = oh[0:C] + qo[i][:, W2:2 * W2]
        state[p] = oh[C:C + N] + diag_blocks(az[i][:, W2:2 * W2])
    for p in range(H // 2):
        h_ref[p] = state[p]

    o = o_scr[...]
    head_mean = _head_block(D, N, 1.0 / N)
    mean = _dot_stat_rhs(o, head_mean)
    d = o - mean
    var = _dot_stat_rhs(d * d, head_mean)
    o = d * lax.rsqrt(var + RWKV_GN_EPS) * lnw_ref[...] + lnb_ref[...]
    bonus = _dot_stat_rhs(r * k * rk_ref[...], head_ones) * v
    o_ref[...] = ((o + bonus) * g).astype(o_ref.dtype)


def _rwkv_call(p_rwkv, mu, w0, w2, a0, a2, g2, k_k, k_a, r_k, ln_w, ln_b, tb=SCAN_TILE):
    S = p_rwkv.shape[0]
    D = RWKV_D
    row = lambda t: t.reshape(1, -1).astype(F32)
    full = lambda shape: pl.BlockSpec(shape, lambda i: tuple(0 for _ in shape))
    return pl.pallas_call(
        _rwkv_kernel,
        grid=(S // tb,),
        in_specs=[pl.BlockSpec((tb, RWKV_COLS), lambda i: (i, 0)),
                  full((1, RWKV_COLS)), full((1, D)), full((64, D)), full((1, D)), full((64, D)), full((128, D)),
                  full((1, D)), full((1, D)), full((1, D)), full((1, D)), full((1, D))],
        out_specs=pl.BlockSpec((tb, D), lambda i: (i, 0)),
        out_shape=jax.ShapeDtypeStruct((S, D), BF16),
        scratch_shapes=[pltpu.VMEM((1, RWKV_COLS), F32),
                        pltpu.VMEM((RWKV_HEADS // 2, RWKV_HD, 2 * RWKV_HD), F32),
                        pltpu.VMEM((tb, D), F32)],
        compiler_params=_cparams(("arbitrary",)),
        name="rwkv7",
    )(p_rwkv, row(mu), row(w0), w2.astype(BF16), row(a0), a2.astype(BF16), g2.astype(BF16),
      row(k_k), row(k_a), row(r_k), row(ln_w), row(ln_b))


XATTN_HEADS, XATTN_HD = 4, 256


def _xattn_kernel(x_ref, wq_ref, k_ref, v_ref, wo_ref, g_ref, b_ref, o_ref):
    x = x_ref[...]
    q = _dot(x, wq_ref[...])
    outs = []
    for h in range(XATTN_HEADS):
        hs = slice(h * XATTN_HD, (h + 1) * XATTN_HD)
        s = _dot_nt(q[:, hs], k_ref[:, hs])
        e = jnp.exp(s - jnp.max(s, -1, keepdims=True))
        p = e / jnp.sum(e, -1, keepdims=True)
        outs.append(_dot(p, v_ref[:, hs]))
    o = jnp.concatenate(outs, axis=1)
    y = DN_ALPHA * x + _dot(o, wo_ref[...])
    o_ref[...] = _layer_norm_rows(y, g_ref[...], b_ref[...])


def _xattn_call(x, wq_scaled, k_mem, v_mem, wo, g, b, tm):
    S, D = x.shape
    M = k_mem.shape[0]
    const = lambda shape: pl.BlockSpec(shape, lambda i: (0, 0))
    return pl.pallas_call(
        _xattn_kernel,
        grid=(S // tm,),
        in_specs=[pl.BlockSpec((tm, D), lambda i: (i, 0)), const((D, D)), const((M, D)), const((M, D)),
                  const((D, D)), const((1, D)), const((1, D))],
        out_specs=pl.BlockSpec((tm, D), lambda i: (i, 0)),
        out_shape=jax.ShapeDtypeStruct((S, D), F32),
        compiler_params=_cparams(("parallel",)),
        name="xattn",
    )(x, wq_scaled, k_mem, v_mem, wo, g.reshape(1, D), b.reshape(1, D))


def _ffn_kernel(x_ref, wg_ref, wu_ref, wd_ref, g_ref, b_ref, o_ref, acc_ref):
    j = pl.program_id(1)
    x = x_ref[...]
    xb = x.astype(BF16)
    part = _dot(_silu(_dot(xb, wg_ref[...])) * _dot(xb, wu_ref[...]), wd_ref[...])

    last = pl.num_programs(1) - 1

    @pl.when(j == 0)
    def _():
        acc_ref[...] = DN_ALPHA * x + part

    @pl.when((j > 0) & (j < last))
    def _():
        acc_ref[...] += part

    @pl.when(j == last)
    def _():
        o_ref[...] = _layer_norm_rows(acc_ref[...] + part, g_ref[...], b_ref[...])


def _ffn_call(x, wg, wu, wd, g, b, tm, tf):
    S, D = x.shape
    F = wg.shape[1]
    assert F % tf == 0 and F // tf >= 2
    return pl.pallas_call(
        _ffn_kernel,
        grid=(S // tm, F // tf),
        in_specs=[pl.BlockSpec((tm, D), lambda i, j: (i, 0)),
                  pl.BlockSpec((D, tf), lambda i, j: (0, j)),
                  pl.BlockSpec((D, tf), lambda i, j: (0, j)),
                  pl.BlockSpec((tf, D), lambda i, j: (j, 0)),
                  pl.BlockSpec((1, D), lambda i, j: (0, 0)),
                  pl.BlockSpec((1, D), lambda i, j: (0, 0))],
        out_specs=pl.BlockSpec((tm, D), lambda i, j: (i, 0)),
        out_shape=jax.ShapeDtypeStruct((S, D), F32),
        scratch_shapes=[pltpu.VMEM((tm, D), F32)],
        compiler_params=_cparams(("parallel", "arbitrary")),
        name="ffn",
    )(x, wg, wu, wd, g.reshape(1, D), b.reshape(1, D))


SSD_HD, SSD_HEADS, SSD_INNER, SSD_GROUPS, SSD_STATE = 64, 16, 1024, 2, 128
SSD_BC, SSD_CONV, SSD_CONV_CH, SSD_CHUNK = 256, 4, 1536, 128
SSD_COLS = SSD_INNER + SSD_CONV_CH
SSD_GW = SSD_INNER // SSD_GROUPS


def _softplus(x):
    return jnp.maximum(x, 0.0) + jnp.log(1.0 + jnp.exp(-jnp.abs(x)))


def _ssd_kernel(p_ref, cw_ref, cb_ref, dtb_ref, a_ref, dsk_ref, ng_ref, o_ref, prev_ref, st_ref, y_scr):
    L, G, NS, HD = SSD_CHUNK, SSD_GROUPS, SSD_STATE, SSD_HD
    HG = SSD_HEADS // G

    @pl.when(pl.program_id(0) == 0)
    def _():
        prev_ref[...] = jnp.zeros_like(prev_ref)
        st_ref[...] = jnp.zeros_like(st_ref)

    cur = p_ref[:, SSD_INNER:SSD_COLS]
    tail = prev_ref[...]
    row = _iota2(tail.shape, 0)
    conv = cur * cw_ref[SSD_CONV - 1:SSD_CONV, :] + cb_ref[...]
    for kk in range(1, SSD_CONV):
        rolled = pltpu.roll(cur, kk, 0)
        head = jnp.where(row < kk, pltpu.roll(tail, kk, 0), rolled[0:8])
        shifted = jnp.concatenate([head, rolled[8:]], axis=0)
        conv = conv + shifted * cw_ref[SSD_CONV - 1 - kk:SSD_CONV - kk, :]
    prev_ref[...] = cur[L - 8:L]
    xbc = _silu(conv)
    xs = xbc[:, :SSD_INNER]

    dt = _softplus(p_ref[:, SSD_COLS:] + dtb_ref[...])
    a_col = dt * a_ref[...]
    li = _iota2((L, L), 0)
    lj = _iota2((L, L), 1)
    cs = _dot_exact_lhs(jnp.where(lj <= li, 1.0, 0.0), a_col)
    cs_row = cs.T
    expand = jnp.where(_iota2((128, SSD_INNER), 1) // HD == _iota2((128, SSD_INNER), 0), 1.0, 0.0)
    dt_x = _dot_stat_rhs(dt, expand)
    cs_x = _dot_exact_rhs(cs, expand)
    cs_end = cs_x[L - 1:L]
    xd = xs * dt_x
    xd_dec = xd * jnp.exp(cs_end - cs_x)
    out_dec = jnp.exp(cs_x)
    chunk_dec = jnp.exp(cs_end)
    tril = lj <= li

    for g in range(G):
        gs = slice(g * SSD_GW, (g + 1) * SSD_GW)
        b_g = xbc[:, SSD_INNER + g * NS:SSD_INNER + (g + 1) * NS]
        c_g = xbc[:, SSD_INNER + SSD_BC + g * NS:SSD_INNER + SSD_BC + (g + 1) * NS]
        cb = _dot_nt(c_g, b_g)
        for j in range(HG):
            h = g * HG + j
            hs = slice(h * HD, (h + 1) * HD)
            seg = jnp.where(tril, jnp.exp(cs[:, h:h + 1] - cs_row[h:h + 1, :]), 0.0)
            y_scr[:, hs] = _dot(cb * seg, xd[:, hs])
        st = st_ref[g]
        y_off = _dot(c_g, st) * out_dec[:, gs]
        st_ref[g] = st * chunk_dec[:, gs] + _dot_tn(b_g, xd_dec[:, gs])
        y_scr[:, gs] = y_scr[:, gs] + y_off

    y = (y_scr[...] + dsk_ref[...] * xs) * _silu(p_ref[:, :SSD_INNER])
    for g in range(G):
        gs = slice(g * SSD_GW, (g + 1) * SSD_GW)
        y_g = y[:, gs]
        o_ref[:, gs] = (y_g * lax.rsqrt(jnp.mean(y_g * y_g, -1, keepdims=True) + 1e-5) * ng_ref[:, gs]).astype(o_ref.dtype)


def _ssd_call(p_ssd, conv_w, conv_b, dt_bias, a_log, d_skip, norm_g):
    S = p_ssd.shape[0]
    L = SSD_CHUNK
    a_neg = -jnp.exp(a_log.astype(F32))
    pad = lambda t: jnp.pad(t.astype(F32), (0, 128 - SSD_HEADS)).reshape(1, 128)
    const = lambda shape: pl.BlockSpec(shape, lambda i: (0, 0))
    return pl.pallas_call(
        _ssd_kernel,
        grid=(S // L,),
        in_specs=[pl.BlockSpec((L, SSD_COLS + 128), lambda i: (i, 0)),
                  const((SSD_CONV, SSD_CONV_CH)), const((1, SSD_CONV_CH)),
                  const((1, 128)), const((1, 128)),
                  const((1, SSD_INNER)), const((1, SSD_INNER))],
        out_specs=pl.BlockSpec((L, SSD_INNER), lambda i: (i, 0)),
        out_shape=jax.ShapeDtypeStruct((S, SSD_INNER), BF16),
        scratch_shapes=[pltpu.VMEM((8, SSD_CONV_CH), F32),
                        pltpu.VMEM((SSD_GROUPS, SSD_STATE, SSD_GW), F32),
                        pltpu.VMEM((L, SSD_INNER), F32)],
        compiler_params=_cparams(("arbitrary",)),
        name="ssd",
    )(p_ssd, conv_w.astype(F32), conv_b.reshape(1, -1).astype(F32), pad(dt_bias), pad(a_neg),
      jnp.repeat(d_skip.astype(F32), SSD_HD).reshape(1, -1), norm_g.reshape(1, -1).astype(F32))


MOBA_HD, MOBA_HEADS, MOBA_D, MOBA_BLOCK, MOBA_TOPK = 64, 8, 512, 256, 3
MOBA_GROUP = 4
MOBA_UNDERFLOW = -160.0
MOBA_BOUND_SLACK = 1.001
NEG_BIG = -1e30
LOG2E = math.log2(math.e)


MOBA_VROWS = MOBA_HD + 16


def _moba_proj_kernel(x_ref, wqt_ref, wk_ref, wvt_ref, qt_ref, k_ref, kmean_ref, v_ref, knorm_ref, qnorm_ref):
    xb = x_ref[...].astype(BF16)
    qt = _dot_nt(wqt_ref[...], xb).astype(qt_ref.dtype)
    qt_ref[...] = qt
    vt = _dot_nt(wvt_ref[...], xb)
    extra = jnp.where(_iota2((MOBA_VROWS - MOBA_HD, MOBA_BLOCK), 0) == 0, 1.0, 0.0)
    for h in range(MOBA_HEADS):
        v_ref[h, 0] = jnp.concatenate([vt[h * MOBA_HD:(h + 1) * MOBA_HD], extra], axis=0).astype(v_ref.dtype)
    k = _dot(xb, wk_ref[...])
    kmean_ref[0] = jnp.mean(k, 0, keepdims=True)
    shape = (MOBA_BLOCK, 2 * MOBA_HD)
    lane = _iota2(shape, 1)
    pos = jnp.where((lane == MOBA_HD) | (lane == MOBA_HD + 1), _iota2(shape, 0).astype(F32), 0.0).astype(k_ref.dtype)
    k_b = k.astype(k_ref.dtype)
    for h in range(MOBA_HEADS):
        k_ref[h, 0] = pos
        k_ref[h, 0, :, 0:MOBA_HD] = k_b[:, h * MOBA_HD:(h + 1) * MOBA_HD]
    k_f = k_b.astype(F32)
    q_f = qt.astype(F32)
    head_cols = jnp.where(_iota2((MOBA_D, 128), 0) // MOBA_HD == _iota2((MOBA_D, 128), 1), 1.0, 0.0)
    head_rows = jnp.where(_iota2((128, MOBA_D), 1) // MOBA_HD == _iota2((128, MOBA_D), 0), 1.0, 0.0)
    knorm_ref[0] = jnp.sqrt(jnp.max(_dot_stat_rhs(k_f * k_f, head_cols), 0, keepdims=True))
    qnorm_ref[0] = jnp.sqrt(jnp.max(_dot_exact_lhs(head_rows, q_f * q_f), 1, keepdims=True))


def _moba_proj_call(x, wqt, wk, wvt):
    S, D = x.shape
    nb = S // MOBA_BLOCK
    const = lambda shape: pl.BlockSpec(shape, lambda i: (0, 0))
    return pl.pallas_call(
        _moba_proj_kernel,
        grid=(nb,),
        in_specs=[pl.BlockSpec((MOBA_BLOCK, D), lambda i: (i, 0)), const((MOBA_D, D)), const((D, MOBA_D)),
                  const((MOBA_D, D))],
        out_specs=[pl.BlockSpec((MOBA_D, MOBA_BLOCK), lambda i: (0, i)),
                   pl.BlockSpec((MOBA_HEADS, 1, MOBA_BLOCK, 2 * MOBA_HD), lambda i: (0, i, 0, 0)),
                   pl.BlockSpec((1, 1, MOBA_D), lambda i: (i, 0, 0)),
                   pl.BlockSpec((MOBA_HEADS, 1, MOBA_VROWS, MOBA_BLOCK), lambda i: (0, i, 0, 0)),
                   pl.BlockSpec((1, 1, 128), lambda i: (i, 0, 0)),
                   pl.BlockSpec((1, 128, 1), lambda i: (i, 0, 0))],
        out_shape=[jax.ShapeDtypeStruct((MOBA_D, S), BF16),
                   jax.ShapeDtypeStruct((MOBA_HEADS, nb, MOBA_BLOCK, 2 * MOBA_HD), BF16),
                   jax.ShapeDtypeStruct((nb, 1, MOBA_D), F32),
                   jax.ShapeDtypeStruct((MOBA_HEADS, nb, MOBA_VROWS, MOBA_BLOCK), BF16),
                   jax.ShapeDtypeStruct((nb, 1, 128), F32),
                   jax.ShapeDtypeStruct((nb, 128, 1), F32)],
        compiler_params=_cparams(("parallel",)),
        name="moba_proj",
    )(x, wqt, wk, wvt)


def _moba_kernel(first_ref, qt_ref, k_ref, vt_ref, kmean_ref, o_ref, sel_ref, s0_ref, s1_ref, p0_ref, p1_ref):
    BS, HD = MOBA_BLOCK, MOBA_HD
    h = pl.program_id(0)
    i = pl.program_id(1)
    nb = k_ref.shape[0]
    qt = qt_ref[...]
    slope = LOG2E * jnp.exp2(jnp.zeros((1, BS), F32) - (h + 1).astype(F32) * (8.0 / MOBA_HEADS))
    slope_hi = slope.astype(qt.dtype).astype(F32)
    slope_lo = slope - slope_hi
    row = _iota2((HD, BS), 0)
    qt_ext = jnp.concatenate(
        [qt, jnp.where(row == 0, slope_hi, jnp.where(row == 1, slope_lo, 0.0)).astype(qt.dtype)], axis=0)

    gate = _dot(kmean_ref[...], qt)
    blk = _iota2((nb, BS), 0).astype(F32)
    cand = blk < i.astype(F32)
    sel = jnp.zeros((nb, BS), F32)
    for _ in range(MOBA_TOPK):
        best = jnp.max(jnp.where(cand, gate, -jnp.inf), 0, keepdims=True)
        idx = jnp.min(jnp.where(cand & (gate == best), blk, float(nb)), 0, keepdims=True)
        pick = blk == idx
        sel = jnp.where(pick, 1.0, sel)
        cand = cand & jnp.logical_not(pick)
    sel_ref[...] = sel

    G = MOBA_GROUP
    last = nb - 1
    s_ref = (s0_ref, s1_ref)
    p_ref = (p0_ref, p1_ref)

    first = first_ref[h * nb + i]

    def group(u):
        return [jnp.clip(first + G * u + x, 0, last) for x in range(G)]

    def issue_scores(u, slot):
        tops = []
        for x, j in enumerate(group(u)):
            sc = _dot(k_ref[j], qt_ext)
            s_ref[slot][x] = sc
            tops.append(jnp.max(sc, 0, keepdims=True))
        return tuple(tops)

    def value_blocks(u):
        js = group(u)
        js[0] = jnp.where(u == -1, i, js[0])
        return js

    def weighted_values(u, slot):
        out = None
        for x, j in enumerate(value_blocks(u)):
            part = _dot(vt_ref[j], p_ref[slot][x])
            out = part if out is None else out + part
        return out

    def step(u, slot, tops, a_prev, m, acc, issue_next=True):
        on, shift = [], []
        m_new = m
        for x, j in enumerate(group(u)):
            on.append(sel_ref[pl.ds(j, 1), :] > 0.0)
            shift.append(slope * ((j - i) * BS).astype(F32))
            m_new = jnp.maximum(m_new, jnp.where(on[x], tops[x] + shift[x], NEG_BIG))
        tops_next, pv = [], None
        for x, (j_prev, j_next) in enumerate(zip(value_blocks(u - 1), group(u + 1))):
            part = _dot(vt_ref[j_prev], p_ref[1 - slot][x])
            pv = part if pv is None else pv + part
            if issue_next:
                sc = _dot(k_ref[j_next], qt_ext)
                s_ref[1 - slot][x] = sc
                tops_next.append(jnp.max(sc, 0, keepdims=True))
            p = jnp.exp2(s_ref[slot][x] - (jnp.where(on[x], m_new, -NEG_BIG) - shift[x]))
            p_ref[slot][x] = p.astype(BF16)
        return tuple(tops_next), jnp.exp2(m - m_new), m_new, a_prev * acc + pv

    def body(w, carry):
        carry = step(2 * w, 0, *carry)
        return step(2 * w + 1, 1, *carry)

    tops0 = issue_scores(0, 0)
    s_own = jnp.where(_iota2((BS, BS), 1) >= _iota2((BS, BS), 0), _dot(k_ref[i], qt_ext), NEG_BIG)
    m_own = jnp.max(s_own, 0, keepdims=True)
    p1_ref[1:G] = jnp.zeros((G - 1, BS, BS), BF16)
    p1_ref[0] = jnp.exp2(s_own - m_own).astype(BF16)
    init = (tops0, jnp.ones((1, BS), F32), m_own, jnp.zeros((vt_ref.shape[1], BS), F32))
    steps = (i - first + G - 1) // G
    pairs = steps // 2
    carry = lax.fori_loop(0, pairs, body, init)

    def odd_tail(carry):
        _, a_last, _, acc = step(2 * pairs, 0, *carry, issue_next=False)
        return a_last * acc + weighted_values(2 * pairs, 0)

    def even_tail(carry):
        _, a_prev, _, acc = carry
        return a_prev * acc + weighted_values(2 * pairs - 1, 1)

    acc = lax.cond(steps % 2 == 1, odd_tail, even_tail, carry)
    o_ref[...] = (acc[0:HD] / acc[HD:HD + 1]).astype(o_ref.dtype)


def _moba_first_block(knorm, qnorm, nb):
    BS = MOBA_BLOCK
    k_norm = knorm[:, 0, 0:MOBA_HEADS].T
    q_norm = qnorm[:, 0:MOBA_HEADS, 0].T
    slope = LOG2E * jnp.exp2(-(jnp.arange(MOBA_HEADS, dtype=F32) + 1.0) * (8.0 / MOBA_HEADS))
    i_idx = jnp.arange(nb, dtype=F32)[None, :, None]
    j_idx = jnp.arange(nb, dtype=F32)[None, None, :]
    reach = slope[:, None, None] * ((BS - 1.0) - BS * (i_idx - j_idx))
    bound = q_norm[:, :, None] * (k_norm[:, None, :] + k_norm[:, :, None]) * MOBA_BOUND_SLACK + reach
    matters = (bound >= MOBA_UNDERFLOW) & (j_idx < i_idx)
    first = jnp.min(jnp.where(matters, j_idx, i_idx), axis=2).astype(jnp.int32)
    return first.reshape(-1)


def _moba_call(first_block, qt, k4, vt4, kmean):
    S = qt.shape[1]
    nb = S // MOBA_BLOCK
    return pl.pallas_call(
        _moba_kernel,
        grid_spec=pltpu.PrefetchScalarGridSpec(
            num_scalar_prefetch=1,
            grid=(MOBA_HEADS, nb),
            in_specs=[pl.BlockSpec((MOBA_HD, MOBA_BLOCK), lambda h, i, first: (h, i)),
                      pl.BlockSpec((None, nb, MOBA_BLOCK, 2 * MOBA_HD), lambda h, i, first: (h, 0, 0, 0)),
                      pl.BlockSpec((None, nb, MOBA_VROWS, MOBA_BLOCK), lambda h, i, first: (h, 0, 0, 0)),
                      pl.BlockSpec((None, nb, MOBA_HD), lambda h, i, first: (h, 0, 0))],
            out_specs=pl.BlockSpec((MOBA_HD, MOBA_BLOCK), lambda h, i, first: (h, i)),
            scratch_shapes=[pltpu.VMEM((nb, MOBA_BLOCK), F32)]
            + [pltpu.VMEM((MOBA_GROUP, MOBA_BLOCK, MOBA_BLOCK), F32)] * 2
            + [pltpu.VMEM((MOBA_GROUP, MOBA_BLOCK, MOBA_BLOCK), BF16)] * 2),
        out_shape=jax.ShapeDtypeStruct((MOBA_D, S), BF16),
        compiler_params=_cparams(("parallel", "arbitrary")),
        name="moba",
    )(first_block, qt, k4, vt4, kmean)


N_EXPERTS, TOP_K, EXPERT_FF = 8, 2, 2816
MOE_ROWS = 512


def _router_kernel(x_ref, whi_ref, wlo_ref, b_ref, o_ref):
    x = x_ref[...]
    xhi = x.astype(BF16)
    xlo = (x - xhi.astype(F32)).astype(BF16)
    logits = (jnp.dot(xhi, whi_ref[...], preferred_element_type=F32)
              + jnp.dot(xhi, wlo_ref[...], preferred_element_type=F32)
              + jnp.dot(xlo, whi_ref[...], preferred_element_type=F32)) + b_ref[...]
    lane = _iota2(logits.shape, 1).astype(F32)
    logits = jnp.where(lane < N_EXPERTS, logits, -jnp.inf)
    m1 = jnp.max(logits, -1, keepdims=True)
    i1 = jnp.min(jnp.where(logits == m1, lane, 128.0), -1, keepdims=True)
    rest = jnp.where(lane == i1, -jnp.inf, logits)
    m2 = jnp.max(rest, -1, keepdims=True)
    i2 = jnp.min(jnp.where(rest == m2, lane, 128.0), -1, keepdims=True)
    e = jnp.exp(m2 - m1)
    g1 = 1.0 / (1.0 + e)
    g2 = e / (1.0 + e)
    out = jnp.where(lane == 0, i1, 0.0)
    out = jnp.where(lane == 1, i2, out)
    out = jnp.where(lane == 2, g1, out)
    out = jnp.where(lane == 3, g2, out)
    o_ref[...] = out


def _router_call(x, w_router, b_router, tm):
    S, D = x.shape
    wp = jnp.pad(w_router.astype(F32), ((0, 0), (0, 128 - N_EXPERTS)))
    whi = wp.astype(BF16)
    wlo = (wp - whi.astype(F32)).astype(BF16)
    bp = jnp.pad(b_router.astype(F32), (0, 128 - N_EXPERTS)).reshape(1, 128)
    const = lambda shape: pl.BlockSpec(shape, lambda i: (0, 0))
    return pl.pallas_call(
        _router_kernel,
        grid=(S // tm,),
        in_specs=[pl.BlockSpec((tm, D), lambda i: (i, 0)), const((D, 128)), const((D, 128)), const((1, 128))],
        out_specs=pl.BlockSpec((tm, 128), lambda i: (i, 0)),
        out_shape=jax.ShapeDtypeStruct((S, 128), F32),
        compiler_params=_cparams(("parallel",)),
        name="router",
    )(x, whi, wlo, bp)


def _row_copy(src_hbm, dst_ref, src_row, dst_row, sem):
    return pltpu.make_async_copy(src_hbm.at[pl.ds(src_row, 1)], dst_ref.at[pl.ds(dst_row, 1)], sem)


def _scatter_rows_kernel(dest_ref, lo_ref, hi_ref, x_ref, o_hbm, zero_ref, sem):
    step = pl.program_id(0)
    n_tiles = pl.num_programs(0) - 1
    tm = x_ref.shape[0]

    def row_out(src_ref, src_row, dst_row):
        return pltpu.make_async_copy(src_ref.at[pl.ds(src_row, 1)], o_hbm.at[pl.ds(dst_row, 1)], sem)

    @pl.when(step < n_tiles)
    def _():
        base = step * tm

        def start(r, c):
            for slot in range(TOP_K):
                row_out(x_ref, r, dest_ref[TOP_K * (base + r) + slot]).start()
            return c

        lax.fori_loop(0, tm, start, 0, unroll=8)
        for slot in range(TOP_K):
            pltpu.make_async_copy(x_ref, o_hbm.at[pl.ds(0, tm)], sem).wait()

    @pl.when(step == n_tiles)
    def _():
        zero_ref[...] = jnp.zeros_like(zero_ref)
        for e in range(N_EXPERTS):
            def start(r, c):
                row_out(zero_ref, 0, r).start()
                return c

            def wait(r, c):
                row_out(zero_ref, 0, r).wait()
                return c

            lax.fori_loop(lo_ref[e], hi_ref[e], start, 0)
            lax.fori_loop(lo_ref[e], hi_ref[e], wait, 0)


def _scatter_rows(x, dest, pad_lo, pad_hi, n_rows, tm):
    T, D = x.shape
    n_tiles = T // tm
    return pl.pallas_call(
        _scatter_rows_kernel,
        grid_spec=pltpu.PrefetchScalarGridSpec(
            num_scalar_prefetch=3,
            grid=(n_tiles + 1,),
            in_specs=[pl.BlockSpec((tm, D), lambda i, d, lo, hi: (jnp.minimum(i, n_tiles - 1), 0))],
            out_specs=pl.BlockSpec(memory_space=pl.ANY),
            scratch_shapes=[pltpu.VMEM((8, D), x.dtype), pltpu.SemaphoreType.DMA(())]),
        out_shape=jax.ShapeDtypeStruct((n_rows, D), x.dtype),
        compiler_params=_cparams(("arbitrary",)),
        name="moe_scatter",
    )(dest, pad_lo, pad_hi, x)


def _moe_ffn_kernel(be_ref, nu_ref, x_ref, wg_ref, wu_ref, wd_ref, o_ref, acc_ref):
    i = pl.program_id(0)
    j = pl.program_id(1)

    last = pl.num_programs(1) - 1

    @pl.when(i < nu_ref[0])
    def _():
        xb = x_ref[...].astype(BF16)
        part = _dot(_silu(_dot(xb, wg_ref[...])) * _dot(xb, wu_ref[...]), wd_ref[...])

        @pl.when(j == 0)
        def _():
            acc_ref[...] = part

        @pl.when((j > 0) & (j < last))
        def _():
            acc_ref[...] += part

        @pl.when(j == last)
        def _():
            o_ref[...] = acc_ref[...] + part

    @pl.when((j == last) & (i >= nu_ref[0]))
    def _():
        o_ref[...] = jnp.zeros_like(o_ref)


def _moe_ffn_call(x_rows, block_e, n_used, wg, wu, wd, tf):
    n, D = x_rows.shape
    F = wg.shape[2]
    R = MOE_ROWS
    assert F % tf == 0 and F // tf >= 2
    return pl.pallas_call(
        _moe_ffn_kernel,
        grid_spec=pltpu.PrefetchScalarGridSpec(
            num_scalar_prefetch=2,
            grid=(n // R, F // tf),
            in_specs=[pl.BlockSpec((R, D), lambda i, j, be, nu: (i, 0)),
                      pl.BlockSpec((None, D, tf), lambda i, j, be, nu: (be[i], 0, j)),
                      pl.BlockSpec((None, D, tf), lambda i, j, be, nu: (be[i], 0, j)),
                      pl.BlockSpec((None, tf, D), lambda i, j, be, nu: (be[i], j, 0))],
            out_specs=pl.BlockSpec((R, D), lambda i, j, be, nu: (i, 0)),
            scratch_shapes=[pltpu.VMEM((R, D), F32)]),
        out_shape=jax.ShapeDtypeStruct((n, D), F32),
        compiler_params=_cparams(("arbitrary", "arbitrary")),
        name="moe_ffn",
    )(block_e, n_used, x_rows, wg, wu, wd)


def _moe_combine_kernel(d_ref, y_hbm, x_ref, r_ref, g_ref, b_ref, o_ref, y1_ref, y2_ref, sem):
    tm = x_ref.shape[0]
    t = pl.program_id(0)

    def issue(tile, buf):
        base = tile * tm

        def start(r, c):
            _row_copy(y_hbm, y1_ref.at[buf], d_ref[2 * (base + r)], r, sem.at[buf, 0]).start()
            _row_copy(y_hbm, y2_ref.at[buf], d_ref[2 * (base + r) + 1], r, sem.at[buf, 1]).start()
            return c

        lax.fori_loop(0, tm, start, 0, unroll=8)

    @pl.when(t == 0)
    def _():
        issue(0, 0)

    @pl.when(t + 1 < pl.num_programs(0))
    def _():
        issue(t + 1, (t + 1) % 2)

    buf = t % 2
    pltpu.make_async_copy(y_hbm.at[pl.ds(0, tm)], y1_ref.at[buf], sem.at[buf, 0]).wait()
    pltpu.make_async_copy(y_hbm.at[pl.ds(0, tm)], y2_ref.at[buf], sem.at[buf, 1]).wait()
    y = r_ref[:, 2:3] * y1_ref[buf] + r_ref[:, 3:4] * y2_ref[buf]
    o_ref[...] = _layer_norm_rows(DN_ALPHA * x_ref[...] + y, g_ref[...], b_ref[...])


def _moe_combine_call(dest, y_rows, x, routed, g, b, tm):
    S, D = x.shape
    return pl.pallas_call(
        _moe_combine_kernel,
        grid_spec=pltpu.PrefetchScalarGridSpec(
            num_scalar_prefetch=1,
            grid=(S // tm,),
            in_specs=[pl.BlockSpec(memory_space=pl.ANY),
                      pl.BlockSpec((tm, D), lambda i, d: (i, 0)),
                      pl.BlockSpec((tm, 128), lambda i, d: (i, 0)),
                      pl.BlockSpec((1, D), lambda i, d: (0, 0)),
                      pl.BlockSpec((1, D), lambda i, d: (0, 0))],
            out_specs=pl.BlockSpec((tm, D), lambda i, d: (i, 0)),
            scratch_shapes=[pltpu.VMEM((2, tm, D), F32), pltpu.VMEM((2, tm, D), F32),
                            pltpu.SemaphoreType.DMA((2, 2))]),
        out_shape=jax.ShapeDtypeStruct((S, D), F32),
        compiler_params=_cparams(("arbitrary",)),
        name="moe_combine",
    )(dest, y_rows, x, routed, g.reshape(1, D), b.reshape(1, D))


def _moe_sublayer(x, w_router, b_router, wg, wu, wd, g, b):
    T = x.shape[0]
    R = MOE_ROWS
    routed = _router_call(x, w_router, b_router, min(ROW_TILE, T))
    top_e = routed[:, 0:TOP_K].astype(jnp.int32)
    tok_oh = jnp.sum((top_e[:, :, None] == jnp.arange(N_EXPERTS)[None, None, :]).astype(jnp.int32), axis=1)
    counts = jnp.sum(tok_oh, axis=0)
    rank = jnp.cumsum(tok_oh, axis=0) - tok_oh
    padded = (counts + R - 1) // R * R
    pend = jnp.cumsum(padded)
    pstart = pend - padded
    dest = pstart[top_e] + jnp.take_along_axis(rank, top_e, axis=1)
    n_rows = (T * TOP_K + N_EXPERTS * (R - 1)) // R * R
    n_blocks = n_rows // R
    block_first_row = jnp.arange(n_blocks, dtype=jnp.int32) * R
    block_e = jnp.minimum(jnp.sum((pend[None, :] <= block_first_row[:, None]).astype(jnp.int32), axis=1),
                          N_EXPERTS - 1)
    n_used = (pend[-1] // R).astype(jnp.int32).reshape(1)
    dest = dest.reshape(-1).astype(jnp.int32)
    pad_lo = (pstart + counts).astype(jnp.int32)
    pad_hi = jnp.concatenate([pstart[1:], jnp.array([n_rows])]).astype(jnp.int32)
    x_rows = _scatter_rows(x, dest, pad_lo, pad_hi, n_rows, min(ROW_TILE, T))
    y_rows = _moe_ffn_call(x_rows, block_e, n_used, wg, wu, wd, EXPERT_FF // 2)
    return _moe_combine_call(dest, y_rows, x, routed, g, b, min(COMBINE_TILE, T))


GLA_IN = 2 * GLA_QK + 2 * GLA_V + GLA_LR


def _gla_rwkv_sublayer(x, w_in, gla_wa2, gla_ba, gla_norm, mu, w0, w2, a0, a2, g2, k_k, k_a, r_k, ln_w, ln_b,
                       w_out, ln_g, ln_bias):
    S = x.shape[0]
    tm = min(ROW_TILE, S)
    lr_pad = 128 - GLA_LR
    w_gla = jnp.pad(w_in[:, :GLA_IN], ((0, 0), (0, lr_pad))).astype(BF16)
    w_rwkv = w_in[:, GLA_IN:].astype(BF16)
    p_gla = _matmul(x, w_gla, F32, tm, GLA_COLS)
    p_rwkv = _matmul(x, w_rwkv, F32, tm, RWKV_COLS)
    wa2p = jnp.pad(gla_wa2, ((0, lr_pad), (0, 0))).astype(BF16)
    o_gla = _gla_call(p_gla, wa2p, gla_ba, gla_norm)
    o_rwkv = _rwkv_call(p_rwkv, mu, w0, w2, a0, a2, g2, k_k, k_a, r_k, ln_w, ln_b)
    return _matmul_ln([o_gla, o_rwkv], [w_out[:GLA_V].astype(BF16), w_out[GLA_V:].astype(BF16)], x,
                      ln_g, ln_bias, tm)


def _xattn_sublayer(x, mem, wq, wk, wv, wo, ln_g, ln_bias):
    M = mem.shape[0]
    k_mem = _matmul(mem, wk.astype(BF16), BF16, M, D_MODEL)
    v_mem = _matmul(mem, wv.astype(BF16), BF16, M, D_MODEL)
    wq_scaled = (wq * XATTN_HD ** -0.5).astype(BF16)
    return _xattn_call(x, wq_scaled, k_mem, v_mem, wo.astype(BF16), ln_g, ln_bias, min(ROW_TILE, x.shape[0]))


def _ssd_moba_sublayer(x, w_in, conv_w, conv_b, dt_bias, a_log, d_skip, ssd_norm, w_out, ln_g, ln_bias):
    S = x.shape[0]
    tm = min(ROW_TILE, S)
    nb = S // MOBA_BLOCK
    o_dt = SSD_COLS
    o_q = o_dt + SSD_HEADS
    w_ssd = jnp.pad(w_in[:, :o_q], ((0, 0), (0, 128 - SSD_HEADS))).astype(BF16)
    w_q = w_in[:, o_q:o_q + MOBA_D]
    w_k = w_in[:, o_q + MOBA_D:o_q + 2 * MOBA_D]
    w_v = w_in[:, o_q + 2 * MOBA_D:]
    p_ssd = _matmul(x, w_ssd, F32, tm, SSD_COLS + 128)
    o_ssd = _ssd_call(p_ssd, conv_w, conv_b, dt_bias, a_log, d_skip, ssd_norm)
    qt, k4, kmean, vt4, knorm, qnorm = _moba_proj_call(x, (w_q.T * (MOBA_HD ** -0.5 * LOG2E)).astype(BF16), w_k.astype(BF16),
                                         w_v.T.astype(BF16))
    kmean_h = kmean.reshape(nb, MOBA_HEADS, MOBA_HD).transpose(1, 0, 2)
    ot_moba = _moba_call(_moba_first_block(knorm, qnorm, nb), qt, k4, vt4, kmean_h)
    return _matmul_ln([o_ssd, ot_moba], [w_out[:SSD_INNER].astype(BF16), w_out[SSD_INNER:].astype(BF16)], x,
                      ln_g, ln_bias, tm, transposed=(False, True))


def kernel(x, mem, l0_w_in, l0_gla_wa2, l0_gla_ba, l0_gla_norm, l0_rwkv_mu, l0_rwkv_w0, l0_rwkv_w2, l0_rwkv_a0, l0_rwkv_a2, l0_rwkv_g2, l0_rwkv_kk, l0_rwkv_ka, l0_rwkv_rk, l0_rwkv_lnw, l0_rwkv_lnb, l0_w_out, l0_ln1_g, l0_ln1_b, l0_xq, l0_xk, l0_xv, l0_xo, l0_ln2_g, l0_ln2_b, l0_ffn_wg, l0_ffn_wu, l0_ffn_wd, l0_ln3_g, l0_ln3_b, l1_w_in, l1_conv_w, l1_conv_b, l1_dt_bias, l1_a_log, l1_d_skip, l1_ssd_norm, l1_w_out, l1_ln1_g, l1_ln1_b, l1_xq, l1_xk, l1_xv, l1_xo, l1_ln2_g, l1_ln2_b, l1_router, l1_router_b, l1_exp_wg, l1_exp_wu, l1_exp_wd, l1_ln3_g, l1_ln3_b):
    x2 = x.reshape(-1, D_MODEL)
    mem2 = mem.reshape(-1, D_MODEL)
    x2 = _gla_rwkv_sublayer(x2, l0_w_in, l0_gla_wa2, l0_gla_ba, l0_gla_norm, l0_rwkv_mu, l0_rwkv_w0, l0_rwkv_w2,
                            l0_rwkv_a0, l0_rwkv_a2, l0_rwkv_g2, l0_rwkv_kk, l0_rwkv_ka, l0_rwkv_rk, l0_rwkv_lnw,
                            l0_rwkv_lnb, l0_w_out, l0_ln1_g, l0_ln1_b)
    x2 = _xattn_sublayer(x2, mem2, l0_xq, l0_xk, l0_xv, l0_xo, l0_ln2_g, l0_ln2_b)
    tm = min(ROW_TILE, x2.shape[0])
    x2 = _ffn_call(x2, l0_ffn_wg.astype(BF16), l0_ffn_wu.astype(BF16), l0_ffn_wd.astype(BF16),
                   l0_ln3_g, l0_ln3_b, tm, l0_ffn_wg.shape[1] // 2)
    x2 = _ssd_moba_sublayer(x2, l1_w_in, l1_conv_w, l1_conv_b, l1_dt_bias, l1_a_log, l1_d_skip, l1_ssd_norm,
                            l1_w_out, l1_ln1_g, l1_ln1_b)
    x2 = _xattn_sublayer(x2, mem2, l1_xq, l1_xk, l1_xv, l1_xo, l1_ln2_g, l1_ln2_b)
    x2 = _moe_sublayer(x2, l1_router, l1_router_b, l1_exp_wg.astype(BF16), l1_exp_wu.astype(BF16),
                       l1_exp_wd.astype(BF16), l1_ln3_g, l1_ln3_b)
    return x2.reshape(x.shape)
```

```python
import functools
import math

import jax
import jax.numpy as jnp
from jax import lax
from jax.experimental import pallas as pl
from jax.experimental.pallas import tpu as pltpu

BF16 = jnp.bfloat16
F32 = jnp.float32

D_MODEL = 1024
LN_EPS = 1e-5
DEPTH = 2
DN_ALPHA = (2 * DEPTH) ** 0.25

GLA_HEADS, GLA_DK, GLA_DV, GLA_CHUNK = 4, 64, 128, 64
GLA_QK, GLA_V, GLA_LR, GLA_TAU = 256, 512, 16, 16.0
GLA_COLS = 2 * GLA_QK + 2 * GLA_V + 128

RWKV_HEADS, RWKV_HD, RWKV_D, RWKV_CHUNK = 8, 64, 512, 64
RWKV_COLS = 1792
RWKV_DECAY_SCALE = math.exp(-0.5)
RWKV_GN_EPS = 64e-5

VMEM_LIMIT = 56 * 1024 * 1024
ROW_TILE = 512
SCAN_TILE = 256
COMBINE_TILE = 256


def _cparams(sem):
    return pltpu.CompilerParams(dimension_semantics=sem, vmem_limit_bytes=VMEM_LIMIT)


def _dot(a, b):
    return jnp.dot(a.astype(BF16), b.astype(BF16), preferred_element_type=F32)


def _dot_nt(a, b):
    return lax.dot_general(a.astype(BF16), b.astype(BF16), (((1,), (1,)), ((), ())), preferred_element_type=F32)


def _dot_tn(a, b):
    return lax.dot_general(a.astype(BF16), b.astype(BF16), (((0,), (0,)), ((), ())), preferred_element_type=F32)


def _split3(x):
    hi = x.astype(BF16)
    r1 = x - hi.astype(F32)
    mid = r1.astype(BF16)
    lo = (r1 - mid.astype(F32)).astype(BF16)
    return hi, mid, lo


def _dot_exact_lhs(m, x):
    mb = m.astype(BF16)
    hi, mid, lo = _split3(x)
    return (jnp.dot(mb, hi, preferred_element_type=F32) + jnp.dot(mb, mid, preferred_element_type=F32)
            + jnp.dot(mb, lo, preferred_element_type=F32))


def _dot_exact_rhs(x, m):
    mb = m.astype(BF16)
    hi, mid, lo = _split3(x)
    return (jnp.dot(hi, mb, preferred_element_type=F32) + jnp.dot(mid, mb, preferred_element_type=F32)
            + jnp.dot(lo, mb, preferred_element_type=F32))


def _dot_stat_rhs(x, m):
    mb = m.astype(BF16)
    hi = x.astype(BF16)
    lo = (x - hi.astype(F32)).astype(BF16)
    return jnp.dot(hi, mb, preferred_element_type=F32) + jnp.dot(lo, mb, preferred_element_type=F32)


def _sigmoid(x):
    return 1.0 / (1.0 + jnp.exp(-x))


def _silu(x):
    return x * _sigmoid(x)


def _iota2(shape, axis):
    return lax.broadcasted_iota(jnp.int32, shape, axis)


def _chunk_tril(n, chunk):
    r = _iota2((n, n), 0)
    c = _iota2((n, n), 1)
    return jnp.where((c <= r) & ((r // chunk) == (c // chunk)), 1.0, 0.0)


def _head_block(n, width, value):
    r = _iota2((n, n), 0)
    c = _iota2((n, n), 1)
    return jnp.where((r // width) == (c // width), value, 0.0)


def _mm_kernel(x_ref, w_ref, o_ref):
    o_ref[...] = _dot(x_ref[...], w_ref[...]).astype(o_ref.dtype)


def _matmul(x, w, out_dtype, tm, tn):
    S, K = x.shape
    N = w.shape[1]
    return pl.pallas_call(
        _mm_kernel,
        grid=(S // tm, N // tn),
        in_specs=[pl.BlockSpec((tm, K), lambda i, j: (i, 0)),
                  pl.BlockSpec((K, tn), lambda i, j: (0, j))],
        out_specs=pl.BlockSpec((tm, tn), lambda i, j: (i, j)),
        out_shape=jax.ShapeDtypeStruct((S, N), out_dtype),
        compiler_params=_cparams(("parallel", "arbitrary")),
        name="matmul",
    )(x, w)


def _layer_norm_rows(y, g, b):
    mu = jnp.mean(y, -1, keepdims=True)
    d = y - mu
    var = jnp.mean(d * d, -1, keepdims=True)
    return d * lax.rsqrt(var + LN_EPS) * g + b


def _mm_ln_kernel(transposed, *refs):
    n_in = len(transposed)
    a_refs = refs[:n_in]
    w_refs = refs[n_in:2 * n_in]
    x_ref, g_ref, b_ref, o_ref = refs[2 * n_in:]
    acc = DN_ALPHA * x_ref[...]
    for a_ref, w_ref, tr in zip(a_refs, w_refs, transposed):
        acc = acc + (_dot_tn if tr else _dot)(a_ref[...], w_ref[...])
    o_ref[...] = _layer_norm_rows(acc, g_ref[...], b_ref[...])


def _matmul_ln(a_list, w_list, x, g, b, tm, transposed=None):
    S, D = x.shape
    transposed = tuple(transposed or (False,) * len(a_list))
    in_specs = ([pl.BlockSpec((a.shape[0], tm), lambda i: (0, i)) if tr else
                 pl.BlockSpec((tm, a.shape[1]), lambda i: (i, 0)) for a, tr in zip(a_list, transposed)]
                + [pl.BlockSpec(w.shape, lambda i: (0, 0)) for w in w_list]
                + [pl.BlockSpec((tm, D), lambda i: (i, 0)),
                   pl.BlockSpec((1, D), lambda i: (0, 0)),
                   pl.BlockSpec((1, D), lambda i: (0, 0))])
    return pl.pallas_call(
        functools.partial(_mm_ln_kernel, transposed),
        grid=(S // tm,),
        in_specs=in_specs,
        out_specs=pl.BlockSpec((tm, D), lambda i: (i, 0)),
        out_shape=jax.ShapeDtypeStruct((S, D), F32),
        compiler_params=_cparams(("parallel",)),
        name="matmul_ln",
    )(*a_list, *w_list, x, g.reshape(1, D), b.reshape(1, D))


def _gla_kernel(p_ref, wa2_ref, ba_ref, ng_ref, o_ref, st_ref, o_scr):
    C, H, dk, dv = GLA_CHUNK, GLA_HEADS, GLA_DK, GLA_DV
    tb = p_ref.shape[0]

    @pl.when(pl.program_id(0) == 0)
    def _():
        st_ref[...] = jnp.zeros_like(st_ref)

    z = _dot(p_ref[:, 2 * GLA_QK + 2 * GLA_V:], wa2_ref[...]) + ba_ref[...]
    log_a = -(jnp.maximum(-z, 0.0) + jnp.log(1.0 + jnp.exp(-jnp.abs(z)))) / GLA_TAU
    b = _dot_exact_lhs(_chunk_tril(tb, C), log_a)
    causal = _iota2((C, C), 1) <= _iota2((C, C), 0)

    nc = tb // C
    q_h, k_h, ke_h, v_h, dec_h = [], [], [], [], []
    for c in range(nc):
        rows = slice(c * C, (c + 1) * C)
        b_c = b[rows]
        b_last = b_c[C - 1:C]
        q_dec = p_ref[rows, 0:GLA_QK] * (dk ** -0.5) * jnp.exp(b_c)
        k_c = p_ref[rows, GLA_QK:2 * GLA_QK]
        k_dec = k_c * jnp.exp(-b_c)
        k_end = k_c * jnp.exp(b_last - b_c)
        decay = jnp.exp(b_last)
        for h in range(H):
            ks = slice(h * dk, (h + 1) * dk)
            q_h.append(q_dec[:, ks])
            k_h.append(k_dec[:, ks])
            ke_h.append(k_end[:, ks])
            dec_h.append(decay[:, ks])
            v_h.append(p_ref[rows, 2 * GLA_QK + h * dv:2 * GLA_QK + (h + 1) * dv])
    n = nc * H
    attn = [jnp.where(causal, _dot_nt(q_h[i], k_h[i]), 0.0) for i in range(n)]
    kv = [_dot_tn(v_h[i], ke_h[i]) for i in range(n)]
    intra = [_dot(attn[i], v_h[i]) for i in range(n)]
    state = [st_ref[:, h * dk:(h + 1) * dk] for h in range(H)]
    entering = []
    for i in range(n):
        entering.append(state[i % H])
        state[i % H] = state[i % H] * dec_h[i] + kv[i]
    for i in range(n):
        c, h = divmod(i, H)
        o_scr[c * C:(c + 1) * C, h * dv:(h + 1) * dv] = intra[i] + _dot_nt(q_h[i], entering[i])
    for h in range(H):
        st_ref[:, h * dk:(h + 1) * dk] = state[h]

    for h in range(H):
        vs = slice(h * dv, (h + 1) * dv)
        o_h = o_scr[:, vs]
        g_h = p_ref[:, 2 * GLA_QK + GLA_V + h * dv:2 * GLA_QK + GLA_V + (h + 1) * dv]
        o_h = o_h * lax.rsqrt(jnp.mean(o_h * o_h, -1, keepdims=True) + 1e-5) * ng_ref[:, vs]
        o_ref[:, vs] = (o_h * _silu(g_h)).astype(o_ref.dtype)


def _gla_call(p_gla, wa2p, ba, norm_g, tb=SCAN_TILE):
    S = p_gla.shape[0]
    return pl.pallas_call(
        _gla_kernel,
        grid=(S // tb,),
        in_specs=[pl.BlockSpec((tb, GLA_COLS), lambda i: (i, 0)),
                  pl.BlockSpec((128, GLA_QK), lambda i: (0, 0)),
                  pl.BlockSpec((1, GLA_QK), lambda i: (0, 0)),
                  pl.BlockSpec((1, GLA_V), lambda i: (0, 0))],
        out_specs=pl.BlockSpec((tb, GLA_V), lambda i: (i, 0)),
        out_shape=jax.ShapeDtypeStruct((S, GLA_V), BF16),
        scratch_shapes=[pltpu.VMEM((GLA_DV, GLA_QK), F32), pltpu.VMEM((tb, GLA_V), F32)],
        compiler_params=_cparams(("arbitrary",)),
        name="gla",
    )(p_gla, wa2p, ba.reshape(1, GLA_QK), norm_g.reshape(1, GLA_V))


def _rwkv_kernel(p_ref, mu_ref, w0_ref, w2_ref, a0_ref, a2_ref, g2_ref, kk_ref, ka_ref, rk_ref, lnw_ref, lnb_ref,
                 o_ref, prev_ref, h_ref, o_scr):
    C, H, N, D = RWKV_CHUNK, RWKV_HEADS, RWKV_HD, RWKV_D
    tb = p_ref.shape[0]
    first = pl.program_id(0) == 0

    @pl.when(first)
    def _():
        prev_ref[...] = jnp.zeros_like(prev_ref)
        h_ref[...] = jnp.zeros_like(h_ref)

    p = p_ref[...]
    shifted = jnp.where(_iota2(p.shape, 0) == 0, prev_ref[...], pltpu.roll(p, 1, 0))
    prev_ref[...] = p[tb - 1:tb]
    p = p + mu_ref[...] * (shifted - p)
    r = p[:, 0:D]
    k = p[:, D:2 * D]
    v = p[:, 2 * D:3 * D]
    xw = p[:, 3 * D:3 * D + 64]
    xa = p[:, 3 * D + 64:3 * D + 128]
    xg = p[:, 3 * D + 128:3 * D + 256]
    lw = -RWKV_DECAY_SCALE * _sigmoid(w0_ref[...] + _dot(jnp.tanh(xw), w2_ref[...]))
    a = _sigmoid(a0_ref[...] + _dot(xa, a2_ref[...]))
    g = _dot(_sigmoid(xg), g2_ref[...])
    head_ones = _head_block(D, N, 1.0)
    kk = k * kk_ref[...]
    kk = kk * lax.rsqrt(jnp.maximum(_dot_stat_rhs(kk * kk, head_ones), 1e-24))
    k = k * (1.0 + (a - 1.0) * ka_ref[...])
    pv = -kk * a
    cw = _dot_exact_lhs(_chunk_tril(tb, C), lw)
    cwx = cw - lw

    W2 = 2 * N
    left = _iota2((C, W2), 1) < N

    def blockdiag(a):
        return jnp.concatenate([jnp.where(left, a, 0.0), jnp.where(left, 0.0, a)], axis=0)

    def diag_blocks(full):
        return jnp.where(left, full[0:C], full[C:2 * C])

    gi = _iota2((2 * C, 2 * W2), 0)
    gj = _iota2((2 * C, 2 * W2), 1) % N
    gram_mask = ((gi < C) & (gj < gi)) | ((gi >= C) & (gj <= gi - C))
    eye = _iota2((C, W2), 0) == _iota2((C, W2), 1) % N
    eye_f = jnp.where(eye, 1.0, 0.0)
    zeros_c = jnp.zeros((C, W2), F32)
    zeros_w = jnp.zeros((W2, W2), F32)

    nc = tb // C
    items = [(c, p) for c in range(nc) for p in range(H // 2)]
    xs, ys, pk_e, b_h, v_h, r_h, g_h = [], [], [], [], [], [], []
    for c in range(nc):
        rows = slice(c * C, (c + 1) * C)
        cw_c = cw[rows]
        cw_end = cw_c[C - 1:C]
        e_pos = jnp.exp(cw_c)
        e_neg = jnp.exp(-cw_c)
        e_end = jnp.exp(cw_end - cw_c)
        r_t = r[rows] * e_pos
        b_t = kk[rows] * jnp.exp(cwx[rows])
        p_t = pv[rows] * e_neg
        k_t = k[rows] * e_neg
        p_e = pv[rows] * e_end
        k_e = k[rows] * e_end
        g_end = jnp.exp(cw_end)
        v_c = v[rows]
        for p in range(H // 2):
            ps = slice(p * W2, (p + 1) * W2)
            xs.append(jnp.concatenate([b_t[:, ps], r_t[:, ps]], axis=0))
            ys.append(jnp.concatenate([blockdiag(p_t[:, ps]), blockdiag(k_t[:, ps])], axis=0))
            pk_e.append(jnp.concatenate([p_e[:, ps], k_e[:, ps]], axis=0))
            b_h.append(b_t[:, ps])
            v_h.append(v_c[:, ps])
            r_h.append(r_t[:, ps])
            g_h.append(g_end[:, ps])
    n = len(items)
    grams = [jnp.where(gram_mask, _dot_nt(xs[i], ys[i]), 0.0) for i in range(n)]
    l_p = [g[0:C, 0:W2] for g in grams]
    m_pk = [g[C:2 * C, :] for g in grams]
    v_bd = [blockdiag(v_h[i]) for i in range(n)]
    lkv = [_dot(grams[i][0:C, W2:2 * W2], v_bd[i]) for i in range(n)]
    x = [_dot(lp, blockdiag(lp)) for lp in l_p]
    t = [eye_f + lp for lp in l_p]
    for _ in range(4):
        tx = [_dot(jnp.concatenate([t[i], x[i]], axis=0), blockdiag(x[i])) for i in range(n)]
        t = [t[i] + tx[i][0:C] for i in range(n)]
        x = [tx[i][C:2 * C] for i in range(n)]
    t = [t[i] + _dot(t[i], blockdiag(x[i])) for i in range(n)]
    wu = [_dot(t[i], jnp.concatenate([blockdiag(b_h[i]), blockdiag(lkv[i])], axis=1)) for i in range(n)]
    az = [_dot_tn(pk_e[i], jnp.concatenate([wu[i], jnp.concatenate([zeros_c, v_h[i]], axis=1)], axis=0))
          for i in range(n)]
    qo = [_dot(m_pk[i], jnp.concatenate(
        [jnp.concatenate([blockdiag(wu[i][:, 0:W2]), blockdiag(wu[i][:, W2:2 * W2])], axis=1),
         jnp.concatenate([zeros_w, v_bd[i]], axis=1)], axis=0)) for i in range(n)]
    state = [h_ref[p] for p in range(H // 2)]
    for i, (c, p) in enumerate(items):
        a_mat = diag_blocks(az[i][:, 0:W2]) + jnp.where(eye, g_h[i], 0.0)
        q_mat = qo[i][:, 0:W2] + r_h[i]
        oh = _dot(jnp.concatenate([q_mat, a_mat], axis=0), blockdiag(state[p]))
        o_scr[c * C:(c + 1) * C, p * W2:(p + 1) * W2] = oh[0:C] + qo[i][:, W2:2 * W2]
        state[p] = oh[C:C + N] + diag_blocks(az[i][:, W2:2 * W2])
    for p in range(H // 2):
        h_ref[p] = state[p]

    o = o_scr[...]
    head_mean = _head_block(D, N, 1.0 / N)
    mean = _dot_stat_rhs(o, head_mean)
    d = o - mean
    var = _dot_stat_rhs(d * d, head_mean)
    o = d * lax.rsqrt(var + RWKV_GN_EPS) * lnw_ref[...] + lnb_ref[...]
    bonus = _dot_stat_rhs(r * k * rk_ref[...], head_ones) * v
    o_ref[...] = ((o + bonus) * g).astype(o_ref.dtype)


def _rwkv_call(p_rwkv, mu, w0, w2, a0, a2, g2, k_k, k_a, r_k, ln_w, ln_b, tb=SCAN_TILE):
    S = p_rwkv.shape[0]
    D = RWKV_D
    row = lambda t: t.reshape(1, -1).astype(F32)
    full = lambda shape: pl.BlockSpec(shape, lambda i: tuple(0 for _ in shape))
    return pl.pallas_call(
        _rwkv_kernel,
        grid=(S // tb,),
        in_specs=[pl.BlockSpec((tb, RWKV_COLS), lambda i: (i, 0)),
                  full((1, RWKV_COLS)), full((1, D)), full((64, D)), full((1, D)), full((64, D)), full((128, D)),
                  full((1, D)), full((1, D)), full((1, D)), full((1, D)), full((1, D))],
        out_specs=pl.BlockSpec((tb, D), lambda i: (i, 0)),
        out_shape=jax.ShapeDtypeStruct((S, D), BF16),
        scratch_shapes=[pltpu.VMEM((1, RWKV_COLS), F32),
                        pltpu.VMEM((RWKV_HEADS // 2, RWKV_HD, 2 * RWKV_HD), F32),
                        pltpu.VMEM((tb, D), F32)],
        compiler_params=_cparams(("arbitrary",)),
        name="rwkv7",
    )(p_rwkv, row(mu), row(w0), w2.astype(BF16), row(a0), a2.astype(BF16), g2.astype(BF16),
      row(k_k), row(k_a), row(r_k), row(ln_w), row(ln_b))


XATTN_HEADS, XATTN_HD = 4, 256


def _xattn_kernel(x_ref, wq_ref, k_ref, v_ref, wo_ref, g_ref, b_ref, o_ref):
    x = x_ref[...]
    q = _dot(x, wq_ref[...])
    outs = []
    for h in range(XATTN_HEADS):
        hs = slice(h * XATTN_HD, (h + 1) * XATTN_HD)
        s = _dot_nt(q[:, hs], k_ref[:, hs])
        e = jnp.exp(s - jnp.max(s, -1, keepdims=True))
        p = e / jnp.sum(e, -1, keepdims=True)
        outs.append(_dot(p, v_ref[:, hs]))
    o = jnp.concatenate(outs, axis=1)
    y = DN_ALPHA * x + _dot(o, wo_ref[...])
    o_ref[...] = _layer_norm_rows(y, g_ref[...], b_ref[...])


def _xattn_call(x, wq_scaled, k_mem, v_mem, wo, g, b, tm):
    S, D = x.shape
    M = k_mem.shape[0]
    const = lambda shape: pl.BlockSpec(shape, lambda i: (0, 0))
    return pl.pallas_call(
        _xattn_kernel,
        grid=(S // tm,),
        in_specs=[pl.BlockSpec((tm, D), lambda i: (i, 0)), const((D, D)), const((M, D)), const((M, D)),
                  const((D, D)), const((1, D)), const((1, D))],
        out_specs=pl.BlockSpec((tm, D), lambda i: (i, 0)),
        out_shape=jax.ShapeDtypeStruct((S, D), F32),
        compiler_params=_cparams(("parallel",)),
        name="xattn",
    )(x, wq_scaled, k_mem, v_mem, wo, g.reshape(1, D), b.reshape(1, D))


def _ffn_kernel(x_ref, wg_ref, wu_ref, wd_ref, g_ref, b_ref, o_ref, acc_ref):
    j = pl.program_id(1)
    x = x_ref[...]
    xb = x.astype(BF16)
    part = _dot(_silu(_dot(xb, wg_ref[...])) * _dot(xb, wu_ref[...]), wd_ref[...])

    last = pl.num_programs(1) - 1

    @pl.when(j == 0)
    def _():
        acc_ref[...] = DN_ALPHA * x + part

    @pl.when((j > 0) & (j < last))
    def _():
        acc_ref[...] += part

    @pl.when(j == last)
    def _():
        o_ref[...] = _layer_norm_rows(acc_ref[...] + part, g_ref[...], b_ref[...])


def _ffn_call(x, wg, wu, wd, g, b, tm, tf):
    S, D = x.shape
    F = wg.shape[1]
    assert F % tf == 0 and F // tf >= 2
    return pl.pallas_call(
        _ffn_kernel,
        grid=(S // tm, F // tf),
        in_specs=[pl.BlockSpec((tm, D), lambda i, j: (i, 0)),
                  pl.BlockSpec((D, tf), lambda i, j: (0, j)),
                  pl.BlockSpec((D, tf), lambda i, j: (0, j)),
                  pl.BlockSpec((tf, D), lambda i, j: (j, 0)),
                  pl.BlockSpec((1, D), lambda i, j: (0, 0)),
                  pl.BlockSpec((1, D), lambda i, j: (0, 0))],
        out_specs=pl.BlockSpec((tm, D), lambda i, j: (i, 0)),
        out_shape=jax.ShapeDtypeStruct((S, D), F32),
        scratch_shapes=[pltpu.VMEM((tm, D), F32)],
        compiler_params=_cparams(("parallel", "arbitrary")),
        name="ffn",
    )(x, wg, wu, wd, g.reshape(1, D), b.reshape(1, D))


SSD_HD, SSD_HEADS, SSD_INNER, SSD_GROUPS, SSD_STATE = 64, 16, 1024, 2, 128
SSD_BC, SSD_CONV, SSD_CONV_CH, SSD_CHUNK = 256, 4, 1536, 128
SSD_COLS = SSD_INNER + SSD_CONV_CH
SSD_GW = SSD_INNER // SSD_GROUPS


def _softplus(x):
    return jnp.maximum(x, 0.0) + jnp.log(1.0 + jnp.exp(-jnp.abs(x)))


def _ssd_kernel(p_ref, cw_ref, cb_ref, dtb_ref, a_ref, dsk_ref, ng_ref, o_ref, prev_ref, st_ref, y_scr):
    L, G, NS, HD = SSD_CHUNK, SSD_GROUPS, SSD_STATE, SSD_HD
    HG = SSD_HEADS // G

    @pl.when(pl.program_id(0) == 0)
    def _():
        prev_ref[...] = jnp.zeros_like(prev_ref)
        st_ref[...] = jnp.zeros_like(st_ref)

    cur = p_ref[:, SSD_INNER:SSD_COLS]
    tail = prev_ref[...]
    row = _iota2(tail.shape, 0)
    conv = cur * cw_ref[SSD_CONV - 1:SSD_CONV, :] + cb_ref[...]
    for kk in range(1, SSD_CONV):
        rolled = pltpu.roll(cur, kk, 0)
        head = jnp.where(row < kk, pltpu.roll(tail, kk, 0), rolled[0:8])
        shifted = jnp.concatenate([head, rolled[8:]], axis=0)
        conv = conv + shifted * cw_ref[SSD_CONV - 1 - kk:SSD_CONV - kk, :]
    prev_ref[...] = cur[L - 8:L]
    xbc = _silu(conv)
    xs = xbc[:, :SSD_INNER]

    dt = _softplus(p_ref[:, SSD_COLS:] + dtb_ref[...])
    a_col = dt * a_ref[...]
    li = _iota2((L, L), 0)
    lj = _iota2((L, L), 1)
    cs = _dot_exact_lhs(jnp.where(lj <= li, 1.0, 0.0), a_col)
    cs_row = cs.T
    expand = jnp.where(_iota2((128, SSD_INNER), 1) // HD == _iota2((128, SSD_INNER), 0), 1.0, 0.0)
    dt_x = _dot_stat_rhs(dt, expand)
    cs_x = _dot_exact_rhs(cs, expand)
    cs_end = cs_x[L - 1:L]
    xd = xs * dt_x
    xd_dec = xd * jnp.exp(cs_end - cs_x)
    out_dec = jnp.exp(cs_x)
    chunk_dec = jnp.exp(cs_end)
    tril = lj <= li

    for g in range(G):
        gs = slice(g * SSD_GW, (g + 1) * SSD_GW)
        b_g = xbc[:, SSD_INNER + g * NS:SSD_INNER + (g + 1) * NS]
        c_g = xbc[:, SSD_INNER + SSD_BC + g * NS:SSD_INNER + SSD_BC + (g + 1) * NS]
        cb = _dot_nt(c_g, b_g)
        for j in range(HG):
            h = g * HG + j
            hs = slice(h * HD, (h + 1) * HD)
            seg = jnp.where(tril, jnp.exp(cs[:, h:h + 1] - cs_row[h:h + 1, :]), 0.0)
            y_scr[:, hs] = _dot(cb * seg, xd[:, hs])
        st = st_ref[g]
        y_off = _dot(c_g, st) * out_dec[:, gs]
        st_ref[g] = st * chunk_dec[:, gs] + _dot_tn(b_g, xd_dec[:, gs])
        y_scr[:, gs] = y_scr[:, gs] + y_off

    y = (y_scr[...] + dsk_ref[...] * xs) * _silu(p_ref[:, :SSD_INNER])
    for g in range(G):
        gs = slice(g * SSD_GW, (g + 1) * SSD_GW)
        y_g = y[:, gs]
        o_ref[:, gs] = (y_g * lax.rsqrt(jnp.mean(y_g * y_g, -1, keepdims=True) + 1e-5) * ng_ref[:, gs]).astype(o_ref.dtype)


def _ssd_call(p_ssd, conv_w, conv_b, dt_bias, a_log, d_skip, norm_g):
    S = p_ssd.shape[0]
    L = SSD_CHUNK
    a_neg = -jnp.exp(a_log.astype(F32))
    pad = lambda t: jnp.pad(t.astype(F32), (0, 128 - SSD_HEADS)).reshape(1, 128)
    const = lambda shape: pl.BlockSpec(shape, lambda i: (0, 0))
    return pl.pallas_call(
        _ssd_kernel,
        grid=(S // L,),
        in_specs=[pl.BlockSpec((L, SSD_COLS + 128), lambda i: (i, 0)),
                  const((SSD_CONV, SSD_CONV_CH)), const((1, SSD_CONV_CH)),
                  const((1, 128)), const((1, 128)),
                  const((1, SSD_INNER)), const((1, SSD_INNER))],
        out_specs=pl.BlockSpec((L, SSD_INNER), lambda i: (i, 0)),
        out_shape=jax.ShapeDtypeStruct((S, SSD_INNER), BF16),
        scratch_shapes=[pltpu.VMEM((8, SSD_CONV_CH), F32),
                        pltpu.VMEM((SSD_GROUPS, SSD_STATE, SSD_GW), F32),
                        pltpu.VMEM((L, SSD_INNER), F32)],
        compiler_params=_cparams(("arbitrary",)),
        name="ssd",
    )(p_ssd, conv_w.astype(F32), conv_b.reshape(1, -1).astype(F32), pad(dt_bias), pad(a_neg),
      jnp.repeat(d_skip.astype(F32), SSD_HD).reshape(1, -1), norm_g.reshape(1, -1).astype(F32))


MOBA_HD, MOBA_HEADS, MOBA_D, MOBA_BLOCK, MOBA_TOPK = 64, 8, 512, 256, 3
MOBA_GROUP = 4
MOBA_UNDERFLOW = -160.0
MOBA_BOUND_SLACK = 1.001
NEG_BIG = -1e30
LOG2E = math.log2(math.e)


MOBA_VROWS = MOBA_HD + 16


def _moba_proj_kernel(x_ref, wqt_ref, wk_ref, wvt_ref, qt_ref, k_ref, kmean_ref, v_ref, knorm_ref, qnorm_ref):
    xb = x_ref[...].astype(BF16)
    qt = _dot_nt(wqt_ref[...], xb).astype(qt_ref.dtype)
    qt_ref[...] = qt
    vt = _dot_nt(wvt_ref[...], xb)
    extra = jnp.where(_iota2((MOBA_VROWS - MOBA_HD, MOBA_BLOCK), 0) == 0, 1.0, 0.0)
    for h in range(MOBA_HEADS):
        v_ref[h, 0] = jnp.concatenate([vt[h * MOBA_HD:(h + 1) * MOBA_HD], extra], axis=0).astype(v_ref.dtype)
    k = _dot(xb, wk_ref[...])
    kmean_ref[0] = jnp.mean(k, 0, keepdims=True)
    shape = (MOBA_BLOCK, 2 * MOBA_HD)
    lane = _iota2(shape, 1)
    pos = jnp.where((lane == MOBA_HD) | (lane == MOBA_HD + 1), _iota2(shape, 0).astype(F32), 0.0).astype(k_ref.dtype)
    k_b = k.astype(k_ref.dtype)
    for h in range(MOBA_HEADS):
        k_ref[h, 0] = pos
        k_ref[h, 0, :, 0:MOBA_HD] = k_b[:, h * MOBA_HD:(h + 1) * MOBA_HD]
    k_f = k_b.astype(F32)
    q_f = qt.astype(F32)
    head_cols = jnp.where(_iota2((MOBA_D, 128), 0) // MOBA_HD == _iota2((MOBA_D, 128), 1), 1.0, 0.0)
    head_rows = jnp.where(_iota2((128, MOBA_D), 1) // MOBA_HD == _iota2((128, MOBA_D), 0), 1.0, 0.0)
    knorm_ref[0] = jnp.sqrt(jnp.max(_dot_stat_rhs(k_f * k_f, head_cols), 0, keepdims=True))
    qnorm_ref[0] = jnp.sqrt(jnp.max(_dot_exact_lhs(head_rows, q_f * q_f), 1, keepdims=True))


def _moba_proj_call(x, wqt, wk, wvt):
    S, D = x.shape
    nb = S // MOBA_BLOCK
    const = lambda shape: pl.BlockSpec(shape, lambda i: (0, 0))
    return pl.pallas_call(
        _moba_proj_kernel,
        grid=(nb,),
        in_specs=[pl.BlockSpec((MOBA_BLOCK, D), lambda i: (i, 0)), const((MOBA_D, D)), const((D, MOBA_D)),
                  const((MOBA_D, D))],
        out_specs=[pl.BlockSpec((MOBA_D, MOBA_BLOCK), lambda i: (0, i)),
                   pl.BlockSpec((MOBA_HEADS, 1, MOBA_BLOCK, 2 * MOBA_HD), lambda i: (0, i, 0, 0)),
                   pl.BlockSpec((1, 1, MOBA_D), lambda i: (i, 0, 0)),
                   pl.BlockSpec((MOBA_HEADS, 1, MOBA_VROWS, MOBA_BLOCK), lambda i: (0, i, 0, 0)),
                   pl.BlockSpec((1, 1, 128), lambda i: (i, 0, 0)),
                   pl.BlockSpec((1, 128, 1), lambda i: (i, 0, 0))],
        out_shape=[jax.ShapeDtypeStruct((MOBA_D, S), BF16),
                   jax.ShapeDtypeStruct((MOBA_HEADS, nb, MOBA_BLOCK, 2 * MOBA_HD), BF16),
                   jax.ShapeDtypeStruct((nb, 1, MOBA_D), F32),
                   jax.ShapeDtypeStruct((MOBA_HEADS, nb, MOBA_VROWS, MOBA_BLOCK), BF16),
                   jax.ShapeDtypeStruct((nb, 1, 128), F32),
                   jax.ShapeDtypeStruct((nb, 128, 1), F32)],
        compiler_params=_cparams(("parallel",)),
        name="moba_proj",
    )(x, wqt, wk, wvt)


def _moba_kernel(first_ref, qt_ref, k_ref, vt_ref, kmean_ref, o_ref, sel_ref, s0_ref, s1_ref, p0_ref, p1_ref):
    BS, HD = MOBA_BLOCK, MOBA_HD
    h = pl.program_id(0)
    i = pl.program_id(1)
    nb = k_ref.shape[0]
    qt = qt_ref[...]
    slope = LOG2E * jnp.exp2(jnp.zeros((1, BS), F32) - (h + 1).astype(F32) * (8.0 / MOBA_HEADS))
    slope_hi = slope.astype(qt.dtype).astype(F32)
    slope_lo = slope - slope_hi
    row = _iota2((HD, BS), 0)
    qt_ext = jnp.concatenate(
        [qt, jnp.where(row == 0, slope_hi, jnp.where(row == 1, slope_lo, 0.0)).astype(qt.dtype)], axis=0)

    gate = _dot(kmean_ref[...], qt)
    blk = _iota2((nb, BS), 0).astype(F32)
    cand = blk < i.astype(F32)
    sel = jnp.zeros((nb, BS), F32)
    for _ in range(MOBA_TOPK):
        best = jnp.max(jnp.where(cand, gate, -jnp.inf), 0, keepdims=True)
        idx = jnp.min(jnp.where(cand & (gate == best), blk, float(nb)), 0, keepdims=True)
        pick = blk == idx
        sel = jnp.where(pick, 1.0, sel)
        cand = cand & jnp.logical_not(pick)
    sel_ref[...] = sel

    G = MOBA_GROUP
    last = nb - 1
    s_ref = (s0_ref, s1_ref)
    p_ref = (p0_ref, p1_ref)

    first = first_ref[h * nb + i]

    def group(u):
        return [jnp.clip(first + G * u + x, 0, last) for x in range(G)]

    def issue_scores(u, slot):
        tops = []
        for x, j in enumerate(group(u)):
            sc = _dot(k_ref[j], qt_ext)
            s_ref[slot][x] = sc
            tops.append(jnp.max(sc, 0, keepdims=True))
        return tuple(tops)

    def value_blocks(u):
        js = group(u)
        js[0] = jnp.where(u == -1, i, js[0])
        return js

    def weighted_values(u, slot):
        out = None
        for x, j in enumerate(value_blocks(u)):
            part = _dot(vt_ref[j], p_ref[slot][x])
            out = part if out is None else out + part
        return out

    def step(u, slot, tops, a_prev, m, acc, issue_next=True):
        on, shift = [], []
        m_new = m
        for x, j in enumerate(group(u)):
            on.append(sel_ref[pl.ds(j, 1), :] > 0.0)
            shift.append(slope * ((j - i) * BS).astype(F32))
            m_new = jnp.maximum(m_new, jnp.where(on[x], tops[x] + shift[x], NEG_BIG))
        tops_next, pv = [], None
        for x, (j_prev, j_next) in enumerate(zip(value_blocks(u - 1), group(u + 1))):
            part = _dot(vt_ref[j_prev], p_ref[1 - slot][x])
            pv = part if pv is None else pv + part
            if issue_next:
                sc = _dot(k_ref[j_next], qt_ext)
                s_ref[1 - slot][x] = sc
                tops_next.append(jnp.max(sc, 0, keepdims=True))
            p = jnp.exp2(s_ref[slot][x] - (jnp.where(on[x], m_new, -NEG_BIG) - shift[x]))
            p_ref[slot][x] = p.astype(BF16)
        return tuple(tops_next), jnp.exp2(m - m_new), m_new, a_prev * acc + pv

    def body(w, carry):
        carry = step(2 * w, 0, *carry)
        return step(2 * w + 1, 1, *carry)

    tops0 = issue_scores(0, 0)
    s_own = jnp.where(_iota2((BS, BS), 1) >= _iota2((BS, BS), 0), _dot(k_ref[i], qt_ext), NEG_BIG)
    m_own = jnp.max(s_own, 0, keepdims=True)
    p1_ref[1:G] = jnp.zeros((G - 1, BS, BS), BF16)
    p1_ref[0] = jnp.exp2(s_own - m_own).astype(BF16)
    init = (tops0, jnp.ones((1, BS), F32), m_own, jnp.zeros((vt_ref.shape[1], BS), F32))
    steps = (i - first + G - 1) // G
    pairs = steps // 2
    carry = lax.fori_loop(0, pairs, body, init)

    def odd_tail(carry):
        _, a_last, _, acc = step(2 * pairs, 0, *carry, issue_next=False)
        return a_last * acc + weighted_values(2 * pairs, 0)

    def even_tail(carry):
        _, a_prev, _, acc = carry
        return a_prev * acc + weighted_values(2 * pairs - 1, 1)

    acc = lax.cond(steps % 2 == 1, odd_tail, even_tail, carry)
    o_ref[...] = (acc[0:HD] / acc[HD:HD + 1]).astype(o_ref.dtype)


def _moba_first_block(knorm, qnorm, nb):
    BS = MOBA_BLOCK
    k_norm = knorm[:, 0, 0:MOBA_HEADS].T
    q_norm = qnorm[:, 0:MOBA_HEADS, 0].T
    slope = LOG2E * jnp.exp2(-(jnp.arange(MOBA_HEADS, dtype=F32) + 1.0) * (8.0 / MOBA_HEADS))
    i_idx = jnp.arange(nb, dtype=F32)[None, :, None]
    j_idx = jnp.arange(nb, dtype=F32)[None, None, :]
    reach = slope[:, None, None] * ((BS - 1.0) - BS * (i_idx - j_idx))
    bound = q_norm[:, :, None] * (k_norm[:, None, :] + k_norm[:, :, None]) * MOBA_BOUND_SLACK + reach
    matters = (bound >= MOBA_UNDERFLOW) & (j_idx < i_idx)
    first = jnp.min(jnp.where(matters, j_idx, i_idx), axis=2).astype(jnp.int32)
    return first.reshape(-1)


def _moba_call(first_block, qt, k4, vt4, kmean):
    S = qt.shape[1]
    nb = S // MOBA_BLOCK
    return pl.pallas_call(
        _moba_kernel,
        grid_spec=pltpu.PrefetchScalarGridSpec(
            num_scalar_prefetch=1,
            grid=(MOBA_HEADS, nb),
            in_specs=[pl.BlockSpec((MOBA_HD, MOBA_BLOCK), lambda h, i, first: (h, i)),
                      pl.BlockSpec((None, nb, MOBA_BLOCK, 2 * MOBA_HD), lambda h, i, first: (h, 0, 0, 0)),
                      pl.BlockSpec((None, nb, MOBA_VROWS, MOBA_BLOCK), lambda h, i, first: (h, 0, 0, 0)),
                      pl.BlockSpec((None, nb, MOBA_HD), lambda h, i, first: (h, 0, 0))],
            out_specs=pl.BlockSpec((MOBA_HD, MOBA_BLOCK), lambda h, i, first: (h, i)),
            scratch_shapes=[pltpu.VMEM((nb, MOBA_BLOCK), F32)]
            + [pltpu.VMEM((MOBA_GROUP, MOBA_BLOCK, MOBA_BLOCK), F32)] * 2
            + [pltpu.VMEM((MOBA_GROUP, MOBA_BLOCK, MOBA_BLOCK), BF16)] * 2),
        out_shape=jax.ShapeDtypeStruct((MOBA_D, S), BF16),
        compiler_params=_cparams(("parallel", "arbitrary")),
        name="moba",
    )(first_block, qt, k4, vt4, kmean)


N_EXPERTS, TOP_K, EXPERT_FF = 8, 2, 2816
MOE_ROWS = 512


def _router_kernel(x_ref, whi_ref, wlo_ref, b_ref, o_ref):
    x = x_ref[...]
    xhi = x.astype(BF16)
    xlo = (x - xhi.astype(F32)).astype(BF16)
    logits = (jnp.dot(xhi, whi_ref[...], preferred_element_type=F32)
              + jnp.dot(xhi, wlo_ref[...], preferred_element_type=F32)
              + jnp.dot(xlo, whi_ref[...], preferred_element_type=F32)) + b_ref[...]
    lane = _iota2(logits.shape, 1).astype(F32)
    logits = jnp.where(lane < N_EXPERTS, logits, -jnp.inf)
    m1 = jnp.max(logits, -1, keepdims=True)
    i1 = jnp.min(jnp.where(logits == m1, lane, 128.0), -1, keepdims=True)
    rest = jnp.where(lane == i1, -jnp.inf, logits)
    m2 = jnp.max(rest, -1, keepdims=True)
    i2 = jnp.min(jnp.where(rest == m2, lane, 128.0), -1, keepdims=True)
    e = jnp.exp(m2 - m1)
    g1 = 1.0 / (1.0 + e)
    g2 = e / (1.0 + e)
    out = jnp.where(lane == 0, i1, 0.0)
    out = jnp.where(lane == 1, i2, out)
    out = jnp.where(lane == 2, g1, out)
    out = jnp.where(lane == 3, g2, out)
    o_ref[...] = out


def _router_call(x, w_router, b_router, tm):
    S, D = x.shape
    wp = jnp.pad(w_router.astype(F32), ((0, 0), (0, 128 - N_EXPERTS)))
    whi = wp.astype(BF16)
    wlo = (wp - whi.astype(F32)).astype(BF16)
    bp = jnp.pad(b_router.astype(F32), (0, 128 - N_EXPERTS)).reshape(1, 128)
    const = lambda shape: pl.BlockSpec(shape, lambda i: (0, 0))
    return pl.pallas_call(
        _router_kernel,
        grid=(S // tm,),
        in_specs=[pl.BlockSpec((tm, D), lambda i: (i, 0)), const((D, 128)), const((D, 128)), const((1, 128))],
        out_specs=pl.BlockSpec((tm, 128), lambda i: (i, 0)),
        out_shape=jax.ShapeDtypeStruct((S, 128), F32),
        compiler_params=_cparams(("parallel",)),
        name="router",
    )(x, whi, wlo, bp)


def _row_copy(src_hbm, dst_ref, src_row, dst_row, sem):
    return pltpu.make_async_copy(src_hbm.at[pl.ds(src_row, 1)], dst_ref.at[pl.ds(dst_row, 1)], sem)


def _scatter_rows_kernel(dest_ref, lo_ref, hi_ref, x_ref, o_hbm, zero_ref, xs_ref, sem, tile_sem):
    step = pl.program_id(0)
    n_tiles = pl.num_programs(0) - 1
    tm = x_ref.shape[0]

    def row_out(src_ref, src_row, dst_row, s):
        return pltpu.make_async_copy(src_ref.at[pl.ds(src_row, 1)], o_hbm.at[pl.ds(dst_row, 1)], s)

    def wait_tile(buf):
        for slot in range(TOP_K):
            pltpu.make_async_copy(xs_ref.at[buf], o_hbm.at[pl.ds(0, tm)], tile_sem.at[buf]).wait()

    @pl.when(step < n_tiles)
    def _():
        base = step * tm
        buf = step % 2
        xs_ref[buf] = x_ref[...]

        def start(r, c):
            for slot in range(TOP_K):
                row_out(xs_ref.at[buf], r, dest_ref[TOP_K * (base + r) + slot], tile_sem.at[buf]).start()
            return c

        lax.fori_loop(0, tm, start, 0, unroll=8)

        @pl.when(step > 0)
        def _():
            wait_tile(1 - buf)

    @pl.when(step == n_tiles)
    def _():
        wait_tile((n_tiles - 1) % 2)
        zero_ref[...] = jnp.zeros_like(zero_ref)
        for e in range(N_EXPERTS):
            def start(r, c):
                row_out(zero_ref, 0, r, sem).start()
                return c

            def wait(r, c):
                row_out(zero_ref, 0, r, sem).wait()
                return c

            lax.fori_loop(lo_ref[e], hi_ref[e], start, 0)
            lax.fori_loop(lo_ref[e], hi_ref[e], wait, 0)


def _scatter_rows(x, dest, pad_lo, pad_hi, n_rows, tm):
    T, D = x.shape
    n_tiles = T // tm
    return pl.pallas_call(
        _scatter_rows_kernel,
        grid_spec=pltpu.PrefetchScalarGridSpec(
            num_scalar_prefetch=3,
            grid=(n_tiles + 1,),
            in_specs=[pl.BlockSpec((tm, D), lambda i, d, lo, hi: (jnp.minimum(i, n_tiles - 1), 0))],
            out_specs=pl.BlockSpec(memory_space=pl.ANY),
            scratch_shapes=[pltpu.VMEM((8, D), x.dtype), pltpu.VMEM((2, tm, D), x.dtype),
                            pltpu.SemaphoreType.DMA(()), pltpu.SemaphoreType.DMA((2,))]),
        out_shape=jax.ShapeDtypeStruct((n_rows, D), x.dtype),
        compiler_params=_cparams(("arbitrary",)),
        name="moe_scatter",
    )(dest, pad_lo, pad_hi, x)


def _moe_ffn_kernel(be_ref, nu_ref, x_ref, wg_ref, wu_ref, wd_ref, o_ref, acc_ref):
    i = pl.program_id(0)
    j = pl.program_id(1)

    last = pl.num_programs(1) - 1

    @pl.when(i < nu_ref[0])
    def _():
        xb = x_ref[...].astype(BF16)
        part = _dot(_silu(_dot(xb, wg_ref[...])) * _dot(xb, wu_ref[...]), wd_ref[...])

        @pl.when(j == 0)
        def _():
            acc_ref[...] = part

        @pl.when((j > 0) & (j < last))
        def _():
            acc_ref[...] += part

        @pl.when(j == last)
        def _():
            o_ref[...] = acc_ref[...] + part

    @pl.when((j == last) & (i >= nu_ref[0]))
    def _():
        o_ref[...] = jnp.zeros_like(o_ref)


def _moe_ffn_call(x_rows, block_e, n_used, wg, wu, wd, tf):
    n, D = x_rows.shape
    F = wg.shape[2]
    R = MOE_ROWS
    assert F % tf == 0 and F // tf >= 2
    return pl.pallas_call(
        _moe_ffn_kernel,
        grid_spec=pltpu.PrefetchScalarGridSpec(
            num_scalar_prefetch=2,
            grid=(n // R, F // tf),
            in_specs=[pl.BlockSpec((R, D), lambda i, j, be, nu: (i, 0)),
                      pl.BlockSpec((None, D, tf), lambda i, j, be, nu: (be[i], 0, j)),
                      pl.BlockSpec((None, D, tf), lambda i, j, be, nu: (be[i], 0, j)),
                      pl.BlockSpec((None, tf, D), lambda i, j, be, nu: (be[i], j, 0))],
            out_specs=pl.BlockSpec((R, D), lambda i, j, be, nu: (i, 0)),
            scratch_shapes=[pltpu.VMEM((R, D), F32)]),
        out_shape=jax.ShapeDtypeStruct((n, D), F32),
        compiler_params=_cparams(("arbitrary", "arbitrary")),
        name="moe_ffn",
    )(block_e, n_used, x_rows, wg, wu, wd)


def _moe_combine_kernel(d_ref, y_hbm, x_ref, r_ref, g_ref, b_ref, o_ref, y1_ref, y2_ref, sem):
    tm = x_ref.shape[0]
    t = pl.program_id(0)

    def issue(tile, buf):
        base = tile * tm

        def start(r, c):
            _row_copy(y_hbm, y1_ref.at[buf], d_ref[2 * (base + r)], r, sem.at[buf, 0]).start()
            _row_copy(y_hbm, y2_ref.at[buf], d_ref[2 * (base + r) + 1], r, sem.at[buf, 1]).start()
            return c

        lax.fori_loop(0, tm, start, 0, unroll=8)

    @pl.when(t == 0)
    def _():
        issue(0, 0)

    @pl.when(t + 1 < pl.num_programs(0))
    def _():
        issue(t + 1, (t + 1) % 2)

    buf = t % 2
    pltpu.make_async_copy(y_hbm.at[pl.ds(0, tm)], y1_ref.at[buf], sem.at[buf, 0]).wait()
    pltpu.make_async_copy(y_hbm.at[pl.ds(0, tm)], y2_ref.at[buf], sem.at[buf, 1]).wait()
    y = r_ref[:, 2:3] * y1_ref[buf] + r_ref[:, 3:4] * y2_ref[buf]
    o_ref[...] = _layer_norm_rows(DN_ALPHA * x_ref[...] + y, g_ref[...], b_ref[...])


def _moe_combine_call(dest, y_rows, x, routed, g, b, tm):
    S, D = x.shape
    return pl.pallas_call(
        _moe_combine_kernel,
        grid_spec=pltpu.PrefetchScalarGridSpec(
            num_scalar_prefetch=1,
            grid=(S // tm,),
            in_specs=[pl.BlockSpec(memory_space=pl.ANY),
                      pl.BlockSpec((tm, D), lambda i, d: (i, 0)),
                      pl.BlockSpec((tm, 128), lambda i, d: (i, 0)),
                      pl.BlockSpec((1, D), lambda i, d: (0, 0)),
                      pl.BlockSpec((1, D), lambda i, d: (0, 0))],
            out_specs=pl.BlockSpec((tm, D), lambda i, d: (i, 0)),
            scratch_shapes=[pltpu.VMEM((2, tm, D), F32), pltpu.VMEM((2, tm, D), F32),
                            pltpu.SemaphoreType.DMA((2, 2))]),
        out_shape=jax.ShapeDtypeStruct((S, D), F32),
        compiler_params=_cparams(("arbitrary",)),
        name="moe_combine",
    )(dest, y_rows, x, routed, g.reshape(1, D), b.reshape(1, D))


def _moe_sublayer(x, w_router, b_router, wg, wu, wd, g, b):
    T = x.shape[0]
    R = MOE_ROWS
    routed = _router_call(x, w_router, b_router, min(ROW_TILE, T))
    top_e = routed[:, 0:TOP_K].astype(jnp.int32)
    tok_oh = jnp.sum((top_e[:, :, None] == jnp.arange(N_EXPERTS)[None, None, :]).astype(jnp.int32), axis=1)
    counts = jnp.sum(tok_oh, axis=0)
    rank = jnp.cumsum(tok_oh, axis=0) - tok_oh
    padded = (counts + R - 1) // R * R
    pend = jnp.cumsum(padded)
    pstart = pend - padded
    dest = pstart[top_e] + jnp.take_along_axis(rank, top_e, axis=1)
    n_rows = (T * TOP_K + N_EXPERTS * (R - 1)) // R * R
    n_blocks = n_rows // R
    block_first_row = jnp.arange(n_blocks, dtype=jnp.int32) * R
    block_e = jnp.minimum(jnp.sum((pend[None, :] <= block_first_row[:, None]).astype(jnp.int32), axis=1),
                          N_EXPERTS - 1)
    n_used = (pend[-1] // R).astype(jnp.int32).reshape(1)
    dest = dest.reshape(-1).astype(jnp.int32)
    pad_lo = (pstart + counts).astype(jnp.int32)
    pad_hi = jnp.concatenate([pstart[1:], jnp.array([n_rows])]).astype(jnp.int32)
    x_rows = _scatter_rows(x, dest, pad_lo, pad_hi, n_rows, min(ROW_TILE, T))
    y_rows = _moe_ffn_call(x_rows, block_e, n_used, wg, wu, wd, EXPERT_FF // 2)
    return _moe_combine_call(dest, y_rows, x, routed, g, b, min(COMBINE_TILE, T))


GLA_IN = 2 * GLA_QK + 2 * GLA_V + GLA_LR


def _gla_rwkv_sublayer(x, w_in, gla_wa2, gla_ba, gla_norm, mu, w0, w2, a0, a2, g2, k_k, k_a, r_k, ln_w, ln_b,
                       w_out, ln_g, ln_bias):
    S = x.shape[0]
    tm = min(ROW_TILE, S)
    lr_pad = 128 - GLA_LR
    w_gla = jnp.pad(w_in[:, :GLA_IN], ((0, 0), (0, lr_pad))).astype(BF16)
    w_rwkv = w_in[:, GLA_IN:].astype(BF16)
    p_gla = _matmul(x, w_gla, F32, tm, GLA_COLS)
    p_rwkv = _matmul(x, w_rwkv, F32, tm, RWKV_COLS)
    wa2p = jnp.pad(gla_wa2, ((0, lr_pad), (0, 0))).astype(BF16)
    o_gla = _gla_call(p_gla, wa2p, gla_ba, gla_norm)
    o_rwkv = _rwkv_call(p_rwkv, mu, w0, w2, a0, a2, g2, k_k, k_a, r_k, ln_w, ln_b)
    return _matmul_ln([o_gla, o_rwkv], [w_out[:GLA_V].astype(BF16), w_out[GLA_V:].astype(BF16)], x,
                      ln_g, ln_bias, tm)


def _xattn_sublayer(x, mem, wq, wk, wv, wo, ln_g, ln_bias):
    M = mem.shape[0]
    k_mem = _matmul(mem, wk.astype(BF16), BF16, M, D_MODEL)
    v_mem = _matmul(mem, wv.astype(BF16), BF16, M, D_MODEL)
    wq_scaled = (wq * XATTN_HD ** -0.5).astype(BF16)
    return _xattn_call(x, wq_scaled, k_mem, v_mem, wo.astype(BF16), ln_g, ln_bias, min(ROW_TILE, x.shape[0]))


def _ssd_moba_sublayer(x, w_in, conv_w, conv_b, dt_bias, a_log, d_skip, ssd_norm, w_out, ln_g, ln_bias):
    S = x.shape[0]
    tm = min(ROW_TILE, S)
    nb = S // MOBA_BLOCK
    o_dt = SSD_COLS
    o_q = o_dt + SSD_HEADS
    w_ssd = jnp.pad(w_in[:, :o_q], ((0, 0), (0, 128 - SSD_HEADS))).astype(BF16)
    w_q = w_in[:, o_q:o_q + MOBA_D]
    w_k = w_in[:, o_q + MOBA_D:o_q + 2 * MOBA_D]
    w_v = w_in[:, o_q + 2 * MOBA_D:]
    p_ssd = _matmul(x, w_ssd, F32, tm, SSD_COLS + 128)
    o_ssd = _ssd_call(p_ssd, conv_w, conv_b, dt_bias, a_log, d_skip, ssd_norm)
    qt, k4, kmean, vt4, knorm, qnorm = _moba_proj_call(x, (w_q.T * (MOBA_HD ** -0.5 * LOG2E)).astype(BF16), w_k.astype(BF16),
                                         w_v.T.astype(BF16))
    kmean_h = kmean.reshape(nb, MOBA_HEADS, MOBA_HD).transpose(1, 0, 2)
    ot_moba = _moba_call(_moba_first_block(knorm, qnorm, nb), qt, k4, vt4, kmean_h)
    return _matmul_ln([o_ssd, ot_moba], [w_out[:SSD_INNER].astype(BF16), w_out[SSD_INNER:].astype(BF16)], x,
                      ln_g, ln_bias, tm, transposed=(False, True))


def kernel(x, mem, l0_w_in, l0_gla_wa2, l0_gla_ba, l0_gla_norm, l0_rwkv_mu, l0_rwkv_w0, l0_rwkv_w2, l0_rwkv_a0, l0_rwkv_a2, l0_rwkv_g2, l0_rwkv_kk, l0_rwkv_ka, l0_rwkv_rk, l0_rwkv_lnw, l0_rwkv_lnb, l0_w_out, l0_ln1_g, l0_ln1_b, l0_xq, l0_xk, l0_xv, l0_xo, l0_ln2_g, l0_ln2_b, l0_ffn_wg, l0_ffn_wu, l0_ffn_wd, l0_ln3_g, l0_ln3_b, l1_w_in, l1_conv_w, l1_conv_b, l1_dt_bias, l1_a_log, l1_d_skip, l1_ssd_norm, l1_w_out, l1_ln1_g, l1_ln1_b, l1_xq, l1_xk, l1_xv, l1_xo, l1_ln2_g, l1_ln2_b, l1_router, l1_router_b, l1_exp_wg, l1_exp_wu, l1_exp_wd, l1_ln3_g, l1_ln3_b):
    x2 = x.reshape(-1, D_MODEL)
    mem2 = mem.reshape(-1, D_MODEL)
    x2 = _gla_rwkv_sublayer(x2, l0_w_in, l0_gla_wa2, l0_gla_ba, l0_gla_norm, l0_rwkv_mu, l0_rwkv_w0, l0_rwkv_w2,
                            l0_rwkv_a0, l0_rwkv_a2, l0_rwkv_g2, l0_rwkv_kk, l0_rwkv_ka, l0_rwkv_rk, l0_rwkv_lnw,
                            l0_rwkv_lnb, l0_w_out, l0_ln1_g, l0_ln1_b)
    x2 = _xattn_sublayer(x2, mem2, l0_xq, l0_xk, l0_xv, l0_xo, l0_ln2_g, l0_ln2_b)
    tm = min(ROW_TILE, x2.shape[0])
    x2 = _ffn_call(x2, l0_ffn_wg.astype(BF16), l0_ffn_wu.astype(BF16), l0_ffn_wd.astype(BF16),
                   l0_ln3_g, l0_ln3_b, tm, l0_ffn_wg.shape[1] // 2)
    x2 = _ssd_moba_sublayer(x2, l1_w_in, l1_conv_w, l1_conv_b, l1_dt_bias, l1_a_log, l1_d_skip, l1_ssd_norm,
                            l1_w_out, l1_ln1_g, l1_ln1_b)
    x2 = _xattn_sublayer(x2, mem2, l1_xq, l1_xk, l1_xv, l1_xo, l1_ln2_g, l1_ln2_b)
    x2 = _moe_sublayer(x2, l1_router, l1_router_b, l1_exp_wg.astype(BF16), l1_exp_wu.astype(BF16),
                       l1_exp_wd.astype(BF16), l1_ln3_g, l1_ln3_b)
    return x2.reshape(x.shape)
```

```python
import functools
import math

import jax
import jax.numpy as jnp
from jax import lax
from jax.experimental import pallas as pl
from jax.experimental.pallas import tpu as pltpu

BF16 = jnp.bfloat16
F32 = jnp.float32

D_MODEL = 1024
LN_EPS = 1e-5
DEPTH = 2
DN_ALPHA = (2 * DEPTH) ** 0.25

GLA_HEADS, GLA_DK, GLA_DV, GLA_CHUNK = 4, 64, 128, 64
GLA_QK, GLA_V, GLA_LR, GLA_TAU = 256, 512, 16, 16.0
GLA_COLS = 2 * GLA_QK + 2 * GLA_V + 128

RWKV_HEADS, RWKV_HD, RWKV_D, RWKV_CHUNK = 8, 64, 512, 64
RWKV_COLS = 1792
RWKV_DECAY_SCALE = math.exp(-0.5)
RWKV_GN_EPS = 64e-5

VMEM_LIMIT = 56 * 1024 * 1024
ROW_TILE = 512
WIDE_TILE = 1024
SCAN_TILE = 256
COMBINE_TILE = 256


def _cparams(sem):
    return pltpu.CompilerParams(dimension_semantics=sem, vmem_limit_bytes=VMEM_LIMIT)


def _dot(a, b):
    return jnp.dot(a.astype(BF16), b.astype(BF16), preferred_element_type=F32)


def _dot_nt(a, b):
    return lax.dot_general(a.astype(BF16), b.astype(BF16), (((1,), (1,)), ((), ())), preferred_element_type=F32)


def _dot_tn(a, b):
    return lax.dot_general(a.astype(BF16), b.astype(BF16), (((0,), (0,)), ((), ())), preferred_element_type=F32)


def _split3(x):
    hi = x.astype(BF16)
    r1 = x - hi.astype(F32)
    mid = r1.astype(BF16)
    lo = (r1 - mid.astype(F32)).astype(BF16)
    return hi, mid, lo


def _dot_exact_lhs(m, x):
    mb = m.astype(BF16)
    hi, mid, lo = _split3(x)
    return (jnp.dot(mb, hi, preferred_element_type=F32) + jnp.dot(mb, mid, preferred_element_type=F32)
            + jnp.dot(mb, lo, preferred_element_type=F32))


def _dot_exact_rhs(x, m):
    mb = m.astype(BF16)
    hi, mid, lo = _split3(x)
    return (jnp.dot(hi, mb, preferred_element_type=F32) + jnp.dot(mid, mb, preferred_element_type=F32)
            + jnp.dot(lo, mb, preferred_element_type=F32))


def _dot_stat_rhs(x, m):
    mb = m.astype(BF16)
    hi = x.astype(BF16)
    lo = (x - hi.astype(F32)).astype(BF16)
    return jnp.dot(hi, mb, preferred_element_type=F32) + jnp.dot(lo, mb, preferred_element_type=F32)


def _sigmoid(x):
    return 1.0 / (1.0 + jnp.exp(-x))


def _silu(x):
    return x * _sigmoid(x)


def _iota2(shape, axis):
    return lax.broadcasted_iota(jnp.int32, shape, axis)


def _chunk_tril(n, chunk):
    r = _iota2((n, n), 0)
    c = _iota2((n, n), 1)
    return jnp.where((c <= r) & ((r // chunk) == (c // chunk)), 1.0, 0.0)


def _head_block(n, width, value):
    r = _iota2((n, n), 0)
    c = _iota2((n, n), 1)
    return jnp.where((r // width) == (c // width), value, 0.0)


def _mm_kernel(x_ref, w_ref, o_ref):
    o_ref[...] = _dot(x_ref[...], w_ref[...]).astype(o_ref.dtype)


def _matmul(x, w, out_dtype, tm, tn):
    S, K = x.shape
    N = w.shape[1]
    return pl.pallas_call(
        _mm_kernel,
        grid=(S // tm, N // tn),
        in_specs=[pl.BlockSpec((tm, K), lambda i, j: (i, 0)),
                  pl.BlockSpec((K, tn), lambda i, j: (0, j))],
        out_specs=pl.BlockSpec((tm, tn), lambda i, j: (i, j)),
        out_shape=jax.ShapeDtypeStruct((S, N), out_dtype),
        compiler_params=_cparams(("parallel", "arbitrary")),
        name="matmul",
    )(x, w)


def _layer_norm_rows(y, g, b):
    mu = jnp.mean(y, -1, keepdims=True)
    d = y - mu
    var = jnp.mean(d * d, -1, keepdims=True)
    return d * lax.rsqrt(var + LN_EPS) * g + b


def _mm_ln_kernel(transposed, *refs):
    n_in = len(transposed)
    a_refs = refs[:n_in]
    w_refs = refs[n_in:2 * n_in]
    x_ref, g_ref, b_ref, o_ref = refs[2 * n_in:]
    acc = DN_ALPHA * x_ref[...]
    for a_ref, w_ref, tr in zip(a_refs, w_refs, transposed):
        acc = acc + (_dot_tn if tr else _dot)(a_ref[...], w_ref[...])
    o_ref[...] = _layer_norm_rows(acc, g_ref[...], b_ref[...])


def _matmul_ln(a_list, w_list, x, g, b, tm, transposed=None):
    S, D = x.shape
    transposed = tuple(transposed or (False,) * len(a_list))
    in_specs = ([pl.BlockSpec((a.shape[0], tm), lambda i: (0, i)) if tr else
                 pl.BlockSpec((tm, a.shape[1]), lambda i: (i, 0)) for a, tr in zip(a_list, transposed)]
                + [pl.BlockSpec(w.shape, lambda i: (0, 0)) for w in w_list]
                + [pl.BlockSpec((tm, D), lambda i: (i, 0)),
                   pl.BlockSpec((1, D), lambda i: (0, 0)),
                   pl.BlockSpec((1, D), lambda i: (0, 0))])
    return pl.pallas_call(
        functools.partial(_mm_ln_kernel, transposed),
        grid=(S // tm,),
        in_specs=in_specs,
        out_specs=pl.BlockSpec((tm, D), lambda i: (i, 0)),
        out_shape=jax.ShapeDtypeStruct((S, D), F32),
        compiler_params=_cparams(("parallel",)),
        name="matmul_ln",
    )(*a_list, *w_list, x, g.reshape(1, D), b.reshape(1, D))


def _gla_kernel(p_ref, wa2_ref, ba_ref, ng_ref, o_ref, st_ref, o_scr):
    C, H, dk, dv = GLA_CHUNK, GLA_HEADS, GLA_DK, GLA_DV
    tb = p_ref.shape[0]

    @pl.when(pl.program_id(0) == 0)
    def _():
        st_ref[...] = jnp.zeros_like(st_ref)

    z = _dot(p_ref[:, 2 * GLA_QK + 2 * GLA_V:], wa2_ref[...]) + ba_ref[...]
    log_a = -(jnp.maximum(-z, 0.0) + jnp.log(1.0 + jnp.exp(-jnp.abs(z)))) / GLA_TAU
    b = _dot_exact_lhs(_chunk_tril(tb, C), log_a)
    causal = _iota2((C, C), 1) <= _iota2((C, C), 0)

    nc = tb // C
    q_h, k_h, ke_h, v_h, dec_h = [], [], [], [], []
    for c in range(nc):
        rows = slice(c * C, (c + 1) * C)
        b_c = b[rows]
        b_last = b_c[C - 1:C]
        q_dec = p_ref[rows, 0:GLA_QK] * (dk ** -0.5) * jnp.exp(b_c)
        k_c = p_ref[rows, GLA_QK:2 * GLA_QK]
        k_dec = k_c * jnp.exp(-b_c)
        k_end = k_c * jnp.exp(b_last - b_c)
        decay = jnp.exp(b_last)
        for h in range(H):
            ks = slice(h * dk, (h + 1) * dk)
            q_h.append(q_dec[:, ks])
            k_h.append(k_dec[:, ks])
            ke_h.append(k_end[:, ks])
            dec_h.append(decay[:, ks])
            v_h.append(p_ref[rows, 2 * GLA_QK + h * dv:2 * GLA_QK + (h + 1) * dv])
    n = nc * H
    attn = [jnp.where(causal, _dot_nt(q_h[i], k_h[i]), 0.0) for i in range(n)]
    kv = [_dot_tn(v_h[i], ke_h[i]) for i in range(n)]
    intra = [_dot(attn[i], v_h[i]) for i in range(n)]
    state = [st_ref[:, h * dk:(h + 1) * dk] for h in range(H)]
    entering = []
    for i in range(n):
        entering.append(state[i % H])
        state[i % H] = state[i % H] * dec_h[i] + kv[i]
    for i in range(n):
        c, h = divmod(i, H)
        o_scr[c * C:(c + 1) * C, h * dv:(h + 1) * dv] = intra[i] + _dot_nt(q_h[i], entering[i])
    for h in range(H):
        st_ref[:, h * dk:(h + 1) * dk] = state[h]

    for h in range(H):
        vs = slice(h * dv, (h + 1) * dv)
        o_h = o_scr[:, vs]
        g_h = p_ref[:, 2 * GLA_QK + GLA_V + h * dv:2 * GLA_QK + GLA_V + (h + 1) * dv]
        o_h = o_h * lax.rsqrt(jnp.mean(o_h * o_h, -1, keepdims=True) + 1e-5) * ng_ref[:, vs]
        o_ref[:, vs] = (o_h * _silu(g_h)).astype(o_ref.dtype)


def _gla_call(p_gla, wa2p, ba, norm_g, tb=SCAN_TILE):
    S = p_gla.shape[0]
    return pl.pallas_call(
        _gla_kernel,
        grid=(S // tb,),
        in_specs=[pl.BlockSpec((tb, GLA_COLS), lambda i: (i, 0)),
                  pl.BlockSpec((128, GLA_QK), lambda i: (0, 0)),
                  pl.BlockSpec((1, GLA_QK), lambda i: (0, 0)),
                  pl.BlockSpec((1, GLA_V), lambda i: (0, 0))],
        out_specs=pl.BlockSpec((tb, GLA_V), lambda i: (i, 0)),
        out_shape=jax.ShapeDtypeStruct((S, GLA_V), BF16),
        scratch_shapes=[pltpu.VMEM((GLA_DV, GLA_QK), F32), pltpu.VMEM((tb, GLA_V), F32)],
        compiler_params=_cparams(("arbitrary",)),
        name="gla",
    )(p_gla, wa2p, ba.reshape(1, GLA_QK), norm_g.reshape(1, GLA_V))


def _rwkv_kernel(p_ref, mu_ref, w0_ref, w2_ref, a0_ref, a2_ref, g2_ref, kk_ref, ka_ref, rk_ref, lnw_ref, lnb_ref,
                 o_ref, prev_ref, h_ref, o_scr):
    C, H, N, D = RWKV_CHUNK, RWKV_HEADS, RWKV_HD, RWKV_D
    tb = p_ref.shape[0]
    first = pl.program_id(0) == 0

    @pl.when(first)
    def _():
        prev_ref[...] = jnp.zeros_like(prev_ref)
        h_ref[...] = jnp.zeros_like(h_ref)

    p = p_ref[...]
    shifted = jnp.where(_iota2(p.shape, 0) == 0, prev_ref[...], pltpu.roll(p, 1, 0))
    prev_ref[...] = p[tb - 1:tb]
    p = p + mu_ref[...] * (shifted - p)
    r = p[:, 0:D]
    k = p[:, D:2 * D]
    v = p[:, 2 * D:3 * D]
    xw = p[:, 3 * D:3 * D + 64]
    xa = p[:, 3 * D + 64:3 * D + 128]
    xg = p[:, 3 * D + 128:3 * D + 256]
    lw = -RWKV_DECAY_SCALE * _sigmoid(w0_ref[...] + _dot(jnp.tanh(xw), w2_ref[...]))
    a = _sigmoid(a0_ref[...] + _dot(xa, a2_ref[...]))
    g = _dot(_sigmoid(xg), g2_ref[...])
    head_ones = _head_block(D, N, 1.0)
    kk = k * kk_ref[...]
    kk = kk * lax.rsqrt(jnp.maximum(_dot_stat_rhs(kk * kk, head_ones), 1e-24))
    k = k * (1.0 + (a - 1.0) * ka_ref[...])
    pv = -kk * a
    cw = _dot_exact_lhs(_chunk_tril(tb, C), lw)
    cwx = cw - lw

    W2 = 2 * N
    left = _iota2((C, W2), 1) < N

    def blockdiag(a):
        return jnp.concatenate([jnp.where(left, a, 0.0), jnp.where(left, 0.0, a)], axis=0)

    def diag_blocks(full):
        return jnp.where(left, full[0:C], full[C:2 * C])

    gi = _iota2((2 * C, 2 * W2), 0)
    gj = _iota2((2 * C, 2 * W2), 1) % N
    gram_mask = ((gi < C) & (gj < gi)) | ((gi >= C) & (gj <= gi - C))
    eye = _iota2((C, W2), 0) == _iota2((C, W2), 1) % N
    eye_f = jnp.where(eye, 1.0, 0.0)
    zeros_c = jnp.zeros((C, W2), F32)
    zeros_w = jnp.zeros((W2, W2), F32)

    nc = tb // C
    items = [(c, p) for c in range(nc) for p in range(H // 2)]
    xs, ys, pk_e, b_h, v_h, r_h, g_h = [], [], [], [], [], [], []
    for c in range(nc):
        rows = slice(c * C, (c + 1) * C)
        cw_c = cw[rows]
        cw_end = cw_c[C - 1:C]
        e_pos = jnp.exp(cw_c)
        e_neg = jnp.exp(-cw_c)
        e_end = jnp.exp(cw_end - cw_c)
        r_t = r[rows] * e_pos
        b_t = kk[rows] * jnp.exp(cwx[rows])
        p_t = pv[rows] * e_neg
        k_t = k[rows] * e_neg
        p_e = pv[rows] * e_end
        k_e = k[rows] * e_end
        g_end = jnp.exp(cw_end)
        v_c = v[rows]
        for p in range(H // 2):
            ps = slice(p * W2, (p + 1) * W2)
            xs.append(jnp.concatenate([b_t[:, ps], r_t[:, ps]], axis=0))
            ys.append(jnp.concatenate([blockdiag(p_t[:, ps]), blockdiag(k_t[:, ps])], axis=0))
            pk_e.append(jnp.concatenate([p_e[:, ps], k_e[:, ps]], axis=0))
            b_h.append(b_t[:, ps])
            v_h.append(v_c[:, ps])
            r_h.append(r_t[:, ps])
            g_h.append(g_end[:, ps])
    n = len(items)
    grams = [jnp.where(gram_mask, _dot_nt(xs[i], ys[i]), 0.0) for i in range(n)]
    l_p = [g[0:C, 0:W2] for g in grams]
    m_pk = [g[C:2 * C, :] for g in grams]
    v_bd = [blockdiag(v_h[i]) for i in range(n)]
    lkv = [_dot(grams[i][0:C, W2:2 * W2], v_bd[i]) for i in range(n)]
    x = [_dot(lp, blockdiag(lp)) for lp in l_p]
    t = [eye_f + lp for lp in l_p]
    for _ in range(4):
        tx = [_dot(jnp.concatenate([t[i], x[i]], axis=0), blockdiag(x[i])) for i in range(n)]
        t = [t[i] + tx[i][0:C] for i in range(n)]
        x = [tx[i][C:2 * C] for i in range(n)]
    t = [t[i] + _dot(t[i], blockdiag(x[i])) for i in range(n)]
    wu = [_dot(t[i], jnp.concatenate([blockdiag(b_h[i]), blockdiag(lkv[i])], axis=1)) for i in range(n)]
    az = [_dot_tn(pk_e[i], jnp.concatenate([wu[i], jnp.concatenate([zeros_c, v_h[i]], axis=1)], axis=0))
          for i in range(n)]
    qo = [_dot(m_pk[i], jnp.concatenate(
        [jnp.concatenate([blockdiag(wu[i][:, 0:W2]), blockdiag(wu[i][:, W2:2 * W2])], axis=1),
         jnp.concatenate([zeros_w, v_bd[i]], axis=1)], axis=0)) for i in range(n)]
    state = [h_ref[p] for p in range(H // 2)]
    for i, (c, p) in enumerate(items):
        a_mat = diag_blocks(az[i][:, 0:W2]) + jnp.where(eye, g_h[i], 0.0)
        q_mat = qo[i][:, 0:W2] + r_h[i]
        oh = _dot(jnp.concatenate([q_mat, a_mat], axis=0), blockdiag(state[p]))
        o_scr[c * C:(c + 1) * C, p * W2:(p + 1) * W2] = oh[0:C] + qo[i][:, W2:2 * W2]
        state[p] = oh[C:C + N] + diag_blocks(az[i][:, W2:2 * W2])
    for p in range(H // 2):
        h_ref[p] = state[p]

    o = o_scr[...]
    head_mean = _head_block(D, N, 1.0 / N)
    mean = _dot_stat_rhs(o, head_mean)
    d = o - mean
    var = _dot_stat_rhs(d * d, head_mean)
    o = d * lax.rsqrt(var + RWKV_GN_EPS) * lnw_ref[...] + lnb_ref[...]
    bonus = _dot_stat_rhs(r * k * rk_ref[...], head_ones) * v
    o_ref[...] = ((o + bonus) * g).astype(o_ref.dtype)


def _rwkv_call(p_rwkv, mu, w0, w2, a0, a2, g2, k_k, k_a, r_k, ln_w, ln_b, tb=SCAN_TILE):
    S = p_rwkv.shape[0]
    D = RWKV_D
    row = lambda t: t.reshape(1, -1).astype(F32)
    full = lambda shape: pl.BlockSpec(shape, lambda i: tuple(0 for _ in shape))
    return pl.pallas_call(
        _rwkv_kernel,
        grid=(S // tb,),
        in_specs=[pl.BlockSpec((tb, RWKV_COLS), lambda i: (i, 0)),
                  full((1, RWKV_COLS)), full((1, D)), full((64, D)), full((1, D)), full((64, D)), full((128, D)),
                  full((1, D)), full((1, D)), full((1, D)), full((1, D)), full((1, D))],
        out_specs=pl.BlockSpec((tb, D), lambda i: (i, 0)),
        out_shape=jax.ShapeDtypeStruct((S, D), BF16),
        scratch_shapes=[pltpu.VMEM((1, RWKV_COLS), F32),
                        pltpu.VMEM((RWKV_HEADS // 2, RWKV_HD, 2 * RWKV_HD), F32),
                        pltpu.VMEM((tb, D), F32)],
        compiler_params=_cparams(("arbitrary",)),
        name="rwkv7",
    )(p_rwkv, row(mu), row(w0), w2.astype(BF16), row(a0), a2.astype(BF16), g2.astype(BF16),
      row(k_k), row(k_a), row(r_k), row(ln_w), row(ln_b))


XATTN_HEADS, XATTN_HD = 4, 256


def _xattn_kernel(x_ref, wq_ref, k_ref, v_ref, wo_ref, g_ref, b_ref, o_ref):
    x = x_ref[...]
    q = _dot(x, wq_ref[...])
    outs = []
    for h in range(XATTN_HEADS):
        hs = slice(h * XATTN_HD, (h + 1) * XATTN_HD)
        s = _dot_nt(q[:, hs], k_ref[:, hs])
        e = jnp.exp(s - jnp.max(s, -1, keepdims=True))
        p = e / jnp.sum(e, -1, keepdims=True)
        outs.append(_dot(p, v_ref[:, hs]))
    o = jnp.concatenate(outs, axis=1)
    y = DN_ALPHA * x + _dot(o, wo_ref[...])
    o_ref[...] = _layer_norm_rows(y, g_ref[...], b_ref[...])


def _xattn_call(x, wq_scaled, k_mem, v_mem, wo, g, b, tm):
    S, D = x.shape
    M = k_mem.shape[0]
    const = lambda shape: pl.BlockSpec(shape, lambda i: (0, 0))
    return pl.pallas_call(
        _xattn_kernel,
        grid=(S // tm,),
        in_specs=[pl.BlockSpec((tm, D), lambda i: (i, 0)), const((D, D)), const((M, D)), const((M, D)),
                  const((D, D)), const((1, D)), const((1, D))],
        out_specs=pl.BlockSpec((tm, D), lambda i: (i, 0)),
        out_shape=jax.ShapeDtypeStruct((S, D), F32),
        compiler_params=_cparams(("parallel",)),
        name="xattn",
    )(x, wq_scaled, k_mem, v_mem, wo, g.reshape(1, D), b.reshape(1, D))


def _ffn_kernel(x_ref, wg_ref, wu_ref, wd_ref, g_ref, b_ref, o_ref, acc_ref):
    j = pl.program_id(1)
    x = x_ref[...]
    xb = x.astype(BF16)
    part = _dot(_silu(_dot(xb, wg_ref[...])) * _dot(xb, wu_ref[...]), wd_ref[...])

    last = pl.num_programs(1) - 1

    @pl.when(j == 0)
    def _():
        acc_ref[...] = DN_ALPHA * x + part

    @pl.when((j > 0) & (j < last))
    def _():
        acc_ref[...] += part

    @pl.when(j == last)
    def _():
        o_ref[...] = _layer_norm_rows(acc_ref[...] + part, g_ref[...], b_ref[...])


def _ffn_call(x, wg, wu, wd, g, b, tm, tf):
    S, D = x.shape
    F = wg.shape[1]
    assert F % tf == 0 and F // tf >= 2
    return pl.pallas_call(
        _ffn_kernel,
        grid=(S // tm, F // tf),
        in_specs=[pl.BlockSpec((tm, D), lambda i, j: (i, 0)),
                  pl.BlockSpec((D, tf), lambda i, j: (0, j)),
                  pl.BlockSpec((D, tf), lambda i, j: (0, j)),
                  pl.BlockSpec((tf, D), lambda i, j: (j, 0)),
                  pl.BlockSpec((1, D), lambda i, j: (0, 0)),
                  pl.BlockSpec((1, D), lambda i, j: (0, 0))],
        out_specs=pl.BlockSpec((tm, D), lambda i, j: (i, 0)),
        out_shape=jax.ShapeDtypeStruct((S, D), F32),
        scratch_shapes=[pltpu.VMEM((tm, D), F32)],
        compiler_params=_cparams(("parallel", "arbitrary")),
        name="ffn",
    )(x, wg, wu, wd, g.reshape(1, D), b.reshape(1, D))


SSD_HD, SSD_HEADS, SSD_INNER, SSD_GROUPS, SSD_STATE = 64, 16, 1024, 2, 128
SSD_BC, SSD_CONV, SSD_CONV_CH, SSD_CHUNK = 256, 4, 1536, 128
SSD_COLS = SSD_INNER + SSD_CONV_CH
SSD_GW = SSD_INNER // SSD_GROUPS


def _softplus(x):
    return jnp.maximum(x, 0.0) + jnp.log(1.0 + jnp.exp(-jnp.abs(x)))


def _ssd_kernel(p_ref, cw_ref, cb_ref, dtb_ref, a_ref, dsk_ref, ng_ref, o_ref, prev_ref, st_ref, y_scr):
    L, G, NS, HD = SSD_CHUNK, SSD_GROUPS, SSD_STATE, SSD_HD
    HG = SSD_HEADS // G

    @pl.when(pl.program_id(0) == 0)
    def _():
        prev_ref[...] = jnp.zeros_like(prev_ref)
        st_ref[...] = jnp.zeros_like(st_ref)

    cur = p_ref[:, SSD_INNER:SSD_COLS]
    tail = prev_ref[...]
    row = _iota2(tail.shape, 0)
    conv = cur * cw_ref[SSD_CONV - 1:SSD_CONV, :] + cb_ref[...]
    for kk in range(1, SSD_CONV):
        rolled = pltpu.roll(cur, kk, 0)
        head = jnp.where(row < kk, pltpu.roll(tail, kk, 0), rolled[0:8])
        shifted = jnp.concatenate([head, rolled[8:]], axis=0)
        conv = conv + shifted * cw_ref[SSD_CONV - 1 - kk:SSD_CONV - kk, :]
    prev_ref[...] = cur[L - 8:L]
    xbc = _silu(conv)
    xs = xbc[:, :SSD_INNER]

    dt = _softplus(p_ref[:, SSD_COLS:] + dtb_ref[...])
    a_col = dt * a_ref[...]
    li = _iota2((L, L), 0)
    lj = _iota2((L, L), 1)
    cs = _dot_exact_lhs(jnp.where(lj <= li, 1.0, 0.0), a_col)
    cs_row = cs.T
    expand = jnp.where(_iota2((128, SSD_INNER), 1) // HD == _iota2((128, SSD_INNER), 0), 1.0, 0.0)
    dt_x = _dot_stat_rhs(dt, expand)
    cs_x = _dot_exact_rhs(cs, expand)
    cs_end = cs_x[L - 1:L]
    xd = xs * dt_x
    xd_dec = xd * jnp.exp(cs_end - cs_x)
    out_dec = jnp.exp(cs_x)
    chunk_dec = jnp.exp(cs_end)
    tril = lj <= li

    for g in range(G):
        gs = slice(g * SSD_GW, (g + 1) * SSD_GW)
        b_g = xbc[:, SSD_INNER + g * NS:SSD_INNER + (g + 1) * NS]
        c_g = xbc[:, SSD_INNER + SSD_BC + g * NS:SSD_INNER + SSD_BC + (g + 1) * NS]
        cb = _dot_nt(c_g, b_g)
        for j in range(HG):
            h = g * HG + j
            hs = slice(h * HD, (h + 1) * HD)
            seg = jnp.where(tril, jnp.exp(cs[:, h:h + 1] - cs_row[h:h + 1, :]), 0.0)
            y_scr[:, hs] = _dot(cb * seg, xd[:, hs])
        st = st_ref[g]
        y_off = _dot(c_g, st) * out_dec[:, gs]
        st_ref[g] = st * chunk_dec[:, gs] + _dot_tn(b_g, xd_dec[:, gs])
        y_scr[:, gs] = y_scr[:, gs] + y_off

    y = (y_scr[...] + dsk_ref[...] * xs) * _silu(p_ref[:, :SSD_INNER])
    for g in range(G):
        gs = slice(g * SSD_GW, (g + 1) * SSD_GW)
        y_g = y[:, gs]
        o_ref[:, gs] = (y_g * lax.rsqrt(jnp.mean(y_g * y_g, -1, keepdims=True) + 1e-5) * ng_ref[:, gs]).astype(o_ref.dtype)


def _ssd_call(p_ssd, conv_w, conv_b, dt_bias, a_log, d_skip, norm_g):
    S = p_ssd.shape[0]
    L = SSD_CHUNK
    a_neg = -jnp.exp(a_log.astype(F32))
    pad = lambda t: jnp.pad(t.astype(F32), (0, 128 - SSD_HEADS)).reshape(1, 128)
    const = lambda shape: pl.BlockSpec(shape, lambda i: (0, 0))
    return pl.pallas_call(
        _ssd_kernel,
        grid=(S // L,),
        in_specs=[pl.BlockSpec((L, SSD_COLS + 128), lambda i: (i, 0)),
                  const((SSD_CONV, SSD_CONV_CH)), const((1, SSD_CONV_CH)),
                  const((1, 128)), const((1, 128)),
                  const((1, SSD_INNER)), const((1, SSD_INNER))],
        out_specs=pl.BlockSpec((L, SSD_INNER), lambda i: (i, 0)),
        out_shape=jax.ShapeDtypeStruct((S, SSD_INNER), BF16),
        scratch_shapes=[pltpu.VMEM((8, SSD_CONV_CH), F32),
                        pltpu.VMEM((SSD_GROUPS, SSD_STATE, SSD_GW), F32),
                        pltpu.VMEM((L, SSD_INNER), F32)],
        compiler_params=_cparams(("arbitrary",)),
        name="ssd",
    )(p_ssd, conv_w.astype(F32), conv_b.reshape(1, -1).astype(F32), pad(dt_bias), pad(a_neg),
      jnp.repeat(d_skip.astype(F32), SSD_HD).reshape(1, -1), norm_g.reshape(1, -1).astype(F32))


MOBA_HD, MOBA_HEADS, MOBA_D, MOBA_BLOCK, MOBA_TOPK = 64, 8, 512, 256, 3
MOBA_GROUP = 4
MOBA_UNDERFLOW = -160.0
MOBA_BOUND_SLACK = 1.001
NEG_BIG = -1e30
LOG2E = math.log2(math.e)


MOBA_VROWS = MOBA_HD + 16


def _moba_proj_kernel(x_ref, wqt_ref, wk_ref, wvt_ref, qt_ref, k_ref, kmean_ref, v_ref, knorm_ref, qnorm_ref):
    xb = x_ref[...].astype(BF16)
    qt = _dot_nt(wqt_ref[...], xb).astype(qt_ref.dtype)
    qt_ref[...] = qt
    vt = _dot_nt(wvt_ref[...], xb)
    extra = jnp.where(_iota2((MOBA_VROWS - MOBA_HD, MOBA_BLOCK), 0) == 0, 1.0, 0.0)
    for h in range(MOBA_HEADS):
        v_ref[h, 0] = jnp.concatenate([vt[h * MOBA_HD:(h + 1) * MOBA_HD], extra], axis=0).astype(v_ref.dtype)
    k = _dot(xb, wk_ref[...])
    kmean_ref[0] = jnp.mean(k, 0, keepdims=True)
    shape = (MOBA_BLOCK, 2 * MOBA_HD)
    lane = _iota2(shape, 1)
    pos = jnp.where((lane == MOBA_HD) | (lane == MOBA_HD + 1), _iota2(shape, 0).astype(F32), 0.0).astype(k_ref.dtype)
    k_b = k.astype(k_ref.dtype)
    for h in range(MOBA_HEADS):
        k_ref[h, 0] = pos
        k_ref[h, 0, :, 0:MOBA_HD] = k_b[:, h * MOBA_HD:(h + 1) * MOBA_HD]
    k_f = k_b.astype(F32)
    q_f = qt.astype(F32)
    head_cols = jnp.where(_iota2((MOBA_D, 128), 0) // MOBA_HD == _iota2((MOBA_D, 128), 1), 1.0, 0.0)
    head_rows = jnp.where(_iota2((128, MOBA_D), 1) // MOBA_HD == _iota2((128, MOBA_D), 0), 1.0, 0.0)
    knorm_ref[0] = jnp.sqrt(jnp.max(_dot_stat_rhs(k_f * k_f, head_cols), 0, keepdims=True))
    qnorm_ref[0] = jnp.sqrt(jnp.max(_dot_exact_lhs(head_rows, q_f * q_f), 1, keepdims=True))


def _moba_proj_call(x, wqt, wk, wvt):
    S, D = x.shape
    nb = S // MOBA_BLOCK
    const = lambda shape: pl.BlockSpec(shape, lambda i: (0, 0))
    return pl.pallas_call(
        _moba_proj_kernel,
        grid=(nb,),
        in_specs=[pl.BlockSpec((MOBA_BLOCK, D), lambda i: (i, 0)), const((MOBA_D, D)), const((D, MOBA_D)),
                  const((MOBA_D, D))],
        out_specs=[pl.BlockSpec((MOBA_D, MOBA_BLOCK), lambda i: (0, i)),
                   pl.BlockSpec((MOBA_HEADS, 1, MOBA_BLOCK, 2 * MOBA_HD), lambda i: (0, i, 0, 0)),
                   pl.BlockSpec((1, 1, MOBA_D), lambda i: (i, 0, 0)),
                   pl.BlockSpec((MOBA_HEADS, 1, MOBA_VROWS, MOBA_BLOCK), lambda i: (0, i, 0, 0)),
                   pl.BlockSpec((1, 1, 128), lambda i: (i, 0, 0)),
                   pl.BlockSpec((1, 128, 1), lambda i: (i, 0, 0))],
        out_shape=[jax.ShapeDtypeStruct((MOBA_D, S), BF16),
                   jax.ShapeDtypeStruct((MOBA_HEADS, nb, MOBA_BLOCK, 2 * MOBA_HD), BF16),
                   jax.ShapeDtypeStruct((nb, 1, MOBA_D), F32),
                   jax.ShapeDtypeStruct((MOBA_HEADS, nb, MOBA_VROWS, MOBA_BLOCK), BF16),
                   jax.ShapeDtypeStruct((nb, 1, 128), F32),
                   jax.ShapeDtypeStruct((nb, 128, 1), F32)],
        compiler_params=_cparams(("parallel",)),
        name="moba_proj",
    )(x, wqt, wk, wvt)


def _moba_kernel(first_ref, qt_ref, k_ref, vt_ref, kmean_ref, o_ref, sel_ref, s0_ref, s1_ref, p0_ref, p1_ref):
    BS, HD = MOBA_BLOCK, MOBA_HD
    h = pl.program_id(0)
    i = pl.program_id(1)
    nb = k_ref.shape[0]
    qt = qt_ref[...]
    slope = LOG2E * jnp.exp2(jnp.zeros((1, BS), F32) - (h + 1).astype(F32) * (8.0 / MOBA_HEADS))
    slope_hi = slope.astype(qt.dtype).astype(F32)
    slope_lo = slope - slope_hi
    row = _iota2((HD, BS), 0)
    qt_ext = jnp.concatenate(
        [qt, jnp.where(row == 0, slope_hi, jnp.where(row == 1, slope_lo, 0.0)).astype(qt.dtype)], axis=0)

    gate = _dot(kmean_ref[...], qt)
    blk = _iota2((nb, BS), 0).astype(F32)
    cand = blk < i.astype(F32)
    sel = jnp.zeros((nb, BS), F32)
    for _ in range(MOBA_TOPK):
        best = jnp.max(jnp.where(cand, gate, -jnp.inf), 0, keepdims=True)
        idx = jnp.min(jnp.where(cand & (gate == best), blk, float(nb)), 0, keepdims=True)
        pick = blk == idx
        sel = jnp.where(pick, 1.0, sel)
        cand = cand & jnp.logical_not(pick)
    sel_ref[...] = sel

    G = MOBA_GROUP
    last = nb - 1
    s_ref = (s0_ref, s1_ref)
    p_ref = (p0_ref, p1_ref)

    first = first_ref[h * nb + i]

    def group(u):
        return [jnp.clip(first + G * u + x, 0, last) for x in range(G)]

    def issue_scores(u, slot):
        tops = []
        for x, j in enumerate(group(u)):
            sc = _dot(k_ref[j], qt_ext)
            s_ref[slot][x] = sc
            tops.append(jnp.max(sc, 0, keepdims=True))
        return tuple(tops)

    def value_blocks(u):
        js = group(u)
        js[0] = jnp.where(u == -1, i, js[0])
        return js

    def weighted_values(u, slot):
        out = None
        for x, j in enumerate(value_blocks(u)):
            part = _dot(vt_ref[j], p_ref[slot][x])
            out = part if out is None else out + part
        return out

    def step(u, slot, tops, a_prev, m, acc, issue_next=True):
        on, shift = [], []
        m_new = m
        for x, j in enumerate(group(u)):
            on.append(sel_ref[pl.ds(j, 1), :] > 0.0)
            shift.append(slope * ((j - i) * BS).astype(F32))
            m_new = jnp.maximum(m_new, jnp.where(on[x], tops[x] + shift[x], NEG_BIG))
        tops_next, pv = [], None
        for x, (j_prev, j_next) in enumerate(zip(value_blocks(u - 1), group(u + 1))):
            part = _dot(vt_ref[j_prev], p_ref[1 - slot][x])
            pv = part if pv is None else pv + part
            if issue_next:
                sc = _dot(k_ref[j_next], qt_ext)
                s_ref[1 - slot][x] = sc
                tops_next.append(jnp.max(sc, 0, keepdims=True))
            p = jnp.exp2(s_ref[slot][x] - (jnp.where(on[x], m_new, -NEG_BIG) - shift[x]))
            p_ref[slot][x] = p.astype(BF16)
        return tuple(tops_next), jnp.exp2(m - m_new), m_new, a_prev * acc + pv

    def body(w, carry):
        carry = step(2 * w, 0, *carry)
        return step(2 * w + 1, 1, *carry)

    tops0 = issue_scores(0, 0)
    s_own = jnp.where(_iota2((BS, BS), 1) >= _iota2((BS, BS), 0), _dot(k_ref[i], qt_ext), NEG_BIG)
    m_own = jnp.max(s_own, 0, keepdims=True)
    p1_ref[1:G] = jnp.zeros((G - 1, BS, BS), BF16)
    p1_ref[0] = jnp.exp2(s_own - m_own).astype(BF16)
    init = (tops0, jnp.ones((1, BS), F32), m_own, jnp.zeros((vt_ref.shape[1], BS), F32))
    steps = (i - first + G - 1) // G
    pairs = steps // 2
    carry = lax.fori_loop(0, pairs, body, init)

    def odd_tail(carry):
        _, a_last, _, acc = step(2 * pairs, 0, *carry, issue_next=False)
        return a_last * acc + weighted_values(2 * pairs, 0)

    def even_tail(carry):
        _, a_prev, _, acc = carry
        return a_prev * acc + weighted_values(2 * pairs - 1, 1)

    acc = lax.cond(steps % 2 == 1, odd_tail, even_tail, carry)
    o_ref[...] = (acc[0:HD] / acc[HD:HD + 1]).astype(o_ref.dtype)


def _moba_first_block(knorm, qnorm, nb):
    BS = MOBA_BLOCK
    k_norm = knorm[:, 0, 0:MOBA_HEADS].T
    q_norm = qnorm[:, 0:MOBA_HEADS, 0].T
    slope = LOG2E * jnp.exp2(-(jnp.arange(MOBA_HEADS, dtype=F32) + 1.0) * (8.0 / MOBA_HEADS))
    i_idx = jnp.arange(nb, dtype=F32)[None, :, None]
    j_idx = jnp.arange(nb, dtype=F32)[None, None, :]
    reach = slope[:, None, None] * ((BS - 1.0) - BS * (i_idx - j_idx))
    bound = q_norm[:, :, None] * (k_norm[:, None, :] + k_norm[:, :, None]) * MOBA_BOUND_SLACK + reach
    matters = (bound >= MOBA_UNDERFLOW) & (j_idx < i_idx)
    first = jnp.min(jnp.where(matters, j_idx, i_idx), axis=2).astype(jnp.int32)
    return first.reshape(-1)


def _moba_call(first_block, qt, k4, vt4, kmean):
    S = qt.shape[1]
    nb = S // MOBA_BLOCK
    return pl.pallas_call(
        _moba_kernel,
        grid_spec=pltpu.PrefetchScalarGridSpec(
            num_scalar_prefetch=1,
            grid=(MOBA_HEADS, nb),
            in_specs=[pl.BlockSpec((MOBA_HD, MOBA_BLOCK), lambda h, i, first: (h, i)),
                      pl.BlockSpec((None, nb, MOBA_BLOCK, 2 * MOBA_HD), lambda h, i, first: (h, 0, 0, 0)),
                      pl.BlockSpec((None, nb, MOBA_VROWS, MOBA_BLOCK), lambda h, i, first: (h, 0, 0, 0)),
                      pl.BlockSpec((None, nb, MOBA_HD), lambda h, i, first: (h, 0, 0))],
            out_specs=pl.BlockSpec((MOBA_HD, MOBA_BLOCK), lambda h, i, first: (h, i)),
            scratch_shapes=[pltpu.VMEM((nb, MOBA_BLOCK), F32)]
            + [pltpu.VMEM((MOBA_GROUP, MOBA_BLOCK, MOBA_BLOCK), F32)] * 2
            + [pltpu.VMEM((MOBA_GROUP, MOBA_BLOCK, MOBA_BLOCK), BF16)] * 2),
        out_shape=jax.ShapeDtypeStruct((MOBA_D, S), BF16),
        compiler_params=_cparams(("parallel", "arbitrary")),
        name="moba",
    )(first_block, qt, k4, vt4, kmean)


N_EXPERTS, TOP_K, EXPERT_FF = 8, 2, 2816
MOE_ROWS = 512


def _router_kernel(x_ref, whi_ref, wlo_ref, b_ref, o_ref):
    x = x_ref[...]
    xhi = x.astype(BF16)
    xlo = (x - xhi.astype(F32)).astype(BF16)
    logits = (jnp.dot(xhi, whi_ref[...], preferred_element_type=F32)
              + jnp.dot(xhi, wlo_ref[...], preferred_element_type=F32)
              + jnp.dot(xlo, whi_ref[...], preferred_element_type=F32)) + b_ref[...]
    lane = _iota2(logits.shape, 1).astype(F32)
    logits = jnp.where(lane < N_EXPERTS, logits, -jnp.inf)
    m1 = jnp.max(logits, -1, keepdims=True)
    i1 = jnp.min(jnp.where(logits == m1, lane, 128.0), -1, keepdims=True)
    rest = jnp.where(lane == i1, -jnp.inf, logits)
    m2 = jnp.max(rest, -1, keepdims=True)
    i2 = jnp.min(jnp.where(rest == m2, lane, 128.0), -1, keepdims=True)
    e = jnp.exp(m2 - m1)
    g1 = 1.0 / (1.0 + e)
    g2 = e / (1.0 + e)
    out = jnp.where(lane == 0, i1, 0.0)
    out = jnp.where(lane == 1, i2, out)
    out = jnp.where(lane == 2, g1, out)
    out = jnp.where(lane == 3, g2, out)
    o_ref[...] = out


def _router_call(x, w_router, b_router, tm):
    S, D = x.shape
    wp = jnp.pad(w_router.astype(F32), ((0, 0), (0, 128 - N_EXPERTS)))
    whi = wp.astype(BF16)
    wlo = (wp - whi.astype(F32)).astype(BF16)
    bp = jnp.pad(b_router.astype(F32), (0, 128 - N_EXPERTS)).reshape(1, 128)
    const = lambda shape: pl.BlockSpec(shape, lambda i: (0, 0))
    return pl.pallas_call(
        _router_kernel,
        grid=(S // tm,),
        in_specs=[pl.BlockSpec((tm, D), lambda i: (i, 0)), const((D, 128)), const((D, 128)), const((1, 128))],
        out_specs=pl.BlockSpec((tm, 128), lambda i: (i, 0)),
        out_shape=jax.ShapeDtypeStruct((S, 128), F32),
        compiler_params=_cparams(("parallel",)),
        name="router",
    )(x, whi, wlo, bp)


def _row_copy(src_hbm, dst_ref, src_row, dst_row, sem):
    return pltpu.make_async_copy(src_hbm.at[pl.ds(src_row, 1)], dst_ref.at[pl.ds(dst_row, 1)], sem)


def _scatter_rows_kernel(dest_ref, lo_ref, hi_ref, x_ref, o_hbm, zero_ref, xs_ref, sem, tile_sem):
    step = pl.program_id(0)
    n_tiles = pl.num_programs(0) - 1
    tm = x_ref.shape[0]

    def row_out(src_ref, src_row, dst_row, s):
        return pltpu.make_async_copy(src_ref.at[pl.ds(src_row, 1)], o_hbm.at[pl.ds(dst_row, 1)], s)

    def wait_tile(buf):
        for slot in range(TOP_K):
            pltpu.make_async_copy(xs_ref.at[buf], o_hbm.at[pl.ds(0, tm)], tile_sem.at[buf]).wait()

    @pl.when(step < n_tiles)
    def _():
        base = step * tm
        buf = step % 2
        xs_ref[buf] = x_ref[...]

        def start(r, c):
            for slot in range(TOP_K):
                row_out(xs_ref.at[buf], r, dest_ref[TOP_K * (base + r) + slot], tile_sem.at[buf]).start()
            return c

        lax.fori_loop(0, tm, start, 0, unroll=8)

        @pl.when(step > 0)
        def _():
            wait_tile(1 - buf)

    @pl.when(step == n_tiles)
    def _():
        wait_tile((n_tiles - 1) % 2)
        zero_ref[...] = jnp.zeros_like(zero_ref)
        for e in range(N_EXPERTS):
            def start(r, c):
                row_out(zero_ref, 0, r, sem).start()
                return c

            def wait(r, c):
                row_out(zero_ref, 0, r, sem).wait()
                return c

            lax.fori_loop(lo_ref[e], hi_ref[e], start, 0)
            lax.fori_loop(lo_ref[e], hi_ref[e], wait, 0)


def _scatter_rows(x, dest, pad_lo, pad_hi, n_rows, tm):
    T, D = x.shape
    n_tiles = T // tm
    return pl.pallas_call(
        _scatter_rows_kernel,
        grid_spec=pltpu.PrefetchScalarGridSpec(
            num_scalar_prefetch=3,
            grid=(n_tiles + 1,),
            in_specs=[pl.BlockSpec((tm, D), lambda i, d, lo, hi: (jnp.minimum(i, n_tiles - 1), 0))],
            out_specs=pl.BlockSpec(memory_space=pl.ANY),
            scratch_shapes=[pltpu.VMEM((8, D), x.dtype), pltpu.VMEM((2, tm, D), x.dtype),
                            pltpu.SemaphoreType.DMA(()), pltpu.SemaphoreType.DMA((2,))]),
        out_shape=jax.ShapeDtypeStruct((n_rows, D), x.dtype),
        compiler_params=_cparams(("arbitrary",)),
        name="moe_scatter",
    )(dest, pad_lo, pad_hi, x)


def _moe_ffn_kernel(be_ref, nu_ref, x_ref, wg_ref, wu_ref, wd_ref, o_ref, acc_ref):
    i = pl.program_id(0)
    j = pl.program_id(1)

    last = pl.num_programs(1) - 1

    @pl.when(i < nu_ref[0])
    def _():
        xb = x_ref[...].astype(BF16)
        part = _dot(_silu(_dot(xb, wg_ref[...])) * _dot(xb, wu_ref[...]), wd_ref[...])

        @pl.when(j == 0)
        def _():
            acc_ref[...] = part

        @pl.when((j > 0) & (j < last))
        def _():
            acc_ref[...] += part

        @pl.when(j == last)
        def _():
            o_ref[...] = acc_ref[...] + part

    @pl.when((j == last) & (i >= nu_ref[0]))
    def _():
        o_ref[...] = jnp.zeros_like(o_ref)


def _moe_ffn_call(x_rows, block_e, n_used, wg, wu, wd, tf):
    n, D = x_rows.shape
    F = wg.shape[2]
    R = MOE_ROWS
    assert F % tf == 0 and F // tf >= 2
    return pl.pallas_call(
        _moe_ffn_kernel,
        grid_spec=pltpu.PrefetchScalarGridSpec(
            num_scalar_prefetch=2,
            grid=(n // R, F // tf),
            in_specs=[pl.BlockSpec((R, D), lambda i, j, be, nu: (i, 0)),
                      pl.BlockSpec((None, D, tf), lambda i, j, be, nu: (be[i], 0, j)),
                      pl.BlockSpec((None, D, tf), lambda i, j, be, nu: (be[i], 0, j)),
                      pl.BlockSpec((None, tf, D), lambda i, j, be, nu: (be[i], j, 0))],
            out_specs=pl.BlockSpec((R, D), lambda i, j, be, nu: (i, 0)),
            scratch_shapes=[pltpu.VMEM((R, D), F32)]),
        out_shape=jax.ShapeDtypeStruct((n, D), F32),
        compiler_params=_cparams(("arbitrary", "arbitrary")),
        name="moe_ffn",
    )(block_e, n_used, x_rows, wg, wu, wd)


def _moe_combine_kernel(d_ref, y_hbm, x_ref, r_ref, g_ref, b_ref, o_ref, y1_ref, y2_ref, sem):
    tm = x_ref.shape[0]
    t = pl.program_id(0)

    def issue(tile, buf):
        base = tile * tm

        def start(r, c):
            _row_copy(y_hbm, y1_ref.at[buf], d_ref[2 * (base + r)], r, sem.at[buf, 0]).start()
            _row_copy(y_hbm, y2_ref.at[buf], d_ref[2 * (base + r) + 1], r, sem.at[buf, 1]).start()
            return c

        lax.fori_loop(0, tm, start, 0, unroll=8)

    @pl.when(t == 0)
    def _():
        issue(0, 0)

    @pl.when(t + 1 < pl.num_programs(0))
    def _():
        issue(t + 1, (t + 1) % 2)

    buf = t % 2
    pltpu.make_async_copy(y_hbm.at[pl.ds(0, tm)], y1_ref.at[buf], sem.at[buf, 0]).wait()
    pltpu.make_async_copy(y_hbm.at[pl.ds(0, tm)], y2_ref.at[buf], sem.at[buf, 1]).wait()
    y = r_ref[:, 2:3] * y1_ref[buf] + r_ref[:, 3:4] * y2_ref[buf]
    o_ref[...] = _layer_norm_rows(DN_ALPHA * x_ref[...] + y, g_ref[...], b_ref[...])


def _moe_combine_call(dest, y_rows, x, routed, g, b, tm):
    S, D = x.shape
    return pl.pallas_call(
        _moe_combine_kernel,
        grid_spec=pltpu.PrefetchScalarGridSpec(
            num_scalar_prefetch=1,
            grid=(S // tm,),
            in_specs=[pl.BlockSpec(memory_space=pl.ANY),
                      pl.BlockSpec((tm, D), lambda i, d: (i, 0)),
                      pl.BlockSpec((tm, 128), lambda i, d: (i, 0)),
                      pl.BlockSpec((1, D), lambda i, d: (0, 0)),
                      pl.BlockSpec((1, D), lambda i, d: (0, 0))],
            out_specs=pl.BlockSpec((tm, D), lambda i, d: (i, 0)),
            scratch_shapes=[pltpu.VMEM((2, tm, D), F32), pltpu.VMEM((2, tm, D), F32),
                            pltpu.SemaphoreType.DMA((2, 2))]),
        out_shape=jax.ShapeDtypeStruct((S, D), F32),
        compiler_params=_cparams(("arbitrary",)),
        name="moe_combine",
    )(dest, y_rows, x, routed, g.reshape(1, D), b.reshape(1, D))


def _moe_sublayer(x, w_router, b_router, wg, wu, wd, g, b):
    T = x.shape[0]
    R = MOE_ROWS
    routed = _router_call(x, w_router, b_router, min(ROW_TILE, T))
    top_e = routed[:, 0:TOP_K].astype(jnp.int32)
    tok_oh = jnp.sum((top_e[:, :, None] == jnp.arange(N_EXPERTS)[None, None, :]).astype(jnp.int32), axis=1)
    counts = jnp.sum(tok_oh, axis=0)
    rank = jnp.cumsum(tok_oh, axis=0) - tok_oh
    padded = (counts + R - 1) // R * R
    pend = jnp.cumsum(padded)
    pstart = pend - padded
    dest = pstart[top_e] + jnp.take_along_axis(rank, top_e, axis=1)
    n_rows = (T * TOP_K + N_EXPERTS * (R - 1)) // R * R
    n_blocks = n_rows // R
    block_first_row = jnp.arange(n_blocks, dtype=jnp.int32) * R
    block_e = jnp.minimum(jnp.sum((pend[None, :] <= block_first_row[:, None]).astype(jnp.int32), axis=1),
                          N_EXPERTS - 1)
    n_used = (pend[-1] // R).astype(jnp.int32).reshape(1)
    dest = dest.reshape(-1).astype(jnp.int32)
    pad_lo = (pstart + counts).astype(jnp.int32)
    pad_hi = jnp.concatenate([pstart[1:], jnp.array([n_rows])]).astype(jnp.int32)
    x_rows = _scatter_rows(x, dest, pad_lo, pad_hi, n_rows, min(ROW_TILE, T))
    y_rows = _moe_ffn_call(x_rows, block_e, n_used, wg, wu, wd, EXPERT_FF // 2)
    return _moe_combine_call(dest, y_rows, x, routed, g, b, min(COMBINE_TILE, T))


GLA_IN = 2 * GLA_QK + 2 * GLA_V + GLA_LR


def _gla_rwkv_sublayer(x, w_in, gla_wa2, gla_ba, gla_norm, mu, w0, w2, a0, a2, g2, k_k, k_a, r_k, ln_w, ln_b,
                       w_out, ln_g, ln_bias):
    S = x.shape[0]
    tm = min(WIDE_TILE, S)
    lr_pad = 128 - GLA_LR
    w_gla = jnp.pad(w_in[:, :GLA_IN], ((0, 0), (0, lr_pad))).astype(BF16)
    w_rwkv = w_in[:, GLA_IN:].astype(BF16)
    p_gla = _matmul(x, w_gla, F32, tm, GLA_COLS)
    p_rwkv = _matmul(x, w_rwkv, F32, tm, RWKV_COLS)
    wa2p = jnp.pad(gla_wa2, ((0, lr_pad), (0, 0))).astype(BF16)
    o_gla = _gla_call(p_gla, wa2p, gla_ba, gla_norm)
    o_rwkv = _rwkv_call(p_rwkv, mu, w0, w2, a0, a2, g2, k_k, k_a, r_k, ln_w, ln_b)
    return _matmul_ln([o_gla, o_rwkv], [w_out[:GLA_V].astype(BF16), w_out[GLA_V:].astype(BF16)], x,
                      ln_g, ln_bias, tm)


def _xattn_sublayer(x, mem, wq, wk, wv, wo, ln_g, ln_bias):
    M = mem.shape[0]
    k_mem = _matmul(mem, wk.astype(BF16), BF16, M, D_MODEL)
    v_mem = _matmul(mem, wv.astype(BF16), BF16, M, D_MODEL)
    wq_scaled = (wq * XATTN_HD ** -0.5).astype(BF16)
    return _xattn_call(x, wq_scaled, k_mem, v_mem, wo.astype(BF16), ln_g, ln_bias, min(WIDE_TILE, x.shape[0]))


def _ssd_moba_sublayer(x, w_in, conv_w, conv_b, dt_bias, a_log, d_skip, ssd_norm, w_out, ln_g, ln_bias):
    S = x.shape[0]
    tm = min(WIDE_TILE, S)
    nb = S // MOBA_BLOCK
    o_dt = SSD_COLS
    o_q = o_dt + SSD_HEADS
    w_ssd = jnp.pad(w_in[:, :o_q], ((0, 0), (0, 128 - SSD_HEADS))).astype(BF16)
    w_q = w_in[:, o_q:o_q + MOBA_D]
    w_k = w_in[:, o_q + MOBA_D:o_q + 2 * MOBA_D]
    w_v = w_in[:, o_q + 2 * MOBA_D:]
    p_ssd = _matmul(x, w_ssd, F32, tm, SSD_COLS + 128)
    o_ssd = _ssd_call(p_ssd, conv_w, conv_b, dt_bias, a_log, d_skip, ssd_norm)
    qt, k4, kmean, vt4, knorm, qnorm = _moba_proj_call(x, (w_q.T * (MOBA_HD ** -0.5 * LOG2E)).astype(BF16), w_k.astype(BF16),
                                         w_v.T.astype(BF16))
    kmean_h = kmean.reshape(nb, MOBA_HEADS, MOBA_HD).transpose(1, 0, 2)
    ot_moba = _moba_call(_moba_first_block(knorm, qnorm, nb), qt, k4, vt4, kmean_h)
    return _matmul_ln([o_ssd, ot_moba], [w_out[:SSD_INNER].astype(BF16), w_out[SSD_INNER:].astype(BF16)], x,
                      ln_g, ln_bias, tm, transposed=(False, True))


def kernel(x, mem, l0_w_in, l0_gla_wa2, l0_gla_ba, l0_gla_norm, l0_rwkv_mu, l0_rwkv_w0, l0_rwkv_w2, l0_rwkv_a0, l0_rwkv_a2, l0_rwkv_g2, l0_rwkv_kk, l0_rwkv_ka, l0_rwkv_rk, l0_rwkv_lnw, l0_rwkv_lnb, l0_w_out, l0_ln1_g, l0_ln1_b, l0_xq, l0_xk, l0_xv, l0_xo, l0_ln2_g, l0_ln2_b, l0_ffn_wg, l0_ffn_wu, l0_ffn_wd, l0_ln3_g, l0_ln3_b, l1_w_in, l1_conv_w, l1_conv_b, l1_dt_bias, l1_a_log, l1_d_skip, l1_ssd_norm, l1_w_out, l1_ln1_g, l1_ln1_b, l1_xq, l1_xk, l1_xv, l1_xo, l1_ln2_g, l1_ln2_b, l1_router, l1_router_b, l1_exp_wg, l1_exp_wu, l1_exp_wd, l1_ln3_g, l1_ln3_b):
    x2 = x.reshape(-1, D_MODEL)
    mem2 = mem.reshape(-1, D_MODEL)
    x2 = _gla_rwkv_sublayer(x2, l0_w_in, l0_gla_wa2, l0_gla_ba, l0_gla_norm, l0_rwkv_mu, l0_rwkv_w0, l0_rwkv_w2,
                            l0_rwkv_a0, l0_rwkv_a2, l0_rwkv_g2, l0_rwkv_kk, l0_rwkv_ka, l0_rwkv_rk, l0_rwkv_lnw,
                            l0_rwkv_lnb, l0_w_out, l0_ln1_g, l0_ln1_b)
    x2 = _xattn_sublayer(x2, mem2, l0_xq, l0_xk, l0_xv, l0_xo, l0_ln2_g, l0_ln2_b)
    tm = min(ROW_TILE, x2.shape[0])
    x2 = _ffn_call(x2, l0_ffn_wg.astype(BF16), l0_ffn_wu.astype(BF16), l0_ffn_wd.astype(BF16),
                   l0_ln3_g, l0_ln3_b, tm, l0_ffn_wg.shape[1] // 2)
    x2 = _ssd_moba_sublayer(x2, l1_w_in, l1_conv_w, l1_conv_b, l1_dt_bias, l1_a_log, l1_d_skip, l1_ssd_norm,
                            l1_w_out, l1_ln1_g, l1_ln1_b)
    x2 = _xattn_sublayer(x2, mem2, l1_xq, l1_xk, l1_xv, l1_xo, l1_ln2_g, l1_ln2_b)
    x2 = _moe_sublayer(x2, l1_router, l1_router_b, l1_exp_wg.astype(BF16), l1_exp_wu.astype(BF16),
                       l1_exp_wd.astype(BF16), l1_ln3_g, l1_ln3_b)
    return x2.reshape(x.shape)
```

```python
import functools
import math

import jax
import jax.numpy as jnp
from jax import lax
from jax.experimental import pallas as pl
from jax.experimental.pallas import tpu as pltpu

BF16 = jnp.bfloat16
F32 = jnp.float32

D_MODEL = 1024
LN_EPS = 1e-5
DEPTH = 2
DN_ALPHA = (2 * DEPTH) ** 0.25

GLA_HEADS, GLA_DK, GLA_DV, GLA_CHUNK = 4, 64, 128, 64
GLA_QK, GLA_V, GLA_LR, GLA_TAU = 256, 512, 16, 16.0
GLA_COLS = 2 * GLA_QK + 2 * GLA_V + 128

RWKV_HEADS, RWKV_HD, RWKV_D, RWKV_CHUNK = 8, 64, 512, 64
RWKV_COLS = 1792
RWKV_DECAY_SCALE = math.exp(-0.5)
RWKV_GN_EPS = 64e-5

VMEM_LIMIT = 56 * 1024 * 1024
ROW_TILE = 512
WIDE_TILE = 1024
SCAN_TILE = 256
COMBINE_TILE = 512


def _cparams(sem):
    return pltpu.CompilerParams(dimension_semantics=sem, vmem_limit_bytes=VMEM_LIMIT)


def _dot(a, b):
    return jnp.dot(a.astype(BF16), b.astype(BF16), preferred_element_type=F32)


def _dot_nt(a, b):
    return lax.dot_general(a.astype(BF16), b.astype(BF16), (((1,), (1,)), ((), ())), preferred_element_type=F32)


def _dot_tn(a, b):
    return lax.dot_general(a.astype(BF16), b.astype(BF16), (((0,), (0,)), ((), ())), preferred_element_type=F32)


def _split3(x):
    hi = x.astype(BF16)
    r1 = x - hi.astype(F32)
    mid = r1.astype(BF16)
    lo = (r1 - mid.astype(F32)).astype(BF16)
    return hi, mid, lo


def _dot_exact_lhs(m, x):
    mb = m.astype(BF16)
    hi, mid, lo = _split3(x)
    return (jnp.dot(mb, hi, preferred_element_type=F32) + jnp.dot(mb, mid, preferred_element_type=F32)
            + jnp.dot(mb, lo, preferred_element_type=F32))


def _dot_exact_rhs(x, m):
    mb = m.astype(BF16)
    hi, mid, lo = _split3(x)
    return (jnp.dot(hi, mb, preferred_element_type=F32) + jnp.dot(mid, mb, preferred_element_type=F32)
            + jnp.dot(lo, mb, preferred_element_type=F32))


def _dot_stat_rhs(x, m):
    mb = m.astype(BF16)
    hi = x.astype(BF16)
    lo = (x - hi.astype(F32)).astype(BF16)
    return jnp.dot(hi, mb, preferred_element_type=F32) + jnp.dot(lo, mb, preferred_element_type=F32)


def _sigmoid(x):
    return 1.0 / (1.0 + jnp.exp(-x))


def _silu(x):
    return x * _sigmoid(x)


def _iota2(shape, axis):
    return lax.broadcasted_iota(jnp.int32, shape, axis)


def _chunk_tril(n, chunk):
    r = _iota2((n, n), 0)
    c = _iota2((n, n), 1)
    return jnp.where((c <= r) & ((r // chunk) == (c // chunk)), 1.0, 0.0)


def _head_block(n, width, value):
    r = _iota2((n, n), 0)
    c = _iota2((n, n), 1)
    return jnp.where((r // width) == (c // width), value, 0.0)


def _mm_kernel(x_ref, w_ref, o_ref):
    o_ref[...] = _dot(x_ref[...], w_ref[...]).astype(o_ref.dtype)


def _matmul(x, w, out_dtype, tm, tn):
    S, K = x.shape
    N = w.shape[1]
    return pl.pallas_call(
        _mm_kernel,
        grid=(S // tm, N // tn),
        in_specs=[pl.BlockSpec((tm, K), lambda i, j: (i, 0)),
                  pl.BlockSpec((K, tn), lambda i, j: (0, j))],
        out_specs=pl.BlockSpec((tm, tn), lambda i, j: (i, j)),
        out_shape=jax.ShapeDtypeStruct((S, N), out_dtype),
        compiler_params=_cparams(("parallel", "arbitrary")),
        name="matmul",
    )(x, w)


def _layer_norm_rows(y, g, b):
    mu = jnp.mean(y, -1, keepdims=True)
    d = y - mu
    var = jnp.mean(d * d, -1, keepdims=True)
    return d * lax.rsqrt(var + LN_EPS) * g + b


def _mm_ln_kernel(transposed, *refs):
    n_in = len(transposed)
    a_refs = refs[:n_in]
    w_refs = refs[n_in:2 * n_in]
    x_ref, g_ref, b_ref, o_ref = refs[2 * n_in:]
    acc = DN_ALPHA * x_ref[...]
    for a_ref, w_ref, tr in zip(a_refs, w_refs, transposed):
        acc = acc + (_dot_tn if tr else _dot)(a_ref[...], w_ref[...])
    o_ref[...] = _layer_norm_rows(acc, g_ref[...], b_ref[...])


def _matmul_ln(a_list, w_list, x, g, b, tm, transposed=None):
    S, D = x.shape
    transposed = tuple(transposed or (False,) * len(a_list))
    in_specs = ([pl.BlockSpec((a.shape[0], tm), lambda i: (0, i)) if tr else
                 pl.BlockSpec((tm, a.shape[1]), lambda i: (i, 0)) for a, tr in zip(a_list, transposed)]
                + [pl.BlockSpec(w.shape, lambda i: (0, 0)) for w in w_list]
                + [pl.BlockSpec((tm, D), lambda i: (i, 0)),
                   pl.BlockSpec((1, D), lambda i: (0, 0)),
                   pl.BlockSpec((1, D), lambda i: (0, 0))])
    return pl.pallas_call(
        functools.partial(_mm_ln_kernel, transposed),
        grid=(S // tm,),
        in_specs=in_specs,
        out_specs=pl.BlockSpec((tm, D), lambda i: (i, 0)),
        out_shape=jax.ShapeDtypeStruct((S, D), F32),
        compiler_params=_cparams(("parallel",)),
        name="matmul_ln",
    )(*a_list, *w_list, x, g.reshape(1, D), b.reshape(1, D))


def _gla_kernel(p_ref, wa2_ref, ba_ref, ng_ref, o_ref, st_ref, o_scr):
    C, H, dk, dv = GLA_CHUNK, GLA_HEADS, GLA_DK, GLA_DV
    tb = p_ref.shape[0]

    @pl.when(pl.program_id(0) == 0)
    def _():
        st_ref[...] = jnp.zeros_like(st_ref)

    z = _dot(p_ref[:, 2 * GLA_QK + 2 * GLA_V:], wa2_ref[...]) + ba_ref[...]
    log_a = -(jnp.maximum(-z, 0.0) + jnp.log(1.0 + jnp.exp(-jnp.abs(z)))) / GLA_TAU
    b = _dot_exact_lhs(_chunk_tril(tb, C), log_a)
    causal = _iota2((C, C), 1) <= _iota2((C, C), 0)

    nc = tb // C
    q_h, k_h, ke_h, v_h, dec_h = [], [], [], [], []
    for c in range(nc):
        rows = slice(c * C, (c + 1) * C)
        b_c = b[rows]
        b_last = b_c[C - 1:C]
        q_dec = p_ref[rows, 0:GLA_QK] * (dk ** -0.5) * jnp.exp(b_c)
        k_c = p_ref[rows, GLA_QK:2 * GLA_QK]
        k_dec = k_c * jnp.exp(-b_c)
        k_end = k_c * jnp.exp(b_last - b_c)
        decay = jnp.exp(b_last)
        for h in range(H):
            ks = slice(h * dk, (h + 1) * dk)
            q_h.append(q_dec[:, ks])
            k_h.append(k_dec[:, ks])
            ke_h.append(k_end[:, ks])
            dec_h.append(decay[:, ks])
            v_h.append(p_ref[rows, 2 * GLA_QK + h * dv:2 * GLA_QK + (h + 1) * dv])
    n = nc * H
    attn = [jnp.where(causal, _dot_nt(q_h[i], k_h[i]), 0.0) for i in range(n)]
    kv = [_dot_tn(v_h[i], ke_h[i]) for i in range(n)]
    intra = [_dot(attn[i], v_h[i]) for i in range(n)]
    state = [st_ref[:, h * dk:(h + 1) * dk] for h in range(H)]
    entering = []
    for i in range(n):
        entering.append(state[i % H])
        state[i % H] = state[i % H] * dec_h[i] + kv[i]
    for i in range(n):
        c, h = divmod(i, H)
        o_scr[c * C:(c + 1) * C, h * dv:(h + 1) * dv] = intra[i] + _dot_nt(q_h[i], entering[i])
    for h in range(H):
        st_ref[:, h * dk:(h + 1) * dk] = state[h]

    for h in range(H):
        vs = slice(h * dv, (h + 1) * dv)
        o_h = o_scr[:, vs]
        g_h = p_ref[:, 2 * GLA_QK + GLA_V + h * dv:2 * GLA_QK + GLA_V + (h + 1) * dv]
        o_h = o_h * lax.rsqrt(jnp.mean(o_h * o_h, -1, keepdims=True) + 1e-5) * ng_ref[:, vs]
        o_ref[:, vs] = (o_h * _silu(g_h)).astype(o_ref.dtype)


def _gla_call(p_gla, wa2p, ba, norm_g, tb=SCAN_TILE):
    S = p_gla.shape[0]
    return pl.pallas_call(
        _gla_kernel,
        grid=(S // tb,),
        in_specs=[pl.BlockSpec((tb, GLA_COLS), lambda i: (i, 0)),
                  pl.BlockSpec((128, GLA_QK), lambda i: (0, 0)),
                  pl.BlockSpec((1, GLA_QK), lambda i: (0, 0)),
                  pl.BlockSpec((1, GLA_V), lambda i: (0, 0))],
        out_specs=pl.BlockSpec((tb, GLA_V), lambda i: (i, 0)),
        out_shape=jax.ShapeDtypeStruct((S, GLA_V), BF16),
        scratch_shapes=[pltpu.VMEM((GLA_DV, GLA_QK), F32), pltpu.VMEM((tb, GLA_V), F32)],
        compiler_params=_cparams(("arbitrary",)),
        name="gla",
    )(p_gla, wa2p, ba.reshape(1, GLA_QK), norm_g.reshape(1, GLA_V))


def _rwkv_kernel(p_ref, mu_ref, w0_ref, w2_ref, a0_ref, a2_ref, g2_ref, kk_ref, ka_ref, rk_ref, lnw_ref, lnb_ref,
                 o_ref, prev_ref, h_ref, o_scr):
    C, H, N, D = RWKV_CHUNK, RWKV_HEADS, RWKV_HD, RWKV_D
    tb = p_ref.shape[0]
    first = pl.program_id(0) == 0

    @pl.when(first)
    def _():
        prev_ref[...] = jnp.zeros_like(prev_ref)
        h_ref[...] = jnp.zeros_like(h_ref)

    p = p_ref[...]
    shifted = jnp.where(_iota2(p.shape, 0) == 0, prev_ref[...], pltpu.roll(p, 1, 0))
    prev_ref[...] = p[tb - 1:tb]
    p = p + mu_ref[...] * (shifted - p)
    r = p[:, 0:D]
    k = p[:, D:2 * D]
    v = p[:, 2 * D:3 * D]
    xw = p[:, 3 * D:3 * D + 64]
    xa = p[:, 3 * D + 64:3 * D + 128]
    xg = p[:, 3 * D + 128:3 * D + 256]
    lw = -RWKV_DECAY_SCALE * _sigmoid(w0_ref[...] + _dot(jnp.tanh(xw), w2_ref[...]))
    a = _sigmoid(a0_ref[...] + _dot(xa, a2_ref[...]))
    g = _dot(_sigmoid(xg), g2_ref[...])
    head_ones = _head_block(D, N, 1.0)
    kk = k * kk_ref[...]
    kk = kk * lax.rsqrt(jnp.maximum(_dot_stat_rhs(kk * kk, head_ones), 1e-24))
    k = k * (1.0 + (a - 1.0) * ka_ref[...])
    pv = -kk * a
    cw = _dot_exact_lhs(_chunk_tril(tb, C), lw)
    cwx = cw - lw

    W2 = 2 * N
    left = _iota2((C, W2), 1) < N

    def blockdiag(a):
        return jnp.concatenate([jnp.where(left, a, 0.0), jnp.where(left, 0.0, a)], axis=0)

    def diag_blocks(full):
        return jnp.where(left, full[0:C], full[C:2 * C])

    gi = _iota2((2 * C, 2 * W2), 0)
    gj = _iota2((2 * C, 2 * W2), 1) % N
    gram_mask = ((gi < C) & (gj < gi)) | ((gi >= C) & (gj <= gi - C))
    eye = _iota2((C, W2), 0) == _iota2((C, W2), 1) % N
    eye_f = jnp.where(eye, 1.0, 0.0)
    zeros_c = jnp.zeros((C, W2), F32)
    zeros_w = jnp.zeros((W2, W2), F32)

    nc = tb // C
    items = [(c, p) for c in range(nc) for p in range(H // 2)]
    xs, ys, pk_e, b_h, v_h, r_h, g_h = [], [], [], [], [], [], []
    for c in range(nc):
        rows = slice(c * C, (c + 1) * C)
        cw_c = cw[rows]
        cw_end = cw_c[C - 1:C]
        e_pos = jnp.exp(cw_c)
        e_neg = jnp.exp(-cw_c)
        e_end = jnp.exp(cw_end - cw_c)
        r_t = r[rows] * e_pos
        b_t = kk[rows] * jnp.exp(cwx[rows])
        p_t = pv[rows] * e_neg
        k_t = k[rows] * e_neg
        p_e = pv[rows] * e_end
        k_e = k[rows] * e_end
        g_end = jnp.exp(cw_end)
        v_c = v[rows]
        for p in range(H // 2):
            ps = slice(p * W2, (p + 1) * W2)
            xs.append(jnp.concatenate([b_t[:, ps], r_t[:, ps]], axis=0))
            ys.append(jnp.concatenate([blockdiag(p_t[:, ps]), blockdiag(k_t[:, ps])], axis=0))
            pk_e.append(jnp.concatenate([p_e[:, ps], k_e[:, ps]], axis=0))
            b_h.append(b_t[:, ps])
            v_h.append(v_c[:, ps])
            r_h.append(r_t[:, ps])
            g_h.append(g_end[:, ps])
    n = len(items)
    grams = [jnp.where(gram_mask, _dot_nt(xs[i], ys[i]), 0.0) for i in range(n)]
    l_p = [g[0:C, 0:W2] for g in grams]
    m_pk = [g[C:2 * C, :] for g in grams]
    v_bd = [blockdiag(v_h[i]) for i in range(n)]
    lkv = [_dot(grams[i][0:C, W2:2 * W2], v_bd[i]) for i in range(n)]
    x = [_dot(lp, blockdiag(lp)) for lp in l_p]
    t = [eye_f + lp for lp in l_p]
    for _ in range(4):
        tx = [_dot(jnp.concatenate([t[i], x[i]], axis=0), blockdiag(x[i])) for i in range(n)]
        t = [t[i] + tx[i][0:C] for i in range(n)]
        x = [tx[i][C:2 * C] for i in range(n)]
    t = [t[i] + _dot(t[i], blockdiag(x[i])) for i in range(n)]
    wu = [_dot(t[i], jnp.concatenate([blockdiag(b_h[i]), blockdiag(lkv[i])], axis=1)) for i in range(n)]
    az = [_dot_tn(pk_e[i], jnp.concatenate([wu[i], jnp.concatenate([zeros_c, v_h[i]], axis=1)], axis=0))
          for i in range(n)]
    qo = [_dot(m_pk[i], jnp.concatenate(
        [jnp.concatenate([blockdiag(wu[i][:, 0:W2]), blockdiag(wu[i][:, W2:2 * W2])], axis=1),
         jnp.concatenate([zeros_w, v_bd[i]], axis=1)], axis=0)) for i in range(n)]
    state = [h_ref[p] for p in range(H // 2)]
    for i, (c, p) in enumerate(items):
        a_mat = diag_blocks(az[i][:, 0:W2]) + jnp.where(eye, g_h[i], 0.0)
        q_mat = qo[i][:, 0:W2] + r_h[i]
        oh = _dot(jnp.concatenate([q_mat, a_mat], axis=0), blockdiag(state[p]))
        o_scr[c * C:(c + 1) * C, p * W2:(p + 1) * W2] = oh[0:C] + qo[i][:, W2:2 * W2]
        state[p] = oh[C:C + N] + diag_blocks(az[i][:, W2:2 * W2])
    for p in range(H // 2):
        h_ref[p] = state[p]

    o = o_scr[...]
    head_mean = _head_block(D, N, 1.0 / N)
    mean = _dot_stat_rhs(o, head_mean)
    d = o - mean
    var = _dot_stat_rhs(d * d, head_mean)
    o = d * lax.rsqrt(var + RWKV_GN_EPS) * lnw_ref[...] + lnb_ref[...]
    bonus = _dot_stat_rhs(r * k * rk_ref[...], head_ones) * v
    o_ref[...] = ((o + bonus) * g).astype(o_ref.dtype)


def _rwkv_call(p_rwkv, mu, w0, w2, a0, a2, g2, k_k, k_a, r_k, ln_w, ln_b, tb=SCAN_TILE):
    S = p_rwkv.shape[0]
    D = RWKV_D
    row = lambda t: t.reshape(1, -1).astype(F32)
    full = lambda shape: pl.BlockSpec(shape, lambda i: tuple(0 for _ in shape))
    return pl.pallas_call(
        _rwkv_kernel,
        grid=(S // tb,),
        in_specs=[pl.BlockSpec((tb, RWKV_COLS), lambda i: (i, 0)),
                  full((1, RWKV_COLS)), full((1, D)), full((64, D)), full((1, D)), full((64, D)), full((128, D)),
                  full((1, D)), full((1, D)), full((1, D)), full((1, D)), full((1, D))],
        out_specs=pl.BlockSpec((tb, D), lambda i: (i, 0)),
        out_shape=jax.ShapeDtypeStruct((S, D), BF16),
        scratch_shapes=[pltpu.VMEM((1, RWKV_COLS), F32),
                        pltpu.VMEM((RWKV_HEADS // 2, RWKV_HD, 2 * RWKV_HD), F32),
                        pltpu.VMEM((tb, D), F32)],
        compiler_params=_cparams(("arbitrary",)),
        name="rwkv7",
    )(p_rwkv, row(mu), row(w0), w2.astype(BF16), row(a0), a2.astype(BF16), g2.astype(BF16),
      row(k_k), row(k_a), row(r_k), row(ln_w), row(ln_b))


XATTN_HEADS, XATTN_HD = 4, 256


def _xattn_kernel(x_ref, wq_ref, k_ref, v_ref, wo_ref, g_ref, b_ref, o_ref):
    x = x_ref[...]
    q = _dot(x, wq_ref[...])
    outs = []
    for h in range(XATTN_HEADS):
        hs = slice(h * XATTN_HD, (h + 1) * XATTN_HD)
        s = _dot_nt(q[:, hs], k_ref[:, hs])
        e = jnp.exp(s - jnp.max(s, -1, keepdims=True))
        p = e / jnp.sum(e, -1, keepdims=True)
        outs.append(_dot(p, v_ref[:, hs]))
    o = jnp.concatenate(outs, axis=1)
    y = DN_ALPHA * x + _dot(o, wo_ref[...])
    o_ref[...] = _layer_norm_rows(y, g_ref[...], b_ref[...])


def _xattn_call(x, wq_scaled, k_mem, v_mem, wo, g, b, tm):
    S, D = x.shape
    M = k_mem.shape[0]
    const = lambda shape: pl.BlockSpec(shape, lambda i: (0, 0))
    return pl.pallas_call(
        _xattn_kernel,
        grid=(S // tm,),
        in_specs=[pl.BlockSpec((tm, D), lambda i: (i, 0)), const((D, D)), const((M, D)), const((M, D)),
                  const((D, D)), const((1, D)), const((1, D))],
        out_specs=pl.BlockSpec((tm, D), lambda i: (i, 0)),
        out_shape=jax.ShapeDtypeStruct((S, D), F32),
        compiler_params=_cparams(("parallel",)),
        name="xattn",
    )(x, wq_scaled, k_mem, v_mem, wo, g.reshape(1, D), b.reshape(1, D))


def _ffn_kernel(x_ref, wg_ref, wu_ref, wd_ref, g_ref, b_ref, o_ref, acc_ref):
    j = pl.program_id(1)
    x = x_ref[...]
    xb = x.astype(BF16)
    part = _dot(_silu(_dot(xb, wg_ref[...])) * _dot(xb, wu_ref[...]), wd_ref[...])

    last = pl.num_programs(1) - 1

    @pl.when(j == 0)
    def _():
        acc_ref[...] = DN_ALPHA * x + part

    @pl.when((j > 0) & (j < last))
    def _():
        acc_ref[...] += part

    @pl.when(j == last)
    def _():
        o_ref[...] = _layer_norm_rows(acc_ref[...] + part, g_ref[...], b_ref[...])


def _ffn_call(x, wg, wu, wd, g, b, tm, tf):
    S, D = x.shape
    F = wg.shape[1]
    assert F % tf == 0 and F // tf >= 2
    return pl.pallas_call(
        _ffn_kernel,
        grid=(S // tm, F // tf),
        in_specs=[pl.BlockSpec((tm, D), lambda i, j: (i, 0)),
                  pl.BlockSpec((D, tf), lambda i, j: (0, j)),
                  pl.BlockSpec((D, tf), lambda i, j: (0, j)),
                  pl.BlockSpec((tf, D), lambda i, j: (j, 0)),
                  pl.BlockSpec((1, D), lambda i, j: (0, 0)),
                  pl.BlockSpec((1, D), lambda i, j: (0, 0))],
        out_specs=pl.BlockSpec((tm, D), lambda i, j: (i, 0)),
        out_shape=jax.ShapeDtypeStruct((S, D), F32),
        scratch_shapes=[pltpu.VMEM((tm, D), F32)],
        compiler_params=_cparams(("parallel", "arbitrary")),
        name="ffn",
    )(x, wg, wu, wd, g.reshape(1, D), b.reshape(1, D))


SSD_HD, SSD_HEADS, SSD_INNER, SSD_GROUPS, SSD_STATE = 64, 16, 1024, 2, 128
SSD_BC, SSD_CONV, SSD_CONV_CH, SSD_CHUNK = 256, 4, 1536, 128
SSD_COLS = SSD_INNER + SSD_CONV_CH
SSD_GW = SSD_INNER // SSD_GROUPS


def _softplus(x):
    return jnp.maximum(x, 0.0) + jnp.log(1.0 + jnp.exp(-jnp.abs(x)))


def _ssd_kernel(p_ref, cw_ref, cb_ref, dtb_ref, a_ref, dsk_ref, ng_ref, o_ref, prev_ref, st_ref, y_scr):
    L, G, NS, HD = SSD_CHUNK, SSD_GROUPS, SSD_STATE, SSD_HD
    HG = SSD_HEADS // G

    @pl.when(pl.program_id(0) == 0)
    def _():
        prev_ref[...] = jnp.zeros_like(prev_ref)
        st_ref[...] = jnp.zeros_like(st_ref)

    cur = p_ref[:, SSD_INNER:SSD_COLS]
    tail = prev_ref[...]
    row = _iota2(tail.shape, 0)
    conv = cur * cw_ref[SSD_CONV - 1:SSD_CONV, :] + cb_ref[...]
    for kk in range(1, SSD_CONV):
        rolled = pltpu.roll(cur, kk, 0)
        head = jnp.where(row < kk, pltpu.roll(tail, kk, 0), rolled[0:8])
        shifted = jnp.concatenate([head, rolled[8:]], axis=0)
        conv = conv + shifted * cw_ref[SSD_CONV - 1 - kk:SSD_CONV - kk, :]
    prev_ref[...] = cur[L - 8:L]
    xbc = _silu(conv)
    xs = xbc[:, :SSD_INNER]

    dt = _softplus(p_ref[:, SSD_COLS:] + dtb_ref[...])
    a_col = dt * a_ref[...]
    li = _iota2((L, L), 0)
    lj = _iota2((L, L), 1)
    cs = _dot_exact_lhs(jnp.where(lj <= li, 1.0, 0.0), a_col)
    cs_row = cs.T
    expand = jnp.where(_iota2((128, SSD_INNER), 1) // HD == _iota2((128, SSD_INNER), 0), 1.0, 0.0)
    dt_x = _dot_stat_rhs(dt, expand)
    cs_x = _dot_exact_rhs(cs, expand)
    cs_end = cs_x[L - 1:L]
    xd = xs * dt_x
    xd_dec = xd * jnp.exp(cs_end - cs_x)
    out_dec = jnp.exp(cs_x)
    chunk_dec = jnp.exp(cs_end)
    tril = lj <= li

    for g in range(G):
        gs = slice(g * SSD_GW, (g + 1) * SSD_GW)
        b_g = xbc[:, SSD_INNER + g * NS:SSD_INNER + (g + 1) * NS]
        c_g = xbc[:, SSD_INNER + SSD_BC + g * NS:SSD_INNER + SSD_BC + (g + 1) * NS]
        cb = _dot_nt(c_g, b_g)
        for j in range(HG):
            h = g * HG + j
            hs = slice(h * HD, (h + 1) * HD)
            seg = jnp.where(tril, jnp.exp(cs[:, h:h + 1] - cs_row[h:h + 1, :]), 0.0)
            y_scr[:, hs] = _dot(cb * seg, xd[:, hs])
        st = st_ref[g]
        y_off = _dot(c_g, st) * out_dec[:, gs]
        st_ref[g] = st * chunk_dec[:, gs] + _dot_tn(b_g, xd_dec[:, gs])
        y_scr[:, gs] = y_scr[:, gs] + y_off

    y = (y_scr[...] + dsk_ref[...] * xs) * _silu(p_ref[:, :SSD_INNER])
    for g in range(G):
        gs = slice(g * SSD_GW, (g + 1) * SSD_GW)
        y_g = y[:, gs]
        o_ref[:, gs] = (y_g * lax.rsqrt(jnp.mean(y_g * y_g, -1, keepdims=True) + 1e-5) * ng_ref[:, gs]).astype(o_ref.dtype)


def _ssd_call(p_ssd, conv_w, conv_b, dt_bias, a_log, d_skip, norm_g):
    S = p_ssd.shape[0]
    L = SSD_CHUNK
    a_neg = -jnp.exp(a_log.astype(F32))
    pad = lambda t: jnp.pad(t.astype(F32), (0, 128 - SSD_HEADS)).reshape(1, 128)
    const = lambda shape: pl.BlockSpec(shape, lambda i: (0, 0))
    return pl.pallas_call(
        _ssd_kernel,
        grid=(S // L,),
        in_specs=[pl.BlockSpec((L, SSD_COLS + 128), lambda i: (i, 0)),
                  const((SSD_CONV, SSD_CONV_CH)), const((1, SSD_CONV_CH)),
                  const((1, 128)), const((1, 128)),
                  const((1, SSD_INNER)), const((1, SSD_INNER))],
        out_specs=pl.BlockSpec((L, SSD_INNER), lambda i: (i, 0)),
        out_shape=jax.ShapeDtypeStruct((S, SSD_INNER), BF16),
        scratch_shapes=[pltpu.VMEM((8, SSD_CONV_CH), F32),
                        pltpu.VMEM((SSD_GROUPS, SSD_STATE, SSD_GW), F32),
                        pltpu.VMEM((L, SSD_INNER), F32)],
        compiler_params=_cparams(("arbitrary",)),
        name="ssd",
    )(p_ssd, conv_w.astype(F32), conv_b.reshape(1, -1).astype(F32), pad(dt_bias), pad(a_neg),
      jnp.repeat(d_skip.astype(F32), SSD_HD).reshape(1, -1), norm_g.reshape(1, -1).astype(F32))


MOBA_HD, MOBA_HEADS, MOBA_D, MOBA_BLOCK, MOBA_TOPK = 64, 8, 512, 256, 3
MOBA_GROUP = 4
MOBA_UNDERFLOW = -160.0
MOBA_BOUND_SLACK = 1.001
NEG_BIG = -1e30
LOG2E = math.log2(math.e)


MOBA_VROWS = MOBA_HD + 16


def _moba_proj_kernel(x_ref, wqt_ref, wk_ref, wvt_ref, qt_ref, k_ref, kmean_ref, v_ref, knorm_ref, qnorm_ref):
    xb = x_ref[...].astype(BF16)
    qt = _dot_nt(wqt_ref[...], xb).astype(qt_ref.dtype)
    qt_ref[...] = qt
    vt = _dot_nt(wvt_ref[...], xb)
    extra = jnp.where(_iota2((MOBA_VROWS - MOBA_HD, MOBA_BLOCK), 0) == 0, 1.0, 0.0)
    for h in range(MOBA_HEADS):
        v_ref[h, 0] = jnp.concatenate([vt[h * MOBA_HD:(h + 1) * MOBA_HD], extra], axis=0).astype(v_ref.dtype)
    k = _dot(xb, wk_ref[...])
    kmean_ref[0] = jnp.mean(k, 0, keepdims=True)
    shape = (MOBA_BLOCK, 2 * MOBA_HD)
    lane = _iota2(shape, 1)
    pos = jnp.where((lane == MOBA_HD) | (lane == MOBA_HD + 1), _iota2(shape, 0).astype(F32), 0.0).astype(k_ref.dtype)
    k_b = k.astype(k_ref.dtype)
    for h in range(MOBA_HEADS):
        k_ref[h, 0] = pos
        k_ref[h, 0, :, 0:MOBA_HD] = k_b[:, h * MOBA_HD:(h + 1) * MOBA_HD]
    k_f = k_b.astype(F32)
    q_f = qt.astype(F32)
    head_cols = jnp.where(_iota2((MOBA_D, 128), 0) // MOBA_HD == _iota2((MOBA_D, 128), 1), 1.0, 0.0)
    head_rows = jnp.where(_iota2((128, MOBA_D), 1) // MOBA_HD == _iota2((128, MOBA_D), 0), 1.0, 0.0)
    knorm_ref[0] = jnp.sqrt(jnp.max(_dot_stat_rhs(k_f * k_f, head_cols), 0, keepdims=True))
    qnorm_ref[0] = jnp.sqrt(jnp.max(_dot_exact_lhs(head_rows, q_f * q_f), 1, keepdims=True))


def _moba_proj_call(x, wqt, wk, wvt):
    S, D = x.shape
    nb = S // MOBA_BLOCK
    const = lambda shape: pl.BlockSpec(shape, lambda i: (0, 0))
    return pl.pallas_call(
        _moba_proj_kernel,
        grid=(nb,),
        in_specs=[pl.BlockSpec((MOBA_BLOCK, D), lambda i: (i, 0)), const((MOBA_D, D)), const((D, MOBA_D)),
                  const((MOBA_D, D))],
        out_specs=[pl.BlockSpec((MOBA_D, MOBA_BLOCK), lambda i: (0, i)),
                   pl.BlockSpec((MOBA_HEADS, 1, MOBA_BLOCK, 2 * MOBA_HD), lambda i: (0, i, 0, 0)),
                   pl.BlockSpec((1, 1, MOBA_D), lambda i: (i, 0, 0)),
                   pl.BlockSpec((MOBA_HEADS, 1, MOBA_VROWS, MOBA_BLOCK), lambda i: (0, i, 0, 0)),
                   pl.BlockSpec((1, 1, 128), lambda i: (i, 0, 0)),
                   pl.BlockSpec((1, 128, 1), lambda i: (i, 0, 0))],
        out_shape=[jax.ShapeDtypeStruct((MOBA_D, S), BF16),
                   jax.ShapeDtypeStruct((MOBA_HEADS, nb, MOBA_BLOCK, 2 * MOBA_HD), BF16),
                   jax.ShapeDtypeStruct((nb, 1, MOBA_D), F32),
                   jax.ShapeDtypeStruct((MOBA_HEADS, nb, MOBA_VROWS, MOBA_BLOCK), BF16),
                   jax.ShapeDtypeStruct((nb, 1, 128), F32),
                   jax.ShapeDtypeStruct((nb, 128, 1), F32)],
        compiler_params=_cparams(("parallel",)),
        name="moba_proj",
    )(x, wqt, wk, wvt)


def _moba_kernel(first_ref, qt_ref, k_ref, vt_ref, kmean_ref, o_ref, sel_ref, s0_ref, s1_ref, p0_ref, p1_ref):
    BS, HD = MOBA_BLOCK, MOBA_HD
    h = pl.program_id(0)
    i = pl.program_id(1)
    nb = k_ref.shape[0]
    qt = qt_ref[...]
    slope = LOG2E * jnp.exp2(jnp.zeros((1, BS), F32) - (h + 1).astype(F32) * (8.0 / MOBA_HEADS))
    slope_hi = slope.astype(qt.dtype).astype(F32)
    slope_lo = slope - slope_hi
    row = _iota2((HD, BS), 0)
    qt_ext = jnp.concatenate(
        [qt, jnp.where(row == 0, slope_hi, jnp.where(row == 1, slope_lo, 0.0)).astype(qt.dtype)], axis=0)

    gate = _dot(kmean_ref[...], qt)
    blk = _iota2((nb, BS), 0).astype(F32)
    cand = blk < i.astype(F32)
    sel = jnp.zeros((nb, BS), F32)
    for _ in range(MOBA_TOPK):
        best = jnp.max(jnp.where(cand, gate, -jnp.inf), 0, keepdims=True)
        idx = jnp.min(jnp.where(cand & (gate == best), blk, float(nb)), 0, keepdims=True)
        pick = blk == idx
        sel = jnp.where(pick, 1.0, sel)
        cand = cand & jnp.logical_not(pick)
    sel_ref[...] = sel

    G = MOBA_GROUP
    last = nb - 1
    s_ref = (s0_ref, s1_ref)
    p_ref = (p0_ref, p1_ref)

    first = first_ref[h * nb + i]

    def group(u):
        return [jnp.clip(first + G * u + x, 0, last) for x in range(G)]

    def issue_scores(u, slot):
        tops = []
        for x, j in enumerate(group(u)):
            sc = _dot(k_ref[j], qt_ext)
            s_ref[slot][x] = sc
            tops.append(jnp.max(sc, 0, keepdims=True))
        return tuple(tops)

    def value_blocks(u):
        js = group(u)
        js[0] = jnp.where(u == -1, i, js[0])
        return js

    def weighted_values(u, slot):
        out = None
        for x, j in enumerate(value_blocks(u)):
            part = _dot(vt_ref[j], p_ref[slot][x])
            out = part if out is None else out + part
        return out

    def step(u, slot, tops, a_prev, m, acc, issue_next=True):
        on, shift = [], []
        m_new = m
        for x, j in enumerate(group(u)):
            on.append(sel_ref[pl.ds(j, 1), :] > 0.0)
            shift.append(slope * ((j - i) * BS).astype(F32))
            m_new = jnp.maximum(m_new, jnp.where(on[x], tops[x] + shift[x], NEG_BIG))
        tops_next, pv = [], None
        for x, (j_prev, j_next) in enumerate(zip(value_blocks(u - 1), group(u + 1))):
            part = _dot(vt_ref[j_prev], p_ref[1 - slot][x])
            pv = part if pv is None else pv + part
            if issue_next:
                sc = _dot(k_ref[j_next], qt_ext)
                s_ref[1 - slot][x] = sc
                tops_next.append(jnp.max(sc, 0, keepdims=True))
            p = jnp.exp2(s_ref[slot][x] - (jnp.where(on[x], m_new, -NEG_BIG) - shift[x]))
            p_ref[slot][x] = p.astype(BF16)
        return tuple(tops_next), jnp.exp2(m - m_new), m_new, a_prev * acc + pv

    def body(w, carry):
        carry = step(2 * w, 0, *carry)
        return step(2 * w + 1, 1, *carry)

    tops0 = issue_scores(0, 0)
    s_own = jnp.where(_iota2((BS, BS), 1) >= _iota2((BS, BS), 0), _dot(k_ref[i], qt_ext), NEG_BIG)
    m_own = jnp.max(s_own, 0, keepdims=True)
    p1_ref[1:G] = jnp.zeros((G - 1, BS, BS), BF16)
    p1_ref[0] = jnp.exp2(s_own - m_own).astype(BF16)
    init = (tops0, jnp.ones((1, BS), F32), m_own, jnp.zeros((vt_ref.shape[1], BS), F32))
    steps = (i - first + G - 1) // G
    pairs = steps // 2
    carry = lax.fori_loop(0, pairs, body, init)

    def odd_tail(carry):
        _, a_last, _, acc = step(2 * pairs, 0, *carry, issue_next=False)
        return a_last * acc + weighted_values(2 * pairs, 0)

    def even_tail(carry):
        _, a_prev, _, acc = carry
        return a_prev * acc + weighted_values(2 * pairs - 1, 1)

    acc = lax.cond(steps % 2 == 1, odd_tail, even_tail, carry)
    o_ref[...] = (acc[0:HD] / acc[HD:HD + 1]).astype(o_ref.dtype)


def _moba_first_block(knorm, qnorm, nb):
    BS = MOBA_BLOCK
    k_norm = knorm[:, 0, 0:MOBA_HEADS].T
    q_norm = qnorm[:, 0:MOBA_HEADS, 0].T
    slope = LOG2E * jnp.exp2(-(jnp.arange(MOBA_HEADS, dtype=F32) + 1.0) * (8.0 / MOBA_HEADS))
    i_idx = jnp.arange(nb, dtype=F32)[None, :, None]
    j_idx = jnp.arange(nb, dtype=F32)[None, None, :]
    reach = slope[:, None, None] * ((BS - 1.0) - BS * (i_idx - j_idx))
    bound = q_norm[:, :, None] * (k_norm[:, None, :] + k_norm[:, :, None]) * MOBA_BOUND_SLACK + reach
    matters = (bound >= MOBA_UNDERFLOW) & (j_idx < i_idx)
    first = jnp.min(jnp.where(matters, j_idx, i_idx), axis=2).astype(jnp.int32)
    return first.reshape(-1)


def _moba_call(first_block, qt, k4, vt4, kmean):
    S = qt.shape[1]
    nb = S // MOBA_BLOCK
    return pl.pallas_call(
        _moba_kernel,
        grid_spec=pltpu.PrefetchScalarGridSpec(
            num_scalar_prefetch=1,
            grid=(MOBA_HEADS, nb),
            in_specs=[pl.BlockSpec((MOBA_HD, MOBA_BLOCK), lambda h, i, first: (h, i)),
                      pl.BlockSpec((None, nb, MOBA_BLOCK, 2 * MOBA_HD), lambda h, i, first: (h, 0, 0, 0)),
                      pl.BlockSpec((None, nb, MOBA_VROWS, MOBA_BLOCK), lambda h, i, first: (h, 0, 0, 0)),
                      pl.BlockSpec((None, nb, MOBA_HD), lambda h, i, first: (h, 0, 0))],
            out_specs=pl.BlockSpec((MOBA_HD, MOBA_BLOCK), lambda h, i, first: (h, i)),
            scratch_shapes=[pltpu.VMEM((nb, MOBA_BLOCK), F32)]
            + [pltpu.VMEM((MOBA_GROUP, MOBA_BLOCK, MOBA_BLOCK), F32)] * 2
            + [pltpu.VMEM((MOBA_GROUP, MOBA_BLOCK, MOBA_BLOCK), BF16)] * 2),
        out_shape=jax.ShapeDtypeStruct((MOBA_D, S), BF16),
        compiler_params=_cparams(("parallel", "arbitrary")),
        name="moba",
    )(first_block, qt, k4, vt4, kmean)


N_EXPERTS, TOP_K, EXPERT_FF = 8, 2, 2816
MOE_ROWS = 512


def _router_kernel(x_ref, whi_ref, wlo_ref, b_ref, o_ref):
    x = x_ref[...]
    xhi = x.astype(BF16)
    xlo = (x - xhi.astype(F32)).astype(BF16)
    logits = (jnp.dot(xhi, whi_ref[...], preferred_element_type=F32)
              + jnp.dot(xhi, wlo_ref[...], preferred_element_type=F32)
              + jnp.dot(xlo, whi_ref[...], preferred_element_type=F32)) + b_ref[...]
    lane = _iota2(logits.shape, 1).astype(F32)
    logits = jnp.where(lane < N_EXPERTS, logits, -jnp.inf)
    m1 = jnp.max(logits, -1, keepdims=True)
    i1 = jnp.min(jnp.where(logits == m1, lane, 128.0), -1, keepdims=True)
    rest = jnp.where(lane == i1, -jnp.inf, logits)
    m2 = jnp.max(rest, -1, keepdims=True)
    i2 = jnp.min(jnp.where(rest == m2, lane, 128.0), -1, keepdims=True)
    e = jnp.exp(m2 - m1)
    g1 = 1.0 / (1.0 + e)
    g2 = e / (1.0 + e)
    out = jnp.where(lane == 0, i1, 0.0)
    out = jnp.where(lane == 1, i2, out)
    out = jnp.where(lane == 2, g1, out)
    out = jnp.where(lane == 3, g2, out)
    o_ref[...] = out


def _router_call(x, w_router, b_router, tm):
    S, D = x.shape
    wp = jnp.pad(w_router.astype(F32), ((0, 0), (0, 128 - N_EXPERTS)))
    whi = wp.astype(BF16)
    wlo = (wp - whi.astype(F32)).astype(BF16)
    bp = jnp.pad(b_router.astype(F32), (0, 128 - N_EXPERTS)).reshape(1, 128)
    const = lambda shape: pl.BlockSpec(shape, lambda i: (0, 0))
    return pl.pallas_call(
        _router_kernel,
        grid=(S // tm,),
        in_specs=[pl.BlockSpec((tm, D), lambda i: (i, 0)), const((D, 128)), const((D, 128)), const((1, 128))],
        out_specs=pl.BlockSpec((tm, 128), lambda i: (i, 0)),
        out_shape=jax.ShapeDtypeStruct((S, 128), F32),
        compiler_params=_cparams(("parallel",)),
        name="router",
    )(x, whi, wlo, bp)


def _row_copy(src_hbm, dst_ref, src_row, dst_row, sem):
    return pltpu.make_async_copy(src_hbm.at[pl.ds(src_row, 1)], dst_ref.at[pl.ds(dst_row, 1)], sem)


def _scatter_rows_kernel(dest_ref, lo_ref, hi_ref, x_ref, o_hbm, zero_ref, xs_ref, sem, tile_sem):
    step = pl.program_id(0)
    n_tiles = pl.num_programs(0) - 1
    tm = x_ref.shape[0]

    def row_out(src_ref, src_row, dst_row, s):
        return pltpu.make_async_copy(src_ref.at[pl.ds(src_row, 1)], o_hbm.at[pl.ds(dst_row, 1)], s)

    def wait_tile(buf):
        for slot in range(TOP_K):
            pltpu.make_async_copy(xs_ref.at[buf], o_hbm.at[pl.ds(0, tm)], tile_sem.at[buf]).wait()

    @pl.when(step < n_tiles)
    def _():
        base = step * tm
        buf = step % 2
        xs_ref[buf] = x_ref[...]

        def start(r, c):
            for slot in range(TOP_K):
                row_out(xs_ref.at[buf], r, dest_ref[TOP_K * (base + r) + slot], tile_sem.at[buf]).start()
            return c

        lax.fori_loop(0, tm, start, 0, unroll=8)

        @pl.when(step > 0)
        def _():
            wait_tile(1 - buf)

    @pl.when(step == n_tiles)
    def _():
        wait_tile((n_tiles - 1) % 2)
        zero_ref[...] = jnp.zeros_like(zero_ref)
        for e in range(N_EXPERTS):
            def start(r, c):
                row_out(zero_ref, 0, r, sem).start()
                return c

            def wait(r, c):
                row_out(zero_ref, 0, r, sem).wait()
                return c

            lax.fori_loop(lo_ref[e], hi_ref[e], start, 0)
            lax.fori_loop(lo_ref[e], hi_ref[e], wait, 0)


def _scatter_rows(x, dest, pad_lo, pad_hi, n_rows, tm):
    T, D = x.shape
    n_tiles = T // tm
    return pl.pallas_call(
        _scatter_rows_kernel,
        grid_spec=pltpu.PrefetchScalarGridSpec(
            num_scalar_prefetch=3,
            grid=(n_tiles + 1,),
            in_specs=[pl.BlockSpec((tm, D), lambda i, d, lo, hi: (jnp.minimum(i, n_tiles - 1), 0))],
            out_specs=pl.BlockSpec(memory_space=pl.ANY),
            scratch_shapes=[pltpu.VMEM((8, D), x.dtype), pltpu.VMEM((2, tm, D), x.dtype),
                            pltpu.SemaphoreType.DMA(()), pltpu.SemaphoreType.DMA((2,))]),
        out_shape=jax.ShapeDtypeStruct((n_rows, D), x.dtype),
        compiler_params=_cparams(("arbitrary",)),
        name="moe_scatter",
    )(dest, pad_lo, pad_hi, x)


def _moe_ffn_kernel(be_ref, nu_ref, x_ref, wg_ref, wu_ref, wd_ref, o_ref, acc_ref):
    i = pl.program_id(0)
    j = pl.program_id(1)

    last = pl.num_programs(1) - 1

    @pl.when(i < nu_ref[0])
    def _():
        xb = x_ref[...].astype(BF16)
        part = _dot(_silu(_dot(xb, wg_ref[...])) * _dot(xb, wu_ref[...]), wd_ref[...])

        @pl.when(j == 0)
        def _():
            acc_ref[...] = part

        @pl.when((j > 0) & (j < last))
        def _():
            acc_ref[...] += part

        @pl.when(j == last)
        def _():
            o_ref[...] = acc_ref[...] + part

    @pl.when((j == last) & (i >= nu_ref[0]))
    def _():
        o_ref[...] = jnp.zeros_like(o_ref)


def _moe_ffn_call(x_rows, block_e, n_used, wg, wu, wd, tf):
    n, D = x_rows.shape
    F = wg.shape[2]
    R = MOE_ROWS
    assert F % tf == 0 and F // tf >= 2
    return pl.pallas_call(
        _moe_ffn_kernel,
        grid_spec=pltpu.PrefetchScalarGridSpec(
            num_scalar_prefetch=2,
            grid=(n // R, F // tf),
            in_specs=[pl.BlockSpec((R, D), lambda i, j, be, nu: (i, 0)),
                      pl.BlockSpec((None, D, tf), lambda i, j, be, nu: (be[i], 0, j)),
                      pl.BlockSpec((None, D, tf), lambda i, j, be, nu: (be[i], 0, j)),
                      pl.BlockSpec((None, tf, D), lambda i, j, be, nu: (be[i], j, 0))],
            out_specs=pl.BlockSpec((R, D), lambda i, j, be, nu: (i, 0)),
            scratch_shapes=[pltpu.VMEM((R, D), F32)]),
        out_shape=jax.ShapeDtypeStruct((n, D), F32),
        compiler_params=_cparams(("arbitrary", "arbitrary")),
        name="moe_ffn",
    )(block_e, n_used, x_rows, wg, wu, wd)


def _moe_combine_kernel(d_ref, y_hbm, x_ref, r_ref, g_ref, b_ref, o_ref, y1_ref, y2_ref, sem):
    tm = x_ref.shape[0]
    t = pl.program_id(0)

    def issue(tile, buf):
        base = tile * tm

        def start(r, c):
            _row_copy(y_hbm, y1_ref.at[buf], d_ref[2 * (base + r)], r, sem.at[buf, 0]).start()
            _row_copy(y_hbm, y2_ref.at[buf], d_ref[2 * (base + r) + 1], r, sem.at[buf, 1]).start()
            return c

        lax.fori_loop(0, tm, start, 0, unroll=8)

    @pl.when(t == 0)
    def _():
        issue(0, 0)

    @pl.when(t + 1 < pl.num_programs(0))
    def _():
        issue(t + 1, (t + 1) % 2)

    buf = t % 2
    pltpu.make_async_copy(y_hbm.at[pl.ds(0, tm)], y1_ref.at[buf], sem.at[buf, 0]).wait()
    pltpu.make_async_copy(y_hbm.at[pl.ds(0, tm)], y2_ref.at[buf], sem.at[buf, 1]).wait()
    y = r_ref[:, 2:3] * y1_ref[buf] + r_ref[:, 3:4] * y2_ref[buf]
    o_ref[...] = _layer_norm_rows(DN_ALPHA * x_ref[...] + y, g_ref[...], b_ref[...])


def _moe_combine_call(dest, y_rows, x, routed, g, b, tm):
    S, D = x.shape
    return pl.pallas_call(
        _moe_combine_kernel,
        grid_spec=pltpu.PrefetchScalarGridSpec(
            num_scalar_prefetch=1,
            grid=(S // tm,),
            in_specs=[pl.BlockSpec(memory_space=pl.ANY),
                      pl.BlockSpec((tm, D), lambda i, d: (i, 0)),
                      pl.BlockSpec((tm, 128), lambda i, d: (i, 0)),
                      pl.BlockSpec((1, D), lambda i, d: (0, 0)),
                      pl.BlockSpec((1, D), lambda i, d: (0, 0))],
            out_specs=pl.BlockSpec((tm, D), lambda i, d: (i, 0)),
            scratch_shapes=[pltpu.VMEM((2, tm, D), F32), pltpu.VMEM((2, tm, D), F32),
                            pltpu.SemaphoreType.DMA((2, 2))]),
        out_shape=jax.ShapeDtypeStruct((S, D), F32),
        compiler_params=_cparams(("arbitrary",)),
        name="moe_combine",
    )(dest, y_rows, x, routed, g.reshape(1, D), b.reshape(1, D))


def _moe_sublayer(x, w_router, b_router, wg, wu, wd, g, b):
    T = x.shape[0]
    R = MOE_ROWS
    routed = _router_call(x, w_router, b_router, min(WIDE_TILE, T))
    top_e = routed[:, 0:TOP_K].astype(jnp.int32)
    tok_oh = jnp.sum((top_e[:, :, None] == jnp.arange(N_EXPERTS)[None, None, :]).astype(jnp.int32), axis=1)
    counts = jnp.sum(tok_oh, axis=0)
    rank = jnp.cumsum(tok_oh, axis=0) - tok_oh
    padded = (counts + R - 1) // R * R
    pend = jnp.cumsum(padded)
    pstart = pend - padded
    dest = pstart[top_e] + jnp.take_along_axis(rank, top_e, axis=1)
    n_rows = (T * TOP_K + N_EXPERTS * (R - 1)) // R * R
    n_blocks = n_rows // R
    block_first_row = jnp.arange(n_blocks, dtype=jnp.int32) * R
    block_e = jnp.minimum(jnp.sum((pend[None, :] <= block_first_row[:, None]).astype(jnp.int32), axis=1),
                          N_EXPERTS - 1)
    n_used = (pend[-1] // R).astype(jnp.int32).reshape(1)
    dest = dest.reshape(-1).astype(jnp.int32)
    pad_lo = (pstart + counts).astype(jnp.int32)
    pad_hi = jnp.concatenate([pstart[1:], jnp.array([n_rows])]).astype(jnp.int32)
    x_rows = _scatter_rows(x, dest, pad_lo, pad_hi, n_rows, min(ROW_TILE, T))
    y_rows = _moe_ffn_call(x_rows, block_e, n_used, wg, wu, wd, EXPERT_FF // 2)
    return _moe_combine_call(dest, y_rows, x, routed, g, b, min(COMBINE_TILE, T))


GLA_IN = 2 * GLA_QK + 2 * GLA_V + GLA_LR


def _gla_rwkv_sublayer(x, w_in, gla_wa2, gla_ba, gla_norm, mu, w0, w2, a0, a2, g2, k_k, k_a, r_k, ln_w, ln_b,
                       w_out, ln_g, ln_bias):
    S = x.shape[0]
    tm = min(WIDE_TILE, S)
    lr_pad = 128 - GLA_LR
    w_gla = jnp.pad(w_in[:, :GLA_IN], ((0, 0), (0, lr_pad))).astype(BF16)
    w_rwkv = w_in[:, GLA_IN:].astype(BF16)
    p_gla = _matmul(x, w_gla, F32, tm, GLA_COLS)
    p_rwkv = _matmul(x, w_rwkv, F32, tm, RWKV_COLS)
    wa2p = jnp.pad(gla_wa2, ((0, lr_pad), (0, 0))).astype(BF16)
    o_gla = _gla_call(p_gla, wa2p, gla_ba, gla_norm)
    o_rwkv = _rwkv_call(p_rwkv, mu, w0, w2, a0, a2, g2, k_k, k_a, r_k, ln_w, ln_b)
    return _matmul_ln([o_gla, o_rwkv], [w_out[:GLA_V].astype(BF16), w_out[GLA_V:].astype(BF16)], x,
                      ln_g, ln_bias, tm)


def _xattn_sublayer(x, mem, wq, wk, wv, wo, ln_g, ln_bias):
    M = mem.shape[0]
    k_mem = _matmul(mem, wk.astype(BF16), BF16, M, D_MODEL)
    v_mem = _matmul(mem, wv.astype(BF16), BF16, M, D_MODEL)
    wq_scaled = (wq * XATTN_HD ** -0.5).astype(BF16)
    return _xattn_call(x, wq_scaled, k_mem, v_mem, wo.astype(BF16), ln_g, ln_bias, min(WIDE_TILE, x.shape[0]))


def _ssd_moba_sublayer(x, w_in, conv_w, conv_b, dt_bias, a_log, d_skip, ssd_norm, w_out, ln_g, ln_bias):
    S = x.shape[0]
    tm = min(WIDE_TILE, S)
    nb = S // MOBA_BLOCK
    o_dt = SSD_COLS
    o_q = o_dt + SSD_HEADS
    w_ssd = jnp.pad(w_in[:, :o_q], ((0, 0), (0, 128 - SSD_HEADS))).astype(BF16)
    w_q = w_in[:, o_q:o_q + MOBA_D]
    w_k = w_in[:, o_q + MOBA_D:o_q + 2 * MOBA_D]
    w_v = w_in[:, o_q + 2 * MOBA_D:]
    p_ssd = _matmul(x, w_ssd, F32, tm, SSD_COLS + 128)
    o_ssd = _ssd_call(p_ssd, conv_w, conv_b, dt_bias, a_log, d_skip, ssd_norm)
    qt, k4, kmean, vt4, knorm, qnorm = _moba_proj_call(x, (w_q.T * (MOBA_HD ** -0.5 * LOG2E)).astype(BF16), w_k.astype(BF16),
                                         w_v.T.astype(BF16))
    kmean_h = kmean.reshape(nb, MOBA_HEADS, MOBA_HD).transpose(1, 0, 2)
    ot_moba = _moba_call(_moba_first_block(knorm, qnorm, nb), qt, k4, vt4, kmean_h)
    return _matmul_ln([o_ssd, ot_moba], [w_out[:SSD_INNER].astype(BF16), w_out[SSD_INNER:].astype(BF16)], x,
                      ln_g, ln_bias, tm, transposed=(False, True))


def kernel(x, mem, l0_w_in, l0_gla_wa2, l0_gla_ba, l0_gla_norm, l0_rwkv_mu, l0_rwkv_w0, l0_rwkv_w2, l0_rwkv_a0, l0_rwkv_a2, l0_rwkv_g2, l0_rwkv_kk, l0_rwkv_ka, l0_rwkv_rk, l0_rwkv_lnw, l0_rwkv_lnb, l0_w_out, l0_ln1_g, l0_ln1_b, l0_xq, l0_xk, l0_xv, l0_xo, l0_ln2_g, l0_ln2_b, l0_ffn_wg, l0_ffn_wu, l0_ffn_wd, l0_ln3_g, l0_ln3_b, l1_w_in, l1_conv_w, l1_conv_b, l1_dt_bias, l1_a_log, l1_d_skip, l1_ssd_norm, l1_w_out, l1_ln1_g, l1_ln1_b, l1_xq, l1_xk, l1_xv, l1_xo, l1_ln2_g, l1_ln2_b, l1_router, l1_router_b, l1_exp_wg, l1_exp_wu, l1_exp_wd, l1_ln3_g, l1_ln3_b):
    x2 = x.reshape(-1, D_MODEL)
    mem2 = mem.reshape(-1, D_MODEL)
    x2 = _gla_rwkv_sublayer(x2, l0_w_in, l0_gla_wa2, l0_gla_ba, l0_gla_norm, l0_rwkv_mu, l0_rwkv_w0, l0_rwkv_w2,
                            l0_rwkv_a0, l0_rwkv_a2, l0_rwkv_g2, l0_rwkv_kk, l0_rwkv_ka, l0_rwkv_rk, l0_rwkv_lnw,
                            l0_rwkv_lnb, l0_w_out, l0_ln1_g, l0_ln1_b)
    x2 = _xattn_sublayer(x2, mem2, l0_xq, l0_xk, l0_xv, l0_xo, l0_ln2_g, l0_ln2_b)
    tm = min(ROW_TILE, x2.shape[0])
    x2 = _ffn_call(x2, l0_ffn_wg.astype(BF16), l0_ffn_wu.astype(BF16), l0_ffn_wd.astype(BF16),
                   l0_ln3_g, l0_ln3_b, tm, l0_ffn_wg.shape[1] // 2)
    x2 = _ssd_moba_sublayer(x2, l1_w_in, l1_conv_w, l1_conv_b, l1_dt_bias, l1_a_log, l1_d_skip, l1_ssd_norm,
                            l1_w_out, l1_ln1_g, l1_ln1_b)
    x2 = _xattn_sublayer(x2, mem2, l1_xq, l1_xk, l1_xv, l1_xo, l1_ln2_g, l1_ln2_b)
    x2 = _moe_sublayer(x2, l1_router, l1_router_b, l1_exp_wg.astype(BF16), l1_exp_wu.astype(BF16),
                       l1_exp_wd.astype(BF16), l1_ln3_g, l1_ln3_b)
    return x2.reshape(x.shape)
```

```python
import functools
import math

import jax
import jax.numpy as jnp
from jax import lax
from jax.experimental import pallas as pl
from jax.experimental.pallas import tpu as pltpu

BF16 = jnp.bfloat16
F32 = jnp.float32

D_MODEL = 1024
LN_EPS = 1e-5
DEPTH = 2
DN_ALPHA = (2 * DEPTH) ** 0.25

GLA_HEADS, GLA_DK, GLA_DV, GLA_CHUNK = 4, 64, 128, 64
GLA_QK, GLA_V, GLA_LR, GLA_TAU = 256, 512, 16, 16.0
GLA_COLS = 2 * GLA_QK + 2 * GLA_V + 128

RWKV_HEADS, RWKV_HD, RWKV_D, RWKV_CHUNK = 8, 64, 512, 64
RWKV_COLS = 1792
RWKV_DECAY_SCALE = math.exp(-0.5)
RWKV_GN_EPS = 64e-5

VMEM_LIMIT = 56 * 1024 * 1024
ROW_TILE = 512
WIDE_TILE = 1024
SCAN_TILE = 256
COMBINE_TILE = 256


def _cparams(sem):
    return pltpu.CompilerParams(dimension_semantics=sem, vmem_limit_bytes=VMEM_LIMIT)


def _dot(a, b):
    return jnp.dot(a.astype(BF16), b.astype(BF16), preferred_element_type=F32)


def _dot_nt(a, b):
    return lax.dot_general(a.astype(BF16), b.astype(BF16), (((1,), (1,)), ((), ())), preferred_element_type=F32)


def _dot_tn(a, b):
    return lax.dot_general(a.astype(BF16), b.astype(BF16), (((0,), (0,)), ((), ())), preferred_element_type=F32)


def _split3(x):
    hi = x.astype(BF16)
    r1 = x - hi.astype(F32)
    mid = r1.astype(BF16)
    lo = (r1 - mid.astype(F32)).astype(BF16)
    return hi, mid, lo


def _dot_exact_lhs(m, x):
    mb = m.astype(BF16)
    hi, mid, lo = _split3(x)
    return (jnp.dot(mb, hi, preferred_element_type=F32) + jnp.dot(mb, mid, preferred_element_type=F32)
            + jnp.dot(mb, lo, preferred_element_type=F32))


def _dot_exact_rhs(x, m):
    mb = m.astype(BF16)
    hi, mid, lo = _split3(x)
    return (jnp.dot(hi, mb, preferred_element_type=F32) + jnp.dot(mid, mb, preferred_element_type=F32)
            + jnp.dot(lo, mb, preferred_element_type=F32))


def _dot_stat_rhs(x, m):
    mb = m.astype(BF16)
    hi = x.astype(BF16)
    lo = (x - hi.astype(F32)).astype(BF16)
    return jnp.dot(hi, mb, preferred_element_type=F32) + jnp.dot(lo, mb, preferred_element_type=F32)


def _sigmoid(x):
    return 1.0 / (1.0 + jnp.exp(-x))


def _silu(x):
    return x * _sigmoid(x)


def _iota2(shape, axis):
    return lax.broadcasted_iota(jnp.int32, shape, axis)


def _chunk_tril(n, chunk):
    r = _iota2((n, n), 0)
    c = _iota2((n, n), 1)
    return jnp.where((c <= r) & ((r // chunk) == (c // chunk)), 1.0, 0.0)


def _head_block(n, width, value):
    r = _iota2((n, n), 0)
    c = _iota2((n, n), 1)
    return jnp.where((r // width) == (c // width), value, 0.0)


def _mm_kernel(x_ref, w_ref, o_ref):
    o_ref[...] = _dot(x_ref[...], w_ref[...]).astype(o_ref.dtype)


def _matmul(x, w, out_dtype, tm, tn):
    S, K = x.shape
    N = w.shape[1]
    return pl.pallas_call(
        _mm_kernel,
        grid=(S // tm, N // tn),
        in_specs=[pl.BlockSpec((tm, K), lambda i, j: (i, 0)),
                  pl.BlockSpec((K, tn), lambda i, j: (0, j))],
        out_specs=pl.BlockSpec((tm, tn), lambda i, j: (i, j)),
        out_shape=jax.ShapeDtypeStruct((S, N), out_dtype),
        compiler_params=_cparams(("parallel", "arbitrary")),
        name="matmul",
    )(x, w)


def _mm2_kernel(x_ref, w1_ref, w2_ref, o1_ref, o2_ref):
    xb = x_ref[...].astype(BF16)
    o1_ref[...] = _dot(xb, w1_ref[...]).astype(o1_ref.dtype)
    o2_ref[...] = _dot(xb, w2_ref[...]).astype(o2_ref.dtype)


def _matmul2(x, w1, w2, out_dtype, tm):
    S, K = x.shape
    N1, N2 = w1.shape[1], w2.shape[1]
    return pl.pallas_call(
        _mm2_kernel,
        grid=(S // tm,),
        in_specs=[pl.BlockSpec((tm, K), lambda i: (i, 0)),
                  pl.BlockSpec((K, N1), lambda i: (0, 0)),
                  pl.BlockSpec((K, N2), lambda i: (0, 0))],
        out_specs=[pl.BlockSpec((tm, N1), lambda i: (i, 0)), pl.BlockSpec((tm, N2), lambda i: (i, 0))],
        out_shape=[jax.ShapeDtypeStruct((S, N1), out_dtype), jax.ShapeDtypeStruct((S, N2), out_dtype)],
        compiler_params=_cparams(("parallel",)),
        name="matmul2",
    )(x, w1, w2)


def _layer_norm_rows(y, g, b):
    mu = jnp.mean(y, -1, keepdims=True)
    d = y - mu
    var = jnp.mean(d * d, -1, keepdims=True)
    return d * lax.rsqrt(var + LN_EPS) * g + b


def _mm_ln_kernel(transposed, *refs):
    n_in = len(transposed)
    a_refs = refs[:n_in]
    w_refs = refs[n_in:2 * n_in]
    x_ref, g_ref, b_ref, o_ref = refs[2 * n_in:]
    acc = DN_ALPHA * x_ref[...]
    for a_ref, w_ref, tr in zip(a_refs, w_refs, transposed):
        acc = acc + (_dot_tn if tr else _dot)(a_ref[...], w_ref[...])
    o_ref[...] = _layer_norm_rows(acc, g_ref[...], b_ref[...])


def _matmul_ln(a_list, w_list, x, g, b, tm, transposed=None):
    S, D = x.shape
    transposed = tuple(transposed or (False,) * len(a_list))
    in_specs = ([pl.BlockSpec((a.shape[0], tm), lambda i: (0, i)) if tr else
                 pl.BlockSpec((tm, a.shape[1]), lambda i: (i, 0)) for a, tr in zip(a_list, transposed)]
                + [pl.BlockSpec(w.shape, lambda i: (0, 0)) for w in w_list]
                + [pl.BlockSpec((tm, D), lambda i: (i, 0)),
                   pl.BlockSpec((1, D), lambda i: (0, 0)),
                   pl.BlockSpec((1, D), lambda i: (0, 0))])
    return pl.pallas_call(
        functools.partial(_mm_ln_kernel, transposed),
        grid=(S // tm,),
        in_specs=in_specs,
        out_specs=pl.BlockSpec((tm, D), lambda i: (i, 0)),
        out_shape=jax.ShapeDtypeStruct((S, D), F32),
        compiler_params=_cparams(("parallel",)),
        name="matmul_ln",
    )(*a_list, *w_list, x, g.reshape(1, D), b.reshape(1, D))


def _gla_kernel(p_ref, wa2_ref, ba_ref, ng_ref, o_ref, st_ref, o_scr):
    C, H, dk, dv = GLA_CHUNK, GLA_HEADS, GLA_DK, GLA_DV
    tb = p_ref.shape[0]

    @pl.when(pl.program_id(0) == 0)
    def _():
        st_ref[...] = jnp.zeros_like(st_ref)

    z = _dot(p_ref[:, 2 * GLA_QK + 2 * GLA_V:], wa2_ref[...]) + ba_ref[...]
    log_a = -(jnp.maximum(-z, 0.0) + jnp.log(1.0 + jnp.exp(-jnp.abs(z)))) / GLA_TAU
    b = _dot_exact_lhs(_chunk_tril(tb, C), log_a)
    causal = _iota2((C, C), 1) <= _iota2((C, C), 0)

    nc = tb // C
    q_h, k_h, ke_h, v_h, dec_h = [], [], [], [], []
    for c in range(nc):
        rows = slice(c * C, (c + 1) * C)
        b_c = b[rows]
        b_last = b_c[C - 1:C]
        q_dec = p_ref[rows, 0:GLA_QK] * (dk ** -0.5) * jnp.exp(b_c)
        k_c = p_ref[rows, GLA_QK:2 * GLA_QK]
        k_dec = k_c * jnp.exp(-b_c)
        k_end = k_c * jnp.exp(b_last - b_c)
        decay = jnp.exp(b_last)
        for h in range(H):
            ks = slice(h * dk, (h + 1) * dk)
            q_h.append(q_dec[:, ks])
            k_h.append(k_dec[:, ks])
            ke_h.append(k_end[:, ks])
            dec_h.append(decay[:, ks])
            v_h.append(p_ref[rows, 2 * GLA_QK + h * dv:2 * GLA_QK + (h + 1) * dv])
    n = nc * H
    attn = [jnp.where(causal, _dot_nt(q_h[i], k_h[i]), 0.0) for i in range(n)]
    kv = [_dot_tn(v_h[i], ke_h[i]) for i in range(n)]
    intra = [_dot(attn[i], v_h[i]) for i in range(n)]
    state = [st_ref[:, h * dk:(h + 1) * dk] for h in range(H)]
    entering = []
    for i in range(n):
        entering.append(state[i % H])
        state[i % H] = state[i % H] * dec_h[i] + kv[i]
    for i in range(n):
        c, h = divmod(i, H)
        o_scr[c * C:(c + 1) * C, h * dv:(h + 1) * dv] = intra[i] + _dot_nt(q_h[i], entering[i])
    for h in range(H):
        st_ref[:, h * dk:(h + 1) * dk] = state[h]

    for h in range(H):
        vs = slice(h * dv, (h + 1) * dv)
        o_h = o_scr[:, vs]
        g_h = p_ref[:, 2 * GLA_QK + GLA_V + h * dv:2 * GLA_QK + GLA_V + (h + 1) * dv]
        o_h = o_h * lax.rsqrt(jnp.mean(o_h * o_h, -1, keepdims=True) + 1e-5) * ng_ref[:, vs]
        o_ref[:, vs] = (o_h * _silu(g_h)).astype(o_ref.dtype)


def _gla_call(p_gla, wa2p, ba, norm_g, tb=SCAN_TILE):
    S = p_gla.shape[0]
    return pl.pallas_call(
        _gla_kernel,
        grid=(S // tb,),
        in_specs=[pl.BlockSpec((tb, GLA_COLS), lambda i: (i, 0)),
                  pl.BlockSpec((128, GLA_QK), lambda i: (0, 0)),
                  pl.BlockSpec((1, GLA_QK), lambda i: (0, 0)),
                  pl.BlockSpec((1, GLA_V), lambda i: (0, 0))],
        out_specs=pl.BlockSpec((tb, GLA_V), lambda i: (i, 0)),
        out_shape=jax.ShapeDtypeStruct((S, GLA_V), BF16),
        scratch_shapes=[pltpu.VMEM((GLA_DV, GLA_QK), F32), pltpu.VMEM((tb, GLA_V), F32)],
        compiler_params=_cparams(("arbitrary",)),
        name="gla",
    )(p_gla, wa2p, ba.reshape(1, GLA_QK), norm_g.reshape(1, GLA_V))


def _rwkv_kernel(p_ref, mu_ref, w0_ref, w2_ref, a0_ref, a2_ref, g2_ref, kk_ref, ka_ref, rk_ref, lnw_ref, lnb_ref,
                 o_ref, prev_ref, h_ref, o_scr):
    C, H, N, D = RWKV_CHUNK, RWKV_HEADS, RWKV_HD, RWKV_D
    tb = p_ref.shape[0]
    first = pl.program_id(0) == 0

    @pl.when(first)
    def _():
        prev_ref[...] = jnp.zeros_like(prev_ref)
        h_ref[...] = jnp.zeros_like(h_ref)

    p = p_ref[...]
    shifted = jnp.where(_iota2(p.shape, 0) == 0, prev_ref[...], pltpu.roll(p, 1, 0))
    prev_ref[...] = p[tb - 1:tb]
    p = p + mu_ref[...] * (shifted - p)
    r = p[:, 0:D]
    k = p[:, D:2 * D]
    v = p[:, 2 * D:3 * D]
    xw = p[:, 3 * D:3 * D + 64]
    xa = p[:, 3 * D + 64:3 * D + 128]
    xg = p[:, 3 * D + 128:3 * D + 256]
    lw = -RWKV_DECAY_SCALE * _sigmoid(w0_ref[...] + _dot(jnp.tanh(xw), w2_ref[...]))
    a = _sigmoid(a0_ref[...] + _dot(xa, a2_ref[...]))
    g = _dot(_sigmoid(xg), g2_ref[...])
    head_ones = _head_block(D, N, 1.0)
    kk = k * kk_ref[...]
    kk = kk * lax.rsqrt(jnp.maximum(_dot_stat_rhs(kk * kk, head_ones), 1e-24))
    k = k * (1.0 + (a - 1.0) * ka_ref[...])
    pv = -kk * a
    cw = _dot_exact_lhs(_chunk_tril(tb, C), lw)
    cwx = cw - lw

    W2 = 2 * N
    left = _iota2((C, W2), 1) < N

    def blockdiag(a):
        return jnp.concatenate([jnp.where(left, a, 0.0), jnp.where(left, 0.0, a)], axis=0)

    def diag_blocks(full):
        return jnp.where(left, full[0:C], full[C:2 * C])

    gi = _iota2((2 * C, 2 * W2), 0)
    gj = _iota2((2 * C, 2 * W2), 1) % N
    gram_mask = ((gi < C) & (gj < gi)) | ((gi >= C) & (gj <= gi - C))
    eye = _iota2((C, W2), 0) == _iota2((C, W2), 1) % N
    eye_f = jnp.where(eye, 1.0, 0.0)
    zeros_c = jnp.zeros((C, W2), F32)
    zeros_w = jnp.zeros((W2, W2), F32)

    nc = tb // C
    items = [(c, p) for c in range(nc) for p in range(H // 2)]
    xs, ys, pk_e, b_h, v_h, r_h, g_h = [], [], [], [], [], [], []
    for c in range(nc):
        rows = slice(c * C, (c + 1) * C)
        cw_c = cw[rows]
        cw_end = cw_c[C - 1:C]
        e_pos = jnp.exp(cw_c)
        e_neg = jnp.exp(-cw_c)
        e_end = jnp.exp(cw_end - cw_c)
        r_t = r[rows] * e_pos
        b_t = kk[rows] * jnp.exp(cwx[rows])
        p_t = pv[rows] * e_neg
        k_t = k[rows] * e_neg
        p_e = pv[rows] * e_end
        k_e = k[rows] * e_end
        g_end = jnp.exp(cw_end)
        v_c = v[rows]
        for p in range(H // 2):
            ps = slice(p * W2, (p + 1) * W2)
            xs.append(jnp.concatenate([b_t[:, ps], r_t[:, ps]], axis=0))
            ys.append(jnp.concatenate([blockdiag(p_t[:, ps]), blockdiag(k_t[:, ps])], axis=0))
            pk_e.append(jnp.concatenate([p_e[:, ps], k_e[:, ps]], axis=0))
            b_h.append(b_t[:, ps])
            v_h.append(v_c[:, ps])
            r_h.append(r_t[:, ps])
            g_h.append(g_end[:, ps])
    n = len(items)
    grams = [jnp.where(gram_mask, _dot_nt(xs[i], ys[i]), 0.0) for i in range(n)]
    l_p = [g[0:C, 0:W2] for g in grams]
    m_pk = [g[C:2 * C, :] for g in grams]
    v_bd = [blockdiag(v_h[i]) for i in range(n)]
    lkv = [_dot(grams[i][0:C, W2:2 * W2], v_bd[i]) for i in range(n)]
    x = [_dot(lp, blockdiag(lp)) for lp in l_p]
    t = [eye_f + lp for lp in l_p]
    for _ in range(4):
        tx = [_dot(jnp.concatenate([t[i], x[i]], axis=0), blockdiag(x[i])) for i in range(n)]
        t = [t[i] + tx[i][0:C] for i in range(n)]
        x = [tx[i][C:2 * C] for i in range(n)]
    t = [t[i] + _dot(t[i], blockdiag(x[i])) for i in range(n)]
    wu = [_dot(t[i], jnp.concatenate([blockdiag(b_h[i]), blockdiag(lkv[i])], axis=1)) for i in range(n)]
    az = [_dot_tn(pk_e[i], jnp.concatenate([wu[i], jnp.concatenate([zeros_c, v_h[i]], axis=1)], axis=0))
          for i in range(n)]
    qo = [_dot(m_pk[i], jnp.concatenate(
        [jnp.concatenate([blockdiag(wu[i][:, 0:W2]), blockdiag(wu[i][:, W2:2 * W2])], axis=1),
         jnp.concatenate([zeros_w, v_bd[i]], axis=1)], axis=0)) for i in range(n)]
    state = [h_ref[p] for p in range(H // 2)]
    for i, (c, p) in enumerate(items):
        a_mat = diag_blocks(az[i][:, 0:W2]) + jnp.where(eye, g_h[i], 0.0)
        q_mat = qo[i][:, 0:W2] + r_h[i]
        oh = _dot(jnp.concatenate([q_mat, a_mat], axis=0), blockdiag(state[p]))
        o_scr[c * C:(c + 1) * C, p * W2:(p + 1) * W2] = oh[0:C] + qo[i][:, W2:2 * W2]
        state[p] = oh[C:C + N] + diag_blocks(az[i][:, W2:2 * W2])
    for p in range(H // 2):
        h_ref[p] = state[p]

    o = o_scr[...]
    head_mean = _head_block(D, N, 1.0 / N)
    mean = _dot_stat_rhs(o, head_mean)
    d = o - mean
    var = _dot_stat_rhs(d * d, head_mean)
    o = d * lax.rsqrt(var + RWKV_GN_EPS) * lnw_ref[...] + lnb_ref[...]
    bonus = _dot_stat_rhs(r * k * rk_ref[...], head_ones) * v
    o_ref[...] = ((o + bonus) * g).astype(o_ref.dtype)


def _rwkv_call(p_rwkv, mu, w0, w2, a0, a2, g2, k_k, k_a, r_k, ln_w, ln_b, tb=SCAN_TILE):
    S = p_rwkv.shape[0]
    D = RWKV_D
    row = lambda t: t.reshape(1, -1).astype(F32)
    full = lambda shape: pl.BlockSpec(shape, lambda i: tuple(0 for _ in shape))
    return pl.pallas_call(
        _rwkv_kernel,
        grid=(S // tb,),
        in_specs=[pl.BlockSpec((tb, RWKV_COLS), lambda i: (i, 0)),
                  full((1, RWKV_COLS)), full((1, D)), full((64, D)), full((1, D)), full((64, D)), full((128, D)),
                  full((1, D)), full((1, D)), full((1, D)), full((1, D)), full((1, D))],
        out_specs=pl.BlockSpec((tb, D), lambda i: (i, 0)),
        out_shape=jax.ShapeDtypeStruct((S, D), BF16),
        scratch_shapes=[pltpu.VMEM((1, RWKV_COLS), F32),
                        pltpu.VMEM((RWKV_HEADS // 2, RWKV_HD, 2 * RWKV_HD), F32),
                        pltpu.VMEM((tb, D), F32)],
        compiler_params=_cparams(("arbitrary",)),
        name="rwkv7",
    )(p_rwkv, row(mu), row(w0), w2.astype(BF16), row(a0), a2.astype(BF16), g2.astype(BF16),
      row(k_k), row(k_a), row(r_k), row(ln_w), row(ln_b))


XATTN_HEADS, XATTN_HD = 4, 256


def _xattn_kernel(x_ref, wq_ref, k_ref, v_ref, wo_ref, g_ref, b_ref, o_ref):
    x = x_ref[...]
    q = _dot(x, wq_ref[...])
    outs = []
    for h in range(XATTN_HEADS):
        hs = slice(h * XATTN_HD, (h + 1) * XATTN_HD)
        s = _dot_nt(q[:, hs], k_ref[:, hs])
        e = jnp.exp(s - jnp.max(s, -1, keepdims=True))
        p = e / jnp.sum(e, -1, keepdims=True)
        outs.append(_dot(p, v_ref[:, hs]))
    o = jnp.concatenate(outs, axis=1)
    y = DN_ALPHA * x + _dot(o, wo_ref[...])
    o_ref[...] = _layer_norm_rows(y, g_ref[...], b_ref[...])


def _xattn_call(x, wq_scaled, k_mem, v_mem, wo, g, b, tm):
    S, D = x.shape
    M = k_mem.shape[0]
    const = lambda shape: pl.BlockSpec(shape, lambda i: (0, 0))
    return pl.pallas_call(
        _xattn_kernel,
        grid=(S // tm,),
        in_specs=[pl.BlockSpec((tm, D), lambda i: (i, 0)), const((D, D)), const((M, D)), const((M, D)),
                  const((D, D)), const((1, D)), const((1, D))],
        out_specs=pl.BlockSpec((tm, D), lambda i: (i, 0)),
        out_shape=jax.ShapeDtypeStruct((S, D), F32),
        compiler_params=_cparams(("parallel",)),
        name="xattn",
    )(x, wq_scaled, k_mem, v_mem, wo, g.reshape(1, D), b.reshape(1, D))


def _ffn_kernel(x_ref, wg_ref, wu_ref, wd_ref, g_ref, b_ref, o_ref, acc_ref):
    j = pl.program_id(1)
    x = x_ref[...]
    xb = x.astype(BF16)
    part = _dot(_silu(_dot(xb, wg_ref[...])) * _dot(xb, wu_ref[...]), wd_ref[...])

    last = pl.num_programs(1) - 1

    @pl.when(j == 0)
    def _():
        acc_ref[...] = DN_ALPHA * x + part

    @pl.when((j > 0) & (j < last))
    def _():
        acc_ref[...] += part

    @pl.when(j == last)
    def _():
        o_ref[...] = _layer_norm_rows(acc_ref[...] + part, g_ref[...], b_ref[...])


def _ffn_call(x, wg, wu, wd, g, b, tm, tf):
    S, D = x.shape
    F = wg.shape[1]
    assert F % tf == 0 and F // tf >= 2
    return pl.pallas_call(
        _ffn_kernel,
        grid=(S // tm, F // tf),
        in_specs=[pl.BlockSpec((tm, D), lambda i, j: (i, 0)),
                  pl.BlockSpec((D, tf), lambda i, j: (0, j)),
                  pl.BlockSpec((D, tf), lambda i, j: (0, j)),
                  pl.BlockSpec((tf, D), lambda i, j: (j, 0)),
                  pl.BlockSpec((1, D), lambda i, j: (0, 0)),
                  pl.BlockSpec((1, D), lambda i, j: (0, 0))],
        out_specs=pl.BlockSpec((tm, D), lambda i, j: (i, 0)),
        out_shape=jax.ShapeDtypeStruct((S, D), F32),
        scratch_shapes=[pltpu.VMEM((tm, D), F32)],
        compiler_params=_cparams(("parallel", "arbitrary")),
        name="ffn",
    )(x, wg, wu, wd, g.reshape(1, D), b.reshape(1, D))


SSD_HD, SSD_HEADS, SSD_INNER, SSD_GROUPS, SSD_STATE = 64, 16, 1024, 2, 128
SSD_BC, SSD_CONV, SSD_CONV_CH, SSD_CHUNK = 256, 4, 1536, 128
SSD_COLS = SSD_INNER + SSD_CONV_CH
SSD_GW = SSD_INNER // SSD_GROUPS


def _softplus(x):
    return jnp.maximum(x, 0.0) + jnp.log(1.0 + jnp.exp(-jnp.abs(x)))


def _ssd_kernel(p_ref, cw_ref, cb_ref, dtb_ref, a_ref, dsk_ref, ng_ref, o_ref, prev_ref, st_ref, y_scr):
    L, G, NS, HD = SSD_CHUNK, SSD_GROUPS, SSD_STATE, SSD_HD
    HG = SSD_HEADS // G

    @pl.when(pl.program_id(0) == 0)
    def _():
        prev_ref[...] = jnp.zeros_like(prev_ref)
        st_ref[...] = jnp.zeros_like(st_ref)

    cur = p_ref[:, SSD_INNER:SSD_COLS]
    tail = prev_ref[...]
    row = _iota2(tail.shape, 0)
    conv = cur * cw_ref[SSD_CONV - 1:SSD_CONV, :] + cb_ref[...]
    for kk in range(1, SSD_CONV):
        rolled = pltpu.roll(cur, kk, 0)
        head = jnp.where(row < kk, pltpu.roll(tail, kk, 0), rolled[0:8])
        shifted = jnp.concatenate([head, rolled[8:]], axis=0)
        conv = conv + shifted * cw_ref[SSD_CONV - 1 - kk:SSD_CONV - kk, :]
    prev_ref[...] = cur[L - 8:L]
    xbc = _silu(conv)
    xs = xbc[:, :SSD_INNER]

    dt = _softplus(p_ref[:, SSD_COLS:] + dtb_ref[...])
    a_col = dt * a_ref[...]
    li = _iota2((L, L), 0)
    lj = _iota2((L, L), 1)
    cs = _dot_exact_lhs(jnp.where(lj <= li, 1.0, 0.0), a_col)
    cs_row = cs.T
    expand = jnp.where(_iota2((128, SSD_INNER), 1) // HD == _iota2((128, SSD_INNER), 0), 1.0, 0.0)
    dt_x = _dot_stat_rhs(dt, expand)
    cs_x = _dot_exact_rhs(cs, expand)
    cs_end = cs_x[L - 1:L]
    xd = xs * dt_x
    xd_dec = xd * jnp.exp(cs_end - cs_x)
    out_dec = jnp.exp(cs_x)
    chunk_dec = jnp.exp(cs_end)
    tril = lj <= li

    for g in range(G):
        gs = slice(g * SSD_GW, (g + 1) * SSD_GW)
        b_g = xbc[:, SSD_INNER + g * NS:SSD_INNER + (g + 1) * NS]
        c_g = xbc[:, SSD_INNER + SSD_BC + g * NS:SSD_INNER + SSD_BC + (g + 1) * NS]
        cb = _dot_nt(c_g, b_g)
        for j in range(HG):
            h = g * HG + j
            hs = slice(h * HD, (h + 1) * HD)
            seg = jnp.where(tril, jnp.exp(cs[:, h:h + 1] - cs_row[h:h + 1, :]), 0.0)
            y_scr[:, hs] = _dot(cb * seg, xd[:, hs])
        st = st_ref[g]
        y_off = _dot(c_g, st) * out_dec[:, gs]
        st_ref[g] = st * chunk_dec[:, gs] + _dot_tn(b_g, xd_dec[:, gs])
        y_scr[:, gs] = y_scr[:, gs] + y_off

    y = (y_scr[...] + dsk_ref[...] * xs) * _silu(p_ref[:, :SSD_INNER])
    for g in range(G):
        gs = slice(g * SSD_GW, (g + 1) * SSD_GW)
        y_g = y[:, gs]
        o_ref[:, gs] = (y_g * lax.rsqrt(jnp.mean(y_g * y_g, -1, keepdims=True) + 1e-5) * ng_ref[:, gs]).astype(o_ref.dtype)


def _ssd_call(p_ssd, conv_w, conv_b, dt_bias, a_log, d_skip, norm_g):
    S = p_ssd.shape[0]
    L = SSD_CHUNK
    a_neg = -jnp.exp(a_log.astype(F32))
    pad = lambda t: jnp.pad(t.astype(F32), (0, 128 - SSD_HEADS)).reshape(1, 128)
    const = lambda shape: pl.BlockSpec(shape, lambda i: (0, 0))
    return pl.pallas_call(
        _ssd_kernel,
        grid=(S // L,),
        in_specs=[pl.BlockSpec((L, SSD_COLS + 128), lambda i: (i, 0)),
                  const((SSD_CONV, SSD_CONV_CH)), const((1, SSD_CONV_CH)),
                  const((1, 128)), const((1, 128)),
                  const((1, SSD_INNER)), const((1, SSD_INNER))],
        out_specs=pl.BlockSpec((L, SSD_INNER), lambda i: (i, 0)),
        out_shape=jax.ShapeDtypeStruct((S, SSD_INNER), BF16),
        scratch_shapes=[pltpu.VMEM((8, SSD_CONV_CH), F32),
                        pltpu.VMEM((SSD_GROUPS, SSD_STATE, SSD_GW), F32),
                        pltpu.VMEM((L, SSD_INNER), F32)],
        compiler_params=_cparams(("arbitrary",)),
        name="ssd",
    )(p_ssd, conv_w.astype(F32), conv_b.reshape(1, -1).astype(F32), pad(dt_bias), pad(a_neg),
      jnp.repeat(d_skip.astype(F32), SSD_HD).reshape(1, -1), norm_g.reshape(1, -1).astype(F32))


MOBA_HD, MOBA_HEADS, MOBA_D, MOBA_BLOCK, MOBA_TOPK = 64, 8, 512, 256, 3
MOBA_GROUP = 4
MOBA_UNDERFLOW = -160.0
MOBA_BOUND_SLACK = 1.001
NEG_BIG = -1e30
LOG2E = math.log2(math.e)


MOBA_VROWS = MOBA_HD + 16


def _moba_proj_kernel(x_ref, wqt_ref, wk_ref, wvt_ref, qt_ref, k_ref, kmean_ref, v_ref, knorm_ref, qnorm_ref):
    xb = x_ref[...].astype(BF16)
    qt = _dot_nt(wqt_ref[...], xb).astype(qt_ref.dtype)
    qt_ref[...] = qt
    vt = _dot_nt(wvt_ref[...], xb)
    extra = jnp.where(_iota2((MOBA_VROWS - MOBA_HD, MOBA_BLOCK), 0) == 0, 1.0, 0.0)
    for h in range(MOBA_HEADS):
        v_ref[h, 0] = jnp.concatenate([vt[h * MOBA_HD:(h + 1) * MOBA_HD], extra], axis=0).astype(v_ref.dtype)
    k = _dot(xb, wk_ref[...])
    kmean_ref[0] = jnp.mean(k, 0, keepdims=True)
    shape = (MOBA_BLOCK, 2 * MOBA_HD)
    lane = _iota2(shape, 1)
    pos = jnp.where((lane == MOBA_HD) | (lane == MOBA_HD + 1), _iota2(shape, 0).astype(F32), 0.0).astype(k_ref.dtype)
    k_b = k.astype(k_ref.dtype)
    for h in range(MOBA_HEADS):
        k_ref[h, 0] = pos
        k_ref[h, 0, :, 0:MOBA_HD] = k_b[:, h * MOBA_HD:(h + 1) * MOBA_HD]
    k_f = k_b.astype(F32)
    q_f = qt.astype(F32)
    head_cols = jnp.where(_iota2((MOBA_D, 128), 0) // MOBA_HD == _iota2((MOBA_D, 128), 1), 1.0, 0.0)
    head_rows = jnp.where(_iota2((128, MOBA_D), 1) // MOBA_HD == _iota2((128, MOBA_D), 0), 1.0, 0.0)
    knorm_ref[0] = jnp.sqrt(jnp.max(_dot_stat_rhs(k_f * k_f, head_cols), 0, keepdims=True))
    qnorm_ref[0] = jnp.sqrt(jnp.max(_dot_exact_lhs(head_rows, q_f * q_f), 1, keepdims=True))


def _moba_proj_call(x, wqt, wk, wvt):
    S, D = x.shape
    nb = S // MOBA_BLOCK
    const = lambda shape: pl.BlockSpec(shape, lambda i: (0, 0))
    return pl.pallas_call(
        _moba_proj_kernel,
        grid=(nb,),
        in_specs=[pl.BlockSpec((MOBA_BLOCK, D), lambda i: (i, 0)), const((MOBA_D, D)), const((D, MOBA_D)),
                  const((MOBA_D, D))],
        out_specs=[pl.BlockSpec((MOBA_D, MOBA_BLOCK), lambda i: (0, i)),
                   pl.BlockSpec((MOBA_HEADS, 1, MOBA_BLOCK, 2 * MOBA_HD), lambda i: (0, i, 0, 0)),
                   pl.BlockSpec((1, 1, MOBA_D), lambda i: (i, 0, 0)),
                   pl.BlockSpec((MOBA_HEADS, 1, MOBA_VROWS, MOBA_BLOCK), lambda i: (0, i, 0, 0)),
                   pl.BlockSpec((1, 1, 128), lambda i: (i, 0, 0)),
                   pl.BlockSpec((1, 128, 1), lambda i: (i, 0, 0))],
        out_shape=[jax.ShapeDtypeStruct((MOBA_D, S), BF16),
                   jax.ShapeDtypeStruct((MOBA_HEADS, nb, MOBA_BLOCK, 2 * MOBA_HD), BF16),
                   jax.ShapeDtypeStruct((nb, 1, MOBA_D), F32),
                   jax.ShapeDtypeStruct((MOBA_HEADS, nb, MOBA_VROWS, MOBA_BLOCK), BF16),
                   jax.ShapeDtypeStruct((nb, 1, 128), F32),
                   jax.ShapeDtypeStruct((nb, 128, 1), F32)],
        compiler_params=_cparams(("parallel",)),
        name="moba_proj",
    )(x, wqt, wk, wvt)


def _moba_kernel(first_ref, qt_ref, k_ref, vt_ref, kmean_ref, o_ref, sel_ref, s0_ref, s1_ref, p0_ref, p1_ref):
    BS, HD = MOBA_BLOCK, MOBA_HD
    h = pl.program_id(0)
    i = pl.program_id(1)
    nb = k_ref.shape[0]
    qt = qt_ref[...]
    slope = LOG2E * jnp.exp2(jnp.zeros((1, BS), F32) - (h + 1).astype(F32) * (8.0 / MOBA_HEADS))
    slope_hi = slope.astype(qt.dtype).astype(F32)
    slope_lo = slope - slope_hi
    row = _iota2((HD, BS), 0)
    qt_ext = jnp.concatenate(
        [qt, jnp.where(row == 0, slope_hi, jnp.where(row == 1, slope_lo, 0.0)).astype(qt.dtype)], axis=0)

    gate = _dot(kmean_ref[...], qt)
    blk = _iota2((nb, BS), 0).astype(F32)
    cand = blk < i.astype(F32)
    sel = jnp.zeros((nb, BS), F32)
    for _ in range(MOBA_TOPK):
        best = jnp.max(jnp.where(cand, gate, -jnp.inf), 0, keepdims=True)
        idx = jnp.min(jnp.where(cand & (gate == best), blk, float(nb)), 0, keepdims=True)
        pick = blk == idx
        sel = jnp.where(pick, 1.0, sel)
        cand = cand & jnp.logical_not(pick)
    sel_ref[...] = sel

    G = MOBA_GROUP
    last = nb - 1
    s_ref = (s0_ref, s1_ref)
    p_ref = (p0_ref, p1_ref)

    first = first_ref[h * nb + i]

    def group(u):
        return [jnp.clip(first + G * u + x, 0, last) for x in range(G)]

    def issue_scores(u, slot):
        tops = []
        for x, j in enumerate(group(u)):
            sc = _dot(k_ref[j], qt_ext)
            s_ref[slot][x] = sc
            tops.append(jnp.max(sc, 0, keepdims=True))
        return tuple(tops)

    def value_blocks(u):
        js = group(u)
        js[0] = jnp.where(u == -1, i, js[0])
        return js

    def weighted_values(u, slot):
        out = None
        for x, j in enumerate(value_blocks(u)):
            part = _dot(vt_ref[j], p_ref[slot][x])
            out = part if out is None else out + part
        return out

    def step(u, slot, tops, a_prev, m, acc, issue_next=True):
        on, shift = [], []
        m_new = m
        for x, j in enumerate(group(u)):
            on.append(sel_ref[pl.ds(j, 1), :] > 0.0)
            shift.append(slope * ((j - i) * BS).astype(F32))
            m_new = jnp.maximum(m_new, jnp.where(on[x], tops[x] + shift[x], NEG_BIG))
        tops_next, pv = [], None
        for x, (j_prev, j_next) in enumerate(zip(value_blocks(u - 1), group(u + 1))):
            part = _dot(vt_ref[j_prev], p_ref[1 - slot][x])
            pv = part if pv is None else pv + part
            if issue_next:
                sc = _dot(k_ref[j_next], qt_ext)
                s_ref[1 - slot][x] = sc
                tops_next.append(jnp.max(sc, 0, keepdims=True))
            p = jnp.exp2(s_ref[slot][x] - (jnp.where(on[x], m_new, -NEG_BIG) - shift[x]))
            p_ref[slot][x] = p.astype(BF16)
        return tuple(tops_next), jnp.exp2(m - m_new), m_new, a_prev * acc + pv

    def body(w, carry):
        carry = step(2 * w, 0, *carry)
        return step(2 * w + 1, 1, *carry)

    tops0 = issue_scores(0, 0)
    s_own = jnp.where(_iota2((BS, BS), 1) >= _iota2((BS, BS), 0), _dot(k_ref[i], qt_ext), NEG_BIG)
    m_own = jnp.max(s_own, 0, keepdims=True)
    p1_ref[1:G] = jnp.zeros((G - 1, BS, BS), BF16)
    p1_ref[0] = jnp.exp2(s_own - m_own).astype(BF16)
    init = (tops0, jnp.ones((1, BS), F32), m_own, jnp.zeros((vt_ref.shape[1], BS), F32))
    steps = (i - first + G - 1) // G
    pairs = steps // 2
    carry = lax.fori_loop(0, pairs, body, init)

    def odd_tail(carry):
        _, a_last, _, acc = step(2 * pairs, 0, *carry, issue_next=False)
        return a_last * acc + weighted_values(2 * pairs, 0)

    def even_tail(carry):
        _, a_prev, _, acc = carry
        return a_prev * acc + weighted_values(2 * pairs - 1, 1)

    acc = lax.cond(steps % 2 == 1, odd_tail, even_tail, carry)
    o_ref[...] = (acc[0:HD] / acc[HD:HD + 1]).astype(o_ref.dtype)


def _moba_first_block(knorm, qnorm, nb):
    BS = MOBA_BLOCK
    k_norm = knorm[:, 0, 0:MOBA_HEADS].T
    q_norm = qnorm[:, 0:MOBA_HEADS, 0].T
    slope = LOG2E * jnp.exp2(-(jnp.arange(MOBA_HEADS, dtype=F32) + 1.0) * (8.0 / MOBA_HEADS))
    i_idx = jnp.arange(nb, dtype=F32)[None, :, None]
    j_idx = jnp.arange(nb, dtype=F32)[None, None, :]
    reach = slope[:, None, None] * ((BS - 1.0) - BS * (i_idx - j_idx))
    bound = q_norm[:, :, None] * (k_norm[:, None, :] + k_norm[:, :, None]) * MOBA_BOUND_SLACK + reach
    matters = (bound >= MOBA_UNDERFLOW) & (j_idx < i_idx)
    first = jnp.min(jnp.where(matters, j_idx, i_idx), axis=2).astype(jnp.int32)
    return first.reshape(-1)


def _moba_call(first_block, qt, k4, vt4, kmean):
    S = qt.shape[1]
    nb = S // MOBA_BLOCK
    return pl.pallas_call(
        _moba_kernel,
        grid_spec=pltpu.PrefetchScalarGridSpec(
            num_scalar_prefetch=1,
            grid=(MOBA_HEADS, nb),
            in_specs=[pl.BlockSpec((MOBA_HD, MOBA_BLOCK), lambda h, i, first: (h, i)),
                      pl.BlockSpec((None, nb, MOBA_BLOCK, 2 * MOBA_HD), lambda h, i, first: (h, 0, 0, 0)),
                      pl.BlockSpec((None, nb, MOBA_VROWS, MOBA_BLOCK), lambda h, i, first: (h, 0, 0, 0)),
                      pl.BlockSpec((None, nb, MOBA_HD), lambda h, i, first: (h, 0, 0))],
            out_specs=pl.BlockSpec((MOBA_HD, MOBA_BLOCK), lambda h, i, first: (h, i)),
            scratch_shapes=[pltpu.VMEM((nb, MOBA_BLOCK), F32)]
            + [pltpu.VMEM((MOBA_GROUP, MOBA_BLOCK, MOBA_BLOCK), F32)] * 2
            + [pltpu.VMEM((MOBA_GROUP, MOBA_BLOCK, MOBA_BLOCK), BF16)] * 2),
        out_shape=jax.ShapeDtypeStruct((MOBA_D, S), BF16),
        compiler_params=_cparams(("parallel", "arbitrary")),
        name="moba",
    )(first_block, qt, k4, vt4, kmean)


N_EXPERTS, TOP_K, EXPERT_FF = 8, 2, 2816
MOE_ROWS = 512


def _router_kernel(x_ref, whi_ref, wlo_ref, b_ref, o_ref):
    x = x_ref[...]
    xhi = x.astype(BF16)
    xlo = (x - xhi.astype(F32)).astype(BF16)
    logits = (jnp.dot(xhi, whi_ref[...], preferred_element_type=F32)
              + jnp.dot(xhi, wlo_ref[...], preferred_element_type=F32)
              + jnp.dot(xlo, whi_ref[...], preferred_element_type=F32)) + b_ref[...]
    lane = _iota2(logits.shape, 1).astype(F32)
    logits = jnp.where(lane < N_EXPERTS, logits, -jnp.inf)
    m1 = jnp.max(logits, -1, keepdims=True)
    i1 = jnp.min(jnp.where(logits == m1, lane, 128.0), -1, keepdims=True)
    rest = jnp.where(lane == i1, -jnp.inf, logits)
    m2 = jnp.max(rest, -1, keepdims=True)
    i2 = jnp.min(jnp.where(rest == m2, lane, 128.0), -1, keepdims=True)
    e = jnp.exp(m2 - m1)
    g1 = 1.0 / (1.0 + e)
    g2 = e / (1.0 + e)
    out = jnp.where(lane == 0, i1, 0.0)
    out = jnp.where(lane == 1, i2, out)
    out = jnp.where(lane == 2, g1, out)
    out = jnp.where(lane == 3, g2, out)
    o_ref[...] = out


def _router_call(x, w_router, b_router, tm):
    S, D = x.shape
    wp = jnp.pad(w_router.astype(F32), ((0, 0), (0, 128 - N_EXPERTS)))
    whi = wp.astype(BF16)
    wlo = (wp - whi.astype(F32)).astype(BF16)
    bp = jnp.pad(b_router.astype(F32), (0, 128 - N_EXPERTS)).reshape(1, 128)
    const = lambda shape: pl.BlockSpec(shape, lambda i: (0, 0))
    return pl.pallas_call(
        _router_kernel,
        grid=(S // tm,),
        in_specs=[pl.BlockSpec((tm, D), lambda i: (i, 0)), const((D, 128)), const((D, 128)), const((1, 128))],
        out_specs=pl.BlockSpec((tm, 128), lambda i: (i, 0)),
        out_shape=jax.ShapeDtypeStruct((S, 128), F32),
        compiler_params=_cparams(("parallel",)),
        name="router",
    )(x, whi, wlo, bp)


def _row_copy(src_hbm, dst_ref, src_row, dst_row, sem):
    return pltpu.make_async_copy(src_hbm.at[pl.ds(src_row, 1)], dst_ref.at[pl.ds(dst_row, 1)], sem)


def _scatter_rows_kernel(dest_ref, lo_ref, hi_ref, x_ref, o_hbm, zero_ref, xs_ref, sem, tile_sem):
    step = pl.program_id(0)
    n_tiles = pl.num_programs(0) - 1
    tm = x_ref.shape[0]

    def row_out(src_ref, src_row, dst_row, s):
        return pltpu.make_async_copy(src_ref.at[pl.ds(src_row, 1)], o_hbm.at[pl.ds(dst_row, 1)], s)

    def wait_tile(buf):
        for slot in range(TOP_K):
            pltpu.make_async_copy(xs_ref.at[buf], o_hbm.at[pl.ds(0, tm)], tile_sem.at[buf]).wait()

    @pl.when(step < n_tiles)
    def _():
        base = step * tm
        buf = step % 2
        xs_ref[buf] = x_ref[...]

        def start(r, c):
            for slot in range(TOP_K):
                row_out(xs_ref.at[buf], r, dest_ref[TOP_K * (base + r) + slot], tile_sem.at[buf]).start()
            return c

        lax.fori_loop(0, tm, start, 0, unroll=8)

        @pl.when(step > 0)
        def _():
            wait_tile(1 - buf)

    @pl.when(step == n_tiles)
    def _():
        wait_tile((n_tiles - 1) % 2)
        zero_ref[...] = jnp.zeros_like(zero_ref)
        for e in range(N_EXPERTS):
            def start(r, c):
                row_out(zero_ref, 0, r, sem).start()
                return c

            def wait(r, c):
                row_out(zero_ref, 0, r, sem).wait()
                return c

            lax.fori_loop(lo_ref[e], hi_ref[e], start, 0)
            lax.fori_loop(lo_ref[e], hi_ref[e], wait, 0)


def _scatter_rows(x, dest, pad_lo, pad_hi, n_rows, tm):
    T, D = x.shape
    n_tiles = T // tm
    return pl.pallas_call(
        _scatter_rows_kernel,
        grid_spec=pltpu.PrefetchScalarGridSpec(
            num_scalar_prefetch=3,
            grid=(n_tiles + 1,),
            in_specs=[pl.BlockSpec((tm, D), lambda i, d, lo, hi: (jnp.minimum(i, n_tiles - 1), 0))],
            out_specs=pl.BlockSpec(memory_space=pl.ANY),
            scratch_shapes=[pltpu.VMEM((8, D), x.dtype), pltpu.VMEM((2, tm, D), x.dtype),
                            pltpu.SemaphoreType.DMA(()), pltpu.SemaphoreType.DMA((2,))]),
        out_shape=jax.ShapeDtypeStruct((n_rows, D), x.dtype),
        compiler_params=_cparams(("arbitrary",)),
        name="moe_scatter",
    )(dest, pad_lo, pad_hi, x)


def _moe_ffn_kernel(be_ref, nu_ref, x_ref, wg_ref, wu_ref, wd_ref, o_ref, acc_ref):
    i = pl.program_id(0)
    j = pl.program_id(1)

    last = pl.num_programs(1) - 1

    @pl.when(i < nu_ref[0])
    def _():
        xb = x_ref[...].astype(BF16)
        part = _dot(_silu(_dot(xb, wg_ref[...])) * _dot(xb, wu_ref[...]), wd_ref[...])

        @pl.when(j == 0)
        def _():
            acc_ref[...] = part

        @pl.when((j > 0) & (j < last))
        def _():
            acc_ref[...] += part

        @pl.when(j == last)
        def _():
            o_ref[...] = acc_ref[...] + part

    @pl.when((j == last) & (i >= nu_ref[0]))
    def _():
        o_ref[...] = jnp.zeros_like(o_ref)


def _moe_ffn_call(x_rows, block_e, n_used, wg, wu, wd, tf):
    n, D = x_rows.shape
    F = wg.shape[2]
    R = MOE_ROWS
    assert F % tf == 0 and F // tf >= 2
    return pl.pallas_call(
        _moe_ffn_kernel,
        grid_spec=pltpu.PrefetchScalarGridSpec(
            num_scalar_prefetch=2,
            grid=(n // R, F // tf),
            in_specs=[pl.BlockSpec((R, D), lambda i, j, be, nu: (i, 0)),
                      pl.BlockSpec((None, D, tf), lambda i, j, be, nu: (be[i], 0, j)),
                      pl.BlockSpec((None, D, tf), lambda i, j, be, nu: (be[i], 0, j)),
                      pl.BlockSpec((None, tf, D), lambda i, j, be, nu: (be[i], j, 0))],
            out_specs=pl.BlockSpec((R, D), lambda i, j, be, nu: (i, 0)),
            scratch_shapes=[pltpu.VMEM((R, D), F32)]),
        out_shape=jax.ShapeDtypeStruct((n, D), F32),
        compiler_params=_cparams(("arbitrary", "arbitrary")),
        name="moe_ffn",
    )(block_e, n_used, x_rows, wg, wu, wd)


def _moe_combine_kernel(d_ref, y_hbm, x_ref, r_ref, g_ref, b_ref, o_ref, y1_ref, y2_ref, sem):
    tm = x_ref.shape[0]
    t = pl.program_id(0)

    def issue(tile, buf):
        base = tile * tm

        def start(r, c):
            _row_copy(y_hbm, y1_ref.at[buf], d_ref[2 * (base + r)], r, sem.at[buf, 0]).start()
            _row_copy(y_hbm, y2_ref.at[buf], d_ref[2 * (base + r) + 1], r, sem.at[buf, 1]).start()
            return c

        lax.fori_loop(0, tm, start, 0, unroll=8)

    @pl.when(t == 0)
    def _():
        issue(0, 0)

    @pl.when(t + 1 < pl.num_programs(0))
    def _():
        issue(t + 1, (t + 1) % 2)

    buf = t % 2
    pltpu.make_async_copy(y_hbm.at[pl.ds(0, tm)], y1_ref.at[buf], sem.at[buf, 0]).wait()
    pltpu.make_async_copy(y_hbm.at[pl.ds(0, tm)], y2_ref.at[buf], sem.at[buf, 1]).wait()
    y = r_ref[:, 2:3] * y1_ref[buf] + r_ref[:, 3:4] * y2_ref[buf]
    o_ref[...] = _layer_norm_rows(DN_ALPHA * x_ref[...] + y, g_ref[...], b_ref[...])


def _moe_combine_call(dest, y_rows, x, routed, g, b, tm):
    S, D = x.shape
    return pl.pallas_call(
        _moe_combine_kernel,
        grid_spec=pltpu.PrefetchScalarGridSpec(
            num_scalar_prefetch=1,
            grid=(S // tm,),
            in_specs=[pl.BlockSpec(memory_space=pl.ANY),
                      pl.BlockSpec((tm, D), lambda i, d: (i, 0)),
                      pl.BlockSpec((tm, 128), lambda i, d: (i, 0)),
                      pl.BlockSpec((1, D), lambda i, d: (0, 0)),
                      pl.BlockSpec((1, D), lambda i, d: (0, 0))],
            out_specs=pl.BlockSpec((tm, D), lambda i, d: (i, 0)),
            scratch_shapes=[pltpu.VMEM((2, tm, D), F32), pltpu.VMEM((2, tm, D), F32),
                            pltpu.SemaphoreType.DMA((2, 2))]),
        out_shape=jax.ShapeDtypeStruct((S, D), F32),
        compiler_params=_cparams(("arbitrary",)),
        name="moe_combine",
    )(dest, y_rows, x, routed, g.reshape(1, D), b.reshape(1, D))


def _moe_sublayer(x, w_router, b_router, wg, wu, wd, g, b):
    T = x.shape[0]
    R = MOE_ROWS
    routed = _router_call(x, w_router, b_router, min(ROW_TILE, T))
    top_e = routed[:, 0:TOP_K].astype(jnp.int32)
    tok_oh = jnp.sum((top_e[:, :, None] == jnp.arange(N_EXPERTS)[None, None, :]).astype(jnp.int32), axis=1)
    counts = jnp.sum(tok_oh, axis=0)
    rank = jnp.cumsum(tok_oh, axis=0) - tok_oh
    padded = (counts + R - 1) // R * R
    pend = jnp.cumsum(padded)
    pstart = pend - padded
    dest = pstart[top_e] + jnp.take_along_axis(rank, top_e, axis=1)
    n_rows = (T * TOP_K + N_EXPERTS * (R - 1)) // R * R
    n_blocks = n_rows // R
    block_first_row = jnp.arange(n_blocks, dtype=jnp.int32) * R
    block_e = jnp.minimum(jnp.sum((pend[None, :] <= block_first_row[:, None]).astype(jnp.int32), axis=1),
                          N_EXPERTS - 1)
    n_used = (pend[-1] // R).astype(jnp.int32).reshape(1)
    dest = dest.reshape(-1).astype(jnp.int32)
    pad_lo = (pstart + counts).astype(jnp.int32)
    pad_hi = jnp.concatenate([pstart[1:], jnp.array([n_rows])]).astype(jnp.int32)
    x_rows = _scatter_rows(x, dest, pad_lo, pad_hi, n_rows, min(ROW_TILE, T))
    y_rows = _moe_ffn_call(x_rows, block_e, n_used, wg, wu, wd, EXPERT_FF // 2)
    return _moe_combine_call(dest, y_rows, x, routed, g, b, min(COMBINE_TILE, T))


GLA_IN = 2 * GLA_QK + 2 * GLA_V + GLA_LR


def _gla_rwkv_sublayer(x, w_in, gla_wa2, gla_ba, gla_norm, mu, w0, w2, a0, a2, g2, k_k, k_a, r_k, ln_w, ln_b,
                       w_out, ln_g, ln_bias):
    S = x.shape[0]
    tm = min(WIDE_TILE, S)
    lr_pad = 128 - GLA_LR
    w_gla = jnp.pad(w_in[:, :GLA_IN], ((0, 0), (0, lr_pad))).astype(BF16)
    w_rwkv = w_in[:, GLA_IN:].astype(BF16)
    p_gla, p_rwkv = _matmul2(x, w_gla, w_rwkv, F32, min(ROW_TILE, S))
    wa2p = jnp.pad(gla_wa2, ((0, lr_pad), (0, 0))).astype(BF16)
    o_gla = _gla_call(p_gla, wa2p, gla_ba, gla_norm)
    o_rwkv = _rwkv_call(p_rwkv, mu, w0, w2, a0, a2, g2, k_k, k_a, r_k, ln_w, ln_b)
    return _matmul_ln([o_gla, o_rwkv], [w_out[:GLA_V].astype(BF16), w_out[GLA_V:].astype(BF16)], x,
                      ln_g, ln_bias, tm)


def _xattn_sublayer(x, mem, wq, wk, wv, wo, ln_g, ln_bias):
    M = mem.shape[0]
    k_mem = _matmul(mem, wk.astype(BF16), BF16, M, D_MODEL)
    v_mem = _matmul(mem, wv.astype(BF16), BF16, M, D_MODEL)
    wq_scaled = (wq * XATTN_HD ** -0.5).astype(BF16)
    return _xattn_call(x, wq_scaled, k_mem, v_mem, wo.astype(BF16), ln_g, ln_bias, min(WIDE_TILE, x.shape[0]))


def _ssd_moba_sublayer(x, w_in, conv_w, conv_b, dt_bias, a_log, d_skip, ssd_norm, w_out, ln_g, ln_bias):
    S = x.shape[0]
    tm = min(WIDE_TILE, S)
    nb = S // MOBA_BLOCK
    o_dt = SSD_COLS
    o_q = o_dt + SSD_HEADS
    w_ssd = jnp.pad(w_in[:, :o_q], ((0, 0), (0, 128 - SSD_HEADS))).astype(BF16)
    w_q = w_in[:, o_q:o_q + MOBA_D]
    w_k = w_in[:, o_q + MOBA_D:o_q + 2 * MOBA_D]
    w_v = w_in[:, o_q + 2 * MOBA_D:]
    p_ssd = _matmul(x, w_ssd, F32, tm, SSD_COLS + 128)
    o_ssd = _ssd_call(p_ssd, conv_w, conv_b, dt_bias, a_log, d_skip, ssd_norm)
    qt, k4, kmean, vt4, knorm, qnorm = _moba_proj_call(x, (w_q.T * (MOBA_HD ** -0.5 * LOG2E)).astype(BF16), w_k.astype(BF16),
                                         w_v.T.astype(BF16))
    kmean_h = kmean.reshape(nb, MOBA_HEADS, MOBA_HD).transpose(1, 0, 2)
    ot_moba = _moba_call(_moba_first_block(knorm, qnorm, nb), qt, k4, vt4, kmean_h)
    return _matmul_ln([o_ssd, ot_moba], [w_out[:SSD_INNER].astype(BF16), w_out[SSD_INNER:].astype(BF16)], x,
                      ln_g, ln_bias, tm, transposed=(False, True))


def kernel(x, mem, l0_w_in, l0_gla_wa2, l0_gla_ba, l0_gla_norm, l0_rwkv_mu, l0_rwkv_w0, l0_rwkv_w2, l0_rwkv_a0, l0_rwkv_a2, l0_rwkv_g2, l0_rwkv_kk, l0_rwkv_ka, l0_rwkv_rk, l0_rwkv_lnw, l0_rwkv_lnb, l0_w_out, l0_ln1_g, l0_ln1_b, l0_xq, l0_xk, l0_xv, l0_xo, l0_ln2_g, l0_ln2_b, l0_ffn_wg, l0_ffn_wu, l0_ffn_wd, l0_ln3_g, l0_ln3_b, l1_w_in, l1_conv_w, l1_conv_b, l1_dt_bias, l1_a_log, l1_d_skip, l1_ssd_norm, l1_w_out, l1_ln1_g, l1_ln1_b, l1_xq, l1_xk, l1_xv, l1_xo, l1_ln2_g, l1_ln2_b, l1_router, l1_router_b, l1_exp_wg, l1_exp_wu, l1_exp_wd, l1_ln3_g, l1_ln3_b):
    x2 = x.reshape(-1, D_MODEL)
    mem2 = mem.reshape(-1, D_MODEL)
    x2 = _gla_rwkv_sublayer(x2, l0_w_in, l0_gla_wa2, l0_gla_ba, l0_gla_norm, l0_rwkv_mu, l0_rwkv_w0, l0_rwkv_w2,
                            l0_rwkv_a0, l0_rwkv_a2, l0_rwkv_g2, l0_rwkv_kk, l0_rwkv_ka, l0_rwkv_rk, l0_rwkv_lnw,
                            l0_rwkv_lnb, l0_w_out, l0_ln1_g, l0_ln1_b)
    x2 = _xattn_sublayer(x2, mem2, l0_xq, l0_xk, l0_xv, l0_xo, l0_ln2_g, l0_ln2_b)
    tm = min(ROW_TILE, x2.shape[0])
    x2 = _ffn_call(x2, l0_ffn_wg.astype(BF16), l0_ffn_wu.astype(BF16), l0_ffn_wd.astype(BF16),
                   l0_ln3_g, l0_ln3_b, tm, l0_ffn_wg.shape[1] // 2)
    x2 = _ssd_moba_sublayer(x2, l1_w_in, l1_conv_w, l1_conv_b, l1_dt_bias, l1_a_log, l1_d_skip, l1_ssd_norm,
                            l1_w_out, l1_ln1_g, l1_ln1_b)
    x2 = _xattn_sublayer(x2, mem2, l1_xq, l1_xk, l1_xv, l1_xo, l1_ln2_g, l1_ln2_b)
    x2 = _moe_sublayer(x2, l1_router, l1_router_b, l1_exp_wg.astype(BF16), l1_exp_wu.astype(BF16),
                       l1_exp_wd.astype(BF16), l1_ln3_g, l1_ln3_b)
    return x2.reshape(x.shape)
```
